```python
import jax, jax.numpy as jnp
from jax import lax
import numpy as np

D_MODEL = 1024
BATCH = 32
SEQ = 2048
DEPTH = 4

GRID_W = 64
CTX_LEN = 256
N_MIXERS = 3
N_ATTN = (DEPTH + 2) // 3
N_POOL = (DEPTH + 1) // 3
N_RWKV = DEPTH // 3
NORM_EPS = 1e-5

HEAD_DIM = 64
N_HEADS = D_MODEL // HEAD_DIM
N_KV_HEADS = N_HEADS // 4
Q_DIM = N_HEADS * HEAD_DIM
KV_DIM = N_KV_HEADS * HEAD_DIM
QKV_DIM = Q_DIM + 2 * KV_DIM
WINDOW = 128
ATTN_BLOCK = 128
KEY_SPAN = ATTN_BLOCK + 2 * WINDOW
ROPE_BASE = 10000.0

POOL_SIZES = (2, 4, 8, 16)
POOL_GROUPS = len(POOL_SIZES)
POOL_GW = D_MODEL // POOL_GROUPS

RWKV_HEAD = 64
RWKV_HEADS = D_MODEL // RWKV_HEAD
DECAY_LORA = 64
ICLR_LORA = 64
GATE_LORA = 128
GN_EPS = 64e-5

N_EXPERTS = 32
TOP_K = 4
D_FF = D_MODEL
SWIGLU_LIMIT = 7.0
SWIGLU_ALPHA = 1.702
MOE_BLOCK = 256

kernel_name = "hybrid_interleaved_diffusion_trunk"


def rms_norm(x, g):
    xf = x.astype(jnp.float32)
    y = xf * lax.rsqrt(jnp.mean(xf * xf, axis=-1, keepdims=True) + NORM_EPS)
    return (y * g.astype(jnp.float32)).astype(x.dtype)


def modulate(h, shift, scale):
    return h * (1 + scale) + shift


def axial_rope_tables(n_tok):
    rows = n_tok // GRID_W
    n_freq = HEAD_DIM // 4
    inv = ROPE_BASE ** (-jnp.arange(n_freq, dtype=jnp.float32) / n_freq)
    row_ang = jnp.arange(rows, dtype=jnp.float32)[:, None] * inv
    col_ang = jnp.arange(GRID_W, dtype=jnp.float32)[:, None] * inv
    ang_r = jnp.broadcast_to(row_ang[:, None, :], (rows, GRID_W, n_freq)).reshape(n_tok, n_freq)
    ang_c = jnp.broadcast_to(col_ang[None, :, :], (rows, GRID_W, n_freq)).reshape(n_tok, n_freq)
    ang = jnp.concatenate([ang_r, ang_r, ang_c, ang_c], axis=-1)
    return jnp.cos(ang), jnp.sin(ang)


def apply_axial_rope(x, cos, sin):
    xf = x.astype(jnp.float32)
    xr = xf.reshape(x.shape[:-1] + (2, 2, HEAD_DIM // 4))
    rot = jnp.stack([-xr[..., 1, :], xr[..., 0, :]], axis=-2).reshape(x.shape)
    return (xf * cos[:, None, :] + rot * sin[:, None, :]).astype(x.dtype)


def softmax_with_sink(logits, sink):
    sink_col = jnp.broadcast_to(sink[None, :, :, None, None], logits.shape[:-1] + (1,))
    p = jax.nn.softmax(jnp.concatenate([logits, sink_col], axis=-1), axis=-1)
    return p[..., :-1]


def windowed_gqa(hx, hc, w_qkv, b_qkv, w_o, b_o, sinks, need_ctx_out):
    B, T, _ = hx.shape
    L = hc.shape[1]
    G = N_HEADS // N_KV_HEADS
    scale = HEAD_DIM ** -0.5

    def project(h):
        n = h.shape[1]
        qkv = h @ w_qkv + b_qkv
        q, k, v = jnp.split(qkv, [Q_DIM, Q_DIM + KV_DIM], axis=-1)
        return (q.reshape(B, n, N_HEADS, HEAD_DIM), k.reshape(B, n, N_KV_HEADS, HEAD_DIM),
                v.reshape(B, n, N_KV_HEADS, HEAD_DIM))

    qx, kx, vx = project(hx)
    qc, kc, vc = project(hc)
    cos, sin = axial_rope_tables(T)
    qx = apply_axial_rope(qx, cos, sin).reshape(B, T, N_KV_HEADS, G, HEAD_DIM)
    kx = apply_axial_rope(kx, cos, sin)
    sink = sinks.astype(jnp.float32).reshape(N_KV_HEADS, G)

    nb = T // ATTN_BLOCK
    q_blocks = jnp.moveaxis(qx.reshape(B, nb, ATTN_BLOCK, N_KV_HEADS, G, HEAD_DIM), 1, 0)
    kx_p = jnp.pad(kx, ((0, 0), (WINDOW, WINDOW), (0, 0), (0, 0)))
    vx_p = jnp.pad(vx, ((0, 0), (WINDOW, WINDOW), (0, 0), (0, 0)))
    rel = jnp.arange(KEY_SPAN)[None, :] - WINDOW - jnp.arange(ATTN_BLOCK)[:, None]
    band = jnp.abs(rel) <= WINDOW

    def block(args):
        qb, b = args
        start = b * ATTN_BLOCK
        kb = lax.dynamic_slice_in_dim(kx_p, start, KEY_SPAN, axis=1)
        vb = lax.dynamic_slice_in_dim(vx_p, start, KEY_SPAN, axis=1)
        kpos = start - WINDOW + jnp.arange(KEY_SPAN)
        valid = band & ((kpos >= 0) & (kpos < T))[None, :]
        s_loc = jnp.einsum('bqkgd,bskd->bkgqs', qb, kb, preferred_element_type=jnp.float32) * scale
        s_loc = jnp.where(valid, s_loc, -jnp.inf)
        s_ctx = jnp.einsum('bqkgd,bskd->bkgqs', qb, kc, preferred_element_type=jnp.float32) * scale
        p = softmax_with_sink(jnp.concatenate([s_ctx, s_loc], axis=-1), sink).astype(vb.dtype)
        o = (jnp.einsum('bkgqs,bskd->bqkgd', p[..., :L], vc)
             + jnp.einsum('bkgqs,bskd->bqkgd', p[..., L:], vb))
        return o.reshape(B, ATTN_BLOCK, Q_DIM)

    ox = lax.map(block, (q_blocks, jnp.arange(nb)))
    ox = jnp.moveaxis(ox, 0, 1).reshape(B, T, Q_DIM)
    yx = ox @ w_o + b_o
    yc = None
    if need_ctx_out:
        qcg = qc.reshape(B, L, N_KV_HEADS, G, HEAD_DIM)
        s = jnp.einsum('bqkgd,bskd->bkgqs', qcg, kc, preferred_element_type=jnp.float32) * scale
        p = softmax_with_sink(s, sink).astype(vc.dtype)
        oc = jnp.einsum('bkgqs,bskd->bqkgd', p, vc).reshape(B, L, Q_DIM)
        yc = oc @ w_o + b_o
    return yx, yc


def multiscale_pool(h, w, b, ls):
    n = h.shape[1]
    hf = h.astype(jnp.float32)
    cs = jnp.pad(jnp.cumsum(hf, axis=1), ((0, 0), (1, 0), (0, 0)))
    t = jnp.arange(n)
    outs = []
    for g, size in enumerate(POOL_SIZES):
        lo = jnp.clip(t - size // 2, 0, n)
        hi = jnp.clip(t + size - size // 2, 0, n)
        sl = slice(g * POOL_GW, (g + 1) * POOL_GW)
        csg = cs[..., sl]
        mean = (jnp.take(csg, hi, axis=1) - jnp.take(csg, lo, axis=1)) / (hi - lo).astype(jnp.float32)[None, :, None]
        d = (mean - hf[..., sl]).astype(h.dtype)
        outs.append(jnp.einsum('bti,io->bto', d, w[g]) + b[g])
    return jnp.concatenate(outs, axis=-1) * ls


def centred_token_shift(h):
    zero = jnp.zeros_like(h[:, :1])
    prev = jnp.concatenate([zero, h[:, :-1]], axis=1)
    nxt = jnp.concatenate([h[:, 1:], zero], axis=1)
    return 0.5 * (prev + nxt) - h


def rwkv7_streams(h, mix, w_rkv, w0, w1, w2, a0, a1, a2, g1, g2, k_k, k_a):
    B, n, _ = h.shape
    f32 = jnp.float32
    xx = centred_token_shift(h)
    x_r, x_w, x_k, x_v, x_a, x_g = (h + xx * mix[i] for i in range(6))
    r = x_r @ w_rkv[0]
    k = x_k @ w_rkv[1]
    v = x_v @ w_rkv[2]
    g = jax.nn.sigmoid(x_g @ g1) @ g2

    def heads(z):
        return z.astype(f32).reshape(B, n, RWKV_HEADS, RWKV_HEAD)

    kk = heads(k * k_k)
    kk = kk / jnp.maximum(jnp.sqrt(jnp.sum(kk * kk, axis=-1, keepdims=True)), 1e-12)
    dirs = []
    for d in range(2):
        w_log = -jax.nn.softplus(-(w0[d] + jnp.tanh(x_w @ w1[d]) @ w2[d]).astype(f32)) - 0.5
        decay = jnp.exp(-jnp.exp(w_log))
        a = jax.nn.sigmoid((a0[d] + (x_a @ a1[d]) @ a2[d]).astype(f32))
        k_d = k.astype(f32) * (1 + (a - 1) * k_a.astype(f32))
        dirs.append((heads(decay), heads(k_d), -kk, kk * heads(a)))
    return heads(r), heads(v), g, dirs


def rwkv7_scan(r, decay, k, v, a, b, s0, reverse):
    def step(S, inp):
        r_t, w_t, k_t, v_t, a_t, b_t = inp
        sa = jnp.einsum('bhvk,bhk->bhv', S, a_t)
        S = S * w_t[:, :, None, :] + sa[..., None] * b_t[:, :, None, :] + v_t[..., None] * k_t[:, :, None, :]
        return S, jnp.einsum('bhvk,bhk->bhv', S, r_t)
    xs = tuple(jnp.moveaxis(z, 1, 0) for z in (r, decay, k, v, a, b))
    s_last, ys = lax.scan(step, s0, xs, reverse=reverse)
    return s_last, jnp.moveaxis(ys, 0, 1)


def rwkv7_output(y, r, v, g, dirs, r_k, ln_w, ln_b, w_o, dtype):
    B, n = y.shape[:2]
    mu = jnp.mean(y, axis=-1, keepdims=True)
    var = jnp.mean(jnp.square(y - mu), axis=-1, keepdims=True)
    yn = ((y - mu) * lax.rsqrt(var + GN_EPS)).reshape(B, n, D_MODEL) * ln_w.astype(jnp.float32) + ln_b.astype(jnp.float32)
    rk = r_k.astype(jnp.float32)
    bonus = sum(jnp.sum(r * dd[1] * rk, axis=-1, keepdims=True) for dd in dirs) * v
    out = (yn + bonus.reshape(B, n, D_MODEL)).astype(dtype) * g
    return out @ w_o


def rwkv7_mixer(hx, hc, mix, w_rkv, w0, w1, w2, a0, a1, a2, g1, g2, k_k, k_a, r_k, ln_w, ln_b, w_o, need_ctx_out):
    r_x, v_x, g_x, dirs_x = rwkv7_streams(hx, mix, w_rkv, w0, w1, w2, a0, a1, a2, g1, g2, k_k, k_a)
    r_c, v_c, g_c, dirs_c = rwkv7_streams(hc, mix, w_rkv, w0, w1, w2, a0, a1, a2, g1, g2, k_k, k_a)
    s0 = jnp.zeros((hx.shape[0], RWKV_HEADS, RWKV_HEAD, RWKV_HEAD), jnp.float32)
    ys_x, ys_c = [], []
    for d, rev in enumerate((False, True)):
        dc, kc_, ac, bc = dirs_c[d]
        s_ctx, y_c = rwkv7_scan(r_c, dc, kc_, v_c, ac, bc, s0, rev)
        dx, kx_, ax, bx = dirs_x[d]
        _, y_x = rwkv7_scan(r_x, dx, kx_, v_x, ax, bx, s_ctx, rev)
        ys_x.append(y_x)
        ys_c.append(y_c)
    yx = rwkv7_output(ys_x[0] + ys_x[1], r_x, v_x, g_x, dirs_x, r_k, ln_w, ln_b, w_o, hx.dtype)
    yc = None
    if need_ctx_out:
        yc = rwkv7_output(ys_c[0] + ys_c[1], r_c, v_c, g_c, dirs_c, r_k, ln_w, ln_b, w_o, hc.dtype)
    return yx, yc


def moe_ffn(xf, w_router, b_router, w1, b1, w2, b2):
    n = xf.shape[0]
    logits = (xf @ w_router + b_router).astype(jnp.float32)
    top_logits, top_idx = lax.top_k(logits, TOP_K)
    gates = jax.nn.softmax(top_logits, axis=-1)
    nk = n * TOP_K
    e_flat = top_idx.reshape(nk)
    order = jnp.argsort(e_flat)
    e_sorted = e_flat[order]
    tok_sorted = (order // TOP_K).astype(jnp.int32)
    gate_sorted = gates.reshape(nk)[order]
    counts = jnp.bincount(e_flat, length=N_EXPERTS)
    padded = (counts + MOE_BLOCK - 1) // MOE_BLOCK * MOE_BLOCK
    pad_end = jnp.cumsum(padded)
    pad_start = pad_end - padded
    start = jnp.cumsum(counts) - counts
    dest = pad_start[e_sorted] + jnp.arange(nk) - start[e_sorted]
    n_blocks = -(-nk // MOE_BLOCK) + N_EXPERTS
    rows = n_blocks * MOE_BLOCK
    row_tok = jnp.full((rows,), n, jnp.int32).at[dest].set(tok_sorted)
    row_gate = jnp.zeros((rows,), jnp.float32).at[dest].set(gate_sorted)
    block_expert = jnp.minimum(jnp.searchsorted(pad_end, jnp.arange(n_blocks) * MOE_BLOCK, side='right'), N_EXPERTS - 1)
    x_pad = jnp.concatenate([xf, jnp.zeros((1, xf.shape[1]), xf.dtype)], axis=0)
    xb = x_pad[row_tok].reshape(n_blocks, MOE_BLOCK, xf.shape[1])

    def expert_block(args):
        xblk, e = args
        hcat = xblk @ w1[e] + b1[e]
        h_glu = jnp.minimum(hcat[:, :D_FF], SWIGLU_LIMIT)
        h_lin = jnp.clip(hcat[:, D_FF:], -SWIGLU_LIMIT, SWIGLU_LIMIT)
        act = h_glu * jax.nn.sigmoid(SWIGLU_ALPHA * h_glu) * (h_lin + 1)
        return act @ w2[e] + b2[e]

    yb = lax.map(expert_block, (xb, block_expert)).reshape(rows, xf.shape[1])
    yb = yb * row_gate[:, None].astype(yb.dtype)
    return jax.ops.segment_sum(yb, row_tok, num_segments=n + 1)[:n]


def setup_inputs(seed: int = 0) -> dict:
    key = jax.random.key(seed)
    keys = list(jax.random.split(key, 64))
    D = D_MODEL

    def nrm(shape, std):
        return std * jax.random.normal(keys.pop(), shape, jnp.float32)

    def uni(shape, lo, hi):
        return jax.random.uniform(keys.pop(), shape, jnp.float32, lo, hi)

    inp = {}
    inp['x'] = nrm((BATCH, SEQ, D), 1.0)
    inp['c'] = nrm((BATCH, D), 1.0)
    inp['ctx'] = nrm((BATCH, CTX_LEN, D), 1.0)
    inp['c_ctx'] = nrm((D,), 1.0)
    inp['ada_w'] = nrm((DEPTH, D, 6 * D), 0.02)
    inp['ada_b'] = nrm((DEPTH, 6 * D), 0.01)
    inp['norm_g'] = 1.0 + nrm((DEPTH, 2, D), 0.05)
    inp['attn_w_qkv'] = nrm((N_ATTN, D, QKV_DIM), D ** -0.5)
    inp['attn_b_qkv'] = nrm((N_ATTN, QKV_DIM), 0.01)
    inp['attn_w_o'] = nrm((N_ATTN, Q_DIM, D), Q_DIM ** -0.5)
    inp['attn_b_o'] = nrm((N_ATTN, D), 0.01)
    inp['attn_sinks'] = nrm((N_ATTN, N_HEADS), 0.5)
    inp['pool_w'] = nrm((N_POOL, POOL_GROUPS, POOL_GW, POOL_GW), POOL_GW ** -0.5)
    inp['pool_b'] = nrm((N_POOL, POOL_GROUPS, POOL_GW), 0.01)
    inp['pool_scale'] = 1.0 + nrm((N_POOL, D), 0.1)
    inp['rwkv_mix'] = uni((N_RWKV, 6, D), 0.0, 1.0)
    inp['rwkv_w_rkv'] = nrm((N_RWKV, 3, D, D), D ** -0.5)
    inp['rwkv_w0'] = uni((N_RWKV, 2, D), -6.0, 0.0)
    inp['rwkv_w1'] = nrm((N_RWKV, 2, D, DECAY_LORA), D ** -0.5)
    inp['rwkv_w2'] = nrm((N_RWKV, 2, DECAY_LORA, D), 0.3 * DECAY_LORA ** -0.5)
    inp['rwkv_a0'] = nrm((N_RWKV, 2, D), 0.1)
    inp['rwkv_a1'] = nrm((N_RWKV, 2, D, ICLR_LORA), D ** -0.5)
    inp['rwkv_a2'] = nrm((N_RWKV, 2, ICLR_LORA, D), 0.5 * ICLR_LORA ** -0.5)
    inp['rwkv_g1'] = nrm((N_RWKV, D, GATE_LORA), D ** -0.5)
    inp['rwkv_g2'] = nrm((N_RWKV, GATE_LORA, D), GATE_LORA ** -0.5)
    inp['rwkv_k_k'] = 0.85 + nrm((N_RWKV, D), 0.05)
    inp['rwkv_k_a'] = 1.0 + nrm((N_RWKV, D), 0.05)
    inp['rwkv_r_k'] = nrm((N_RWKV, RWKV_HEADS, RWKV_HEAD), 0.1)
    inp['rwkv_ln_w'] = 1.0 + nrm((N_RWKV, D), 0.05)
    inp['rwkv_ln_b'] = nrm((N_RWKV, D), 0.01)
    inp['rwkv_w_o'] = nrm((N_RWKV, D, D), D ** -0.5)
    inp['moe_w_router'] = nrm((DEPTH, D, N_EXPERTS), D ** -0.5)
    inp['moe_b_router'] = nrm((DEPTH, N_EXPERTS), 0.01)
    inp['moe_w1'] = nrm((DEPTH, N_EXPERTS, D, 2 * D_FF), D ** -0.5)
    inp['moe_b1'] = nrm((DEPTH, N_EXPERTS, 2 * D_FF), 0.01)
    inp['moe_w2'] = nrm((DEPTH, N_EXPERTS, D_FF, D), D_FF ** -0.5)
    inp['moe_b2'] = nrm((DEPTH, N_EXPERTS, D), 0.01)
    inp['final_g'] = 1.0 + nrm((D,), 0.05)
    return inp


def reference(x, c, ctx, c_ctx, ada_w, ada_b, norm_g,
              attn_w_qkv, attn_b_qkv, attn_w_o, attn_b_o, attn_sinks,
              pool_w, pool_b, pool_scale,
              rwkv_mix, rwkv_w_rkv, rwkv_w0, rwkv_w1, rwkv_w2, rwkv_a0, rwkv_a1, rwkv_a2,
              rwkv_g1, rwkv_g2, rwkv_k_k, rwkv_k_a, rwkv_r_k, rwkv_ln_w, rwkv_ln_b, rwkv_w_o,
              moe_w_router, moe_b_router, moe_w1, moe_b1, moe_w2, moe_b2, final_g):
    B, T, D = x.shape
    L = ctx.shape[1]
    h_x, h_c = x, ctx
    silu_c = jax.nn.silu(c)
    silu_cc = jax.nn.silu(c_ctx)
    for layer in range(DEPTH):
        last = layer == DEPTH - 1
        kind = layer % N_MIXERS
        j = layer // N_MIXERS
        mod_x = (silu_c @ ada_w[layer] + ada_b[layer])[:, None, :]
        mod_c = (silu_cc @ ada_w[layer] + ada_b[layer])[None, None, :]
        shx1, scx1, gx1, shx2, scx2, gx2 = jnp.split(mod_x, 6, axis=-1)
        shc1, scc1, gc1, shc2, scc2, gc2 = jnp.split(mod_c, 6, axis=-1)

        ax = modulate(rms_norm(h_x, norm_g[layer, 0]), shx1, scx1)
        ac = modulate(rms_norm(h_c, norm_g[layer, 0]), shc1, scc1)
        if kind == 0:
            yx, yc = windowed_gqa(ax, ac, attn_w_qkv[j], attn_b_qkv[j], attn_w_o[j], attn_b_o[j],
                                  attn_sinks[j], not last)
        elif kind == 1:
            yx = multiscale_pool(ax, pool_w[j], pool_b[j], pool_scale[j])
            yc = None if last else multiscale_pool(ac, pool_w[j], pool_b[j], pool_scale[j])
        else:
            yx, yc = rwkv7_mixer(ax, ac, rwkv_mix[j], rwkv_w_rkv[j], rwkv_w0[j], rwkv_w1[j], rwkv_w2[j],
                                 rwkv_a0[j], rwkv_a1[j], rwkv_a2[j], rwkv_g1[j], rwkv_g2[j],
                                 rwkv_k_k[j], rwkv_k_a[j], rwkv_r_k[j], rwkv_ln_w[j], rwkv_ln_b[j],
                                 rwkv_w_o[j], not last)
        h_x = h_x + gx1 * yx

        fx = modulate(rms_norm(h_x, norm_g[layer, 1]), shx2, scx2)
        moe_args = (moe_w_router[layer], moe_b_router[layer], moe_w1[layer], moe_b1[layer],
                    moe_w2[layer], moe_b2[layer])
        if last:
            h_x = h_x + gx2 * moe_ffn(fx.reshape(B * T, D), *moe_args).reshape(B, T, D)
        else:
            h_c = h_c + gc1 * yc
            fc = modulate(rms_norm(h_c, norm_g[layer, 1]), shc2, scc2)
            f_all = jnp.concatenate([fx.reshape(B * T, D), fc.reshape(B * L, D)], axis=0)
            out = moe_ffn(f_all, *moe_args)
            h_x = h_x + gx2 * out[:B * T].reshape(B, T, D)
            h_c = h_c + gc2 * out[B * T:].reshape(B, L, D)
    return rms_norm(h_x, final_g)
```

```python
import functools

import jax
import jax.numpy as jnp
import numpy as np
from jax import lax
from jax.experimental import pallas as pl
from jax.experimental.pallas import tpu as pltpu

F32 = jnp.float32
BF16 = jnp.bfloat16

D = 1024
NORM_EPS = 1e-5
GRID_W = 64
HEAD_DIM = 64
N_HEADS = 16
N_KV = 4
Q_DIM = 1024
KV_DIM = 256
QK_DIM = Q_DIM + KV_DIM
WINDOW = 128
QB = 128
ROPE_BASE = 10000.0
POOL_SIZES = (2, 4, 8, 16)
POOL_GW = 256
HALO = 8
GN_EPS = 64e-5
N_EXPERTS = 32
TOP_K = 4
D_FF = 1024
SWIGLU_LIMIT = 7.0
SWIGLU_ALPHA = 1.702
MOE_BLOCK = 256
RW_CHUNK = 64
RW_GROUP = 256
VMEM_LIMIT = 56 * 1024 * 1024


def _cparams(sem):
    return pltpu.CompilerParams(dimension_semantics=sem, vmem_limit_bytes=VMEM_LIMIT)


def _norm_mod(h, g, shift, scale):
    ms = jnp.mean(h * h, axis=-1, keepdims=True)
    y = h * lax.rsqrt(ms + NORM_EPS) * g
    return y * (1.0 + scale) + shift


class _Layout:
    def __init__(self, B, T, L):
        self.B, self.T, self.L = B, T, L
        self.tm = 256 if L % 256 == 0 else 128
        self.rows_x, self.rows_c = B * T, B * L
        self.N = self.rows_x + self.rows_c
        self.tx, self.tc = T // self.tm, L // self.tm
        self.n_xt = self.rows_x // self.tm
        self.n_t = self.N // self.tm

    def mod_idx(self, i):
        return jnp.where(i < self.n_xt, i // self.tx, self.B)

    def mod_spec(self):
        return pl.BlockSpec((1, 1, 6 * D), lambda i: (self.mod_idx(i), 0, 0))

    def row_spec(self, width=D):
        return pl.BlockSpec((self.tm, width), lambda i: (i, 0))

    def seq_tile(self, i):
        j = jnp.where(i < self.n_xt, i % self.tx, (i - self.n_xt) % self.tc)
        n = jnp.where(i < self.n_xt, self.tx, self.tc)
        return j, n

    def halo_specs(self):
        hb = self.tm // HALO
        last = self.N // HALO - 1
        prev = pl.BlockSpec((HALO, D), lambda i: (jnp.maximum(i * hb - 1, 0), 0))
        nxt = pl.BlockSpec((HALO, D), lambda i: (jnp.minimum((i + 1) * hb, last), 0))
        return prev, nxt


def _const_spec(shape):
    nd = len(shape)
    return pl.BlockSpec(shape, lambda *_: (0,) * nd)


def _ada_kernel(c_ref, w_ref, b_ref, o_ref):
    c = c_ref[...]
    s = c * jax.nn.sigmoid(c)
    o_ref[0] = jnp.dot(s, w_ref[0], preferred_element_type=F32, precision=lax.Precision.HIGHEST) + b_ref[0]


def _ada_mods(c_all, ada_w, ada_b):
    depth = ada_w.shape[0]
    R = c_all.shape[0]
    nt = 1536
    return pl.pallas_call(
        _ada_kernel,
        grid=(depth, 6 * D // nt),
        in_specs=[pl.BlockSpec((R, D), lambda l, j: (0, 0)),
                  pl.BlockSpec((1, D, nt), lambda l, j: (l, 0, j)),
                  pl.BlockSpec((1, 1, nt), lambda l, j: (l, 0, j))],
        out_specs=pl.BlockSpec((1, R, nt), lambda l, j: (l, 0, j)),
        out_shape=jax.ShapeDtypeStruct((depth, R, 6 * D), F32),
        compiler_params=_cparams(("parallel", "parallel")),
        name="ada_mod",
    )(c_all, ada_w, ada_b.reshape(depth, 1, 6 * D))


def _qkv_kernel(h_ref, mod_ref, g_ref, w_ref, b_ref, cos_ref, sin_ref, q_ref, k_ref, v_ref):
    m = mod_ref[0]
    a = _norm_mod(h_ref[...], g_ref[...], m[:, 0:D], m[:, D:2 * D])
    qkv = jnp.dot(a.astype(BF16), w_ref[...], preferred_element_type=F32) + b_ref[...]
    qk = qkv[:, :QK_DIM]
    lane = lax.broadcasted_iota(jnp.int32, (1, QK_DIM), 1)
    first = (lane % 32) < 16
    rot = jnp.where(first, -pltpu.roll(qk, QK_DIM - 16, 1), pltpu.roll(qk, 16, 1))
    qk = qk * cos_ref[...] + rot * sin_ref[...]
    q_ref[...] = (qk[:, :Q_DIM] * (HEAD_DIM ** -0.5)).astype(BF16)
    k_ref[...] = qk[:, Q_DIM:].astype(BF16)
    v_ref[...] = qkv[:, QK_DIM:].astype(BF16)


def _rope_tables(T, L):
    rows = T // GRID_W
    n_freq = HEAD_DIM // 4
    inv = ROPE_BASE ** (-jnp.arange(n_freq, dtype=F32) / n_freq)
    row_ang = jnp.arange(rows, dtype=F32)[:, None] * inv
    col_ang = jnp.arange(GRID_W, dtype=F32)[:, None] * inv
    ang_r = jnp.broadcast_to(row_ang[:, None, :], (rows, GRID_W, n_freq)).reshape(T, n_freq)
    ang_c = jnp.broadcast_to(col_ang[None, :, :], (rows, GRID_W, n_freq)).reshape(T, n_freq)
    ang = jnp.concatenate([ang_r, ang_r, ang_c, ang_c], axis=-1)
    cos = jnp.concatenate([jnp.cos(ang), jnp.ones((L, HEAD_DIM), F32)], axis=0)
    sin = jnp.concatenate([jnp.sin(ang), jnp.zeros((L, HEAD_DIM), F32)], axis=0)
    reps = QK_DIM // HEAD_DIM
    return jnp.tile(cos, (1, reps)), jnp.tile(sin, (1, reps))


def _attn_qkv(lay, h, mod, g, w_qkv, b_qkv, cos, sin):
    tm = lay.tm

    def tab_idx(i):
        return (jnp.where(i < lay.n_xt, i % lay.tx, lay.tx + (i - lay.n_xt) % lay.tc), 0)

    return pl.pallas_call(
        _qkv_kernel,
        grid=(lay.n_t,),
        in_specs=[lay.row_spec(), lay.mod_spec(), _const_spec((1, D)),
                  _const_spec((D, Q_DIM + 2 * KV_DIM)), _const_spec((1, Q_DIM + 2 * KV_DIM)),
                  pl.BlockSpec((tm, QK_DIM), tab_idx), pl.BlockSpec((tm, QK_DIM), tab_idx)],
        out_specs=[lay.row_spec(Q_DIM), lay.row_spec(KV_DIM), lay.row_spec(KV_DIM)],
        out_shape=[jax.ShapeDtypeStruct((lay.N, Q_DIM), BF16),
                   jax.ShapeDtypeStruct((lay.N, KV_DIM), BF16),
                   jax.ShapeDtypeStruct((lay.N, KV_DIM), BF16)],
        compiler_params=_cparams(("parallel",)),
        name="attn_qkv",
    )(h, mod, g.reshape(1, D), w_qkv.astype(BF16), b_qkv.reshape(1, -1), cos, sin)


def _attn_heads(q, kk, vv, mask, sink_ref, o_ref):
    R = q.shape[0]
    G = N_HEADS // N_KV
    for g in range(N_KV):
        kg = kk[:, g * HEAD_DIM:(g + 1) * HEAD_DIM]
        vg = vv[:, g * HEAD_DIM:(g + 1) * HEAD_DIM]
        qs = jnp.concatenate([q[:, (g * G + j) * HEAD_DIM:(g * G + j + 1) * HEAD_DIM] for j in range(G)], axis=0)
        s = lax.dot_general(qs, kg, (((1,), (1,)), ((), ())), preferred_element_type=F32)
        if mask is not None:
            s = jnp.where(jnp.concatenate([mask] * G, axis=0), s, -jnp.inf)
        sink = jnp.concatenate([jnp.full((R, 1), sink_ref[g * G + j], F32) for j in range(G)], axis=0)
        m = jnp.maximum(jnp.max(s, axis=-1, keepdims=True), sink)
        p = jnp.exp(s - m)
        denom = jnp.sum(p, axis=-1, keepdims=True) + jnp.exp(sink - m)
        o = jnp.dot(p.astype(BF16), vg, preferred_element_type=F32) / denom
        for j in range(G):
            hq = g * G + j
            o_ref[:, hq * HEAD_DIM:(hq + 1) * HEAD_DIM] = o[j * R:(j + 1) * R].astype(o_ref.dtype)


def _attn_kernel(sink_ref, q_ref, kp_ref, kc_ref, kn_ref, vp_ref, vc_ref, vn_ref, kx_ref, vx_ref, o_ref, *, T, L):
    i = pl.program_id(1)
    nqb = T // QB

    @pl.when(i < nqb)
    def _():
        kk = jnp.concatenate([kx_ref[...], kp_ref[...], kc_ref[...], kn_ref[...]], axis=0)
        vv = jnp.concatenate([vx_ref[...], vp_ref[...], vc_ref[...], vn_ref[...]], axis=0)
        S = L + 3 * QB
        col = lax.broadcasted_iota(jnp.int32, (QB, S), 1)
        row = lax.broadcasted_iota(jnp.int32, (QB, S), 0)
        rel = col - L - QB - row
        kpos = i * QB - QB + (col - L)
        local_ok = (jnp.abs(rel) <= WINDOW) & (kpos >= 0) & (kpos < T)
        mask = (col < L) | local_ok
        _attn_heads(q_ref[...], kk, vv, mask, sink_ref, o_ref)

    @pl.when(i >= nqb)
    def _():
        _attn_heads(q_ref[...], kx_ref[...], vx_ref[...], None, sink_ref, o_ref)


def _attention(lay, q, k, v, sinks, need_ctx):
    B, T, L = lay.B, lay.T, lay.L
    nqb, ncb = T // QB, L // QB
    cq = lay.rows_x // QB
    cb = lay.rows_x // L
    smem = pl.BlockSpec(memory_space=pltpu.SMEM)

    def cur(b, i):
        return (jnp.where(i < nqb, b * nqb + i, cq + b * ncb + (i - nqb)), 0)

    def prev(b, i):
        return (b * nqb + jnp.clip(i - 1, 0, nqb - 1), 0)

    def mid(b, i):
        return (b * nqb + jnp.minimum(i, nqb - 1), 0)

    def nxt(b, i):
        return (b * nqb + jnp.minimum(i + 1, nqb - 1), 0)

    def ctxb(b, i):
        return (cb + b, 0)

    kv_specs = [pl.BlockSpec((QB, KV_DIM), prev), pl.BlockSpec((QB, KV_DIM), mid), pl.BlockSpec((QB, KV_DIM), nxt)]
    n_out = lay.N if need_ctx else lay.rows_x
    return pl.pallas_call(
        functools.partial(_attn_kernel, T=T, L=L),
        grid=(B, nqb + (ncb if need_ctx else 0)),
        in_specs=[smem, pl.BlockSpec((QB, Q_DIM), cur)] + kv_specs + kv_specs
                 + [pl.BlockSpec((L, KV_DIM), ctxb), pl.BlockSpec((L, KV_DIM), ctxb)],
        out_specs=pl.BlockSpec((QB, Q_DIM), cur),
        out_shape=jax.ShapeDtypeStruct((n_out, Q_DIM), BF16),
        compiler_params=_cparams(("parallel", "arbitrary")),
        name="attention",
    )(sinks, q, k, k, k, v, v, v, k, v)


def _linres_kernel(h_ref, x_ref, mod_ref, w_ref, b_ref, o_ref, *, gate_slot):
    gate = mod_ref[0][:, gate_slot * D:(gate_slot + 1) * D]
    y = jnp.dot(x_ref[...], w_ref[...], preferred_element_type=F32) + b_ref[...]
    o_ref[...] = h_ref[...] + gate * y


def _linear_residual(lay, h, x, mod, w, b, gate_slot, n_rows):
    return pl.pallas_call(
        functools.partial(_linres_kernel, gate_slot=gate_slot),
        grid=(n_rows // lay.tm,),
        in_specs=[lay.row_spec(), lay.row_spec(x.shape[1]), lay.mod_spec(),
                  _const_spec(w.shape), _const_spec((1, D))],
        out_specs=lay.row_spec(),
        out_shape=jax.ShapeDtypeStruct((n_rows, D), F32),
        compiler_params=_cparams(("parallel",)),
        name="linear_residual",
    )(h, x, mod, w.astype(BF16), b.reshape(1, D))


def _fill_ext(ext_ref, h_ref, hp_ref, hn_ref, g, shift, scale, j, n, tm):
    a = _norm_mod(h_ref[...], g, shift, scale)
    ap = _norm_mod(hp_ref[...], g, shift, scale)
    an = _norm_mod(hn_ref[...], g, shift, scale)
    ext_ref[0:HALO, :] = jnp.where(j == 0, 0.0, ap)
    ext_ref[HALO:HALO + tm, :] = a
    ext_ref[HALO + tm:2 * HALO + tm, :] = jnp.where(j == n - 1, 0.0, an)
    return a


def _pool_kernel(h_ref, hp_ref, hn_ref, mod_ref, g_ref, w_ref, b_ref, ls_ref, o_ref, ext_ref, *, lay):
    tm = lay.tm
    i = pl.program_id(0)
    j, n = lay.seq_tile(i)
    m = mod_ref[0]
    a = _fill_ext(ext_ref, h_ref, hp_ref, hn_ref, g_ref[...], m[:, 0:D], m[:, D:2 * D], j, n, tm)
    t = j * tm + lax.broadcasted_iota(jnp.int32, (tm, 1), 0)
    n_tok = n * tm
    outs = []
    for gi, size in enumerate(POOL_SIZES):
        sl = slice(gi * POOL_GW, (gi + 1) * POOL_GW)
        acc = ext_ref[HALO - size // 2:HALO - size // 2 + tm, sl]
        for o in range(-size // 2 + 1, size - size // 2):
            acc = acc + ext_ref[HALO + o:HALO + o + tm, sl]
        lo = jnp.maximum(t - size // 2, 0)
        hi = jnp.minimum(t + size - size // 2, n_tok)
        d = acc / (hi - lo).astype(F32) - a[:, sl]
        outs.append(jnp.dot(d.astype(BF16), w_ref[gi], preferred_element_type=F32) + b_ref[gi])
    y = jnp.concatenate(outs, axis=-1) * ls_ref[...]
    o_ref[...] = h_ref[...] + m[:, 2 * D:3 * D] * y


def _pool_mixer(lay, h, mod, g, w, b, ls, n_rows):
    prev, nxt = lay.halo_specs()
    return pl.pallas_call(
        functools.partial(_pool_kernel, lay=lay),
        grid=(n_rows // lay.tm,),
        in_specs=[lay.row_spec(), prev, nxt, lay.mod_spec(), _const_spec((1, D)),
                  _const_spec(w.shape), _const_spec((4, 1, POOL_GW)), _const_spec((1, D))],
        out_specs=lay.row_spec(),
        out_shape=jax.ShapeDtypeStruct((n_rows, D), F32),
        scratch_shapes=[pltpu.VMEM((lay.tm + 2 * HALO, D), F32)],
        compiler_params=_cparams(("parallel",)),
        name="pool_mixer",
    )(h, h, h, mod, g.reshape(1, D), w.astype(BF16), b.reshape(4, 1, POOL_GW), ls.reshape(1, D))


def _seg_sum(x, bd):
    outs = []
    for q in range(D // RW_GROUP):
        xg = x[:, q * RW_GROUP:(q + 1) * RW_GROUP]
        hi = xg.astype(BF16)
        lo = (xg - hi.astype(F32)).astype(BF16)
        outs.append(jnp.dot(hi, bd, preferred_element_type=F32) + jnp.dot(lo, bd, preferred_element_type=F32))
    return jnp.concatenate(outs, axis=-1)


def _rwkv_proj_kernel(h_ref, hp_ref, hn_ref, mod_ref, g_ref, mix_ref, wr_ref, wk_ref, wv_ref, g1_ref, g2_ref,
                      w1_ref, w2_ref, w0_ref, a1_ref, a2_ref, a0_ref, kk_ref, ka_ref, rk_ref, bd_ref,
                      r_out, v_out, kn_out, g_out, bonus_out, lw0_out, k0_out, b0_out, lw1_out, k1_out, b1_out,
                      ext_ref, *, lay):
    tm = lay.tm
    i = pl.program_id(0)
    j, n = lay.seq_tile(i)
    m = mod_ref[0]
    a = _fill_ext(ext_ref, h_ref, hp_ref, hn_ref, g_ref[...], m[:, 0:D], m[:, D:2 * D], j, n, tm)
    prev = ext_ref[HALO - 1:HALO - 1 + tm, :]
    nxt = ext_ref[HALO + 1:HALO + 1 + tm, :]
    xx = 0.5 * (prev + nxt) - a
    x_r, x_w, x_k, x_v, x_a, x_g = [(a + xx * mix_ref[q:q + 1, :]).astype(BF16) for q in range(6)]
    bd = bd_ref[...]
    r = jnp.dot(x_r, wr_ref[...], preferred_element_type=F32)
    k = jnp.dot(x_k, wk_ref[...], preferred_element_type=F32)
    v = jnp.dot(x_v, wv_ref[...], preferred_element_type=F32)
    gl = jnp.dot(x_g, g1_ref[...], preferred_element_type=F32)
    g = jnp.dot(jax.nn.sigmoid(gl).astype(BF16), g2_ref[...], preferred_element_type=F32)
    kk = k * kk_ref[...]
    kn = kk / jnp.maximum(jnp.sqrt(_seg_sum(kk * kk, bd)), 1e-12)
    tw = jnp.tanh(jnp.dot(x_w, w1_ref[...], preferred_element_type=F32))
    aa = jnp.dot(x_a, a1_ref[...], preferred_element_type=F32)
    lane = lax.broadcasted_iota(jnp.int32, (1, tw.shape[1]), 1)
    r_out[...] = r
    v_out[...] = v
    kn_out[...] = kn
    g_out[...] = g
    rk = rk_ref[...]
    bonus = jnp.zeros_like(v)
    outs = ((lw0_out, k0_out, b0_out), (lw1_out, k1_out, b1_out))
    half = tw.shape[1] // 2
    for d in range(2):
        sel = (lane >= d * half) & (lane < (d + 1) * half)
        wl = w0_ref[d:d + 1, :] + jnp.dot(jnp.where(sel, tw, 0.0).astype(BF16), w2_ref[...], preferred_element_type=F32)
        z = -wl
        w_log = -(jnp.maximum(z, 0.0) + jnp.log(1.0 + jnp.exp(-jnp.abs(z)))) - 0.5
        asig = jax.nn.sigmoid(a0_ref[d:d + 1, :] + jnp.dot(jnp.where(sel, aa, 0.0).astype(BF16), a2_ref[...],
                                                           preferred_element_type=F32))
        k_d = k * (1.0 + (asig - 1.0) * ka_ref[...])
        lw_o, k_o, b_o = outs[d]
        lw_o[...] = -jnp.exp(w_log)
        k_o[...] = k_d
        b_o[...] = kn * asig
        bonus = bonus + _seg_sum(r * k_d * rk, bd) * v
    bonus_out[...] = bonus


def _block_ones():
    idx = np.arange(RW_GROUP) // 64
    return jnp.asarray(idx[:, None] == idx[None, :], dtype=BF16)


def _rwkv_proj(lay, h, mod, g, p):
    prev, nxt = lay.halo_specs()
    cat = lambda w: jnp.concatenate([w[0], w[1]], axis=-1).astype(BF16)
    stack = lambda w: jnp.concatenate([w[0], w[1]], axis=0).astype(BF16)
    args = [h, h, h, mod, g.reshape(1, D), p['mix'],
            p['w_rkv'][0].astype(BF16), p['w_rkv'][1].astype(BF16), p['w_rkv'][2].astype(BF16),
            p['g1'].astype(BF16), p['g2'].astype(BF16),
            cat(p['w1']), stack(p['w2']), p['w0'], cat(p['a1']), stack(p['a2']), p['a0'],
            p['k_k'].reshape(1, D), p['k_a'].reshape(1, D), p['r_k'].reshape(1, D), _block_ones()]
    in_specs = [lay.row_spec(), prev, nxt, lay.mod_spec()] + [_const_spec(x.shape) for x in args[4:]]
    n_out = 11
    return pl.pallas_call(
        functools.partial(_rwkv_proj_kernel, lay=lay),
        grid=(lay.n_t,),
        in_specs=in_specs,
        out_specs=[lay.row_spec()] * n_out,
        out_shape=[jax.ShapeDtypeStruct((lay.N, D), F32)] * n_out,
        scratch_shapes=[pltpu.VMEM((lay.tm + 2 * HALO, D), F32)],
        compiler_params=_cparams(("parallel",)),
        name="rwkv_proj",
    )(*args)


def _rwkv_scan_kernel(rf, vf, nf, lwf, kf, bf, rb, vb, nb, lwb, kb, bb, yf_out, yb_out, z_ref):
    C, G = RW_CHUNK, RW_GROUP
    c = pl.program_id(1)

    @pl.when(c == 0)
    def _():
        z_ref[...] = jnp.zeros_like(z_ref)

    trow = lax.broadcasted_iota(jnp.int32, (C, C), 0)
    tcol = lax.broadcasted_iota(jnp.int32, (C, C), 1)
    ti = lax.broadcasted_iota(jnp.int32, (C, G), 0)
    ii = lax.broadcasted_iota(jnp.int32, (C, G), 1) % C
    brow = lax.broadcasted_iota(jnp.int32, (G, G), 0)
    bcol = lax.broadcasted_iota(jnp.int32, (G, G), 1)
    m_bd_t = (brow // C) == (bcol // 64)
    m_bd_f = (brow // 64) == (bcol // 64)
    eye_g = brow == bcol

    def bd(x):
        return jnp.where(m_bd_t, jnp.concatenate([x] * (G // C), axis=0), 0.0).astype(BF16)

    dirs = ((rf, vf, nf, lwf, kf, bf, yf_out, False), (rb, vb, nb, lwb, kb, bb, yb_out, True))
    for d, (r_ref, v_ref, n_ref, lw_ref, k_ref, b_ref, y_out, rev) in enumerate(dirs):
        tri = (tcol >= trow) if rev else (tcol <= trow)
        strict = (ii > ti) if rev else (ii < ti)
        incl = (ii >= ti) if rev else (ii <= ti)
        lw = lw_ref[...]
        lc = jnp.dot(tri.astype(F32), lw, preferred_element_type=F32, precision=lax.Precision.HIGHEST)
        ltot = lc[0:1, :] if rev else lc[C - 1:C, :]
        e_lc = jnp.exp(lc)
        ie_lc = jnp.exp(-lc)
        e_le = jnp.exp(lc - lw)
        e_bar = jnp.exp(ltot - lc)
        e_tot = jnp.exp(ltot)
        kvec, bvec = k_ref[...], b_ref[...]
        a_t = -n_ref[...] * e_le
        r_t = r_ref[...] * e_lc
        b_t = bvec * ie_lc
        k_t = kvec * ie_lc
        b_b = bvec * e_bar
        k_b = kvec * e_bar
        vall = v_ref[...]
        for g in range(D // G):
            sl = slice(g * G, (g + 1) * G)
            ar = jnp.concatenate([a_t[:, sl], r_t[:, sl]], axis=0).astype(BF16)
            bk = jnp.concatenate([bd(b_t[:, sl]), bd(k_t[:, sl])], axis=0)
            aa = lax.dot_general(ar, bk, (((1,), (1,)), ((), ())), preferred_element_type=F32)
            a_ab = jnp.where(strict, aa[:C, :G], 0.0)
            a_ak = jnp.where(strict, aa[:C, G:], 0.0)
            m_rb = jnp.where(incl, aa[C:, :G], 0.0)
            m_rk = jnp.where(incl, aa[C:, G:], 0.0)
            x = jnp.where(ii == ti, 1.0, 0.0) + a_ab
            p = jnp.dot(a_ab.astype(BF16), bd(a_ab), preferred_element_type=F32)
            n_sq = int(np.log2(C)) - 1
            for s in range(n_sq):
                last = s == n_sq - 1
                lhs = x if last else jnp.concatenate([x, p], axis=0)
                xp = jnp.dot(lhs.astype(BF16), bd(p), preferred_element_type=F32)
                x = x + xp[:C]
                if not last:
                    p = xp[C:]
            v = vall[:, sl]
            wv = jnp.dot(jnp.concatenate([a_ak, m_rk], axis=0).astype(BF16), bd(v), preferred_element_type=F32)
            z = z_ref[d, g]
            az = jnp.dot(ar, z.astype(BF16), preferred_element_type=F32)
            u = jnp.dot(x.astype(BF16), bd(az[:C] + wv[:C]), preferred_element_type=F32)
            y = az[C:] + wv[C:] + jnp.dot(m_rb.astype(BF16), bd(u), preferred_element_type=F32)
            y_out[:, sl] = y
            lhs_t = jnp.concatenate([b_b[:, sl], k_b[:, sl]], axis=0).astype(BF16)
            rhs_t = jnp.concatenate([u, v], axis=0).astype(BF16)
            zu = lax.dot_general(lhs_t, rhs_t, (((0,), (0,)), ((), ())), preferred_element_type=F32)
            decay = jnp.sum(jnp.where(eye_g, jnp.broadcast_to(e_tot[:, sl], (G, G)), 0.0), axis=1, keepdims=True)
            z_ref[d, g] = jnp.where(m_bd_f, decay * z + zu, 0.0)


def _rwkv_scan(lay, r, v, kn, lw0, k0, b0, lw1, k1, b1):
    B, T, L = lay.B, lay.T, lay.L
    C = RW_CHUNK
    cc, cx = L // C, T // C
    nc = cc + cx
    ctx0 = lay.rows_x // C

    def fwd(b, c):
        return (jnp.where(c < cc, ctx0 + b * cc + c, b * cx + c - cc), 0)

    def bwd(b, c):
        return (jnp.where(c < cc, ctx0 + b * cc + (cc - 1 - c), b * cx + (nc - 1 - c)), 0)

    sf, sb = pl.BlockSpec((C, D), fwd), pl.BlockSpec((C, D), bwd)
    return pl.pallas_call(
        _rwkv_scan_kernel,
        grid=(B, nc),
        in_specs=[sf] * 6 + [sb] * 6,
        out_specs=[sf, sb],
        out_shape=[jax.ShapeDtypeStruct((lay.N, D), F32)] * 2,
        scratch_shapes=[pltpu.VMEM((2, D // RW_GROUP, RW_GROUP, RW_GROUP), F32)],
        compiler_params=_cparams(("parallel", "arbitrary")),
        name="rwkv_scan",
    )(r, v, kn, lw0, k0, b0, r, v, kn, lw1, k1, b1)


def _rwkv_out_kernel(h_ref, yf_ref, yb_ref, bonus_ref, g_ref, mod_ref, lnw_ref, lnb_ref, wo_ref, bd_ref, o_ref):
    bd = bd_ref[...]
    y = yf_ref[...] + yb_ref[...]
    mu = _seg_sum(y, bd) * (1.0 / 64)
    yc = y - mu
    var = _seg_sum(yc * yc, bd) * (1.0 / 64)
    yn = yc * lax.rsqrt(var + GN_EPS) * lnw_ref[...] + lnb_ref[...]
    out = (yn + bonus_ref[...]) * g_ref[...]
    res = jnp.dot(out.astype(BF16), wo_ref[...], preferred_element_type=F32)
    o_ref[...] = h_ref[...] + mod_ref[0][:, 2 * D:3 * D] * res


def _rwkv_out(lay, h, yf, yb, bonus, g, mod, ln_w, ln_b, w_o):
    return pl.pallas_call(
        _rwkv_out_kernel,
        grid=(lay.n_t,),
        in_specs=[lay.row_spec()] * 5 + [lay.mod_spec(), _const_spec((1, D)), _const_spec((1, D)),
                                          _const_spec((D, D)), _const_spec((RW_GROUP, RW_GROUP))],
        out_specs=lay.row_spec(),
        out_shape=jax.ShapeDtypeStruct((lay.N, D), F32),
        compiler_params=_cparams(("parallel",)),
        name="rwkv_out",
    )(h, yf, yb, bonus, g, mod, ln_w.reshape(1, D), ln_b.reshape(1, D), w_o.astype(BF16), _block_ones())


def _router_kernel(h_ref, mod_ref, g_ref, wr_ref, br_ref, f_ref, idx_ref, gate_ref):
    m = mod_ref[0]
    f = _norm_mod(h_ref[...], g_ref[...], m[:, 3 * D:4 * D], m[:, 4 * D:5 * D])
    f_ref[...] = f
    logits = lax.dot_general(wr_ref[...], f, (((1,), (1,)), ((), ())), preferred_element_type=F32,
                             precision=lax.Precision.HIGHEST) + br_ref[...]
    erow = lax.broadcasted_iota(jnp.int32, logits.shape, 0)
    tops, idxs = [], []
    l = logits
    for _ in range(TOP_K):
        mx = jnp.max(l, axis=0, keepdims=True)
        ix = jnp.min(jnp.where(l == mx, erow, N_EXPERTS), axis=0, keepdims=True)
        tops.append(mx)
        idxs.append(ix)
        l = jnp.where(erow == ix, -jnp.inf, l)
    es = [jnp.exp(t - tops[0]) for t in tops]
    tot = es[0] + es[1] + es[2] + es[3]
    idx_ref[...] = jnp.concatenate(idxs, axis=0)
    gate_ref[...] = jnp.concatenate([e / tot for e in es], axis=0)


def _moe_router(lay, h, mod, g, w_router, b_router, n_rows):
    tm = lay.tm
    return pl.pallas_call(
        _router_kernel,
        grid=(n_rows // tm,),
        in_specs=[lay.row_spec(), lay.mod_spec(), _const_spec((1, D)),
                  _const_spec((N_EXPERTS, D)), _const_spec((N_EXPERTS, 1))],
        out_specs=[lay.row_spec(), pl.BlockSpec((TOP_K, tm), lambda i: (0, i)),
                   pl.BlockSpec((TOP_K, tm), lambda i: (0, i))],
        out_shape=[jax.ShapeDtypeStruct((n_rows, D), F32),
                   jax.ShapeDtypeStruct((TOP_K, n_rows), jnp.int32),
                   jax.ShapeDtypeStruct((TOP_K, n_rows), F32)],
        compiler_params=_cparams(("parallel",)),
        name="moe_router",
    )(h, mod, g.reshape(1, D), w_router.T, b_router.reshape(N_EXPERTS, 1))


def _moe_plan(idx, gates):
    n = idx.shape[1]
    nk = n * TOP_K
    e_flat = idx.T.reshape(nk)
    g_flat = gates.T.reshape(nk)
    onehot = (e_flat[:, None] == jnp.arange(N_EXPERTS, dtype=jnp.int32)[None, :]).astype(jnp.int32)
    csum = jnp.cumsum(onehot, axis=0)
    counts = csum[-1]
    rank = jnp.take_along_axis(csum, e_flat[:, None], axis=1)[:, 0] - 1
    padded = (counts + MOE_BLOCK - 1) // MOE_BLOCK * MOE_BLOCK
    pad_end = jnp.cumsum(padded)
    pad_start = pad_end - padded
    dest = (pad_start[e_flat] + rank).astype(jnp.int32)
    n_blocks = nk // MOE_BLOCK + N_EXPERTS
    rows = n_blocks * MOE_BLOCK
    row_tok = jnp.zeros((rows,), jnp.int32).at[dest].set(jnp.arange(nk, dtype=jnp.int32) // TOP_K)
    row_gate = jnp.zeros((rows,), F32).at[dest].set(g_flat)
    block_expert = jnp.minimum(jnp.searchsorted(pad_end, jnp.arange(n_blocks) * MOE_BLOCK, side='right'),
                               N_EXPERTS - 1).astype(jnp.int32)
    return dest, row_tok, row_gate, block_expert, n_blocks


def _gather_kernel(tok_ref, f_hbm, o_ref, sem):
    def issue(r, carry):
        pltpu.make_async_copy(f_hbm.at[pl.ds(tok_ref[0, 0, r], 1), :], o_ref.at[pl.ds(r, 1), :], sem).start()
        return carry

    lax.fori_loop(0, MOE_BLOCK, issue, 0)

    def drain(r, carry):
        pltpu.make_async_copy(f_hbm.at[pl.ds(0, 1), :], o_ref.at[pl.ds(r, 1), :], sem).wait()
        return carry

    lax.fori_loop(0, MOE_BLOCK, drain, 0)


def _moe_gather(f, row_tok, n_blocks):
    return pl.pallas_call(
        _gather_kernel,
        grid=(n_blocks,),
        in_specs=[pl.BlockSpec((1, 1, MOE_BLOCK), lambda i: (i, 0, 0), memory_space=pltpu.SMEM),
                  pl.BlockSpec(memory_space=pl.ANY)],
        out_specs=pl.BlockSpec((MOE_BLOCK, D), lambda i: (i, 0)),
        out_shape=jax.ShapeDtypeStruct((n_blocks * MOE_BLOCK, D), F32),
        scratch_shapes=[pltpu.SemaphoreType.DMA(())],
        compiler_params=_cparams(("arbitrary",)),
        name="moe_gather",
    )(row_tok.reshape(n_blocks, 1, MOE_BLOCK), f)


def _expert_kernel(be_ref, x_ref, gate_ref, w1_ref, b1_ref, w2_ref, b2_ref, o_ref):
    del be_ref
    hcat = jnp.dot(x_ref[...].astype(BF16), w1_ref[0], preferred_element_type=F32) + b1_ref[0]
    h_glu = jnp.minimum(hcat[:, :D_FF], SWIGLU_LIMIT)
    h_lin = jnp.clip(hcat[:, D_FF:], -SWIGLU_LIMIT, SWIGLU_LIMIT)
    act = h_glu * jax.nn.sigmoid(SWIGLU_ALPHA * h_glu) * (h_lin + 1.0)
    y = jnp.dot(act.astype(BF16), w2_ref[0], preferred_element_type=F32) + b2_ref[0]
    o_ref[...] = y * gate_ref[...]


def _moe_experts(xs, row_gate, block_expert, n_blocks, w1, b1, w2, b2):
    grid_spec = pltpu.PrefetchScalarGridSpec(
        num_scalar_prefetch=1,
        grid=(n_blocks,),
        in_specs=[pl.BlockSpec((MOE_BLOCK, D), lambda i, be: (i, 0)),
                  pl.BlockSpec((MOE_BLOCK, 1), lambda i, be: (i, 0)),
                  pl.BlockSpec((1, D, 2 * D_FF), lambda i, be: (be[i], 0, 0)),
                  pl.BlockSpec((1, 1, 2 * D_FF), lambda i, be: (be[i], 0, 0)),
                  pl.BlockSpec((1, D_FF, D), lambda i, be: (be[i], 0, 0)),
                  pl.BlockSpec((1, 1, D), lambda i, be: (be[i], 0, 0))],
        out_specs=pl.BlockSpec((MOE_BLOCK, D), lambda i, be: (i, 0)),
    )
    return pl.pallas_call(
        _expert_kernel,
        grid_spec=grid_spec,
        out_shape=jax.ShapeDtypeStruct((n_blocks * MOE_BLOCK, D), F32),
        compiler_params=_cparams(("arbitrary",)),
        name="moe_experts",
    )(block_expert, xs, row_gate.reshape(-1, 1), w1.astype(BF16), b1.reshape(N_EXPERTS, 1, 2 * D_FF),
      w2.astype(BF16), b2.reshape(N_EXPERTS, 1, D))


def _combine_kernel(dest_ref, h_ref, mod_ref, y_hbm, o_ref, buf, sem, *, tm):
    def issue(r, carry):
        for s in range(TOP_K):
            pltpu.make_async_copy(y_hbm.at[pl.ds(dest_ref[0, 0, r * TOP_K + s], 1), :],
                                  buf.at[s, pl.ds(r, 1), :], sem).start()
        return carry

    lax.fori_loop(0, tm, issue, 0)

    def drain(r, carry):
        for s in range(TOP_K):
            pltpu.make_async_copy(y_hbm.at[pl.ds(0, 1), :], buf.at[s, pl.ds(r, 1), :], sem).wait()
        return carry

    lax.fori_loop(0, tm, drain, 0)
    y = (buf[0] + buf[1]) + (buf[2] + buf[3])
    o_ref[...] = h_ref[...] + mod_ref[0][:, 5 * D:6 * D] * y


def _moe_combine(lay, h, mod, yb, dest, n_rows):
    tm = lay.tm
    return pl.pallas_call(
        functools.partial(_combine_kernel, tm=tm),
        grid=(n_rows // tm,),
        in_specs=[pl.BlockSpec((1, 1, tm * TOP_K), lambda i: (i, 0, 0), memory_space=pltpu.SMEM),
                  lay.row_spec(), lay.mod_spec(), pl.BlockSpec(memory_space=pl.ANY)],
        out_specs=lay.row_spec(),
        out_shape=jax.ShapeDtypeStruct((n_rows, D), F32),
        scratch_shapes=[pltpu.VMEM((TOP_K, tm, D), F32), pltpu.SemaphoreType.DMA(())],
        compiler_params=_cparams(("arbitrary",)),
        name="moe_combine",
    )(dest.reshape(n_rows // tm, 1, tm * TOP_K), h, mod, yb)


def _moe_layer(lay, h, mod, g, w_router, b_router, w1, b1, w2, b2, n_rows):
    f, idx, gates = _moe_router(lay, h, mod, g, w_router, b_router, n_rows)
    dest, row_tok, row_gate, block_expert, n_blocks = _moe_plan(idx, gates)
    xs = _moe_gather(f, row_tok, n_blocks)
    yb = _moe_experts(xs, row_gate, block_expert, n_blocks, w1, b1, w2, b2)
    return _moe_combine(lay, h, mod, yb, dest, n_rows)


def _final_kernel(h_ref, g_ref, o_ref):
    h = h_ref[...]
    ms = jnp.mean(h * h, axis=-1, keepdims=True)
    o_ref[...] = h * lax.rsqrt(ms + NORM_EPS) * g_ref[...]


def _final_norm(lay, h, g, n_rows):
    return pl.pallas_call(
        _final_kernel,
        grid=(n_rows // lay.tm,),
        in_specs=[lay.row_spec(), _const_spec((1, D))],
        out_specs=lay.row_spec(),
        out_shape=jax.ShapeDtypeStruct((n_rows, D), F32),
        compiler_params=_cparams(("parallel",)),
        name="final_norm",
    )(h, g.reshape(1, D))


def kernel(x, c, ctx, c_ctx, ada_w, ada_b, norm_g, attn_w_qkv, attn_b_qkv, attn_w_o, attn_b_o, attn_sinks, pool_w, pool_b, pool_scale, rwkv_mix, rwkv_w_rkv, rwkv_w0, rwkv_w1, rwkv_w2, rwkv_a0, rwkv_a1, rwkv_a2, rwkv_g1, rwkv_g2, rwkv_k_k, rwkv_k_a, rwkv_r_k, rwkv_ln_w, rwkv_ln_b, rwkv_w_o, moe_w_router, moe_b_router, moe_w1, moe_b1, moe_w2, moe_b2, final_g):
    B, T, _ = x.shape
    L = ctx.shape[1]
    depth = ada_w.shape[0]
    lay = _Layout(B, T, L)
    h = jnp.concatenate([x.reshape(B * T, D), ctx.reshape(B * L, D)], axis=0)

    n_c = B + 1
    n_c_pad = -(-n_c // 8) * 8
    c_all = jnp.concatenate([c, c_ctx[None, :], jnp.zeros((n_c_pad - n_c, D), F32)], axis=0)
    mods = _ada_mods(c_all, ada_w, ada_b)
    cos, sin = _rope_tables(T, L)

    for layer in range(depth):
        last = layer == depth - 1
        kind, j = layer % 3, layer // 3
        mod = mods[layer].reshape(n_c_pad, 1, 6 * D)
        n_rows = lay.rows_x if last else lay.N
        if kind == 0:
            q, k, v = _attn_qkv(lay, h, mod, norm_g[layer, 0], attn_w_qkv[j], attn_b_qkv[j], cos, sin)
            o = _attention(lay, q, k, v, attn_sinks[j], not last)
            h = _linear_residual(lay, h, o, mod, attn_w_o[j], attn_b_o[j], 2, n_rows)
        elif kind == 1:
            h = _pool_mixer(lay, h, mod, norm_g[layer, 0], pool_w[j], pool_b[j], pool_scale[j], n_rows)
        else:
            p = dict(mix=rwkv_mix[j], w_rkv=rwkv_w_rkv[j], w0=rwkv_w0[j], w1=rwkv_w1[j], w2=rwkv_w2[j],
                     a0=rwkv_a0[j], a1=rwkv_a1[j], a2=rwkv_a2[j], g1=rwkv_g1[j], g2=rwkv_g2[j],
                     k_k=rwkv_k_k[j], k_a=rwkv_k_a[j], r_k=rwkv_r_k[j])
            r, v, kn, g, bonus, lw0, k0, b0, lw1, k1, b1 = _rwkv_proj(lay, h, mod, norm_g[layer, 0], p)
            yf, yb = _rwkv_scan(lay, r, v, kn, lw0, k0, b0, lw1, k1, b1)
            h = _rwkv_out(lay, h, yf, yb, bonus, g, mod, rwkv_ln_w[j], rwkv_ln_b[j], rwkv_w_o[j])
        h = _moe_layer(lay, h, mod, norm_g[layer, 1], moe_w_router[layer], moe_b_router[layer],
                       moe_w1[layer], moe_b1[layer], moe_w2[layer], moe_b2[layer], n_rows)
    out = _final_norm(lay, h, final_g, lay.rows_x)
    return out.reshape(B, T, D)
```

```python
import functools

import jax
import jax.numpy as jnp
import numpy as np
from jax import lax
from jax.experimental import pallas as pl
from jax.experimental.pallas import tpu as pltpu

F32 = jnp.float32
BF16 = jnp.bfloat16

D = 1024
NORM_EPS = 1e-5
GRID_W = 64
HEAD_DIM = 64
N_HEADS = 16
N_KV = 4
Q_DIM = 1024
KV_DIM = 256
QK_DIM = Q_DIM + KV_DIM
WINDOW = 128
QB = 128
ROPE_BASE = 10000.0
POOL_SIZES = (2, 4, 8, 16)
POOL_GW = 256
HALO = 8
GN_EPS = 64e-5
N_EXPERTS = 32
TOP_K = 4
D_FF = 1024
SWIGLU_LIMIT = 7.0
SWIGLU_ALPHA = 1.702
MOE_BLOCK = 256
RW_CHUNK = 64
RW_GROUP = 256
VMEM_LIMIT = 56 * 1024 * 1024


def _cparams(sem):
    return pltpu.CompilerParams(dimension_semantics=sem, vmem_limit_bytes=VMEM_LIMIT)


def _norm_mod(h, g, shift, scale):
    ms = jnp.mean(h * h, axis=-1, keepdims=True)
    y = h * lax.rsqrt(ms + NORM_EPS) * g
    return y * (1.0 + scale) + shift


class _Layout:
    def __init__(self, B, T, L):
        self.B, self.T, self.L = B, T, L
        self.tm = 256 if L % 256 == 0 else 128
        self.rows_x, self.rows_c = B * T, B * L
        self.N = self.rows_x + self.rows_c
        self.tx, self.tc = T // self.tm, L // self.tm
        self.n_xt = self.rows_x // self.tm
        self.n_t = self.N // self.tm

    def mod_idx(self, i):
        return jnp.where(i < self.n_xt, i // self.tx, self.B)

    def mod_spec(self):
        return pl.BlockSpec((1, 1, 6 * D), lambda i: (self.mod_idx(i), 0, 0))

    def row_spec(self, width=D):
        return pl.BlockSpec((self.tm, width), lambda i: (i, 0))

    def seq_tile(self, i):
        j = jnp.where(i < self.n_xt, i % self.tx, (i - self.n_xt) % self.tc)
        n = jnp.where(i < self.n_xt, self.tx, self.tc)
        return j, n

    def halo_specs(self):
        hb = self.tm // HALO
        last = self.N // HALO - 1
        prev = pl.BlockSpec((HALO, D), lambda i: (jnp.maximum(i * hb - 1, 0), 0))
        nxt = pl.BlockSpec((HALO, D), lambda i: (jnp.minimum((i + 1) * hb, last), 0))
        return prev, nxt


def _const_spec(shape):
    nd = len(shape)
    return pl.BlockSpec(shape, lambda *_: (0,) * nd)


def _ada_kernel(c_ref, w_ref, b_ref, o_ref):
    c = c_ref[...]
    s = c * jax.nn.sigmoid(c)
    o_ref[0] = jnp.dot(s, w_ref[0], preferred_element_type=F32, precision=lax.Precision.HIGHEST) + b_ref[0]


def _ada_mods(c_all, ada_w, ada_b):
    depth = ada_w.shape[0]
    R = c_all.shape[0]
    nt = 1536
    return pl.pallas_call(
        _ada_kernel,
        grid=(depth, 6 * D // nt),
        in_specs=[pl.BlockSpec((R, D), lambda l, j: (0, 0)),
                  pl.BlockSpec((1, D, nt), lambda l, j: (l, 0, j)),
                  pl.BlockSpec((1, 1, nt), lambda l, j: (l, 0, j))],
        out_specs=pl.BlockSpec((1, R, nt), lambda l, j: (l, 0, j)),
        out_shape=jax.ShapeDtypeStruct((depth, R, 6 * D), F32),
        compiler_params=_cparams(("parallel", "parallel")),
        name="ada_mod",
    )(c_all, ada_w, ada_b.reshape(depth, 1, 6 * D))


def _qkv_kernel(h_ref, mod_ref, g_ref, w_ref, b_ref, cos_ref, sin_ref, q_ref, k_ref, v_ref):
    m = mod_ref[0]
    a = _norm_mod(h_ref[...], g_ref[...], m[:, 0:D], m[:, D:2 * D])
    qkv = jnp.dot(a.astype(BF16), w_ref[...], preferred_element_type=F32) + b_ref[...]
    qk = qkv[:, :QK_DIM]
    lane = lax.broadcasted_iota(jnp.int32, (1, QK_DIM), 1)
    first = (lane % 32) < 16
    rot = jnp.where(first, -pltpu.roll(qk, QK_DIM - 16, 1), pltpu.roll(qk, 16, 1))
    qk = qk * cos_ref[...] + rot * sin_ref[...]
    q_ref[...] = (qk[:, :Q_DIM] * (HEAD_DIM ** -0.5)).astype(BF16)
    k_ref[...] = qk[:, Q_DIM:].astype(BF16)
    v_ref[...] = qkv[:, QK_DIM:].astype(BF16)


def _rope_tables(T, L):
    rows = T // GRID_W
    n_freq = HEAD_DIM // 4
    inv = ROPE_BASE ** (-jnp.arange(n_freq, dtype=F32) / n_freq)
    row_ang = jnp.arange(rows, dtype=F32)[:, None] * inv
    col_ang = jnp.arange(GRID_W, dtype=F32)[:, None] * inv
    ang_r = jnp.broadcast_to(row_ang[:, None, :], (rows, GRID_W, n_freq)).reshape(T, n_freq)
    ang_c = jnp.broadcast_to(col_ang[None, :, :], (rows, GRID_W, n_freq)).reshape(T, n_freq)
    ang = jnp.concatenate([ang_r, ang_r, ang_c, ang_c], axis=-1)
    cos = jnp.concatenate([jnp.cos(ang), jnp.ones((L, HEAD_DIM), F32)], axis=0)
    sin = jnp.concatenate([jnp.sin(ang), jnp.zeros((L, HEAD_DIM), F32)], axis=0)
    reps = QK_DIM // HEAD_DIM
    return jnp.tile(cos, (1, reps)), jnp.tile(sin, (1, reps))


def _attn_qkv(lay, h, mod, g, w_qkv, b_qkv, cos, sin):
    tm = lay.tm

    def tab_idx(i):
        return (jnp.where(i < lay.n_xt, i % lay.tx, lay.tx + (i - lay.n_xt) % lay.tc), 0)

    return pl.pallas_call(
        _qkv_kernel,
        grid=(lay.n_t,),
        in_specs=[lay.row_spec(), lay.mod_spec(), _const_spec((1, D)),
                  _const_spec((D, Q_DIM + 2 * KV_DIM)), _const_spec((1, Q_DIM + 2 * KV_DIM)),
                  pl.BlockSpec((tm, QK_DIM), tab_idx), pl.BlockSpec((tm, QK_DIM), tab_idx)],
        out_specs=[lay.row_spec(Q_DIM), lay.row_spec(KV_DIM), lay.row_spec(KV_DIM)],
        out_shape=[jax.ShapeDtypeStruct((lay.N, Q_DIM), BF16),
                   jax.ShapeDtypeStruct((lay.N, KV_DIM), BF16),
                   jax.ShapeDtypeStruct((lay.N, KV_DIM), BF16)],
        compiler_params=_cparams(("parallel",)),
        name="attn_qkv",
    )(h, mod, g.reshape(1, D), w_qkv.astype(BF16), b_qkv.reshape(1, -1), cos, sin)


def _attn_heads(q, kk, vv, mask, sink_ref, o_ref):
    R = q.shape[0]
    G = N_HEADS // N_KV
    for g in range(N_KV):
        kg = kk[:, g * HEAD_DIM:(g + 1) * HEAD_DIM]
        vg = vv[:, g * HEAD_DIM:(g + 1) * HEAD_DIM]
        qs = jnp.concatenate([q[:, (g * G + j) * HEAD_DIM:(g * G + j + 1) * HEAD_DIM] for j in range(G)], axis=0)
        s = lax.dot_general(qs, kg, (((1,), (1,)), ((), ())), preferred_element_type=F32)
        if mask is not None:
            s = jnp.where(jnp.concatenate([mask] * G, axis=0), s, -jnp.inf)
        sink = jnp.concatenate([jnp.full((R, 1), sink_ref[g * G + j], F32) for j in range(G)], axis=0)
        m = jnp.maximum(jnp.max(s, axis=-1, keepdims=True), sink)
        p = jnp.exp(s - m)
        denom = jnp.sum(p, axis=-1, keepdims=True) + jnp.exp(sink - m)
        o = jnp.dot(p.astype(BF16), vg, preferred_element_type=F32) / denom
        for j in range(G):
            hq = g * G + j
            o_ref[:, hq * HEAD_DIM:(hq + 1) * HEAD_DIM] = o[j * R:(j + 1) * R].astype(o_ref.dtype)


def _attn_kernel(sink_ref, q_ref, kp_ref, kc_ref, kn_ref, vp_ref, vc_ref, vn_ref, kx_ref, vx_ref, o_ref, *, T, L):
    i = pl.program_id(1)
    nqb = T // QB

    @pl.when(i < nqb)
    def _():
        kk = jnp.concatenate([kx_ref[...], kp_ref[...], kc_ref[...], kn_ref[...]], axis=0)
        vv = jnp.concatenate([vx_ref[...], vp_ref[...], vc_ref[...], vn_ref[...]], axis=0)
        S = L + 3 * QB
        col = lax.broadcasted_iota(jnp.int32, (QB, S), 1)
        row = lax.broadcasted_iota(jnp.int32, (QB, S), 0)
        rel = col - L - QB - row
        kpos = i * QB - QB + (col - L)
        local_ok = (jnp.abs(rel) <= WINDOW) & (kpos >= 0) & (kpos < T)
        mask = (col < L) | local_ok
        _attn_heads(q_ref[...], kk, vv, mask, sink_ref, o_ref)

    @pl.when(i >= nqb)
    def _():
        _attn_heads(q_ref[...], kx_ref[...], vx_ref[...], None, sink_ref, o_ref)


def _attention(lay, q, k, v, sinks, need_ctx):
    B, T, L = lay.B, lay.T, lay.L
    nqb, ncb = T // QB, L // QB
    cq = lay.rows_x // QB
    cb = lay.rows_x // L
    smem = pl.BlockSpec(memory_space=pltpu.SMEM)

    def cur(b, i):
        return (jnp.where(i < nqb, b * nqb + i, cq + b * ncb + (i - nqb)), 0)

    def prev(b, i):
        return (b * nqb + jnp.clip(i - 1, 0, nqb - 1), 0)

    def mid(b, i):
        return (b * nqb + jnp.minimum(i, nqb - 1), 0)

    def nxt(b, i):
        return (b * nqb + jnp.minimum(i + 1, nqb - 1), 0)

    def ctxb(b, i):
        return (cb + b, 0)

    kv_specs = [pl.BlockSpec((QB, KV_DIM), prev), pl.BlockSpec((QB, KV_DIM), mid), pl.BlockSpec((QB, KV_DIM), nxt)]
    n_out = lay.N if need_ctx else lay.rows_x
    return pl.pallas_call(
        functools.partial(_attn_kernel, T=T, L=L),
        grid=(B, nqb + (ncb if need_ctx else 0)),
        in_specs=[smem, pl.BlockSpec((QB, Q_DIM), cur)] + kv_specs + kv_specs
                 + [pl.BlockSpec((L, KV_DIM), ctxb), pl.BlockSpec((L, KV_DIM), ctxb)],
        out_specs=pl.BlockSpec((QB, Q_DIM), cur),
        out_shape=jax.ShapeDtypeStruct((n_out, Q_DIM), BF16),
        compiler_params=_cparams(("parallel", "arbitrary")),
        name="attention",
    )(sinks, q, k, k, k, v, v, v, k, v)


def _linres_kernel(h_ref, x_ref, mod_ref, w_ref, b_ref, o_ref, *, gate_slot):
    gate = mod_ref[0][:, gate_slot * D:(gate_slot + 1) * D]
    y = jnp.dot(x_ref[...], w_ref[...], preferred_element_type=F32) + b_ref[...]
    o_ref[...] = h_ref[...] + gate * y


def _linear_residual(lay, h, x, mod, w, b, gate_slot, n_rows):
    return pl.pallas_call(
        functools.partial(_linres_kernel, gate_slot=gate_slot),
        grid=(n_rows // lay.tm,),
        in_specs=[lay.row_spec(), lay.row_spec(x.shape[1]), lay.mod_spec(),
                  _const_spec(w.shape), _const_spec((1, D))],
        out_specs=lay.row_spec(),
        out_shape=jax.ShapeDtypeStruct((n_rows, D), F32),
        compiler_params=_cparams(("parallel",)),
        name="linear_residual",
    )(h, x, mod, w.astype(BF16), b.reshape(1, D))


def _fill_ext(ext_ref, h_ref, hp_ref, hn_ref, g, shift, scale, j, n, tm):
    a = _norm_mod(h_ref[...], g, shift, scale)
    ap = _norm_mod(hp_ref[...], g, shift, scale)
    an = _norm_mod(hn_ref[...], g, shift, scale)
    ext_ref[0:HALO, :] = jnp.where(j == 0, 0.0, ap)
    ext_ref[HALO:HALO + tm, :] = a
    ext_ref[HALO + tm:2 * HALO + tm, :] = jnp.where(j == n - 1, 0.0, an)
    return a


def _pool_kernel(h_ref, hp_ref, hn_ref, mod_ref, g_ref, w_ref, b_ref, ls_ref, o_ref, ext_ref, *, lay):
    tm = lay.tm
    i = pl.program_id(0)
    j, n = lay.seq_tile(i)
    m = mod_ref[0]
    a = _fill_ext(ext_ref, h_ref, hp_ref, hn_ref, g_ref[...], m[:, 0:D], m[:, D:2 * D], j, n, tm)
    t = j * tm + lax.broadcasted_iota(jnp.int32, (tm, 1), 0)
    n_tok = n * tm
    outs = []
    for gi, size in enumerate(POOL_SIZES):
        sl = slice(gi * POOL_GW, (gi + 1) * POOL_GW)
        acc = ext_ref[HALO - size // 2:HALO - size // 2 + tm, sl]
        for o in range(-size // 2 + 1, size - size // 2):
            acc = acc + ext_ref[HALO + o:HALO + o + tm, sl]
        lo = jnp.maximum(t - size // 2, 0)
        hi = jnp.minimum(t + size - size // 2, n_tok)
        d = acc / (hi - lo).astype(F32) - a[:, sl]
        outs.append(jnp.dot(d.astype(BF16), w_ref[gi], preferred_element_type=F32) + b_ref[gi])
    y = jnp.concatenate(outs, axis=-1) * ls_ref[...]
    o_ref[...] = h_ref[...] + m[:, 2 * D:3 * D] * y


def _pool_mixer(lay, h, mod, g, w, b, ls, n_rows):
    prev, nxt = lay.halo_specs()
    return pl.pallas_call(
        functools.partial(_pool_kernel, lay=lay),
        grid=(n_rows // lay.tm,),
        in_specs=[lay.row_spec(), prev, nxt, lay.mod_spec(), _const_spec((1, D)),
                  _const_spec(w.shape), _const_spec((4, 1, POOL_GW)), _const_spec((1, D))],
        out_specs=lay.row_spec(),
        out_shape=jax.ShapeDtypeStruct((n_rows, D), F32),
        scratch_shapes=[pltpu.VMEM((lay.tm + 2 * HALO, D), F32)],
        compiler_params=_cparams(("parallel",)),
        name="pool_mixer",
    )(h, h, h, mod, g.reshape(1, D), w.astype(BF16), b.reshape(4, 1, POOL_GW), ls.reshape(1, D))


def _seg_sum(x, bd):
    outs = []
    for q in range(D // RW_GROUP):
        xg = x[:, q * RW_GROUP:(q + 1) * RW_GROUP]
        hi = xg.astype(BF16)
        lo = (xg - hi.astype(F32)).astype(BF16)
        outs.append(jnp.dot(hi, bd, preferred_element_type=F32) + jnp.dot(lo, bd, preferred_element_type=F32))
    return jnp.concatenate(outs, axis=-1)


def _rwkv_proj_kernel(h_ref, hp_ref, hn_ref, mod_ref, g_ref, mix_ref, wr_ref, wk_ref, wv_ref, g1_ref, g2_ref,
                      w1_ref, w2_ref, w0_ref, a1_ref, a2_ref, a0_ref, kk_ref, ka_ref, rk_ref, bd_ref,
                      r_out, v_out, kn_out, g_out, bonus_out, lw0_out, k0_out, b0_out, lw1_out, k1_out, b1_out,
                      ext_ref, *, lay):
    tm = lay.tm
    i = pl.program_id(0)
    j, n = lay.seq_tile(i)
    m = mod_ref[0]
    a = _fill_ext(ext_ref, h_ref, hp_ref, hn_ref, g_ref[...], m[:, 0:D], m[:, D:2 * D], j, n, tm)
    prev = ext_ref[HALO - 1:HALO - 1 + tm, :]
    nxt = ext_ref[HALO + 1:HALO + 1 + tm, :]
    xx = 0.5 * (prev + nxt) - a
    x_r, x_w, x_k, x_v, x_a, x_g = [(a + xx * mix_ref[q:q + 1, :]).astype(BF16) for q in range(6)]
    bd = bd_ref[...]
    r = jnp.dot(x_r, wr_ref[...], preferred_element_type=F32)
    k = jnp.dot(x_k, wk_ref[...], preferred_element_type=F32)
    v = jnp.dot(x_v, wv_ref[...], preferred_element_type=F32)
    gl = jnp.dot(x_g, g1_ref[...], preferred_element_type=F32)
    g = jnp.dot(jax.nn.sigmoid(gl).astype(BF16), g2_ref[...], preferred_element_type=F32)
    kk = k * kk_ref[...]
    kn = kk / jnp.maximum(jnp.sqrt(_seg_sum(kk * kk, bd)), 1e-12)
    tw = jnp.tanh(jnp.dot(x_w, w1_ref[...], preferred_element_type=F32))
    aa = jnp.dot(x_a, a1_ref[...], preferred_element_type=F32)
    lane = lax.broadcasted_iota(jnp.int32, (1, tw.shape[1]), 1)
    r_out[...] = r
    v_out[...] = v
    kn_out[...] = kn
    g_out[...] = g
    rk = rk_ref[...]
    bonus = jnp.zeros_like(v)
    outs = ((lw0_out, k0_out, b0_out), (lw1_out, k1_out, b1_out))
    half = tw.shape[1] // 2
    for d in range(2):
        sel = (lane >= d * half) & (lane < (d + 1) * half)
        wl = w0_ref[d:d + 1, :] + jnp.dot(jnp.where(sel, tw, 0.0).astype(BF16), w2_ref[...], preferred_element_type=F32)
        z = -wl
        w_log = -(jnp.maximum(z, 0.0) + jnp.log(1.0 + jnp.exp(-jnp.abs(z)))) - 0.5
        asig = jax.nn.sigmoid(a0_ref[d:d + 1, :] + jnp.dot(jnp.where(sel, aa, 0.0).astype(BF16), a2_ref[...],
                                                           preferred_element_type=F32))
        k_d = k * (1.0 + (asig - 1.0) * ka_ref[...])
        lw_o, k_o, b_o = outs[d]
        lw_o[...] = -jnp.exp(w_log)
        k_o[...] = k_d
        b_o[...] = kn * asig
        bonus = bonus + _seg_sum(r * k_d * rk, bd) * v
    bonus_out[...] = bonus


def _block_ones():
    idx = np.arange(RW_GROUP) // 64
    return jnp.asarray(idx[:, None] == idx[None, :], dtype=BF16)


def _rwkv_proj(lay, h, mod, g, p):
    prev, nxt = lay.halo_specs()
    cat = lambda w: jnp.concatenate([w[0], w[1]], axis=-1).astype(BF16)
    stack = lambda w: jnp.concatenate([w[0], w[1]], axis=0).astype(BF16)
    args = [h, h, h, mod, g.reshape(1, D), p['mix'],
            p['w_rkv'][0].astype(BF16), p['w_rkv'][1].astype(BF16), p['w_rkv'][2].astype(BF16),
            p['g1'].astype(BF16), p['g2'].astype(BF16),
            cat(p['w1']), stack(p['w2']), p['w0'], cat(p['a1']), stack(p['a2']), p['a0'],
            p['k_k'].reshape(1, D), p['k_a'].reshape(1, D), p['r_k'].reshape(1, D), _block_ones()]
    in_specs = [lay.row_spec(), prev, nxt, lay.mod_spec()] + [_const_spec(x.shape) for x in args[4:]]
    n_out = 11
    return pl.pallas_call(
        functools.partial(_rwkv_proj_kernel, lay=lay),
        grid=(lay.n_t,),
        in_specs=in_specs,
        out_specs=[lay.row_spec()] * n_out,
        out_shape=[jax.ShapeDtypeStruct((lay.N, D), F32)] * n_out,
        scratch_shapes=[pltpu.VMEM((lay.tm + 2 * HALO, D), F32)],
        compiler_params=_cparams(("parallel",)),
        name="rwkv_proj",
    )(*args)


def _rwkv_scan_kernel(rf, vf, nf, lwf, kf, bf, rb, vb, nb, lwb, kb, bb, yf_out, yb_out, z_ref):
    C, G = RW_CHUNK, RW_GROUP
    c = pl.program_id(1)

    @pl.when(c == 0)
    def _():
        z_ref[...] = jnp.zeros_like(z_ref)

    trow = lax.broadcasted_iota(jnp.int32, (C, C), 0)
    tcol = lax.broadcasted_iota(jnp.int32, (C, C), 1)
    ti = lax.broadcasted_iota(jnp.int32, (C, G), 0)
    ii = lax.broadcasted_iota(jnp.int32, (C, G), 1) % C
    brow = lax.broadcasted_iota(jnp.int32, (G, G), 0)
    bcol = lax.broadcasted_iota(jnp.int32, (G, G), 1)
    m_bd_t = (brow // C) == (bcol // 64)
    m_bd_f = (brow // 64) == (bcol // 64)
    eye_g = brow == bcol

    def bd(x):
        return jnp.where(m_bd_t, jnp.concatenate([x] * (G // C), axis=0), 0.0).astype(BF16)

    def mm(a, b):
        return jnp.dot(a, b, preferred_element_type=F32)

    dirs = ((rf, vf, nf, lwf, kf, bf, yf_out, False), (rb, vb, nb, lwb, kb, bb, yb_out, True))
    ch = []
    for d, (r_ref, v_ref, n_ref, lw_ref, k_ref, b_ref, y_out, rev) in enumerate(dirs):
        tri = (tcol >= trow) if rev else (tcol <= trow)
        strict = (ii > ti) if rev else (ii < ti)
        incl = (ii >= ti) if rev else (ii <= ti)
        lw = lw_ref[...]
        lc = jnp.dot(tri.astype(F32), lw, preferred_element_type=F32, precision=lax.Precision.HIGHEST)
        ltot = lc[0:1, :] if rev else lc[C - 1:C, :]
        e_lc = jnp.exp(lc)
        ie_lc = jnp.exp(-lc)
        e_le = jnp.exp(lc - lw)
        e_bar = jnp.exp(ltot - lc)
        e_tot = jnp.exp(ltot)
        kvec, bvec = k_ref[...], b_ref[...]
        a_t = -n_ref[...] * e_le
        r_t = r_ref[...] * e_lc
        b_t = bvec * ie_lc
        k_t = kvec * ie_lc
        b_b = bvec * e_bar
        k_b = kvec * e_bar
        vall = v_ref[...]
        for g in range(D // G):
            sl = slice(g * G, (g + 1) * G)
            ch.append(dict(
                d=d, g=g, sl=sl, y_out=y_out, strict=strict, incl=incl, v=vall[:, sl], e_tot=e_tot[:, sl],
                ar=jnp.concatenate([a_t[:, sl], r_t[:, sl]], axis=0).astype(BF16),
                bk=jnp.concatenate([bd(b_t[:, sl]), bd(k_t[:, sl])], axis=0),
                lhs_t=jnp.concatenate([b_b[:, sl], k_b[:, sl]], axis=0).astype(BF16)))

    for q in ch:
        aa = lax.dot_general(q['ar'], q['bk'], (((1,), (1,)), ((), ())), preferred_element_type=F32)
        q['a_ab'] = jnp.where(q['strict'], aa[:C, :G], 0.0)
        q['a_ak'] = jnp.where(q['strict'], aa[:C, G:], 0.0)
        q['m_rb'] = jnp.where(q['incl'], aa[C:, :G], 0.0).astype(BF16)
        q['m_rk'] = jnp.where(q['incl'], aa[C:, G:], 0.0)
    for q in ch:
        q['x'] = jnp.where(ii == ti, 1.0, 0.0) + q['a_ab']
        q['p'] = mm(q['a_ab'].astype(BF16), bd(q['a_ab']))
    for q in ch:
        q['wv'] = mm(jnp.concatenate([q['a_ak'], q['m_rk']], axis=0).astype(BF16), bd(q['v']))
        q['z'] = z_ref[q['d'], q['g']]
        q['az'] = mm(q['ar'], q['z'].astype(BF16))
    n_sq = int(np.log2(C)) - 1
    for s in range(n_sq):
        last = s == n_sq - 1
        for q in ch:
            lhs = q['x'] if last else jnp.concatenate([q['x'], q['p']], axis=0)
            xp = mm(lhs.astype(BF16), bd(q['p']))
            q['x'] = q['x'] + xp[:C]
            if not last:
                q['p'] = xp[C:]
    for q in ch:
        q['u'] = mm(q['x'].astype(BF16), bd(q['az'][:C] + q['wv'][:C]))
    for q in ch:
        y = q['az'][C:] + q['wv'][C:] + mm(q['m_rb'], bd(q['u']))
        q['y_out'][:, q['sl']] = y
        rhs_t = jnp.concatenate([q['u'], q['v']], axis=0).astype(BF16)
        zu = lax.dot_general(q['lhs_t'], rhs_t, (((0,), (0,)), ((), ())), preferred_element_type=F32)
        decay = jnp.sum(jnp.where(eye_g, jnp.broadcast_to(q['e_tot'], (G, G)), 0.0), axis=1, keepdims=True)
        z_ref[q['d'], q['g']] = jnp.where(m_bd_f, decay * q['z'] + zu, 0.0)


def _rwkv_scan(lay, r, v, kn, lw0, k0, b0, lw1, k1, b1):
    B, T, L = lay.B, lay.T, lay.L
    C = RW_CHUNK
    cc, cx = L // C, T // C
    nc = cc + cx
    ctx0 = lay.rows_x // C

    def fwd(b, c):
        return (jnp.where(c < cc, ctx0 + b * cc + c, b * cx + c - cc), 0)

    def bwd(b, c):
        return (jnp.where(c < cc, ctx0 + b * cc + (cc - 1 - c), b * cx + (nc - 1 - c)), 0)

    sf, sb = pl.BlockSpec((C, D), fwd), pl.BlockSpec((C, D), bwd)
    return pl.pallas_call(
        _rwkv_scan_kernel,
        grid=(B, nc),
        in_specs=[sf] * 6 + [sb] * 6,
        out_specs=[sf, sb],
        out_shape=[jax.ShapeDtypeStruct((lay.N, D), F32)] * 2,
        scratch_shapes=[pltpu.VMEM((2, D // RW_GROUP, RW_GROUP, RW_GROUP), F32)],
        compiler_params=_cparams(("parallel", "arbitrary")),
        name="rwkv_scan",
    )(r, v, kn, lw0, k0, b0, r, v, kn, lw1, k1, b1)


def _rwkv_out_kernel(h_ref, yf_ref, yb_ref, bonus_ref, g_ref, mod_ref, lnw_ref, lnb_ref, wo_ref, bd_ref, o_ref):
    bd = bd_ref[...]
    y = yf_ref[...] + yb_ref[...]
    mu = _seg_sum(y, bd) * (1.0 / 64)
    yc = y - mu
    var = _seg_sum(yc * yc, bd) * (1.0 / 64)
    yn = yc * lax.rsqrt(var + GN_EPS) * lnw_ref[...] + lnb_ref[...]
    out = (yn + bonus_ref[...]) * g_ref[...]
    res = jnp.dot(out.astype(BF16), wo_ref[...], preferred_element_type=F32)
    o_ref[...] = h_ref[...] + mod_ref[0][:, 2 * D:3 * D] * res


def _rwkv_out(lay, h, yf, yb, bonus, g, mod, ln_w, ln_b, w_o):
    return pl.pallas_call(
        _rwkv_out_kernel,
        grid=(lay.n_t,),
        in_specs=[lay.row_spec()] * 5 + [lay.mod_spec(), _const_spec((1, D)), _const_spec((1, D)),
                                          _const_spec((D, D)), _const_spec((RW_GROUP, RW_GROUP))],
        out_specs=lay.row_spec(),
        out_shape=jax.ShapeDtypeStruct((lay.N, D), F32),
        compiler_params=_cparams(("parallel",)),
        name="rwkv_out",
    )(h, yf, yb, bonus, g, mod, ln_w.reshape(1, D), ln_b.reshape(1, D), w_o.astype(BF16), _block_ones())


ROW_SUB, ROW_LANE = 8, 128


def _store_rows(ref, x):
    for c in range(ROW_SUB):
        ref[:, c, :] = x[:, c * ROW_LANE:(c + 1) * ROW_LANE]


def _load_rows(ref):
    return jnp.concatenate([ref[:, c, :] for c in range(ROW_SUB)], axis=1)


def _router_kernel(h_ref, mod_ref, g_ref, wr_ref, br_ref, f_ref, idx_ref, gate_ref):
    m = mod_ref[0]
    f = _norm_mod(h_ref[...], g_ref[...], m[:, 3 * D:4 * D], m[:, 4 * D:5 * D])
    _store_rows(f_ref, f)
    logits = lax.dot_general(wr_ref[...], f, (((1,), (1,)), ((), ())), preferred_element_type=F32,
                             precision=lax.Precision.HIGHEST) + br_ref[...]
    erow = lax.broadcasted_iota(jnp.int32, logits.shape, 0)
    tops, idxs = [], []
    l = logits
    for _ in range(TOP_K):
        mx = jnp.max(l, axis=0, keepdims=True)
        ix = jnp.min(jnp.where(l == mx, erow, N_EXPERTS), axis=0, keepdims=True)
        tops.append(mx)
        idxs.append(ix)
        l = jnp.where(erow == ix, -jnp.inf, l)
    es = [jnp.exp(t - tops[0]) for t in tops]
    tot = es[0] + es[1] + es[2] + es[3]
    idx_ref[...] = jnp.concatenate(idxs, axis=0)
    gate_ref[...] = jnp.concatenate([e / tot for e in es], axis=0)


def _moe_router(lay, h, mod, g, w_router, b_router, n_rows):
    tm = lay.tm
    return pl.pallas_call(
        _router_kernel,
        grid=(n_rows // tm,),
        in_specs=[lay.row_spec(), lay.mod_spec(), _const_spec((1, D)),
                  _const_spec((N_EXPERTS, D)), _const_spec((N_EXPERTS, 1))],
        out_specs=[pl.BlockSpec((tm, ROW_SUB, ROW_LANE), lambda i: (i, 0, 0)),
                   pl.BlockSpec((TOP_K, tm), lambda i: (0, i)),
                   pl.BlockSpec((TOP_K, tm), lambda i: (0, i))],
        out_shape=[jax.ShapeDtypeStruct((n_rows, ROW_SUB, ROW_LANE), F32),
                   jax.ShapeDtypeStruct((TOP_K, n_rows), jnp.int32),
                   jax.ShapeDtypeStruct((TOP_K, n_rows), F32)],
        compiler_params=_cparams(("parallel",)),
        name="moe_router",
    )(h, mod, g.reshape(1, D), w_router.T, b_router.reshape(N_EXPERTS, 1))


def _moe_plan(idx, gates):
    n = idx.shape[1]
    nk = n * TOP_K
    e_flat = idx.T.reshape(nk)
    g_flat = gates.T.reshape(nk)
    onehot = (e_flat[:, None] == jnp.arange(N_EXPERTS, dtype=jnp.int32)[None, :]).astype(jnp.int32)
    csum = jnp.cumsum(onehot, axis=0)
    counts = csum[-1]
    rank = jnp.take_along_axis(csum, e_flat[:, None], axis=1)[:, 0] - 1
    padded = (counts + MOE_BLOCK - 1) // MOE_BLOCK * MOE_BLOCK
    pad_end = jnp.cumsum(padded)
    pad_start = pad_end - padded
    dest = (pad_start[e_flat] + rank).astype(jnp.int32)
    n_blocks = nk // MOE_BLOCK + N_EXPERTS
    rows = n_blocks * MOE_BLOCK
    block_expert = jnp.minimum(jnp.searchsorted(pad_end, jnp.arange(n_blocks) * MOE_BLOCK, side='right'),
                               N_EXPERTS - 1).astype(jnp.int32)
    fill_base = jnp.concatenate([pad_start + counts, pad_end[-1:]]).astype(jnp.int32)
    fill_cnt = jnp.concatenate([padded - counts, rows - pad_end[-1:]]).astype(jnp.int32)
    n_used = (pad_end[-1:] // MOE_BLOCK).astype(jnp.int32)
    return dest, g_flat.reshape(n, TOP_K), block_expert, fill_base, fill_cnt, n_used, n_blocks


def _dispatch_kernel(fill_base_ref, fill_cnt_ref, dest_ref, f_ref, xs_hbm, zero_ref, sem, *, tm):
    def row_copy(src, row):
        return pltpu.make_async_copy(src, xs_hbm.at[row], sem)

    def issue(r, carry):
        for s in range(TOP_K):
            row_copy(f_ref.at[r], dest_ref[0, 0, r * TOP_K + s]).start()
        return carry

    lax.fori_loop(0, tm, issue, 0)

    @pl.when(pl.program_id(0) == 0)
    def _():
        zero_ref[...] = jnp.zeros_like(zero_ref)
        for e in range(N_EXPERTS + 1):
            base, cnt = fill_base_ref[e], fill_cnt_ref[e]

            def fill(r, carry):
                row_copy(zero_ref, base + r).start()
                return carry

            lax.fori_loop(0, cnt, fill, 0)

            def fill_wait(r, carry):
                row_copy(zero_ref, base + r).wait()
                return carry

            lax.fori_loop(0, cnt, fill_wait, 0)

    def drain(r, carry):
        for s in range(TOP_K):
            row_copy(f_ref.at[r], dest_ref[0, 0, r * TOP_K + s]).wait()
        return carry

    lax.fori_loop(0, tm, drain, 0)


def _moe_dispatch(lay, f3, dest, fill_base, fill_cnt, n_rows, n_blocks):
    tm = lay.tm
    grid_spec = pltpu.PrefetchScalarGridSpec(
        num_scalar_prefetch=2,
        grid=(n_rows // tm,),
        in_specs=[pl.BlockSpec((1, 1, tm * TOP_K), lambda i, *_: (i, 0, 0), memory_space=pltpu.SMEM),
                  pl.BlockSpec((tm, ROW_SUB, ROW_LANE), lambda i, *_: (i, 0, 0))],
        out_specs=pl.BlockSpec(memory_space=pl.ANY),
        scratch_shapes=[pltpu.VMEM((ROW_SUB, ROW_LANE), F32), pltpu.SemaphoreType.DMA(())],
    )
    return pl.pallas_call(
        functools.partial(_dispatch_kernel, tm=tm),
        grid_spec=grid_spec,
        out_shape=jax.ShapeDtypeStruct((n_blocks * MOE_BLOCK, ROW_SUB, ROW_LANE), F32),
        compiler_params=_cparams(("arbitrary",)),
        name="moe_dispatch",
    )(fill_base, fill_cnt, dest.reshape(n_rows // tm, 1, tm * TOP_K), f3)


def _expert_kernel(be_ref, nu_ref, x_ref, w1_ref, b1_ref, w2_ref, b2_ref, o_ref):
    del be_ref
    i = pl.program_id(0)

    @pl.when(i < nu_ref[0])
    def _():
        hcat = jnp.dot(_load_rows(x_ref).astype(BF16), w1_ref[0], preferred_element_type=F32) + b1_ref[0]
        h_glu = jnp.minimum(hcat[:, :D_FF], SWIGLU_LIMIT)
        h_lin = jnp.clip(hcat[:, D_FF:], -SWIGLU_LIMIT, SWIGLU_LIMIT)
        act = h_glu * jax.nn.sigmoid(SWIGLU_ALPHA * h_glu) * (h_lin + 1.0)
        _store_rows(o_ref, jnp.dot(act.astype(BF16), w2_ref[0], preferred_element_type=F32) + b2_ref[0])

    @pl.when(i >= nu_ref[0])
    def _():
        o_ref[...] = jnp.zeros_like(o_ref)


def _moe_experts(xs, block_expert, n_used, n_blocks, w1, b1, w2, b2):
    def xrow(i, be, nu):
        return (jnp.minimum(i, nu[0] - 1), 0, 0)

    grid_spec = pltpu.PrefetchScalarGridSpec(
        num_scalar_prefetch=2,
        grid=(n_blocks,),
        in_specs=[pl.BlockSpec((MOE_BLOCK, ROW_SUB, ROW_LANE), xrow),
                  pl.BlockSpec((1, D, 2 * D_FF), lambda i, be, nu: (be[i], 0, 0)),
                  pl.BlockSpec((1, 1, 2 * D_FF), lambda i, be, nu: (be[i], 0, 0)),
                  pl.BlockSpec((1, D_FF, D), lambda i, be, nu: (be[i], 0, 0)),
                  pl.BlockSpec((1, 1, D), lambda i, be, nu: (be[i], 0, 0))],
        out_specs=pl.BlockSpec((MOE_BLOCK, ROW_SUB, ROW_LANE), lambda i, be, nu: (i, 0, 0)),
    )
    return pl.pallas_call(
        _expert_kernel,
        grid_spec=grid_spec,
        out_shape=jax.ShapeDtypeStruct((n_blocks * MOE_BLOCK, ROW_SUB, ROW_LANE), F32),
        compiler_params=_cparams(("arbitrary",)),
        name="moe_experts",
    )(block_expert, n_used, xs, w1.astype(BF16), b1.reshape(N_EXPERTS, 1, 2 * D_FF),
      w2.astype(BF16), b2.reshape(N_EXPERTS, 1, D))


def _combine_kernel(dest_ref, h_ref, gate_ref, mod_ref, y_hbm, o_ref, buf, sem, *, tm):
    def row_copy(r, s):
        return pltpu.make_async_copy(y_hbm.at[dest_ref[0, 0, r * TOP_K + s]], buf.at[s, r], sem)

    def issue(r, carry):
        for s in range(TOP_K):
            row_copy(r, s).start()
        return carry

    lax.fori_loop(0, tm, issue, 0)

    def drain(r, carry):
        for s in range(TOP_K):
            row_copy(r, s).wait()
        return carry

    lax.fori_loop(0, tm, drain, 0)
    gates = gate_ref[...]
    y = gates[:, 0:1] * _load_rows(buf.at[0])
    for s in range(1, TOP_K):
        y = y + gates[:, s:s + 1] * _load_rows(buf.at[s])
    o_ref[...] = h_ref[...] + mod_ref[0][:, 5 * D:6 * D] * y


def _moe_combine(lay, h, mod, yb, dest, gates, n_rows):
    tm = lay.tm
    return pl.pallas_call(
        functools.partial(_combine_kernel, tm=tm),
        grid=(n_rows // tm,),
        in_specs=[pl.BlockSpec((1, 1, tm * TOP_K), lambda i: (i, 0, 0), memory_space=pltpu.SMEM),
                  lay.row_spec(), lay.row_spec(TOP_K), lay.mod_spec(), pl.BlockSpec(memory_space=pl.ANY)],
        out_specs=lay.row_spec(),
        out_shape=jax.ShapeDtypeStruct((n_rows, D), F32),
        scratch_shapes=[pltpu.VMEM((TOP_K, tm, ROW_SUB, ROW_LANE), F32), pltpu.SemaphoreType.DMA(())],
        compiler_params=_cparams(("arbitrary",)),
        name="moe_combine",
    )(dest.reshape(n_rows // tm, 1, tm * TOP_K), h, gates, mod, yb)


def _moe_layer(lay, h, mod, g, w_router, b_router, w1, b1, w2, b2, n_rows):
    f3, idx, gates = _moe_router(lay, h, mod, g, w_router, b_router, n_rows)
    dest, gates_tk, block_expert, fill_base, fill_cnt, n_used, n_blocks = _moe_plan(idx, gates)
    xs = _moe_dispatch(lay, f3, dest, fill_base, fill_cnt, n_rows, n_blocks)
    yb = _moe_experts(xs, block_expert, n_used, n_blocks, w1, b1, w2, b2)
    return _moe_combine(lay, h, mod, yb, dest, gates_tk, n_rows)


def _final_kernel(h_ref, g_ref, o_ref):
    h = h_ref[...]
    ms = jnp.mean(h * h, axis=-1, keepdims=True)
    o_ref[...] = h * lax.rsqrt(ms + NORM_EPS) * g_ref[...]


def _final_norm(lay, h, g, n_rows):
    return pl.pallas_call(
        _final_kernel,
        grid=(n_rows // lay.tm,),
        in_specs=[lay.row_spec(), _const_spec((1, D))],
        out_specs=lay.row_spec(),
        out_shape=jax.ShapeDtypeStruct((n_rows, D), F32),
        compiler_params=_cparams(("parallel",)),
        name="final_norm",
    )(h, g.reshape(1, D))


def kernel(x, c, ctx, c_ctx, ada_w, ada_b, norm_g, attn_w_qkv, attn_b_qkv, attn_w_o, attn_b_o, attn_sinks, pool_w, pool_b, pool_scale, rwkv_mix, rwkv_w_rkv, rwkv_w0, rwkv_w1, rwkv_w2, rwkv_a0, rwkv_a1, rwkv_a2, rwkv_g1, rwkv_g2, rwkv_k_k, rwkv_k_a, rwkv_r_k, rwkv_ln_w, rwkv_ln_b, rwkv_w_o, moe_w_router, moe_b_router, moe_w1, moe_b1, moe_w2, moe_b2, final_g):
    B, T, _ = x.shape
    L = ctx.shape[1]
    depth = ada_w.shape[0]
    lay = _Layout(B, T, L)
    h = jnp.concatenate([x.reshape(B * T, D), ctx.reshape(B * L, D)], axis=0)

    n_c = B + 1
    n_c_pad = -(-n_c // 8) * 8
    c_all = jnp.concatenate([c, c_ctx[None, :], jnp.zeros((n_c_pad - n_c, D), F32)], axis=0)
    mods = _ada_mods(c_all, ada_w, ada_b)
    cos, sin = _rope_tables(T, L)

    for layer in range(depth):
        last = layer == depth - 1
        kind, j = layer % 3, layer // 3
        mod = mods[layer].reshape(n_c_pad, 1, 6 * D)
        n_rows = lay.rows_x if last else lay.N
        if kind == 0:
            q, k, v = _attn_qkv(lay, h, mod, norm_g[layer, 0], attn_w_qkv[j], attn_b_qkv[j], cos, sin)
            o = _attention(lay, q, k, v, attn_sinks[j], not last)
            h = _linear_residual(lay, h, o, mod, attn_w_o[j], attn_b_o[j], 2, n_rows)
        elif kind == 1:
            h = _pool_mixer(lay, h, mod, norm_g[layer, 0], pool_w[j], pool_b[j], pool_scale[j], n_rows)
        else:
            p = dict(mix=rwkv_mix[j], w_rkv=rwkv_w_rkv[j], w0=rwkv_w0[j], w1=rwkv_w1[j], w2=rwkv_w2[j],
                     a0=rwkv_a0[j], a1=rwkv_a1[j], a2=rwkv_a2[j], g1=rwkv_g1[j], g2=rwkv_g2[j],
                     k_k=rwkv_k_k[j], k_a=rwkv_k_a[j], r_k=rwkv_r_k[j])
            r, v, kn, g, bonus, lw0, k0, b0, lw1, k1, b1 = _rwkv_proj(lay, h, mod, norm_g[layer, 0], p)
            yf, yb = _rwkv_scan(lay, r, v, kn, lw0, k0, b0, lw1, k1, b1)
            h = _rwkv_out(lay, h, yf, yb, bonus, g, mod, rwkv_ln_w[j], rwkv_ln_b[j], rwkv_w_o[j])
        h = _moe_layer(lay, h, mod, norm_g[layer, 1], moe_w_router[layer], moe_b_router[layer],
                       moe_w1[layer], moe_b1[layer], moe_w2[layer], moe_b2[layer], n_rows)
    out = _final_norm(lay, h, final_g, lay.rows_x)
    return out.reshape(B, T, D)
```

```python
import functools

import jax
import jax.numpy as jnp
import numpy as np
from jax import lax
from jax.experimental import pallas as pl
from jax.experimental.pallas import tpu as pltpu

F32 = jnp.float32
BF16 = jnp.bfloat16

D = 1024
NORM_EPS = 1e-5
GRID_W = 64
HEAD_DIM = 64
N_HEADS = 16
N_KV = 4
Q_DIM = 1024
KV_DIM = 256
QK_DIM = Q_DIM + KV_DIM
WINDOW = 128
QB = 128
ROPE_BASE = 10000.0
POOL_SIZES = (2, 4, 8, 16)
POOL_GW = 256
HALO = 8
GN_EPS = 64e-5
N_EXPERTS = 32
TOP_K = 4
D_FF = 1024
SWIGLU_LIMIT = 7.0
SWIGLU_ALPHA = 1.702
MOE_BLOCK = 256
RW_CHUNK = 64
RW_GROUP = 256
VMEM_LIMIT = 56 * 1024 * 1024


def _cparams(sem):
    return pltpu.CompilerParams(dimension_semantics=sem, vmem_limit_bytes=VMEM_LIMIT)


def _norm_mod(h, g, shift, scale):
    ms = jnp.mean(h * h, axis=-1, keepdims=True)
    y = h * lax.rsqrt(ms + NORM_EPS) * g
    return y * (1.0 + scale) + shift


class _Layout:
    def __init__(self, B, T, L):
        self.B, self.T, self.L = B, T, L
        self.tm = 256 if L % 256 == 0 else 128
        self.rows_x, self.rows_c = B * T, B * L
        self.N = self.rows_x + self.rows_c
        self.tx, self.tc = T // self.tm, L // self.tm
        self.n_xt = self.rows_x // self.tm
        self.n_t = self.N // self.tm

    def mod_idx(self, i):
        return jnp.where(i < self.n_xt, i // self.tx, self.B)

    def mod_spec(self):
        return pl.BlockSpec((1, 1, 6 * D), lambda i: (self.mod_idx(i), 0, 0))

    def row_spec(self, width=D):
        return pl.BlockSpec((self.tm, width), lambda i: (i, 0))

    def seq_tile(self, i):
        j = jnp.where(i < self.n_xt, i % self.tx, (i - self.n_xt) % self.tc)
        n = jnp.where(i < self.n_xt, self.tx, self.tc)
        return j, n

    def halo_specs(self):
        hb = self.tm // HALO
        last = self.N // HALO - 1
        prev = pl.BlockSpec((HALO, D), lambda i: (jnp.maximum(i * hb - 1, 0), 0))
        nxt = pl.BlockSpec((HALO, D), lambda i: (jnp.minimum((i + 1) * hb, last), 0))
        return prev, nxt


def _const_spec(shape):
    nd = len(shape)
    return pl.BlockSpec(shape, lambda *_: (0,) * nd)


def _ada_kernel(c_ref, w_ref, b_ref, o_ref):
    c = c_ref[...]
    s = c * jax.nn.sigmoid(c)
    o_ref[0] = jnp.dot(s, w_ref[0], preferred_element_type=F32, precision=lax.Precision.HIGHEST) + b_ref[0]


def _ada_mods(c_all, ada_w, ada_b):
    depth = ada_w.shape[0]
    R = c_all.shape[0]
    nt = 1536
    return pl.pallas_call(
        _ada_kernel,
        grid=(depth, 6 * D // nt),
        in_specs=[pl.BlockSpec((R, D), lambda l, j: (0, 0)),
                  pl.BlockSpec((1, D, nt), lambda l, j: (l, 0, j)),
                  pl.BlockSpec((1, 1, nt), lambda l, j: (l, 0, j))],
        out_specs=pl.BlockSpec((1, R, nt), lambda l, j: (l, 0, j)),
        out_shape=jax.ShapeDtypeStruct((depth, R, 6 * D), F32),
        compiler_params=_cparams(("parallel", "parallel")),
        name="ada_mod",
    )(c_all, ada_w, ada_b.reshape(depth, 1, 6 * D))


def _qkv_kernel(h_ref, mod_ref, g_ref, w_ref, b_ref, cos_ref, sin_ref, q_ref, k_ref, v_ref):
    m = mod_ref[0]
    a = _norm_mod(h_ref[...], g_ref[...], m[:, 0:D], m[:, D:2 * D])
    qkv = jnp.dot(a.astype(BF16), w_ref[...], preferred_element_type=F32) + b_ref[...]
    qk = qkv[:, :QK_DIM]
    lane = lax.broadcasted_iota(jnp.int32, (1, QK_DIM), 1)
    first = (lane % 32) < 16
    rot = jnp.where(first, -pltpu.roll(qk, QK_DIM - 16, 1), pltpu.roll(qk, 16, 1))
    qk = qk * cos_ref[...] + rot * sin_ref[...]
    q_ref[...] = (qk[:, :Q_DIM] * (HEAD_DIM ** -0.5)).astype(BF16)
    k_ref[...] = qk[:, Q_DIM:].astype(BF16)
    v_ref[...] = qkv[:, QK_DIM:].astype(BF16)


def _rope_tables(T, L):
    rows = T // GRID_W
    n_freq = HEAD_DIM // 4
    inv = ROPE_BASE ** (-jnp.arange(n_freq, dtype=F32) / n_freq)
    row_ang = jnp.arange(rows, dtype=F32)[:, None] * inv
    col_ang = jnp.arange(GRID_W, dtype=F32)[:, None] * inv
    ang_r = jnp.broadcast_to(row_ang[:, None, :], (rows, GRID_W, n_freq)).reshape(T, n_freq)
    ang_c = jnp.broadcast_to(col_ang[None, :, :], (rows, GRID_W, n_freq)).reshape(T, n_freq)
    ang = jnp.concatenate([ang_r, ang_r, ang_c, ang_c], axis=-1)
    cos = jnp.concatenate([jnp.cos(ang), jnp.ones((L, HEAD_DIM), F32)], axis=0)
    sin = jnp.concatenate([jnp.sin(ang), jnp.zeros((L, HEAD_DIM), F32)], axis=0)
    reps = QK_DIM // HEAD_DIM
    return jnp.tile(cos, (1, reps)), jnp.tile(sin, (1, reps))


def _attn_qkv(lay, h, mod, g, w_qkv, b_qkv, cos, sin):
    tm = lay.tm

    def tab_idx(i):
        return (jnp.where(i < lay.n_xt, i % lay.tx, lay.tx + (i - lay.n_xt) % lay.tc), 0)

    return pl.pallas_call(
        _qkv_kernel,
        grid=(lay.n_t,),
        in_specs=[lay.row_spec(), lay.mod_spec(), _const_spec((1, D)),
                  _const_spec((D, Q_DIM + 2 * KV_DIM)), _const_spec((1, Q_DIM + 2 * KV_DIM)),
                  pl.BlockSpec((tm, QK_DIM), tab_idx), pl.BlockSpec((tm, QK_DIM), tab_idx)],
        out_specs=[lay.row_spec(Q_DIM), lay.row_spec(KV_DIM), lay.row_spec(KV_DIM)],
        out_shape=[jax.ShapeDtypeStruct((lay.N, Q_DIM), BF16),
                   jax.ShapeDtypeStruct((lay.N, KV_DIM), BF16),
                   jax.ShapeDtypeStruct((lay.N, KV_DIM), BF16)],
        compiler_params=_cparams(("parallel",)),
        name="attn_qkv",
    )(h, mod, g.reshape(1, D), w_qkv.astype(BF16), b_qkv.reshape(1, -1), cos, sin)


def _attn_heads(q, kk, vv, mask, sink_ref, o_ref):
    R = q.shape[0]
    G = N_HEADS // N_KV
    for g in range(N_KV):
        kg = kk[:, g * HEAD_DIM:(g + 1) * HEAD_DIM]
        vg = vv[:, g * HEAD_DIM:(g + 1) * HEAD_DIM]
        qs = jnp.concatenate([q[:, (g * G + j) * HEAD_DIM:(g * G + j + 1) * HEAD_DIM] for j in range(G)], axis=0)
        s = lax.dot_general(qs, kg, (((1,), (1,)), ((), ())), preferred_element_type=F32)
        if mask is not None:
            s = jnp.where(jnp.concatenate([mask] * G, axis=0), s, -jnp.inf)
        sink = jnp.concatenate([jnp.full((R, 1), sink_ref[g * G + j], F32) for j in range(G)], axis=0)
        m = jnp.maximum(jnp.max(s, axis=-1, keepdims=True), sink)
        p = jnp.exp(s - m)
        denom = jnp.sum(p, axis=-1, keepdims=True) + jnp.exp(sink - m)
        o = jnp.dot(p.astype(BF16), vg, preferred_element_type=F32) / denom
        for j in range(G):
            hq = g * G + j
            o_ref[:, hq * HEAD_DIM:(hq + 1) * HEAD_DIM] = o[j * R:(j + 1) * R].astype(o_ref.dtype)


def _attn_kernel(sink_ref, q_ref, kp_ref, kc_ref, kn_ref, vp_ref, vc_ref, vn_ref, kx_ref, vx_ref, o_ref, *, T, L):
    i = pl.program_id(1)
    nqb = T // QB

    @pl.when(i < nqb)
    def _():
        kk = jnp.concatenate([kx_ref[...], kp_ref[...], kc_ref[...], kn_ref[...]], axis=0)
        vv = jnp.concatenate([vx_ref[...], vp_ref[...], vc_ref[...], vn_ref[...]], axis=0)
        S = L + 3 * QB
        col = lax.broadcasted_iota(jnp.int32, (QB, S), 1)
        row = lax.broadcasted_iota(jnp.int32, (QB, S), 0)
        rel = col - L - QB - row
        kpos = i * QB - QB + (col - L)
        local_ok = (jnp.abs(rel) <= WINDOW) & (kpos >= 0) & (kpos < T)
        mask = (col < L) | local_ok
        _attn_heads(q_ref[...], kk, vv, mask, sink_ref, o_ref)

    @pl.when(i >= nqb)
    def _():
        _attn_heads(q_ref[...], kx_ref[...], vx_ref[...], None, sink_ref, o_ref)


def _attention(lay, q, k, v, sinks, need_ctx):
    B, T, L = lay.B, lay.T, lay.L
    nqb, ncb = T // QB, L // QB
    cq = lay.rows_x // QB
    cb = lay.rows_x // L
    smem = pl.BlockSpec(memory_space=pltpu.SMEM)

    def cur(b, i):
        return (jnp.where(i < nqb, b * nqb + i, cq + b * ncb + (i - nqb)), 0)

    def prev(b, i):
        return (b * nqb + jnp.clip(i - 1, 0, nqb - 1), 0)

    def mid(b, i):
        return (b * nqb + jnp.minimum(i, nqb - 1), 0)

    def nxt(b, i):
        return (b * nqb + jnp.minimum(i + 1, nqb - 1), 0)

    def ctxb(b, i):
        return (cb + b, 0)

    kv_specs = [pl.BlockSpec((QB, KV_DIM), prev), pl.BlockSpec((QB, KV_DIM), mid), pl.BlockSpec((QB, KV_DIM), nxt)]
    n_out = lay.N if need_ctx else lay.rows_x
    return pl.pallas_call(
        functools.partial(_attn_kernel, T=T, L=L),
        grid=(B, nqb + (ncb if need_ctx else 0)),
        in_specs=[smem, pl.BlockSpec((QB, Q_DIM), cur)] + kv_specs + kv_specs
                 + [pl.BlockSpec((L, KV_DIM), ctxb), pl.BlockSpec((L, KV_DIM), ctxb)],
        out_specs=pl.BlockSpec((QB, Q_DIM), cur),
        out_shape=jax.ShapeDtypeStruct((n_out, Q_DIM), BF16),
        compiler_params=_cparams(("parallel", "arbitrary")),
        name="attention",
    )(sinks, q, k, k, k, v, v, v, k, v)


def _linres_kernel(h_ref, x_ref, mod_ref, w_ref, b_ref, o_ref, *, gate_slot):
    gate = mod_ref[0][:, gate_slot * D:(gate_slot + 1) * D]
    y = jnp.dot(x_ref[...], w_ref[...], preferred_element_type=F32) + b_ref[...]
    o_ref[...] = h_ref[...] + gate * y


def _linear_residual(lay, h, x, mod, w, b, gate_slot, n_rows):
    return pl.pallas_call(
        functools.partial(_linres_kernel, gate_slot=gate_slot),
        grid=(n_rows // lay.tm,),
        in_specs=[lay.row_spec(), lay.row_spec(x.shape[1]), lay.mod_spec(),
                  _const_spec(w.shape), _const_spec((1, D))],
        out_specs=lay.row_spec(),
        out_shape=jax.ShapeDtypeStruct((n_rows, D), F32),
        compiler_params=_cparams(("parallel",)),
        name="linear_residual",
    )(h, x, mod, w.astype(BF16), b.reshape(1, D))


def _fill_ext(ext_ref, h_ref, hp_ref, hn_ref, g, shift, scale, j, n, tm):
    a = _norm_mod(h_ref[...], g, shift, scale)
    ap = _norm_mod(hp_ref[...], g, shift, scale)
    an = _norm_mod(hn_ref[...], g, shift, scale)
    ext_ref[0:HALO, :] = jnp.where(j == 0, 0.0, ap)
    ext_ref[HALO:HALO + tm, :] = a
    ext_ref[HALO + tm:2 * HALO + tm, :] = jnp.where(j == n - 1, 0.0, an)
    return a


def _pool_kernel(h_ref, hp_ref, hn_ref, mod_ref, g_ref, w_ref, b_ref, ls_ref, o_ref, ext_ref, *, lay):
    tm = lay.tm
    i = pl.program_id(0)
    j, n = lay.seq_tile(i)
    m = mod_ref[0]
    a = _fill_ext(ext_ref, h_ref, hp_ref, hn_ref, g_ref[...], m[:, 0:D], m[:, D:2 * D], j, n, tm)
    t = j * tm + lax.broadcasted_iota(jnp.int32, (tm, 1), 0)
    n_tok = n * tm
    outs = []
    for gi, size in enumerate(POOL_SIZES):
        sl = slice(gi * POOL_GW, (gi + 1) * POOL_GW)
        acc = ext_ref[HALO - size // 2:HALO - size // 2 + tm, sl]
        for o in range(-size // 2 + 1, size - size // 2):
            acc = acc + ext_ref[HALO + o:HALO + o + tm, sl]
        lo = jnp.maximum(t - size // 2, 0)
        hi = jnp.minimum(t + size - size // 2, n_tok)
        d = acc / (hi - lo).astype(F32) - a[:, sl]
        outs.append(jnp.dot(d.astype(BF16), w_ref[gi], preferred_element_type=F32) + b_ref[gi])
    y = jnp.concatenate(outs, axis=-1) * ls_ref[...]
    o_ref[...] = h_ref[...] + m[:, 2 * D:3 * D] * y


def _pool_mixer(lay, h, mod, g, w, b, ls, n_rows):
    prev, nxt = lay.halo_specs()
    return pl.pallas_call(
        functools.partial(_pool_kernel, lay=lay),
        grid=(n_rows // lay.tm,),
        in_specs=[lay.row_spec(), prev, nxt, lay.mod_spec(), _const_spec((1, D)),
                  _const_spec(w.shape), _const_spec((4, 1, POOL_GW)), _const_spec((1, D))],
        out_specs=lay.row_spec(),
        out_shape=jax.ShapeDtypeStruct((n_rows, D), F32),
        scratch_shapes=[pltpu.VMEM((lay.tm + 2 * HALO, D), F32)],
        compiler_params=_cparams(("parallel",)),
        name="pool_mixer",
    )(h, h, h, mod, g.reshape(1, D), w.astype(BF16), b.reshape(4, 1, POOL_GW), ls.reshape(1, D))


def _seg_sum(x, bd):
    outs = []
    for q in range(D // RW_GROUP):
        xg = x[:, q * RW_GROUP:(q + 1) * RW_GROUP]
        hi = xg.astype(BF16)
        lo = (xg - hi.astype(F32)).astype(BF16)
        outs.append(jnp.dot(hi, bd, preferred_element_type=F32) + jnp.dot(lo, bd, preferred_element_type=F32))
    return jnp.concatenate(outs, axis=-1)


def _rwkv_proj_kernel(h_ref, hp_ref, hn_ref, mod_ref, g_ref, mix_ref, wr_ref, wk_ref, wv_ref, g1_ref, g2_ref,
                      w1_ref, w2_ref, w0_ref, a1_ref, a2_ref, a0_ref, kk_ref, ka_ref, rk_ref, bd_ref,
                      r_out, v_out, kn_out, g_out, bonus_out, lw0_out, k0_out, b0_out, lw1_out, k1_out, b1_out,
                      ext_ref, *, lay):
    tm = lay.tm
    i = pl.program_id(0)
    j, n = lay.seq_tile(i)
    m = mod_ref[0]
    a = _fill_ext(ext_ref, h_ref, hp_ref, hn_ref, g_ref[...], m[:, 0:D], m[:, D:2 * D], j, n, tm)
    prev = ext_ref[HALO - 1:HALO - 1 + tm, :]
    nxt = ext_ref[HALO + 1:HALO + 1 + tm, :]
    xx = 0.5 * (prev + nxt) - a
    x_r, x_w, x_k, x_v, x_a, x_g = [(a + xx * mix_ref[q:q + 1, :]).astype(BF16) for q in range(6)]
    bd = bd_ref[...]
    r = jnp.dot(x_r, wr_ref[...], preferred_element_type=F32)
    k = jnp.dot(x_k, wk_ref[...], preferred_element_type=F32)
    v = jnp.dot(x_v, wv_ref[...], preferred_element_type=F32)
    gl = jnp.dot(x_g, g1_ref[...], preferred_element_type=F32)
    g = jnp.dot(jax.nn.sigmoid(gl).astype(BF16), g2_ref[...], preferred_element_type=F32)
    kk = k * kk_ref[...]
    kn = kk / jnp.maximum(jnp.sqrt(_seg_sum(kk * kk, bd)), 1e-12)
    tw = jnp.tanh(jnp.dot(x_w, w1_ref[...], preferred_element_type=F32))
    aa = jnp.dot(x_a, a1_ref[...], preferred_element_type=F32)
    lane = lax.broadcasted_iota(jnp.int32, (1, tw.shape[1]), 1)
    r_out[...] = r
    v_out[...] = v
    kn_out[...] = kn
    g_out[...] = g
    rk = rk_ref[...]
    bonus = jnp.zeros_like(v)
    outs = ((lw0_out, k0_out, b0_out), (lw1_out, k1_out, b1_out))
    half = tw.shape[1] // 2
    for d in range(2):
        sel = (lane >= d * half) & (lane < (d + 1) * half)
        wl = w0_ref[d:d + 1, :] + jnp.dot(jnp.where(sel, tw, 0.0).astype(BF16), w2_ref[...], preferred_element_type=F32)
        z = -wl
        w_log = -(jnp.maximum(z, 0.0) + jnp.log(1.0 + jnp.exp(-jnp.abs(z)))) - 0.5
        asig = jax.nn.sigmoid(a0_ref[d:d + 1, :] + jnp.dot(jnp.where(sel, aa, 0.0).astype(BF16), a2_ref[...],
                                                           preferred_element_type=F32))
        k_d = k * (1.0 + (asig - 1.0) * ka_ref[...])
        lw_o, k_o, b_o = outs[d]
        lw_o[...] = -jnp.exp(w_log)
        k_o[...] = k_d
        b_o[...] = kn * asig
        bonus = bonus + _seg_sum(r * k_d * rk, bd) * v
    bonus_out[...] = bonus


def _block_ones():
    idx = np.arange(RW_GROUP) // 64
    return jnp.asarray(idx[:, None] == idx[None, :], dtype=BF16)


def _rwkv_proj(lay, h, mod, g, p):
    prev, nxt = lay.halo_specs()
    cat = lambda w: jnp.concatenate([w[0], w[1]], axis=-1).astype(BF16)
    stack = lambda w: jnp.concatenate([w[0], w[1]], axis=0).astype(BF16)
    args = [h, h, h, mod, g.reshape(1, D), p['mix'],
            p['w_rkv'][0].astype(BF16), p['w_rkv'][1].astype(BF16), p['w_rkv'][2].astype(BF16),
            p['g1'].astype(BF16), p['g2'].astype(BF16),
            cat(p['w1']), stack(p['w2']), p['w0'], cat(p['a1']), stack(p['a2']), p['a0'],
            p['k_k'].reshape(1, D), p['k_a'].reshape(1, D), p['r_k'].reshape(1, D), _block_ones()]
    in_specs = [lay.row_spec(), prev, nxt, lay.mod_spec()] + [_const_spec(x.shape) for x in args[4:]]
    n_out = 11
    return pl.pallas_call(
        functools.partial(_rwkv_proj_kernel, lay=lay),
        grid=(lay.n_t,),
        in_specs=in_specs,
        out_specs=[lay.row_spec()] * n_out,
        out_shape=[jax.ShapeDtypeStruct((lay.N, D), F32)] * n_out,
        scratch_shapes=[pltpu.VMEM((lay.tm + 2 * HALO, D), F32)],
        compiler_params=_cparams(("parallel",)),
        name="rwkv_proj",
    )(*args)


def _rwkv_scan_kernel(rf, vf, nf, lwf, kf, bf, rb, vb, nb, lwb, kb, bb, yf_out, yb_out, z_ref):
    C, G = RW_CHUNK, RW_GROUP
    c = pl.program_id(1)

    @pl.when(c == 0)
    def _():
        z_ref[...] = jnp.zeros_like(z_ref)

    trow = lax.broadcasted_iota(jnp.int32, (C, C), 0)
    tcol = lax.broadcasted_iota(jnp.int32, (C, C), 1)
    ti = lax.broadcasted_iota(jnp.int32, (C, G), 0)
    ii = lax.broadcasted_iota(jnp.int32, (C, G), 1) % C
    brow = lax.broadcasted_iota(jnp.int32, (G, G), 0)
    bcol = lax.broadcasted_iota(jnp.int32, (G, G), 1)
    m_bd_t = (brow // C) == (bcol // 64)
    m_bd_f = (brow // 64) == (bcol // 64)
    eye_g = brow == bcol

    def bd(x):
        return jnp.where(m_bd_t, jnp.concatenate([x] * (G // C), axis=0), 0.0).astype(BF16)

    def mm(a, b):
        return jnp.dot(a, b, preferred_element_type=F32)

    dirs = ((rf, vf, nf, lwf, kf, bf, yf_out, False), (rb, vb, nb, lwb, kb, bb, yb_out, True))
    ch = []
    for d, (r_ref, v_ref, n_ref, lw_ref, k_ref, b_ref, y_out, rev) in enumerate(dirs):
        tri = (tcol >= trow) if rev else (tcol <= trow)
        strict = (ii > ti) if rev else (ii < ti)
        incl = (ii >= ti) if rev else (ii <= ti)
        lw = lw_ref[...]
        lc = jnp.dot(tri.astype(F32), lw, preferred_element_type=F32, precision=lax.Precision.HIGHEST)
        ltot = lc[0:1, :] if rev else lc[C - 1:C, :]
        e_lc = jnp.exp(lc)
        ie_lc = jnp.exp(-lc)
        e_le = jnp.exp(lc - lw)
        e_bar = jnp.exp(ltot - lc)
        e_tot = jnp.exp(ltot)
        kvec, bvec = k_ref[...], b_ref[...]
        a_t = -n_ref[...] * e_le
        r_t = r_ref[...] * e_lc
        b_t = bvec * ie_lc
        k_t = kvec * ie_lc
        b_b = bvec * e_bar
        k_b = kvec * e_bar
        vall = v_ref[...]
        for g in range(D // G):
            sl = slice(g * G, (g + 1) * G)
            ch.append(dict(
                d=d, g=g, sl=sl, y_out=y_out, strict=strict, incl=incl, v=vall[:, sl], e_tot=e_tot[:, sl],
                ar=jnp.concatenate([a_t[:, sl], r_t[:, sl]], axis=0).astype(BF16),
                bk=jnp.concatenate([bd(b_t[:, sl]), bd(k_t[:, sl])], axis=0),
                lhs_t=jnp.concatenate([b_b[:, sl], k_b[:, sl]], axis=0).astype(BF16)))

    for q in ch:
        aa = lax.dot_general(q['ar'], q['bk'], (((1,), (1,)), ((), ())), preferred_element_type=F32)
        q['a_ab'] = jnp.where(q['strict'], aa[:C, :G], 0.0)
        q['a_ak'] = jnp.where(q['strict'], aa[:C, G:], 0.0)
        q['m_rb'] = jnp.where(q['incl'], aa[C:, :G], 0.0).astype(BF16)
        q['m_rk'] = jnp.where(q['incl'], aa[C:, G:], 0.0)
    for q in ch:
        q['x'] = jnp.where(ii == ti, 1.0, 0.0) + q['a_ab']
        q['p'] = mm(q['a_ab'].astype(BF16), bd(q['a_ab']))
    for q in ch:
        q['wv'] = mm(jnp.concatenate([q['a_ak'], q['m_rk']], axis=0).astype(BF16), bd(q['v']))
        q['z'] = z_ref[q['d'], q['g']]
        q['az'] = mm(q['ar'], q['z'].astype(BF16))
    n_sq = int(np.log2(C)) - 1
    for s in range(n_sq):
        last = s == n_sq - 1
        for q in ch:
            lhs = q['x'] if last else jnp.concatenate([q['x'], q['p']], axis=0)
            xp = mm(lhs.astype(BF16), bd(q['p']))
            q['x'] = q['x'] + xp[:C]
            if not last:
                q['p'] = xp[C:]
    for q in ch:
        q['u'] = mm(q['x'].astype(BF16), bd(q['az'][:C] + q['wv'][:C]))
    for q in ch:
        y = q['az'][C:] + q['wv'][C:] + mm(q['m_rb'], bd(q['u']))
        q['y_out'][:, q['sl']] = y
        rhs_t = jnp.concatenate([q['u'], q['v']], axis=0).astype(BF16)
        zu = lax.dot_general(q['lhs_t'], rhs_t, (((0,), (0,)), ((), ())), preferred_element_type=F32)
        decay = jnp.sum(jnp.where(eye_g, jnp.broadcast_to(q['e_tot'], (G, G)), 0.0), axis=1, keepdims=True)
        z_ref[q['d'], q['g']] = jnp.where(m_bd_f, decay * q['z'] + zu, 0.0)


def _rwkv_scan(lay, r, v, kn, lw0, k0, b0, lw1, k1, b1):
    B, T, L = lay.B, lay.T, lay.L
    C = RW_CHUNK
    cc, cx = L // C, T // C
    nc = cc + cx
    ctx0 = lay.rows_x // C

    def fwd(b, c):
        return (jnp.where(c < cc, ctx0 + b * cc + c, b * cx + c - cc), 0)

    def bwd(b, c):
        return (jnp.where(c < cc, ctx0 + b * cc + (cc - 1 - c), b * cx + (nc - 1 - c)), 0)

    sf, sb = pl.BlockSpec((C, D), fwd), pl.BlockSpec((C, D), bwd)
    return pl.pallas_call(
        _rwkv_scan_kernel,
        grid=(B, nc),
        in_specs=[sf] * 6 + [sb] * 6,
        out_specs=[sf, sb],
        out_shape=[jax.ShapeDtypeStruct((lay.N, D), F32)] * 2,
        scratch_shapes=[pltpu.VMEM((2, D // RW_GROUP, RW_GROUP, RW_GROUP), F32)],
        compiler_params=_cparams(("parallel", "arbitrary")),
        name="rwkv_scan",
    )(r, v, kn, lw0, k0, b0, r, v, kn, lw1, k1, b1)


def _rwkv_out_kernel(h_ref, yf_ref, yb_ref, bonus_ref, g_ref, mod_ref, lnw_ref, lnb_ref, wo_ref, bd_ref, o_ref):
    bd = bd_ref[...]
    y = yf_ref[...] + yb_ref[...]
    mu = _seg_sum(y, bd) * (1.0 / 64)
    yc = y - mu
    var = _seg_sum(yc * yc, bd) * (1.0 / 64)
    yn = yc * lax.rsqrt(var + GN_EPS) * lnw_ref[...] + lnb_ref[...]
    out = (yn + bonus_ref[...]) * g_ref[...]
    res = jnp.dot(out.astype(BF16), wo_ref[...], preferred_element_type=F32)
    o_ref[...] = h_ref[...] + mod_ref[0][:, 2 * D:3 * D] * res


def _rwkv_out(lay, h, yf, yb, bonus, g, mod, ln_w, ln_b, w_o):
    return pl.pallas_call(
        _rwkv_out_kernel,
        grid=(lay.n_t,),
        in_specs=[lay.row_spec()] * 5 + [lay.mod_spec(), _const_spec((1, D)), _const_spec((1, D)),
                                          _const_spec((D, D)), _const_spec((RW_GROUP, RW_GROUP))],
        out_specs=lay.row_spec(),
        out_shape=jax.ShapeDtypeStruct((lay.N, D), F32),
        compiler_params=_cparams(("parallel",)),
        name="rwkv_out",
    )(h, yf, yb, bonus, g, mod, ln_w.reshape(1, D), ln_b.reshape(1, D), w_o.astype(BF16), _block_ones())


RUN_ALIGN = 8


def _router_kernel(h_ref, mod_ref, g_ref, wr_ref, br_ref, upper_ref, gate_ref, lpos_ref, cnt_ref):
    m = mod_ref[0]
    f = _norm_mod(h_ref[...], g_ref[...], m[:, 3 * D:4 * D], m[:, 4 * D:5 * D])
    logits = lax.dot_general(wr_ref[...], f, (((1,), (1,)), ((), ())), preferred_element_type=F32,
                             precision=lax.Precision.HIGHEST) + br_ref[...]
    erow = lax.broadcasted_iota(jnp.int32, logits.shape, 0)
    tops, hots = [], []
    l = logits
    for _ in range(TOP_K):
        mx = jnp.max(l, axis=0, keepdims=True)
        ix = jnp.min(jnp.where(l == mx, erow, N_EXPERTS), axis=0, keepdims=True)
        tops.append(mx)
        hots.append(erow == ix)
        l = jnp.where(erow == ix, -jnp.inf, l)
    es = [jnp.exp(t - tops[0]) for t in tops]
    tot = es[0] + es[1] + es[2] + es[3]
    gate_ref[...] = jnp.concatenate([e / tot for e in es], axis=0)

    ohs = [jnp.where(hh, 1.0, 0.0) for hh in hots]
    cnts = [jnp.sum(oh, axis=1, keepdims=True) for oh in ohs]
    cnt = cnts[0] + cnts[1] + cnts[2] + cnts[3]
    run = jnp.ceil(cnt * (1.0 / RUN_ALIGN)) * RUN_ALIGN
    ee = lax.broadcasted_iota(jnp.int32, (N_EXPERTS, N_EXPERTS), 0)
    ec = lax.broadcasted_iota(jnp.int32, (N_EXPERTS, N_EXPERTS), 1)
    run_row = jnp.sum(jnp.where(ee == ec, jnp.broadcast_to(run, (N_EXPERTS, N_EXPERTS)), 0.0), axis=0, keepdims=True)
    offs = jnp.sum(jnp.where(ec < ee, jnp.broadcast_to(run_row, (N_EXPERTS, N_EXPERTS)), 0.0), axis=1, keepdims=True)
    base = offs
    lpos = []
    for k in range(TOP_K):
        before = jnp.dot(ohs[k].astype(BF16), upper_ref[...], preferred_element_type=F32)
        lpos.append(jnp.sum(ohs[k] * (before + base), axis=0, keepdims=True))
        base = base + cnts[k]
    lpos_ref[...] = jnp.concatenate(lpos, axis=0).astype(jnp.int32)
    cnt_ref[0] = cnt.astype(jnp.int32)


def _moe_router(lay, h, mod, g, w_router, b_router, n_rows):
    tm = lay.tm
    nt = n_rows // tm
    upper = jnp.asarray(np.triu(np.ones((tm, tm), np.float32), 1), dtype=BF16)
    return pl.pallas_call(
        _router_kernel,
        grid=(nt,),
        in_specs=[lay.row_spec(), lay.mod_spec(), _const_spec((1, D)),
                  _const_spec((N_EXPERTS, D)), _const_spec((N_EXPERTS, 1)), _const_spec((tm, tm))],
        out_specs=[pl.BlockSpec((TOP_K, tm), lambda i: (0, i)),
                   pl.BlockSpec((TOP_K, tm), lambda i: (0, i)),
                   pl.BlockSpec((1, N_EXPERTS, 1), lambda i: (i, 0, 0))],
        out_shape=[jax.ShapeDtypeStruct((TOP_K, n_rows), F32),
                   jax.ShapeDtypeStruct((TOP_K, n_rows), jnp.int32),
                   jax.ShapeDtypeStruct((nt, N_EXPERTS, 1), jnp.int32)],
        compiler_params=_cparams(("parallel",)),
        name="moe_router",
    )(h, mod, g.reshape(1, D), w_router.T, b_router.reshape(N_EXPERTS, 1), upper)


def _comp_rows(tm):
    return tm * TOP_K + N_EXPERTS * RUN_ALIGN


def _moe_plan(cnt, n, tm):
    nt = n // tm
    run = (cnt + RUN_ALIGN - 1) // RUN_ALIGN * RUN_ALIGN
    offs = jnp.cumsum(run, axis=1) - run
    tot = jnp.sum(run, axis=0)
    padded = (tot + MOE_BLOCK - 1) // MOE_BLOCK * MOE_BLOCK
    pad_end = jnp.cumsum(padded)
    pad_start = pad_end - padded
    base = pad_start[None, :] + jnp.cumsum(run, axis=0) - run
    rows_max = n * TOP_K + nt * N_EXPERTS * (RUN_ALIGN - 1) + N_EXPERTS * MOE_BLOCK
    n_blocks = -(-rows_max // MOE_BLOCK)
    rows = n_blocks * MOE_BLOCK
    block_expert = jnp.minimum(jnp.searchsorted(pad_end, jnp.arange(n_blocks) * MOE_BLOCK, side='right'),
                               N_EXPERTS - 1).astype(jnp.int32)
    fill_base = jnp.concatenate([pad_start + tot, pad_end[-1:]]).astype(jnp.int32)
    fill_cnt = (jnp.concatenate([padded - tot, rows - pad_end[-1:]]) // RUN_ALIGN).astype(jnp.int32)
    n_used = (pad_end[-1:] // MOE_BLOCK).astype(jnp.int32)
    as3 = lambda a: a.astype(jnp.int32).reshape(nt, 1, N_EXPERTS)
    return as3(base), as3(offs), as3(run // RUN_ALIGN), block_expert, fill_base, fill_cnt, n_used, n_blocks


def _run_copies(groups_ref, src_of, dst_of, sem, wait):
    for e in range(N_EXPERTS):
        def body(c, carry):
            cp = pltpu.make_async_copy(src_of(e, c), dst_of(e, c), sem)
            cp.wait() if wait else cp.start()
            return carry

        lax.fori_loop(0, groups_ref[0, 0, e], body, 0)


def _rows_at(ref, start):
    return ref.at[pl.ds(pl.multiple_of(start, RUN_ALIGN), RUN_ALIGN), :]


def _compact_kernel(fill_base_ref, fill_cnt_ref, base_ref, offs_ref, groups_ref, h_ref, mod_ref, g_ref, lpos_ref,
                    xs_hbm, comp_ref, zero_ref, sem):
    m = mod_ref[0]
    f = _norm_mod(h_ref[...], g_ref[...], m[:, 3 * D:4 * D], m[:, 4 * D:5 * D]).astype(BF16)
    cr, tm = comp_ref.shape[0], h_ref.shape[0]
    q = lax.broadcasted_iota(jnp.int32, (cr, tm), 0)
    lp = lpos_ref[...]
    hit = (lp[0:1, :] == q) | (lp[1:2, :] == q) | (lp[2:3, :] == q) | (lp[3:4, :] == q)
    comp_ref[...] = jnp.dot(jnp.where(hit, 1.0, 0.0).astype(BF16), f, preferred_element_type=F32)

    src = lambda e, c: _rows_at(comp_ref, offs_ref[0, 0, e] + c * RUN_ALIGN)
    dst = lambda e, c: _rows_at(xs_hbm, base_ref[0, 0, e] + c * RUN_ALIGN)
    _run_copies(groups_ref, src, dst, sem, wait=False)

    @pl.when(pl.program_id(0) == 0)
    def _():
        zero_ref[...] = jnp.zeros_like(zero_ref)
        for e in range(N_EXPERTS + 1):
            def fill(c, carry, wait=False, e=e):
                cp = pltpu.make_async_copy(zero_ref, _rows_at(xs_hbm, fill_base_ref[e] + c * RUN_ALIGN), sem)
                cp.wait() if wait else cp.start()
                return carry

            lax.fori_loop(0, fill_cnt_ref[e], fill, 0)
            lax.fori_loop(0, fill_cnt_ref[e], functools.partial(fill, wait=True), 0)

    _run_copies(groups_ref, src, dst, sem, wait=True)


def _tile_tables():
    spec = pl.BlockSpec((1, 1, N_EXPERTS), lambda i, *_: (i, 0, 0), memory_space=pltpu.SMEM)
    return [spec, spec, spec]


def _moe_compact(lay, h, mod, g, lpos, base, offs, groups, fill_base, fill_cnt, n_rows, n_blocks):
    tm = lay.tm
    grid_spec = pltpu.PrefetchScalarGridSpec(
        num_scalar_prefetch=2,
        grid=(n_rows // tm,),
        in_specs=_tile_tables() + [pl.BlockSpec((tm, D), lambda i, *_: (i, 0)),
                                   pl.BlockSpec((1, 1, 6 * D), lambda i, *_: (lay.mod_idx(i), 0, 0)),
                                   pl.BlockSpec((1, D), lambda i, *_: (0, 0)),
                                   pl.BlockSpec((TOP_K, tm), lambda i, *_: (0, i))],
        out_specs=pl.BlockSpec(memory_space=pl.ANY),
        scratch_shapes=[pltpu.VMEM((_comp_rows(tm), D), F32), pltpu.VMEM((RUN_ALIGN, D), F32),
                        pltpu.SemaphoreType.DMA(())],
    )
    return pl.pallas_call(
        _compact_kernel,
        grid_spec=grid_spec,
        out_shape=jax.ShapeDtypeStruct((n_blocks * MOE_BLOCK, D), F32),
        compiler_params=_cparams(("arbitrary",)),
        name="moe_compact",
    )(fill_base, fill_cnt, base, offs, groups, h, mod, g.reshape(1, D), lpos)


def _expert_kernel(be_ref, nu_ref, x_ref, w1_ref, b1_ref, w2_ref, b2_ref, o_ref):
    del be_ref
    i = pl.program_id(0)

    @pl.when(i < nu_ref[0])
    def _():
        hcat = jnp.dot(x_ref[...].astype(BF16), w1_ref[0], preferred_element_type=F32) + b1_ref[0]
        h_glu = jnp.minimum(hcat[:, :D_FF], SWIGLU_LIMIT)
        h_lin = jnp.clip(hcat[:, D_FF:], -SWIGLU_LIMIT, SWIGLU_LIMIT)
        act = h_glu * jax.nn.sigmoid(SWIGLU_ALPHA * h_glu) * (h_lin + 1.0)
        o_ref[...] = jnp.dot(act.astype(BF16), w2_ref[0], preferred_element_type=F32) + b2_ref[0]

    @pl.when(i >= nu_ref[0])
    def _():
        o_ref[...] = jnp.zeros_like(o_ref)


def _moe_experts(xs, block_expert, n_used, n_blocks, w1, b1, w2, b2):
    def xrow(i, be, nu):
        return (jnp.minimum(i, nu[0] - 1), 0)

    grid_spec = pltpu.PrefetchScalarGridSpec(
        num_scalar_prefetch=2,
        grid=(n_blocks,),
        in_specs=[pl.BlockSpec((MOE_BLOCK, D), xrow),
                  pl.BlockSpec((1, D, 2 * D_FF), lambda i, be, nu: (be[i], 0, 0)),
                  pl.BlockSpec((1, 1, 2 * D_FF), lambda i, be, nu: (be[i], 0, 0)),
                  pl.BlockSpec((1, D_FF, D), lambda i, be, nu: (be[i], 0, 0)),
                  pl.BlockSpec((1, 1, D), lambda i, be, nu: (be[i], 0, 0))],
        out_specs=pl.BlockSpec((MOE_BLOCK, D), lambda i, be, nu: (i, 0)),
    )
    return pl.pallas_call(
        _expert_kernel,
        grid_spec=grid_spec,
        out_shape=jax.ShapeDtypeStruct((n_blocks * MOE_BLOCK, D), F32),
        compiler_params=_cparams(("arbitrary",)),
        name="moe_experts",
    )(block_expert, n_used, xs, w1.astype(BF16), b1.reshape(N_EXPERTS, 1, 2 * D_FF),
      w2.astype(BF16), b2.reshape(N_EXPERTS, 1, D))


def _combine_kernel(base_ref, offs_ref, groups_ref, h_ref, gate_ref, lpos_ref, mod_ref, y_hbm, o_ref, comp_ref, sem):
    @pl.when(pl.program_id(0) == 0)
    def _():
        comp_ref[...] = jnp.zeros_like(comp_ref)

    src = lambda e, c: _rows_at(y_hbm, base_ref[0, 0, e] + c * RUN_ALIGN)
    dst = lambda e, c: _rows_at(comp_ref, offs_ref[0, 0, e] + c * RUN_ALIGN)
    _run_copies(groups_ref, src, dst, sem, wait=False)
    cr, tm = comp_ref.shape[0], h_ref.shape[0]
    q = lax.broadcasted_iota(jnp.int32, (tm, cr), 1)
    gates, lp = gate_ref[...], lpos_ref[...]
    pg = jnp.where(lp[:, 0:1] == q, gates[:, 0:1], 0.0)
    for s in range(1, TOP_K):
        pg = pg + jnp.where(lp[:, s:s + 1] == q, gates[:, s:s + 1], 0.0)
    used = (offs_ref[0, 0, N_EXPERTS - 1] + groups_ref[0, 0, N_EXPERTS - 1] * RUN_ALIGN)
    _run_copies(groups_ref, src, dst, sem, wait=True)
    row = lax.broadcasted_iota(jnp.int32, (cr, 1), 0)
    yc = jnp.where(row < used, comp_ref[...], 0.0).astype(BF16)
    y = jnp.dot(pg.astype(BF16), yc, preferred_element_type=F32)
    o_ref[...] = h_ref[...] + mod_ref[0][:, 5 * D:6 * D] * y


def _moe_combine(lay, h, mod, yb, gates_t, lpos_t, base, offs, groups, n_rows):
    tm = lay.tm
    tab = pl.BlockSpec((1, 1, N_EXPERTS), lambda i: (i, 0, 0), memory_space=pltpu.SMEM)
    return pl.pallas_call(
        _combine_kernel,
        grid=(n_rows // tm,),
        in_specs=[tab, tab, tab, lay.row_spec(), lay.row_spec(TOP_K), lay.row_spec(TOP_K), lay.mod_spec(),
                  pl.BlockSpec(memory_space=pl.ANY)],
        out_specs=lay.row_spec(),
        out_shape=jax.ShapeDtypeStruct((n_rows, D), F32),
        scratch_shapes=[pltpu.VMEM((_comp_rows(tm), D), F32), pltpu.SemaphoreType.DMA(())],
        compiler_params=_cparams(("arbitrary",)),
        name="moe_combine",
    )(base, offs, groups, h, gates_t, lpos_t, mod, yb)


def _moe_layer(lay, h, mod, g, w_router, b_router, w1, b1, w2, b2, n_rows):
    gates, lpos, cnt = _moe_router(lay, h, mod, g, w_router, b_router, n_rows)
    base, offs, groups, block_expert, fill_base, fill_cnt, n_used, n_blocks = _moe_plan(cnt[:, :, 0], n_rows, lay.tm)
    xs = _moe_compact(lay, h, mod, g, lpos, base, offs, groups, fill_base, fill_cnt, n_rows, n_blocks)
    yb = _moe_experts(xs, block_expert, n_used, n_blocks, w1, b1, w2, b2)
    return _moe_combine(lay, h, mod, yb, gates.T, lpos.T, base, offs, groups, n_rows)


def _final_kernel(h_ref, g_ref, o_ref):
    h = h_ref[...]
    ms = jnp.mean(h * h, axis=-1, keepdims=True)
    o_ref[...] = h * lax.rsqrt(ms + NORM_EPS) * g_ref[...]


def _final_norm(lay, h, g, n_rows):
    return pl.pallas_call(
        _final_kernel,
        grid=(n_rows // lay.tm,),
        in_specs=[lay.row_spec(), _const_spec((1, D))],
        out_specs=lay.row_spec(),
        out_shape=jax.ShapeDtypeStruct((n_rows, D), F32),
        compiler_params=_cparams(("parallel",)),
        name="final_norm",
    )(h, g.reshape(1, D))


def kernel(x, c, ctx, c_ctx, ada_w, ada_b, norm_g, attn_w_qkv, attn_b_qkv, attn_w_o, attn_b_o, attn_sinks, pool_w, pool_b, pool_scale, rwkv_mix, rwkv_w_rkv, rwkv_w0, rwkv_w1, rwkv_w2, rwkv_a0, rwkv_a1, rwkv_a2, rwkv_g1, rwkv_g2, rwkv_k_k, rwkv_k_a, rwkv_r_k, rwkv_ln_w, rwkv_ln_b, rwkv_w_o, moe_w_router, moe_b_router, moe_w1, moe_b1, moe_w2, moe_b2, final_g):
    B, T, _ = x.shape
    L = ctx.shape[1]
    depth = ada_w.shape[0]
    lay = _Layout(B, T, L)
    h = jnp.concatenate([x.reshape(B * T, D), ctx.reshape(B * L, D)], axis=0)

    n_c = B + 1
    n_c_pad = -(-n_c // 8) * 8
    c_all = jnp.concatenate([c, c_ctx[None, :], jnp.zeros((n_c_pad - n_c, D), F32)], axis=0)
    mods = _ada_mods(c_all, ada_w, ada_b)
    cos, sin = _rope_tables(T, L)

    for layer in range(depth):
        last = layer == depth - 1
        kind, j = layer % 3, layer // 3
        mod = mods[layer].reshape(n_c_pad, 1, 6 * D)
        n_rows = lay.rows_x if last else lay.N
        if kind == 0:
            q, k, v = _attn_qkv(lay, h, mod, norm_g[layer, 0], attn_w_qkv[j], attn_b_qkv[j], cos, sin)
            o = _attention(lay, q, k, v, attn_sinks[j], not last)
            h = _linear_residual(lay, h, o, mod, attn_w_o[j], attn_b_o[j], 2, n_rows)
        elif kind == 1:
            h = _pool_mixer(lay, h, mod, norm_g[layer, 0], pool_w[j], pool_b[j], pool_scale[j], n_rows)
        else:
            p = dict(mix=rwkv_mix[j], w_rkv=rwkv_w_rkv[j], w0=rwkv_w0[j], w1=rwkv_w1[j], w2=rwkv_w2[j],
                     a0=rwkv_a0[j], a1=rwkv_a1[j], a2=rwkv_a2[j], g1=rwkv_g1[j], g2=rwkv_g2[j],
                     k_k=rwkv_k_k[j], k_a=rwkv_k_a[j], r_k=rwkv_r_k[j])
            r, v, kn, g, bonus, lw0, k0, b0, lw1, k1, b1 = _rwkv_proj(lay, h, mod, norm_g[layer, 0], p)
            yf, yb = _rwkv_scan(lay, r, v, kn, lw0, k0, b0, lw1, k1, b1)
            h = _rwkv_out(lay, h, yf, yb, bonus, g, mod, rwkv_ln_w[j], rwkv_ln_b[j], rwkv_w_o[j])
        h = _moe_layer(lay, h, mod, norm_g[layer, 1], moe_w_router[layer], moe_b_router[layer],
                       moe_w1[layer], moe_b1[layer], moe_w2[layer], moe_b2[layer], n_rows)
    out = _final_norm(lay, h, final_g, lay.rows_x)
    return out.reshape(B, T, D)
```

```python
import functools

import jax
import jax.numpy as jnp
import numpy as np
from jax import lax
from jax.experimental import pallas as pl
from jax.experimental.pallas import tpu as pltpu

F32 = jnp.float32
BF16 = jnp.bfloat16

D = 1024
NORM_EPS = 1e-5
GRID_W = 64
HEAD_DIM = 64
N_HEADS = 16
N_KV = 4
Q_DIM = 1024
KV_DIM = 256
QK_DIM = Q_DIM + KV_DIM
KVD_DIM = 2 * KV_DIM
QKD_DIM = Q_DIM + KVD_DIM
WINDOW = 128
QB = 128
ROPE_BASE = 10000.0
POOL_SIZES = (2, 4, 8, 16)
POOL_GW = 256
HALO = 8
GN_EPS = 64e-5
N_EXPERTS = 32
TOP_K = 4
D_FF = 1024
SWIGLU_LIMIT = 7.0
SWIGLU_ALPHA = 1.702
MOE_BLOCK = 256
RW_CHUNK = 64
RW_GROUP = 256
VMEM_LIMIT = 56 * 1024 * 1024


def _cparams(sem):
    return pltpu.CompilerParams(dimension_semantics=sem, vmem_limit_bytes=VMEM_LIMIT)


def _norm_mod(h, g, shift, scale):
    ms = jnp.mean(h * h, axis=-1, keepdims=True)
    y = h * lax.rsqrt(ms + NORM_EPS) * g
    return y * (1.0 + scale) + shift


class _Layout:
    def __init__(self, B, T, L):
        self.B, self.T, self.L = B, T, L
        self.tm = 256 if L % 256 == 0 else 128
        self.rows_x, self.rows_c = B * T, B * L
        self.N = self.rows_x + self.rows_c
        self.tx, self.tc = T // self.tm, L // self.tm
        self.n_xt = self.rows_x // self.tm
        self.n_t = self.N // self.tm

    def mod_idx(self, i):
        return jnp.where(i < self.n_xt, i // self.tx, self.B)

    def mod_spec(self):
        return pl.BlockSpec((1, 1, 6 * D), lambda i: (self.mod_idx(i), 0, 0))

    def row_spec(self, width=D):
        return pl.BlockSpec((self.tm, width), lambda i: (i, 0))

    def seq_tile(self, i):
        j = jnp.where(i < self.n_xt, i % self.tx, (i - self.n_xt) % self.tc)
        n = jnp.where(i < self.n_xt, self.tx, self.tc)
        return j, n

    def halo_specs(self):
        hb = self.tm // HALO
        last = self.N // HALO - 1
        prev = pl.BlockSpec((HALO, D), lambda i: (jnp.maximum(i * hb - 1, 0), 0))
        nxt = pl.BlockSpec((HALO, D), lambda i: (jnp.minimum((i + 1) * hb, last), 0))
        return prev, nxt


def _const_spec(shape):
    nd = len(shape)
    return pl.BlockSpec(shape, lambda *_: (0,) * nd)


def _ada_kernel(c_ref, w_ref, b_ref, o_ref):
    c = c_ref[...]
    s = c * jax.nn.sigmoid(c)
    o_ref[0] = jnp.dot(s, w_ref[0], preferred_element_type=F32, precision=lax.Precision.HIGHEST) + b_ref[0]


def _ada_mods(c_all, ada_w, ada_b):
    depth = ada_w.shape[0]
    R = c_all.shape[0]
    nt = 1536
    return pl.pallas_call(
        _ada_kernel,
        grid=(depth, 6 * D // nt),
        in_specs=[pl.BlockSpec((R, D), lambda l, j: (0, 0)),
                  pl.BlockSpec((1, D, nt), lambda l, j: (l, 0, j)),
                  pl.BlockSpec((1, 1, nt), lambda l, j: (l, 0, j))],
        out_specs=pl.BlockSpec((1, R, nt), lambda l, j: (l, 0, j)),
        out_shape=jax.ShapeDtypeStruct((depth, R, 6 * D), F32),
        compiler_params=_cparams(("parallel", "parallel")),
        name="ada_mod",
    )(c_all, ada_w, ada_b.reshape(depth, 1, 6 * D))


def _qkv_kernel(h_ref, mod_ref, g_ref, w_ref, b_ref, cos_ref, sin_ref, q_ref, k_ref, v_ref):
    m = mod_ref[0]
    a = _norm_mod(h_ref[...], g_ref[...], m[:, 0:D], m[:, D:2 * D])
    qkv = jnp.dot(a.astype(BF16), w_ref[...], preferred_element_type=F32) + b_ref[...]
    qk = qkv[:, :QKD_DIM]
    lane = lax.broadcasted_iota(jnp.int32, (1, QKD_DIM), 1)
    first = (lane % 32) < 16
    rot = jnp.where(first, -pltpu.roll(qk, QKD_DIM - 16, 1), pltpu.roll(qk, 16, 1))
    qk = qk * cos_ref[...] + rot * sin_ref[...]
    q_ref[...] = (qk[:, :Q_DIM] * (HEAD_DIM ** -0.5)).astype(BF16)
    k_ref[...] = qk[:, Q_DIM:].astype(BF16)
    v_ref[...] = qkv[:, QKD_DIM:].astype(BF16)


def _dup_heads(w):
    w4 = w.reshape(w.shape[:-1] + (N_KV, 1, HEAD_DIM))
    return jnp.broadcast_to(w4, w.shape[:-1] + (N_KV, 2, HEAD_DIM)).reshape(w.shape[:-1] + (KVD_DIM,))


def _rope_tables(T, L):
    rows = T // GRID_W
    n_freq = HEAD_DIM // 4
    inv = ROPE_BASE ** (-jnp.arange(n_freq, dtype=F32) / n_freq)
    row_ang = jnp.arange(rows, dtype=F32)[:, None] * inv
    col_ang = jnp.arange(GRID_W, dtype=F32)[:, None] * inv
    ang_r = jnp.broadcast_to(row_ang[:, None, :], (rows, GRID_W, n_freq)).reshape(T, n_freq)
    ang_c = jnp.broadcast_to(col_ang[None, :, :], (rows, GRID_W, n_freq)).reshape(T, n_freq)
    ang = jnp.concatenate([ang_r, ang_r, ang_c, ang_c], axis=-1)
    cos = jnp.concatenate([jnp.cos(ang), jnp.ones((L, HEAD_DIM), F32)], axis=0)
    sin = jnp.concatenate([jnp.sin(ang), jnp.zeros((L, HEAD_DIM), F32)], axis=0)
    reps = QKD_DIM // HEAD_DIM
    return jnp.tile(cos, (1, reps)), jnp.tile(sin, (1, reps))


def _attn_qkv(lay, h, mod, g, w_qkv, b_qkv, cos, sin):
    tm = lay.tm

    def tab_idx(i):
        return (jnp.where(i < lay.n_xt, i % lay.tx, lay.tx + (i - lay.n_xt) % lay.tc), 0)

    wq, wk, wv = w_qkv[:, :Q_DIM], w_qkv[:, Q_DIM:QK_DIM], w_qkv[:, QK_DIM:]
    bq, bk, bv = b_qkv[:Q_DIM], b_qkv[Q_DIM:QK_DIM], b_qkv[QK_DIM:]
    w = jnp.concatenate([wq, _dup_heads(wk), _dup_heads(wv)], axis=1).astype(BF16)
    b = jnp.concatenate([bq, _dup_heads(bk), _dup_heads(bv)]).reshape(1, -1)
    width = Q_DIM + 2 * KVD_DIM
    return pl.pallas_call(
        _qkv_kernel,
        grid=(lay.n_t,),
        in_specs=[lay.row_spec(), lay.mod_spec(), _const_spec((1, D)),
                  _const_spec((D, width)), _const_spec((1, width)),
                  pl.BlockSpec((tm, QKD_DIM), tab_idx), pl.BlockSpec((tm, QKD_DIM), tab_idx)],
        out_specs=[lay.row_spec(Q_DIM), lay.row_spec(KVD_DIM), lay.row_spec(KVD_DIM)],
        out_shape=[jax.ShapeDtypeStruct((lay.N, Q_DIM), BF16),
                   jax.ShapeDtypeStruct((lay.N, KVD_DIM), BF16),
                   jax.ShapeDtypeStruct((lay.N, KVD_DIM), BF16)],
        compiler_params=_cparams(("parallel",)),
        name="attn_qkv",
    )(h, mod, g.reshape(1, D), w, b, cos, sin)


def _attn_heads(q, kk, vv, mask, sink_ref, o_ref):
    R = q.shape[0]
    G = N_HEADS // N_KV
    TILE = 2 * HEAD_DIM
    lo = lax.broadcasted_iota(jnp.int32, (1, TILE), 1) < HEAD_DIM
    order = (0, 2, 1, 3)
    scores = []
    for g in range(N_KV):
        pieces = []
        for j in order:
            hq = g * G + j
            qt = q[:, (hq // 2) * TILE:(hq // 2 + 1) * TILE]
            pieces.append(jnp.where(lo if hq % 2 == 0 else ~lo, qt, jnp.zeros_like(qt)))
        qs = jnp.concatenate(pieces, axis=0)
        scores.append(lax.dot_general(qs, kk[:, g * TILE:(g + 1) * TILE], (((1,), (1,)), ((), ())),
                                      preferred_element_type=F32))
    probs, esinks = [], []
    for g in range(N_KV):
        s = scores[g]
        if mask is not None:
            s = jnp.where(jnp.concatenate([mask] * G, axis=0), s, -jnp.inf)
        sink = jnp.concatenate([jnp.full((R, 1), sink_ref[g * G + j], F32) for j in order], axis=0)
        m = jnp.maximum(jnp.max(s, axis=-1, keepdims=True), sink)
        probs.append(jnp.exp(s - m).astype(BF16))
        esinks.append(jnp.exp(sink - m))
    for g in range(N_KV):
        vd = vv[:, g * TILE:(g + 1) * TILE]
        one = jnp.ones_like(vd)
        p, es = probs[g], esinks[g]
        half = (G // 2) * R
        oe = jnp.dot(p[:half], jnp.where(lo, vd, one), preferred_element_type=F32)
        oo = jnp.dot(p[half:], jnp.where(lo, one, vd), preferred_element_type=F32)
        re = oe / (pltpu.roll(oe, HEAD_DIM, 1) + es[:half])
        ro = oo / (pltpu.roll(oo, HEAD_DIM, 1) + es[half:])
        for t in range(G // 2):
            tile = jnp.where(lo, re[t * R:(t + 1) * R], ro[t * R:(t + 1) * R])
            c = g * (G // 2) + t
            o_ref[:, c * TILE:(c + 1) * TILE] = tile.astype(o_ref.dtype)


def _attn_kernel(sink_ref, q_ref, kp_ref, kc_ref, kn_ref, vp_ref, vc_ref, vn_ref, kx_ref, vx_ref, o_ref, *, T, L):
    i = pl.program_id(1)
    nqb = T // QB

    @pl.when(i < nqb)
    def _():
        kk = jnp.concatenate([kx_ref[...], kp_ref[...], kc_ref[...], kn_ref[...]], axis=0)
        vv = jnp.concatenate([vx_ref[...], vp_ref[...], vc_ref[...], vn_ref[...]], axis=0)
        S = L + 3 * QB
        col = lax.broadcasted_iota(jnp.int32, (QB, S), 1)
        row = lax.broadcasted_iota(jnp.int32, (QB, S), 0)
        rel = col - L - QB - row
        kpos = i * QB - QB + (col - L)
        local_ok = (jnp.abs(rel) <= WINDOW) & (kpos >= 0) & (kpos < T)
        mask = (col < L) | local_ok
        _attn_heads(q_ref[...], kk, vv, mask, sink_ref, o_ref)

    @pl.when(i >= nqb)
    def _():
        _attn_heads(q_ref[...], kx_ref[...], vx_ref[...], None, sink_ref, o_ref)


def _attention(lay, q, k, v, sinks, need_ctx):
    B, T, L = lay.B, lay.T, lay.L
    nqb, ncb = T // QB, L // QB
    cq = lay.rows_x // QB
    cb = lay.rows_x // L
    smem = pl.BlockSpec(memory_space=pltpu.SMEM)

    def cur(b, i):
        return (jnp.where(i < nqb, b * nqb + i, cq + b * ncb + (i - nqb)), 0)

    def prev(b, i):
        return (b * nqb + jnp.clip(i - 1, 0, nqb - 1), 0)

    def mid(b, i):
        return (b * nqb + jnp.minimum(i, nqb - 1), 0)

    def nxt(b, i):
        return (b * nqb + jnp.minimum(i + 1, nqb - 1), 0)

    def ctxb(b, i):
        return (cb + b, 0)

    kv_specs = [pl.BlockSpec((QB, KVD_DIM), prev), pl.BlockSpec((QB, KVD_DIM), mid), pl.BlockSpec((QB, KVD_DIM), nxt)]
    n_out = lay.N if need_ctx else lay.rows_x
    return pl.pallas_call(
        functools.partial(_attn_kernel, T=T, L=L),
        grid=(B, nqb + (ncb if need_ctx else 0)),
        in_specs=[smem, pl.BlockSpec((QB, Q_DIM), cur)] + kv_specs + kv_specs
                 + [pl.BlockSpec((L, KVD_DIM), ctxb), pl.BlockSpec((L, KVD_DIM), ctxb)],
        out_specs=pl.BlockSpec((QB, Q_DIM), cur),
        out_shape=jax.ShapeDtypeStruct((n_out, Q_DIM), BF16),
        compiler_params=_cparams(("parallel", "arbitrary")),
        name="attention",
    )(sinks, q, k, k, k, v, v, v, k, v)


def _linres_kernel(h_ref, x_ref, mod_ref, w_ref, b_ref, o_ref, *, gate_slot):
    gate = mod_ref[0][:, gate_slot * D:(gate_slot + 1) * D]
    y = jnp.dot(x_ref[...], w_ref[...], preferred_element_type=F32) + b_ref[...]
    o_ref[...] = h_ref[...] + gate * y


def _linear_residual(lay, h, x, mod, w, b, gate_slot, n_rows):
    return pl.pallas_call(
        functools.partial(_linres_kernel, gate_slot=gate_slot),
        grid=(n_rows // lay.tm,),
        in_specs=[lay.row_spec(), lay.row_spec(x.shape[1]), lay.mod_spec(),
                  _const_spec(w.shape), _const_spec((1, D))],
        out_specs=lay.row_spec(),
        out_shape=jax.ShapeDtypeStruct((n_rows, D), F32),
        compiler_params=_cparams(("parallel",)),
        name="linear_residual",
    )(h, x, mod, w.astype(BF16), b.reshape(1, D))


def _fill_ext(ext_ref, h_ref, hp_ref, hn_ref, g, shift, scale, j, n, tm):
    a = _norm_mod(h_ref[...], g, shift, scale)
    ap = _norm_mod(hp_ref[...], g, shift, scale)
    an = _norm_mod(hn_ref[...], g, shift, scale)
    ext_ref[0:HALO, :] = jnp.where(j == 0, 0.0, ap)
    ext_ref[HALO:HALO + tm, :] = a
    ext_ref[HALO + tm:2 * HALO + tm, :] = jnp.where(j == n - 1, 0.0, an)
    return a


def _pool_kernel(h_ref, hp_ref, hn_ref, mod_ref, g_ref, w_ref, b_ref, ls_ref, o_ref, ext_ref, *, lay):
    tm = lay.tm
    i = pl.program_id(0)
    j, n = lay.seq_tile(i)
    m = mod_ref[0]
    a = _fill_ext(ext_ref, h_ref, hp_ref, hn_ref, g_ref[...], m[:, 0:D], m[:, D:2 * D], j, n, tm)
    t = j * tm + lax.broadcasted_iota(jnp.int32, (tm, 1), 0)
    n_tok = n * tm
    outs = []
    for gi, size in enumerate(POOL_SIZES):
        sl = slice(gi * POOL_GW, (gi + 1) * POOL_GW)
        acc = ext_ref[HALO - size // 2:HALO - size // 2 + tm, sl]
        for o in range(-size // 2 + 1, size - size // 2):
            acc = acc + ext_ref[HALO + o:HALO + o + tm, sl]
        lo = jnp.maximum(t - size // 2, 0)
        hi = jnp.minimum(t + size - size // 2, n_tok)
        d = acc / (hi - lo).astype(F32) - a[:, sl]
        outs.append(jnp.dot(d.astype(BF16), w_ref[gi], preferred_element_type=F32) + b_ref[gi])
    y = jnp.concatenate(outs, axis=-1) * ls_ref[...]
    o_ref[...] = h_ref[...] + m[:, 2 * D:3 * D] * y


def _pool_mixer(lay, h, mod, g, w, b, ls, n_rows):
    prev, nxt = lay.halo_specs()
    return pl.pallas_call(
        functools.partial(_pool_kernel, lay=lay),
        grid=(n_rows // lay.tm,),
        in_specs=[lay.row_spec(), prev, nxt, lay.mod_spec(), _const_spec((1, D)),
                  _const_spec(w.shape), _const_spec((4, 1, POOL_GW)), _const_spec((1, D))],
        out_specs=lay.row_spec(),
        out_shape=jax.ShapeDtypeStruct((n_rows, D), F32),
        scratch_shapes=[pltpu.VMEM((lay.tm + 2 * HALO, D), F32)],
        compiler_params=_cparams(("parallel",)),
        name="pool_mixer",
    )(h, h, h, mod, g.reshape(1, D), w.astype(BF16), b.reshape(4, 1, POOL_GW), ls.reshape(1, D))


def _seg_sum(x, bd):
    outs = []
    for q in range(D // RW_GROUP):
        xg = x[:, q * RW_GROUP:(q + 1) * RW_GROUP]
        hi = xg.astype(BF16)
        lo = (xg - hi.astype(F32)).astype(BF16)
        outs.append(jnp.dot(hi, bd, preferred_element_type=F32) + jnp.dot(lo, bd, preferred_element_type=F32))
    return jnp.concatenate(outs, axis=-1)


def _rwkv_proj_kernel(h_ref, hp_ref, hn_ref, mod_ref, g_ref, mix_ref, wr_ref, wk_ref, wv_ref, g1_ref, g2_ref,
                      w1_ref, w2_ref, w0_ref, a1_ref, a2_ref, a0_ref, kk_ref, ka_ref, rk_ref, bd_ref,
                      r_out, v_out, kn_out, g_out, bonus_out, lw0_out, k0_out, b0_out, lw1_out, k1_out, b1_out,
                      ext_ref, *, lay):
    tm = lay.tm
    i = pl.program_id(0)
    j, n = lay.seq_tile(i)
    m = mod_ref[0]
    a = _fill_ext(ext_ref, h_ref, hp_ref, hn_ref, g_ref[...], m[:, 0:D], m[:, D:2 * D], j, n, tm)
    prev = ext_ref[HALO - 1:HALO - 1 + tm, :]
    nxt = ext_ref[HALO + 1:HALO + 1 + tm, :]
    xx = 0.5 * (prev + nxt) - a
    x_r, x_w, x_k, x_v, x_a, x_g = [(a + xx * mix_ref[q:q + 1, :]).astype(BF16) for q in range(6)]
    bd = bd_ref[...]
    r = jnp.dot(x_r, wr_ref[...], preferred_element_type=F32)
    k = jnp.dot(x_k, wk_ref[...], preferred_element_type=F32)
    v = jnp.dot(x_v, wv_ref[...], preferred_element_type=F32)
    gl = jnp.dot(x_g, g1_ref[...], preferred_element_type=F32)
    g = jnp.dot(jax.nn.sigmoid(gl).astype(BF16), g2_ref[...], preferred_element_type=F32)
    kk = k * kk_ref[...]
    kn = kk / jnp.maximum(jnp.sqrt(_seg_sum(kk * kk, bd)), 1e-12)
    tw = jnp.tanh(jnp.dot(x_w, w1_ref[...], preferred_element_type=F32))
    aa = jnp.dot(x_a, a1_ref[...], preferred_element_type=F32)
    lane = lax.broadcasted_iota(jnp.int32, (1, tw.shape[1]), 1)
    r_out[...] = r.astype(r_out.dtype)
    v_out[...] = v.astype(v_out.dtype)
    kn_out[...] = kn.astype(kn_out.dtype)
    g_out[...] = g.astype(g_out.dtype)
    rk = rk_ref[...]
    bonus = jnp.zeros_like(v)
    outs = ((lw0_out, k0_out, b0_out), (lw1_out, k1_out, b1_out))
    half = tw.shape[1] // 2
    for d in range(2):
        sel = (lane >= d * half) & (lane < (d + 1) * half)
        wl = w0_ref[d:d + 1, :] + jnp.dot(jnp.where(sel, tw, 0.0).astype(BF16), w2_ref[...], preferred_element_type=F32)
        z = -wl
        w_log = -(jnp.maximum(z, 0.0) + jnp.log(1.0 + jnp.exp(-jnp.abs(z)))) - 0.5
        asig = jax.nn.sigmoid(a0_ref[d:d + 1, :] + jnp.dot(jnp.where(sel, aa, 0.0).astype(BF16), a2_ref[...],
                                                           preferred_element_type=F32))
        k_d = k * (1.0 + (asig - 1.0) * ka_ref[...])
        lw_o, k_o, b_o = outs[d]
        lw_o[...] = -jnp.exp(w_log)
        k_o[...] = k_d.astype(k_o.dtype)
        b_o[...] = (kn * asig).astype(b_o.dtype)
        bonus = bonus + _seg_sum(r * k_d * rk, bd) * v
    bonus_out[...] = bonus.astype(bonus_out.dtype)


def _block_ones():
    idx = np.arange(RW_GROUP) // 64
    return jnp.asarray(idx[:, None] == idx[None, :], dtype=BF16)


def _rwkv_proj(lay, h, mod, g, p):
    prev, nxt = lay.halo_specs()
    cat = lambda w: jnp.concatenate([w[0], w[1]], axis=-1).astype(BF16)
    stack = lambda w: jnp.concatenate([w[0], w[1]], axis=0).astype(BF16)
    args = [h, h, h, mod, g.reshape(1, D), p['mix'],
            p['w_rkv'][0].astype(BF16), p['w_rkv'][1].astype(BF16), p['w_rkv'][2].astype(BF16),
            p['g1'].astype(BF16), p['g2'].astype(BF16),
            cat(p['w1']), stack(p['w2']), p['w0'], cat(p['a1']), stack(p['a2']), p['a0'],
            p['k_k'].reshape(1, D), p['k_a'].reshape(1, D), p['r_k'].reshape(1, D), _block_ones()]
    in_specs = [lay.row_spec(), prev, nxt, lay.mod_spec()] + [_const_spec(x.shape) for x in args[4:]]
    dtypes = [BF16] * 5 + [F32, BF16, BF16] * 2
    return pl.pallas_call(
        functools.partial(_rwkv_proj_kernel, lay=lay),
        grid=(lay.n_t,),
        in_specs=in_specs,
        out_specs=[lay.row_spec()] * len(dtypes),
        out_shape=[jax.ShapeDtypeStruct((lay.N, D), dt) for dt in dtypes],
        scratch_shapes=[pltpu.VMEM((lay.tm + 2 * HALO, D), F32)],
        compiler_params=_cparams(("parallel",)),
        name="rwkv_proj",
    )(*args)


def _rwkv_scan_kernel(rf, vf, nf, lwf, kf, bf, rb, vb, nb, lwb, kb, bb, yf_out, yb_out, z_ref):
    C, G = RW_CHUNK, RW_GROUP
    c = pl.program_id(1)

    @pl.when(c == 0)
    def _():
        z_ref[...] = jnp.zeros_like(z_ref)

    trow = lax.broadcasted_iota(jnp.int32, (C, C), 0)
    tcol = lax.broadcasted_iota(jnp.int32, (C, C), 1)
    ti = lax.broadcasted_iota(jnp.int32, (C, G), 0)
    ii = lax.broadcasted_iota(jnp.int32, (C, G), 1) % C
    brow = lax.broadcasted_iota(jnp.int32, (G, G), 0)
    bcol = lax.broadcasted_iota(jnp.int32, (G, G), 1)
    m_bd_t = (brow // C) == (bcol // 64)
    m_bd_f = (brow // 64) == (bcol // 64)
    eye_g = brow == bcol

    def bd(x):
        return jnp.where(m_bd_t, jnp.concatenate([x] * (G // C), axis=0), 0.0).astype(BF16)

    def mm(a, b):
        return jnp.dot(a, b, preferred_element_type=F32)

    dirs = ((rf, vf, nf, lwf, kf, bf, yf_out, False), (rb, vb, nb, lwb, kb, bb, yb_out, True))
    ch = []
    for d, (r_ref, v_ref, n_ref, lw_ref, k_ref, b_ref, y_out, rev) in enumerate(dirs):
        tri = (tcol >= trow) if rev else (tcol <= trow)
        strict = (ii > ti) if rev else (ii < ti)
        incl = (ii >= ti) if rev else (ii <= ti)
        lw = lw_ref[...]
        lc = jnp.dot(tri.astype(F32), lw, preferred_element_type=F32, precision=lax.Precision.HIGHEST)
        ltot = lc[0:1, :] if rev else lc[C - 1:C, :]
        e_lc = jnp.exp(lc)
        ie_lc = jnp.exp(-lc)
        e_le = jnp.exp(lc - lw)
        e_bar = jnp.exp(ltot - lc)
        e_tot = jnp.exp(ltot)
        kvec, bvec = k_ref[...], b_ref[...]
        a_t = -n_ref[...] * e_le
        r_t = r_ref[...] * e_lc
        b_t = bvec * ie_lc
        k_t = kvec * ie_lc
        b_b = bvec * e_bar
        k_b = kvec * e_bar
        vall = v_ref[...]
        for g in range(D // G):
            sl = slice(g * G, (g + 1) * G)
            ch.append(dict(
                d=d, g=g, sl=sl, y_out=y_out, strict=strict, incl=incl, v=vall[:, sl], e_tot=e_tot[:, sl],
                ar=jnp.concatenate([a_t[:, sl], r_t[:, sl]], axis=0).astype(BF16),
                bk=jnp.concatenate([bd(b_t[:, sl]), bd(k_t[:, sl])], axis=0),
                lhs_t=jnp.concatenate([b_b[:, sl], k_b[:, sl]], axis=0).astype(BF16)))

    for q in ch:
        aa = lax.dot_general(q['ar'], q['bk'], (((1,), (1,)), ((), ())), preferred_element_type=F32)
        q['a_ab'] = jnp.where(q['strict'], aa[:C, :G], 0.0)
        q['a_ak'] = jnp.where(q['strict'], aa[:C, G:], 0.0)
        q['m_rb'] = jnp.where(q['incl'], aa[C:, :G], 0.0).astype(BF16)
        q['m_rk'] = jnp.where(q['incl'], aa[C:, G:], 0.0)
    for q in ch:
        q['x'] = jnp.where(ii == ti, 1.0, 0.0) + q['a_ab']
        q['p'] = mm(q['a_ab'].astype(BF16), bd(q['a_ab']))
    for q in ch:
        q['wv'] = mm(jnp.concatenate([q['a_ak'], q['m_rk']], axis=0).astype(BF16), bd(q['v']))
        q['z'] = z_ref[q['d'], q['g']]
        q['az'] = mm(q['ar'], q['z'].astype(BF16))
    n_sq = int(np.log2(C)) - 1
    for s in range(n_sq):
        last = s == n_sq - 1
        for q in ch:
            lhs = q['x'] if last else jnp.concatenate([q['x'], q['p']], axis=0)
            xp = mm(lhs.astype(BF16), bd(q['p']))
            q['x'] = q['x'] + xp[:C]
            if not last:
                q['p'] = xp[C:]
    for q in ch:
        q['u'] = mm(q['x'].astype(BF16), bd(q['az'][:C] + q['wv'][:C]))
    for q in ch:
        y = q['az'][C:] + q['wv'][C:] + mm(q['m_rb'], bd(q['u']))
        q['y_out'][:, q['sl']] = y
        rhs_t = jnp.concatenate([q['u'], q['v']], axis=0).astype(BF16)
        zu = lax.dot_general(q['lhs_t'], rhs_t, (((0,), (0,)), ((), ())), preferred_element_type=F32)
        decay = jnp.sum(jnp.where(eye_g, jnp.broadcast_to(q['e_tot'], (G, G)), 0.0), axis=1, keepdims=True)
        z_ref[q['d'], q['g']] = jnp.where(m_bd_f, decay * q['z'] + zu, 0.0)


def _rwkv_scan(lay, r, v, kn, lw0, k0, b0, lw1, k1, b1):
    B, T, L = lay.B, lay.T, lay.L
    C = RW_CHUNK
    cc, cx = L // C, T // C
    nc = cc + cx
    ctx0 = lay.rows_x // C

    def fwd(b, c):
        return (jnp.where(c < cc, ctx0 + b * cc + c, b * cx + c - cc), 0)

    def bwd(b, c):
        return (jnp.where(c < cc, ctx0 + b * cc + (cc - 1 - c), b * cx + (nc - 1 - c)), 0)

    sf, sb = pl.BlockSpec((C, D), fwd), pl.BlockSpec((C, D), bwd)
    return pl.pallas_call(
        _rwkv_scan_kernel,
        grid=(B, nc),
        in_specs=[sf] * 6 + [sb] * 6,
        out_specs=[sf, sb],
        out_shape=[jax.ShapeDtypeStruct((lay.N, D), F32)] * 2,
        scratch_shapes=[pltpu.VMEM((2, D // RW_GROUP, RW_GROUP, RW_GROUP), F32)],
        compiler_params=_cparams(("parallel", "arbitrary")),
        name="rwkv_scan",
    )(r, v, kn, lw0, k0, b0, r, v, kn, lw1, k1, b1)


def _rwkv_out_kernel(h_ref, yf_ref, yb_ref, bonus_ref, g_ref, mod_ref, lnw_ref, lnb_ref, wo_ref, bd_ref, o_ref):
    bd = bd_ref[...]
    y = yf_ref[...] + yb_ref[...]
    mu = _seg_sum(y, bd) * (1.0 / 64)
    yc = y - mu
    var = _seg_sum(yc * yc, bd) * (1.0 / 64)
    yn = yc * lax.rsqrt(var + GN_EPS) * lnw_ref[...] + lnb_ref[...]
    out = (yn + bonus_ref[...]) * g_ref[...]
    res = jnp.dot(out.astype(BF16), wo_ref[...], preferred_element_type=F32)
    o_ref[...] = h_ref[...] + mod_ref[0][:, 2 * D:3 * D] * res


def _rwkv_out(lay, h, yf, yb, bonus, g, mod, ln_w, ln_b, w_o):
    return pl.pallas_call(
        _rwkv_out_kernel,
        grid=(lay.n_t,),
        in_specs=[lay.row_spec()] * 5 + [lay.mod_spec(), _const_spec((1, D)), _const_spec((1, D)),
                                          _const_spec((D, D)), _const_spec((RW_GROUP, RW_GROUP))],
        out_specs=lay.row_spec(),
        out_shape=jax.ShapeDtypeStruct((lay.N, D), F32),
        compiler_params=_cparams(("parallel",)),
        name="rwkv_out",
    )(h, yf, yb, bonus, g, mod, ln_w.reshape(1, D), ln_b.reshape(1, D), w_o.astype(BF16), _block_ones())


RUN_ALIGN = 8


def _router_kernel(h_ref, mod_ref, g_ref, wr_ref, br_ref, upper_ref, gate_ref, lpos_ref, cnt_ref):
    m = mod_ref[0]
    f = _norm_mod(h_ref[...], g_ref[...], m[:, 3 * D:4 * D], m[:, 4 * D:5 * D])
    logits = lax.dot_general(wr_ref[...], f, (((1,), (1,)), ((), ())), preferred_element_type=F32,
                             precision=lax.Precision.HIGHEST) + br_ref[...]
    erow = lax.broadcasted_iota(jnp.int32, logits.shape, 0)
    tops, hots = [], []
    l = logits
    for _ in range(TOP_K):
        mx = jnp.max(l, axis=0, keepdims=True)
        ix = jnp.min(jnp.where(l == mx, erow, N_EXPERTS), axis=0, keepdims=True)
        tops.append(mx)
        hots.append(erow == ix)
        l = jnp.where(erow == ix, -jnp.inf, l)
    es = [jnp.exp(t - tops[0]) for t in tops]
    tot = es[0] + es[1] + es[2] + es[3]
    gate_ref[...] = jnp.concatenate([e / tot for e in es], axis=0)

    ohs = [jnp.where(hh, 1.0, 0.0) for hh in hots]
    cnts = [jnp.sum(oh, axis=1, keepdims=True) for oh in ohs]
    cnt = cnts[0] + cnts[1] + cnts[2] + cnts[3]
    run = jnp.ceil(cnt * (1.0 / RUN_ALIGN)) * RUN_ALIGN
    ee = lax.broadcasted_iota(jnp.int32, (N_EXPERTS, N_EXPERTS), 0)
    ec = lax.broadcasted_iota(jnp.int32, (N_EXPERTS, N_EXPERTS), 1)
    run_row = jnp.sum(jnp.where(ee == ec, jnp.broadcast_to(run, (N_EXPERTS, N_EXPERTS)), 0.0), axis=0, keepdims=True)
    offs = jnp.sum(jnp.where(ec < ee, jnp.broadcast_to(run_row, (N_EXPERTS, N_EXPERTS)), 0.0), axis=1, keepdims=True)
    base = offs
    lpos = []
    for k in range(TOP_K):
        before = jnp.dot(ohs[k].astype(BF16), upper_ref[...], preferred_element_type=F32)
        lpos.append(jnp.sum(ohs[k] * (before + base), axis=0, keepdims=True))
        base = base + cnts[k]
    lpos_ref[...] = jnp.concatenate(lpos, axis=0).astype(jnp.int32)
    cnt_ref[0] = cnt.astype(jnp.int32)


def _moe_router(lay, h, mod, g, w_router, b_router, n_rows):
    tm = lay.tm
    nt = n_rows // tm
    upper = jnp.asarray(np.triu(np.ones((tm, tm), np.float32), 1), dtype=BF16)
    return pl.pallas_call(
        _router_kernel,
        grid=(nt,),
        in_specs=[lay.row_spec(), lay.mod_spec(), _const_spec((1, D)),
                  _const_spec((N_EXPERTS, D)), _const_spec((N_EXPERTS, 1)), _const_spec((tm, tm))],
        out_specs=[pl.BlockSpec((TOP_K, tm), lambda i: (0, i)),
                   pl.BlockSpec((TOP_K, tm), lambda i: (0, i)),
                   pl.BlockSpec((1, N_EXPERTS, 1), lambda i: (i, 0, 0))],
        out_shape=[jax.ShapeDtypeStruct((TOP_K, n_rows), F32),
                   jax.ShapeDtypeStruct((TOP_K, n_rows), jnp.int32),
                   jax.ShapeDtypeStruct((nt, N_EXPERTS, 1), jnp.int32)],
        compiler_params=_cparams(("parallel",)),
        name="moe_router",
    )(h, mod, g.reshape(1, D), w_router.T, b_router.reshape(N_EXPERTS, 1), upper)


def _comp_rows(tm):
    return tm * TOP_K + N_EXPERTS * RUN_ALIGN


def _moe_plan(cnt, n, tm):
    nt = n // tm
    def excl_sum(a, axis):
        k = a.shape[axis]
        before = np.arange(k)[:, None] > np.arange(k)[None, :]
        if axis == 0:
            return jnp.sum(jnp.where(before[:, :, None], a[None, :, :], 0), axis=1)
        return jnp.sum(jnp.where(before[None, :, :], a[:, None, :], 0), axis=2)

    run = (cnt + RUN_ALIGN - 1) // RUN_ALIGN * RUN_ALIGN
    offs = excl_sum(run, 1)
    tot = jnp.sum(run, axis=0)
    padded = (tot + MOE_BLOCK - 1) // MOE_BLOCK * MOE_BLOCK
    pad_start = excl_sum(padded[None, :], 1)[0]
    pad_end = pad_start + padded
    base = pad_start[None, :] + excl_sum(run, 0)
    rows_max = n * TOP_K + nt * N_EXPERTS * (RUN_ALIGN - 1) + N_EXPERTS * MOE_BLOCK
    n_blocks = -(-rows_max // MOE_BLOCK)
    rows = n_blocks * MOE_BLOCK
    blk_row = jnp.arange(n_blocks, dtype=jnp.int32) * MOE_BLOCK
    block_expert = jnp.minimum(jnp.sum((pad_end[None, :] <= blk_row[:, None]).astype(jnp.int32), axis=1),
                               N_EXPERTS - 1).astype(jnp.int32)
    fill_base = jnp.concatenate([pad_start + tot, pad_end[-1:]]).astype(jnp.int32)
    fill_cnt = (jnp.concatenate([padded - tot, rows - pad_end[-1:]]) // RUN_ALIGN).astype(jnp.int32)
    n_used = (pad_end[-1:] // MOE_BLOCK).astype(jnp.int32)
    as3 = lambda a: a.astype(jnp.int32).reshape(nt, 1, N_EXPERTS)
    return as3(base), as3(offs), as3(run // RUN_ALIGN), block_expert, fill_base, fill_cnt, n_used, n_blocks


def _run_copies(groups_ref, src_of, dst_of, sem, wait):
    for e in range(N_EXPERTS):
        def body(c, carry):
            cp = pltpu.make_async_copy(src_of(e, c), dst_of(e, c), sem)
            cp.wait() if wait else cp.start()
            return carry

        lax.fori_loop(0, groups_ref[0, 0, e], body, 0)


def _rows_at(ref, start):
    return ref.at[pl.ds(pl.multiple_of(start, RUN_ALIGN), RUN_ALIGN), :]


def _compact_kernel(fill_base_ref, fill_cnt_ref, base_ref, offs_ref, groups_ref, h_ref, mod_ref, g_ref, lpos_ref,
                    xs_hbm, comp_ref, zero_ref, sem):
    m = mod_ref[0]
    f = _norm_mod(h_ref[...], g_ref[...], m[:, 3 * D:4 * D], m[:, 4 * D:5 * D]).astype(BF16)
    cr, tm = comp_ref.shape[0], h_ref.shape[0]
    q = lax.broadcasted_iota(jnp.int32, (cr, tm), 0)
    lp = lpos_ref[...]
    hit = (lp[0:1, :] == q) | (lp[1:2, :] == q) | (lp[2:3, :] == q) | (lp[3:4, :] == q)
    comp_ref[...] = jnp.dot(jnp.where(hit, 1.0, 0.0).astype(BF16), f, preferred_element_type=F32)

    src = lambda e, c: _rows_at(comp_ref, offs_ref[0, 0, e] + c * RUN_ALIGN)
    dst = lambda e, c: _rows_at(xs_hbm, base_ref[0, 0, e] + c * RUN_ALIGN)
    _run_copies(groups_ref, src, dst, sem, wait=False)

    @pl.when(pl.program_id(0) == 0)
    def _():
        zero_ref[...] = jnp.zeros_like(zero_ref)
        for e in range(N_EXPERTS + 1):
            def fill(c, carry, wait=False, e=e):
                cp = pltpu.make_async_copy(zero_ref, _rows_at(xs_hbm, fill_base_ref[e] + c * RUN_ALIGN), sem)
                cp.wait() if wait else cp.start()
                return carry

            lax.fori_loop(0, fill_cnt_ref[e], fill, 0)
            lax.fori_loop(0, fill_cnt_ref[e], functools.partial(fill, wait=True), 0)

    _run_copies(groups_ref, src, dst, sem, wait=True)


def _tile_tables():
    spec = pl.BlockSpec((1, 1, N_EXPERTS), lambda i, *_: (i, 0, 0), memory_space=pltpu.SMEM)
    return [spec, spec, spec]


def _moe_compact(lay, h, mod, g, lpos, base, offs, groups, fill_base, fill_cnt, n_rows, n_blocks):
    tm = lay.tm
    grid_spec = pltpu.PrefetchScalarGridSpec(
        num_scalar_prefetch=2,
        grid=(n_rows // tm,),
        in_specs=_tile_tables() + [pl.BlockSpec((tm, D), lambda i, *_: (i, 0)),
                                   pl.BlockSpec((1, 1, 6 * D), lambda i, *_: (lay.mod_idx(i), 0, 0)),
                                   pl.BlockSpec((1, D), lambda i, *_: (0, 0)),
                                   pl.BlockSpec((TOP_K, tm), lambda i, *_: (0, i))],
        out_specs=pl.BlockSpec(memory_space=pl.ANY),
        scratch_shapes=[pltpu.VMEM((_comp_rows(tm), D), F32), pltpu.VMEM((RUN_ALIGN, D), F32),
                        pltpu.SemaphoreType.DMA(())],
    )
    return pl.pallas_call(
        _compact_kernel,
        grid_spec=grid_spec,
        out_shape=jax.ShapeDtypeStruct((n_blocks * MOE_BLOCK, D), F32),
        compiler_params=_cparams(("arbitrary",)),
        name="moe_compact",
    )(fill_base, fill_cnt, base, offs, groups, h, mod, g.reshape(1, D), lpos)


def _expert_kernel(be_ref, nu_ref, x_ref, w1_ref, b1_ref, w2_ref, b2_ref, o_ref):
    del be_ref
    i = pl.program_id(0)

    @pl.when(i < nu_ref[0])
    def _():
        hcat = jnp.dot(x_ref[...].astype(BF16), w1_ref[0], preferred_element_type=F32) + b1_ref[0]
        h_glu = jnp.minimum(hcat[:, :D_FF], SWIGLU_LIMIT)
        h_lin = jnp.clip(hcat[:, D_FF:], -SWIGLU_LIMIT, SWIGLU_LIMIT)
        act = h_glu * jax.nn.sigmoid(SWIGLU_ALPHA * h_glu) * (h_lin + 1.0)
        o_ref[...] = jnp.dot(act.astype(BF16), w2_ref[0], preferred_element_type=F32) + b2_ref[0]

    @pl.when(i >= nu_ref[0])
    def _():
        o_ref[...] = jnp.zeros_like(o_ref)


def _moe_experts(xs, block_expert, n_used, n_blocks, w1, b1, w2, b2):
    def xrow(i, be, nu):
        return (jnp.minimum(i, nu[0] - 1), 0)

    grid_spec = pltpu.PrefetchScalarGridSpec(
        num_scalar_prefetch=2,
        grid=(n_blocks,),
        in_specs=[pl.BlockSpec((MOE_BLOCK, D), xrow),
                  pl.BlockSpec((1, D, 2 * D_FF), lambda i, be, nu: (be[i], 0, 0)),
                  pl.BlockSpec((1, 1, 2 * D_FF), lambda i, be, nu: (be[i], 0, 0)),
                  pl.BlockSpec((1, D_FF, D), lambda i, be, nu: (be[i], 0, 0)),
                  pl.BlockSpec((1, 1, D), lambda i, be, nu: (be[i], 0, 0))],
        out_specs=pl.BlockSpec((MOE_BLOCK, D), lambda i, be, nu: (i, 0)),
    )
    return pl.pallas_call(
        _expert_kernel,
        grid_spec=grid_spec,
        out_shape=jax.ShapeDtypeStruct((n_blocks * MOE_BLOCK, D), F32),
        compiler_params=_cparams(("arbitrary",)),
        name="moe_experts",
    )(block_expert, n_used, xs, w1.astype(BF16), b1.reshape(N_EXPERTS, 1, 2 * D_FF),
      w2.astype(BF16), b2.reshape(N_EXPERTS, 1, D))


def _combine_kernel(base_ref, offs_ref, groups_ref, h_ref, gate_ref, lpos_ref, mod_ref, y_hbm, o_ref, comp_ref, sem):
    @pl.when(pl.program_id(0) == 0)
    def _():
        comp_ref[...] = jnp.zeros_like(comp_ref)

    src = lambda e, c: _rows_at(y_hbm, base_ref[0, 0, e] + c * RUN_ALIGN)
    dst = lambda e, c: _rows_at(comp_ref, offs_ref[0, 0, e] + c * RUN_ALIGN)
    _run_copies(groups_ref, src, dst, sem, wait=False)
    cr, tm = comp_ref.shape[0], h_ref.shape[0]
    q = lax.broadcasted_iota(jnp.int32, (tm, cr), 1)
    gates, lp = gate_ref[...], lpos_ref[...]
    pg = jnp.where(lp[:, 0:1] == q, gates[:, 0:1], 0.0)
    for s in range(1, TOP_K):
        pg = pg + jnp.where(lp[:, s:s + 1] == q, gates[:, s:s + 1], 0.0)
    used = (offs_ref[0, 0, N_EXPERTS - 1] + groups_ref[0, 0, N_EXPERTS - 1] * RUN_ALIGN)
    _run_copies(groups_ref, src, dst, sem, wait=True)
    row = lax.broadcasted_iota(jnp.int32, (cr, 1), 0)
    yc = jnp.where(row < used, comp_ref[...], 0.0).astype(BF16)
    y = jnp.dot(pg.astype(BF16), yc, preferred_element_type=F32)
    o_ref[...] = h_ref[...] + mod_ref[0][:, 5 * D:6 * D] * y


def _moe_combine(lay, h, mod, yb, gates_t, lpos_t, base, offs, groups, n_rows):
    tm = lay.tm
    tab = pl.BlockSpec((1, 1, N_EXPERTS), lambda i: (i, 0, 0), memory_space=pltpu.SMEM)
    return pl.pallas_call(
        _combine_kernel,
        grid=(n_rows // tm,),
        in_specs=[tab, tab, tab, lay.row_spec(), lay.row_spec(TOP_K), lay.row_spec(TOP_K), lay.mod_spec(),
                  pl.BlockSpec(memory_space=pl.ANY)],
        out_specs=lay.row_spec(),
        out_shape=jax.ShapeDtypeStruct((n_rows, D), F32),
        scratch_shapes=[pltpu.VMEM((_comp_rows(tm), D), F32), pltpu.SemaphoreType.DMA(())],
        compiler_params=_cparams(("arbitrary",)),
        name="moe_combine",
    )(base, offs, groups, h, gates_t, lpos_t, mod, yb)


def _moe_layer(lay, h, mod, g, w_router, b_router, w1, b1, w2, b2, n_rows):
    gates, lpos, cnt = _moe_router(lay, h, mod, g, w_router, b_router, n_rows)
    base, offs, groups, block_expert, fill_base, fill_cnt, n_used, n_blocks = _moe_plan(cnt[:, :, 0], n_rows, lay.tm)
    xs = _moe_compact(lay, h, mod, g, lpos, base, offs, groups, fill_base, fill_cnt, n_rows, n_blocks)
    yb = _moe_experts(xs, block_expert, n_used, n_blocks, w1, b1, w2, b2)
    return _moe_combine(lay, h, mod, yb, gates.T, lpos.T, base, offs, groups, n_rows)


def _final_kernel(h_ref, g_ref, o_ref):
    h = h_ref[...]
    ms = jnp.mean(h * h, axis=-1, keepdims=True)
    o_ref[...] = h * lax.rsqrt(ms + NORM_EPS) * g_ref[...]


def _final_norm(lay, h, g, n_rows):
    return pl.pallas_call(
        _final_kernel,
        grid=(n_rows // lay.tm,),
        in_specs=[lay.row_spec(), _const_spec((1, D))],
        out_specs=lay.row_spec(),
        out_shape=jax.ShapeDtypeStruct((n_rows, D), F32),
        compiler_params=_cparams(("parallel",)),
        name="final_norm",
    )(h, g.reshape(1, D))


def kernel(x, c, ctx, c_ctx, ada_w, ada_b, norm_g, attn_w_qkv, attn_b_qkv, attn_w_o, attn_b_o, attn_sinks, pool_w, pool_b, pool_scale, rwkv_mix, rwkv_w_rkv, rwkv_w0, rwkv_w1, rwkv_w2, rwkv_a0, rwkv_a1, rwkv_a2, rwkv_g1, rwkv_g2, rwkv_k_k, rwkv_k_a, rwkv_r_k, rwkv_ln_w, rwkv_ln_b, rwkv_w_o, moe_w_router, moe_b_router, moe_w1, moe_b1, moe_w2, moe_b2, final_g):
    B, T, _ = x.shape
    L = ctx.shape[1]
    depth = ada_w.shape[0]
    lay = _Layout(B, T, L)
    h = jnp.concatenate([x.reshape(B * T, D), ctx.reshape(B * L, D)], axis=0)

    n_c = B + 1
    n_c_pad = -(-n_c // 8) * 8
    c_all = jnp.concatenate([c, c_ctx[None, :], jnp.zeros((n_c_pad - n_c, D), F32)], axis=0)
    mods = _ada_mods(c_all, ada_w, ada_b)
    cos, sin = _rope_tables(T, L)

    for layer in range(depth):
        last = layer == depth - 1
        kind, j = layer % 3, layer // 3
        mod = mods[layer].reshape(n_c_pad, 1, 6 * D)
        n_rows = lay.rows_x if last else lay.N
        if kind == 0:
            q, k, v = _attn_qkv(lay, h, mod, norm_g[layer, 0], attn_w_qkv[j], attn_b_qkv[j], cos, sin)
            o = _attention(lay, q, k, v, attn_sinks[j], not last)
            h = _linear_residual(lay, h, o, mod, attn_w_o[j], attn_b_o[j], 2, n_rows)
        elif kind == 1:
            h = _pool_mixer(lay, h, mod, norm_g[layer, 0], pool_w[j], pool_b[j], pool_scale[j], n_rows)
        else:
            p = dict(mix=rwkv_mix[j], w_rkv=rwkv_w_rkv[j], w0=rwkv_w0[j], w1=rwkv_w1[j], w2=rwkv_w2[j],
                     a0=rwkv_a0[j], a1=rwkv_a1[j], a2=rwkv_a2[j], g1=rwkv_g1[j], g2=rwkv_g2[j],
                     k_k=rwkv_k_k[j], k_a=rwkv_k_a[j], r_k=rwkv_r_k[j])
            r, v, kn, g, bonus, lw0, k0, b0, lw1, k1, b1 = _rwkv_proj(lay, h, mod, norm_g[layer, 0], p)
            yf, yb = _rwkv_scan(lay, r, v, kn, lw0, k0, b0, lw1, k1, b1)
            h = _rwkv_out(lay, h, yf, yb, bonus, g, mod, rwkv_ln_w[j], rwkv_ln_b[j], rwkv_w_o[j])
        h = _moe_layer(lay, h, mod, norm_g[layer, 1], moe_w_router[layer], moe_b_router[layer],
                       moe_w1[layer], moe_b1[layer], moe_w2[layer], moe_b2[layer], n_rows)
    out = _final_norm(lay, h, final_g, lay.rows_x)
    return out.reshape(B, T, D)
```

```python
import functools

import jax
import jax.numpy as jnp
import numpy as np
from jax import lax
from jax.experimental import pallas as pl
from jax.experimental.pallas import tpu as pltpu

F32 = jnp.float32
BF16 = jnp.bfloat16

D = 1024
NORM_EPS = 1e-5
GRID_W = 64
HEAD_DIM = 64
N_HEADS = 16
N_KV = 4
Q_DIM = 1024
KV_DIM = 256
QK_DIM = Q_DIM + KV_DIM
KVD_DIM = 2 * KV_DIM
QKD_DIM = Q_DIM + KVD_DIM
WINDOW = 128
QB = 128
ROPE_BASE = 10000.0
POOL_SIZES = (2, 4, 8, 16)
POOL_GW = 256
HALO = 8
GN_EPS = 64e-5
N_EXPERTS = 32
TOP_K = 4
D_FF = 1024
SWIGLU_LIMIT = 7.0
SWIGLU_ALPHA = 1.702
MOE_BLOCK = 512
RW_CHUNK = 64
RW_GROUP = 256
VMEM_LIMIT = 56 * 1024 * 1024


def _cparams(sem):
    return pltpu.CompilerParams(dimension_semantics=sem, vmem_limit_bytes=VMEM_LIMIT)


def _norm_mod(h, g, shift, scale):
    ms = jnp.mean(h * h, axis=-1, keepdims=True)
    y = h * lax.rsqrt(ms + NORM_EPS) * g
    return y * (1.0 + scale) + shift


class _Layout:
    def __init__(self, B, T, L):
        self.B, self.T, self.L = B, T, L
        self.tm = 256 if L % 256 == 0 else 128
        self.rows_x, self.rows_c = B * T, B * L
        self.N = self.rows_x + self.rows_c
        self.tx, self.tc = T // self.tm, L // self.tm
        self.n_xt = self.rows_x // self.tm
        self.n_t = self.N // self.tm

    def mod_idx(self, i):
        return jnp.where(i < self.n_xt, i // self.tx, self.B)

    def mod_spec(self):
        return pl.BlockSpec((1, 1, 6 * D), lambda i: (self.mod_idx(i), 0, 0))

    def row_spec(self, width=D):
        return pl.BlockSpec((self.tm, width), lambda i: (i, 0))

    def seq_tile(self, i):
        j = jnp.where(i < self.n_xt, i % self.tx, (i - self.n_xt) % self.tc)
        n = jnp.where(i < self.n_xt, self.tx, self.tc)
        return j, n

    def halo_specs(self):
        hb = self.tm // HALO
        last = self.N // HALO - 1
        prev = pl.BlockSpec((HALO, D), lambda i: (jnp.maximum(i * hb - 1, 0), 0))
        nxt = pl.BlockSpec((HALO, D), lambda i: (jnp.minimum((i + 1) * hb, last), 0))
        return prev, nxt


def _const_spec(shape):
    nd = len(shape)
    return pl.BlockSpec(shape, lambda *_: (0,) * nd)


def _ada_kernel(c_ref, w_ref, b_ref, o_ref):
    c = c_ref[...]
    s = c * jax.nn.sigmoid(c)
    o_ref[0] = jnp.dot(s, w_ref[0], preferred_element_type=F32, precision=lax.Precision.HIGHEST) + b_ref[0]


def _ada_mods(c_all, ada_w, ada_b):
    depth = ada_w.shape[0]
    R = c_all.shape[0]
    nt = 1536
    return pl.pallas_call(
        _ada_kernel,
        grid=(depth, 6 * D // nt),
        in_specs=[pl.BlockSpec((R, D), lambda l, j: (0, 0)),
                  pl.BlockSpec((1, D, nt), lambda l, j: (l, 0, j)),
                  pl.BlockSpec((1, 1, nt), lambda l, j: (l, 0, j))],
        out_specs=pl.BlockSpec((1, R, nt), lambda l, j: (l, 0, j)),
        out_shape=jax.ShapeDtypeStruct((depth, R, 6 * D), F32),
        compiler_params=_cparams(("parallel", "parallel")),
        name="ada_mod",
    )(c_all, ada_w, ada_b.reshape(depth, 1, 6 * D))


def _qkv_kernel(h_ref, mod_ref, g_ref, w_ref, b_ref, cos_ref, sin_ref, q_ref, k_ref, v_ref):
    m = mod_ref[0]
    a = _norm_mod(h_ref[...], g_ref[...], m[:, 0:D], m[:, D:2 * D])
    qkv = jnp.dot(a.astype(BF16), w_ref[...], preferred_element_type=F32) + b_ref[...]
    qk = qkv[:, :QKD_DIM]
    lane = lax.broadcasted_iota(jnp.int32, (1, QKD_DIM), 1)
    first = (lane % 32) < 16
    rot = jnp.where(first, -pltpu.roll(qk, QKD_DIM - 16, 1), pltpu.roll(qk, 16, 1))
    qk = qk * cos_ref[...] + rot * sin_ref[...]
    q_ref[...] = (qk[:, :Q_DIM] * (HEAD_DIM ** -0.5)).astype(BF16)
    k_ref[...] = qk[:, Q_DIM:].astype(BF16)
    v_ref[...] = qkv[:, QKD_DIM:].astype(BF16)


def _dup_heads(w):
    w4 = w.reshape(w.shape[:-1] + (N_KV, 1, HEAD_DIM))
    return jnp.broadcast_to(w4, w.shape[:-1] + (N_KV, 2, HEAD_DIM)).reshape(w.shape[:-1] + (KVD_DIM,))


def _rope_tables(T, L):
    rows = T // GRID_W
    n_freq = HEAD_DIM // 4
    inv = ROPE_BASE ** (-jnp.arange(n_freq, dtype=F32) / n_freq)
    row_ang = jnp.arange(rows, dtype=F32)[:, None] * inv
    col_ang = jnp.arange(GRID_W, dtype=F32)[:, None] * inv
    ang_r = jnp.broadcast_to(row_ang[:, None, :], (rows, GRID_W, n_freq)).reshape(T, n_freq)
    ang_c = jnp.broadcast_to(col_ang[None, :, :], (rows, GRID_W, n_freq)).reshape(T, n_freq)
    ang = jnp.concatenate([ang_r, ang_r, ang_c, ang_c], axis=-1)
    cos = jnp.concatenate([jnp.cos(ang), jnp.ones((L, HEAD_DIM), F32)], axis=0)
    sin = jnp.concatenate([jnp.sin(ang), jnp.zeros((L, HEAD_DIM), F32)], axis=0)
    reps = QKD_DIM // HEAD_DIM
    return jnp.tile(cos, (1, reps)), jnp.tile(sin, (1, reps))


def _attn_qkv(lay, h, mod, g, w_qkv, b_qkv, cos, sin):
    tm = lay.tm

    def tab_idx(i):
        return (jnp.where(i < lay.n_xt, i % lay.tx, lay.tx + (i - lay.n_xt) % lay.tc), 0)

    wq, wk, wv = w_qkv[:, :Q_DIM], w_qkv[:, Q_DIM:QK_DIM], w_qkv[:, QK_DIM:]
    bq, bk, bv = b_qkv[:Q_DIM], b_qkv[Q_DIM:QK_DIM], b_qkv[QK_DIM:]
    w = jnp.concatenate([wq, _dup_heads(wk), _dup_heads(wv)], axis=1).astype(BF16)
    b = jnp.concatenate([bq, _dup_heads(bk), _dup_heads(bv)]).reshape(1, -1)
    width = Q_DIM + 2 * KVD_DIM
    return pl.pallas_call(
        _qkv_kernel,
        grid=(lay.n_t,),
        in_specs=[lay.row_spec(), lay.mod_spec(), _const_spec((1, D)),
                  _const_spec((D, width)), _const_spec((1, width)),
                  pl.BlockSpec((tm, QKD_DIM), tab_idx), pl.BlockSpec((tm, QKD_DIM), tab_idx)],
        out_specs=[lay.row_spec(Q_DIM), lay.row_spec(KVD_DIM), lay.row_spec(KVD_DIM)],
        out_shape=[jax.ShapeDtypeStruct((lay.N, Q_DIM), BF16),
                   jax.ShapeDtypeStruct((lay.N, KVD_DIM), BF16),
                   jax.ShapeDtypeStruct((lay.N, KVD_DIM), BF16)],
        compiler_params=_cparams(("parallel",)),
        name="attn_qkv",
    )(h, mod, g.reshape(1, D), w, b, cos, sin)


def _attn_heads(q, kk, vv, mask, sink_ref, o_ref):
    R = q.shape[0]
    G = N_HEADS // N_KV
    TILE = 2 * HEAD_DIM
    lo = lax.broadcasted_iota(jnp.int32, (1, TILE), 1) < HEAD_DIM
    order = (0, 2, 1, 3)
    scores = []
    for g in range(N_KV):
        pieces = []
        for j in order:
            hq = g * G + j
            qt = q[:, (hq // 2) * TILE:(hq // 2 + 1) * TILE]
            pieces.append(jnp.where(lo if hq % 2 == 0 else ~lo, qt, jnp.zeros_like(qt)))
        qs = jnp.concatenate(pieces, axis=0)
        scores.append(lax.dot_general(qs, kk[:, g * TILE:(g + 1) * TILE], (((1,), (1,)), ((), ())),
                                      preferred_element_type=F32))
    probs, esinks = [], []
    for g in range(N_KV):
        s = scores[g]
        if mask is not None:
            s = jnp.where(jnp.concatenate([mask] * G, axis=0), s, -jnp.inf)
        sink = jnp.concatenate([jnp.full((R, 1), sink_ref[g * G + j], F32) for j in order], axis=0)
        m = jnp.maximum(jnp.max(s, axis=-1, keepdims=True), sink)
        probs.append(jnp.exp(s - m).astype(BF16))
        esinks.append(jnp.exp(sink - m))
    for g in range(N_KV):
        vd = vv[:, g * TILE:(g + 1) * TILE]
        one = jnp.ones_like(vd)
        p, es = probs[g], esinks[g]
        half = (G // 2) * R
        oe = jnp.dot(p[:half], jnp.where(lo, vd, one), preferred_element_type=F32)
        oo = jnp.dot(p[half:], jnp.where(lo, one, vd), preferred_element_type=F32)
        re = oe / (pltpu.roll(oe, HEAD_DIM, 1) + es[:half])
        ro = oo / (pltpu.roll(oo, HEAD_DIM, 1) + es[half:])
        for t in range(G // 2):
            tile = jnp.where(lo, re[t * R:(t + 1) * R], ro[t * R:(t + 1) * R])
            c = g * (G // 2) + t
            o_ref[:, c * TILE:(c + 1) * TILE] = tile.astype(o_ref.dtype)


def _attn_kernel(sink_ref, q_ref, kp_ref, kc_ref, kn_ref, vp_ref, vc_ref, vn_ref, kx_ref, vx_ref, o_ref, *, T, L):
    i = pl.program_id(1)
    nqb = T // QB

    @pl.when(i < nqb)
    def _():
        kk = jnp.concatenate([kx_ref[...], kp_ref[...], kc_ref[...], kn_ref[...]], axis=0)
        vv = jnp.concatenate([vx_ref[...], vp_ref[...], vc_ref[...], vn_ref[...]], axis=0)
        S = L + 3 * QB
        col = lax.broadcasted_iota(jnp.int32, (QB, S), 1)
        row = lax.broadcasted_iota(jnp.int32, (QB, S), 0)
        rel = col - L - QB - row
        kpos = i * QB - QB + (col - L)
        local_ok = (jnp.abs(rel) <= WINDOW) & (kpos >= 0) & (kpos < T)
        mask = (col < L) | local_ok
        _attn_heads(q_ref[...], kk, vv, mask, sink_ref, o_ref)

    @pl.when(i >= nqb)
    def _():
        _attn_heads(q_ref[...], kx_ref[...], vx_ref[...], None, sink_ref, o_ref)


def _attention(lay, q, k, v, sinks, need_ctx):
    B, T, L = lay.B, lay.T, lay.L
    nqb, ncb = T // QB, L // QB
    cq = lay.rows_x // QB
    cb = lay.rows_x // L
    smem = pl.BlockSpec(memory_space=pltpu.SMEM)

    def cur(b, i):
        return (jnp.where(i < nqb, b * nqb + i, cq + b * ncb + (i - nqb)), 0)

    def prev(b, i):
        return (b * nqb + jnp.clip(i - 1, 0, nqb - 1), 0)

    def mid(b, i):
        return (b * nqb + jnp.minimum(i, nqb - 1), 0)

    def nxt(b, i):
        return (b * nqb + jnp.minimum(i + 1, nqb - 1), 0)

    def ctxb(b, i):
        return (cb + b, 0)

    kv_specs = [pl.BlockSpec((QB, KVD_DIM), prev), pl.BlockSpec((QB, KVD_DIM), mid), pl.BlockSpec((QB, KVD_DIM), nxt)]
    n_out = lay.N if need_ctx else lay.rows_x
    return pl.pallas_call(
        functools.partial(_attn_kernel, T=T, L=L),
        grid=(B, nqb + (ncb if need_ctx else 0)),
        in_specs=[smem, pl.BlockSpec((QB, Q_DIM), cur)] + kv_specs + kv_specs
                 + [pl.BlockSpec((L, KVD_DIM), ctxb), pl.BlockSpec((L, KVD_DIM), ctxb)],
        out_specs=pl.BlockSpec((QB, Q_DIM), cur),
        out_shape=jax.ShapeDtypeStruct((n_out, Q_DIM), BF16),
        compiler_params=_cparams(("parallel", "arbitrary")),
        name="attention",
    )(sinks, q, k, k, k, v, v, v, k, v)


def _linres_kernel(h_ref, x_ref, mod_ref, w_ref, b_ref, o_ref, *, gate_slot):
    gate = mod_ref[0][:, gate_slot * D:(gate_slot + 1) * D]
    y = jnp.dot(x_ref[...], w_ref[...], preferred_element_type=F32) + b_ref[...]
    o_ref[...] = h_ref[...] + gate * y


def _linear_residual(lay, h, x, mod, w, b, gate_slot, n_rows):
    return pl.pallas_call(
        functools.partial(_linres_kernel, gate_slot=gate_slot),
        grid=(n_rows // lay.tm,),
        in_specs=[lay.row_spec(), lay.row_spec(x.shape[1]), lay.mod_spec(),
                  _const_spec(w.shape), _const_spec((1, D))],
        out_specs=lay.row_spec(),
        out_shape=jax.ShapeDtypeStruct((n_rows, D), F32),
        compiler_params=_cparams(("parallel",)),
        name="linear_residual",
    )(h, x, mod, w.astype(BF16), b.reshape(1, D))


def _fill_ext(ext_ref, h_ref, hp_ref, hn_ref, g, shift, scale, j, n, tm):
    a = _norm_mod(h_ref[...], g, shift, scale)
    ap = _norm_mod(hp_ref[...], g, shift, scale)
    an = _norm_mod(hn_ref[...], g, shift, scale)
    ext_ref[0:HALO, :] = jnp.where(j == 0, 0.0, ap)
    ext_ref[HALO:HALO + tm, :] = a
    ext_ref[HALO + tm:2 * HALO + tm, :] = jnp.where(j == n - 1, 0.0, an)
    return a


def _pool_kernel(h_ref, hp_ref, hn_ref, mod_ref, g_ref, w_ref, b_ref, ls_ref, o_ref, ext_ref, *, lay):
    tm = lay.tm
    i = pl.program_id(0)
    j, n = lay.seq_tile(i)
    m = mod_ref[0]
    a = _fill_ext(ext_ref, h_ref, hp_ref, hn_ref, g_ref[...], m[:, 0:D], m[:, D:2 * D], j, n, tm)
    t = j * tm + lax.broadcasted_iota(jnp.int32, (tm, 1), 0)
    n_tok = n * tm
    outs = []
    for gi, size in enumerate(POOL_SIZES):
        sl = slice(gi * POOL_GW, (gi + 1) * POOL_GW)
        acc = ext_ref[HALO - size // 2:HALO - size // 2 + tm, sl]
        for o in range(-size // 2 + 1, size - size // 2):
            acc = acc + ext_ref[HALO + o:HALO + o + tm, sl]
        lo = jnp.maximum(t - size // 2, 0)
        hi = jnp.minimum(t + size - size // 2, n_tok)
        d = acc / (hi - lo).astype(F32) - a[:, sl]
        outs.append(jnp.dot(d.astype(BF16), w_ref[gi], preferred_element_type=F32) + b_ref[gi])
    y = jnp.concatenate(outs, axis=-1) * ls_ref[...]
    o_ref[...] = h_ref[...] + m[:, 2 * D:3 * D] * y


def _pool_mixer(lay, h, mod, g, w, b, ls, n_rows):
    prev, nxt = lay.halo_specs()
    return pl.pallas_call(
        functools.partial(_pool_kernel, lay=lay),
        grid=(n_rows // lay.tm,),
        in_specs=[lay.row_spec(), prev, nxt, lay.mod_spec(), _const_spec((1, D)),
                  _const_spec(w.shape), _const_spec((4, 1, POOL_GW)), _const_spec((1, D))],
        out_specs=lay.row_spec(),
        out_shape=jax.ShapeDtypeStruct((n_rows, D), F32),
        scratch_shapes=[pltpu.VMEM((lay.tm + 2 * HALO, D), F32)],
        compiler_params=_cparams(("parallel",)),
        name="pool_mixer",
    )(h, h, h, mod, g.reshape(1, D), w.astype(BF16), b.reshape(4, 1, POOL_GW), ls.reshape(1, D))


def _seg_sum(x, bd):
    outs = []
    for q in range(D // RW_GROUP):
        xg = x[:, q * RW_GROUP:(q + 1) * RW_GROUP]
        hi = xg.astype(BF16)
        lo = (xg - hi.astype(F32)).astype(BF16)
        outs.append(jnp.dot(hi, bd, preferred_element_type=F32) + jnp.dot(lo, bd, preferred_element_type=F32))
    return jnp.concatenate(outs, axis=-1)


def _rwkv_proj_kernel(h_ref, hp_ref, hn_ref, mod_ref, g_ref, mix_ref, wr_ref, wk_ref, wv_ref, g1_ref, g2_ref,
                      w1_ref, w2_ref, w0_ref, a1_ref, a2_ref, a0_ref, kk_ref, ka_ref, rk_ref, bd_ref,
                      r_out, v_out, kn_out, g_out, bonus_out, lw0_out, k0_out, b0_out, lw1_out, k1_out, b1_out,
                      ext_ref, *, lay):
    tm = lay.tm
    i = pl.program_id(0)
    j, n = lay.seq_tile(i)
    m = mod_ref[0]
    a = _fill_ext(ext_ref, h_ref, hp_ref, hn_ref, g_ref[...], m[:, 0:D], m[:, D:2 * D], j, n, tm)
    prev = ext_ref[HALO - 1:HALO - 1 + tm, :]
    nxt = ext_ref[HALO + 1:HALO + 1 + tm, :]
    xx = 0.5 * (prev + nxt) - a
    x_r, x_w, x_k, x_v, x_a, x_g = [(a + xx * mix_ref[q:q + 1, :]).astype(BF16) for q in range(6)]
    bd = bd_ref[...]
    r = jnp.dot(x_r, wr_ref[...], preferred_element_type=F32)
    k = jnp.dot(x_k, wk_ref[...], preferred_element_type=F32)
    v = jnp.dot(x_v, wv_ref[...], preferred_element_type=F32)
    gl = jnp.dot(x_g, g1_ref[...], preferred_element_type=F32)
    g = jnp.dot(jax.nn.sigmoid(gl).astype(BF16), g2_ref[...], preferred_element_type=F32)
    kk = k * kk_ref[...]
    kn = kk / jnp.maximum(jnp.sqrt(_seg_sum(kk * kk, bd)), 1e-12)
    tw = jnp.tanh(jnp.dot(x_w, w1_ref[...], preferred_element_type=F32))
    aa = jnp.dot(x_a, a1_ref[...], preferred_element_type=F32)
    lane = lax.broadcasted_iota(jnp.int32, (1, tw.shape[1]), 1)
    r_out[...] = r.astype(r_out.dtype)
    v_out[...] = v.astype(v_out.dtype)
    kn_out[...] = kn.astype(kn_out.dtype)
    g_out[...] = g.astype(g_out.dtype)
    rk = rk_ref[...]
    bonus = jnp.zeros_like(v)
    outs = ((lw0_out, k0_out, b0_out), (lw1_out, k1_out, b1_out))
    half = tw.shape[1] // 2
    for d in range(2):
        sel = (lane >= d * half) & (lane < (d + 1) * half)
        wl = w0_ref[d:d + 1, :] + jnp.dot(jnp.where(sel, tw, 0.0).astype(BF16), w2_ref[...], preferred_element_type=F32)
        z = -wl
        w_log = -(jnp.maximum(z, 0.0) + jnp.log(1.0 + jnp.exp(-jnp.abs(z)))) - 0.5
        asig = jax.nn.sigmoid(a0_ref[d:d + 1, :] + jnp.dot(jnp.where(sel, aa, 0.0).astype(BF16), a2_ref[...],
                                                           preferred_element_type=F32))
        k_d = k * (1.0 + (asig - 1.0) * ka_ref[...])
        lw_o, k_o, b_o = outs[d]
        lw_o[...] = -jnp.exp(w_log)
        k_o[...] = k_d.astype(k_o.dtype)
        b_o[...] = (kn * asig).astype(b_o.dtype)
        bonus = bonus + _seg_sum(r * k_d * rk, bd) * v
    bonus_out[...] = bonus.astype(bonus_out.dtype)


def _block_ones():
    idx = np.arange(RW_GROUP) // 64
    return jnp.asarray(idx[:, None] == idx[None, :], dtype=BF16)


def _rwkv_proj(lay, h, mod, g, p):
    prev, nxt = lay.halo_specs()
    cat = lambda w: jnp.concatenate([w[0], w[1]], axis=-1).astype(BF16)
    stack = lambda w: jnp.concatenate([w[0], w[1]], axis=0).astype(BF16)
    args = [h, h, h, mod, g.reshape(1, D), p['mix'],
            p['w_rkv'][0].astype(BF16), p['w_rkv'][1].astype(BF16), p['w_rkv'][2].astype(BF16),
            p['g1'].astype(BF16), p['g2'].astype(BF16),
            cat(p['w1']), stack(p['w2']), p['w0'], cat(p['a1']), stack(p['a2']), p['a0'],
            p['k_k'].reshape(1, D), p['k_a'].reshape(1, D), p['r_k'].reshape(1, D), _block_ones()]
    in_specs = [lay.row_spec(), prev, nxt, lay.mod_spec()] + [_const_spec(x.shape) for x in args[4:]]
    dtypes = [BF16] * 5 + [F32, BF16, BF16] * 2
    return pl.pallas_call(
        functools.partial(_rwkv_proj_kernel, lay=lay),
        grid=(lay.n_t,),
        in_specs=in_specs,
        out_specs=[lay.row_spec()] * len(dtypes),
        out_shape=[jax.ShapeDtypeStruct((lay.N, D), dt) for dt in dtypes],
        scratch_shapes=[pltpu.VMEM((lay.tm + 2 * HALO, D), F32)],
        compiler_params=_cparams(("parallel",)),
        name="rwkv_proj",
    )(*args)


def _rwkv_scan_kernel(rf, vf, nf, lwf, kf, bf, rb, vb, nb, lwb, kb, bb, yf_out, yb_out, z_ref):
    C, G = RW_CHUNK, RW_GROUP
    c = pl.program_id(1)

    @pl.when(c == 0)
    def _():
        z_ref[...] = jnp.zeros_like(z_ref)

    trow = lax.broadcasted_iota(jnp.int32, (C, C), 0)
    tcol = lax.broadcasted_iota(jnp.int32, (C, C), 1)
    ti = lax.broadcasted_iota(jnp.int32, (C, G), 0)
    ii = lax.broadcasted_iota(jnp.int32, (C, G), 1) % C
    brow = lax.broadcasted_iota(jnp.int32, (G, G), 0)
    bcol = lax.broadcasted_iota(jnp.int32, (G, G), 1)
    m_bd_t = (brow // C) == (bcol // 64)
    m_bd_f = (brow // 64) == (bcol // 64)
    eye_g = brow == bcol

    def bd(x):
        return jnp.where(m_bd_t, jnp.concatenate([x] * (G // C), axis=0), 0.0).astype(BF16)

    def mm(a, b):
        return jnp.dot(a, b, preferred_element_type=F32)

    dirs = ((rf, vf, nf, lwf, kf, bf, yf_out, False), (rb, vb, nb, lwb, kb, bb, yb_out, True))
    ch = []
    for d, (r_ref, v_ref, n_ref, lw_ref, k_ref, b_ref, y_out, rev) in enumerate(dirs):
        tri = (tcol >= trow) if rev else (tcol <= trow)
        strict = (ii > ti) if rev else (ii < ti)
        incl = (ii >= ti) if rev else (ii <= ti)
        lw = lw_ref[...]
        lc = jnp.dot(tri.astype(F32), lw, preferred_element_type=F32, precision=lax.Precision.HIGHEST)
        ltot = lc[0:1, :] if rev else lc[C - 1:C, :]
        e_lc = jnp.exp(lc)
        ie_lc = jnp.exp(-lc)
        e_le = jnp.exp(lc - lw)
        e_bar = jnp.exp(ltot - lc)
        e_tot = jnp.exp(ltot)
        kvec, bvec = k_ref[...], b_ref[...]
        a_t = -n_ref[...] * e_le
        r_t = r_ref[...] * e_lc
        b_t = bvec * ie_lc
        k_t = kvec * ie_lc
        b_b = bvec * e_bar
        k_b = kvec * e_bar
        vall = v_ref[...]
        for g in range(D // G):
            sl = slice(g * G, (g + 1) * G)
            ch.append(dict(
                d=d, g=g, sl=sl, y_out=y_out, strict=strict, incl=incl, v=vall[:, sl], e_tot=e_tot[:, sl],
                ar=jnp.concatenate([a_t[:, sl], r_t[:, sl]], axis=0).astype(BF16),
                bk=jnp.concatenate([bd(b_t[:, sl]), bd(k_t[:, sl])], axis=0),
                lhs_t=jnp.concatenate([b_b[:, sl], k_b[:, sl]], axis=0).astype(BF16)))

    for q in ch:
        aa = lax.dot_general(q['ar'], q['bk'], (((1,), (1,)), ((), ())), preferred_element_type=F32)
        q['a_ab'] = jnp.where(q['strict'], aa[:C, :G], 0.0)
        q['a_ak'] = jnp.where(q['strict'], aa[:C, G:], 0.0)
        q['m_rb'] = jnp.where(q['incl'], aa[C:, :G], 0.0).astype(BF16)
        q['m_rk'] = jnp.where(q['incl'], aa[C:, G:], 0.0)
    for q in ch:
        q['x'] = jnp.where(ii == ti, 1.0, 0.0) + q['a_ab']
        q['p'] = mm(q['a_ab'].astype(BF16), bd(q['a_ab']))
    for q in ch:
        q['wv'] = mm(jnp.concatenate([q['a_ak'], q['m_rk']], axis=0).astype(BF16), bd(q['v']))
        q['z'] = z_ref[q['d'], q['g']]
        q['az'] = mm(q['ar'], q['z'].astype(BF16))
    n_sq = int(np.log2(C)) - 1
    for s in range(n_sq):
        last = s == n_sq - 1
        for q in ch:
            lhs = q['x'] if last else jnp.concatenate([q['x'], q['p']], axis=0)
            xp = mm(lhs.astype(BF16), bd(q['p']))
            q['x'] = q['x'] + xp[:C]
            if not last:
                q['p'] = xp[C:]
    for q in ch:
        q['u'] = mm(q['x'].astype(BF16), bd(q['az'][:C] + q['wv'][:C]))
    for q in ch:
        y = q['az'][C:] + q['wv'][C:] + mm(q['m_rb'], bd(q['u']))
        q['y_out'][:, q['sl']] = y
        rhs_t = jnp.concatenate([q['u'], q['v']], axis=0).astype(BF16)
        zu = lax.dot_general(q['lhs_t'], rhs_t, (((0,), (0,)), ((), ())), preferred_element_type=F32)
        decay = jnp.sum(jnp.where(eye_g, jnp.broadcast_to(q['e_tot'], (G, G)), 0.0), axis=1, keepdims=True)
        z_ref[q['d'], q['g']] = jnp.where(m_bd_f, decay * q['z'] + zu, 0.0)


def _rwkv_scan(lay, r, v, kn, lw0, k0, b0, lw1, k1, b1):
    B, T, L = lay.B, lay.T, lay.L
    C = RW_CHUNK
    cc, cx = L // C, T // C
    nc = cc + cx
    ctx0 = lay.rows_x // C

    def fwd(b, c):
        return (jnp.where(c < cc, ctx0 + b * cc + c, b * cx + c - cc), 0)

    def bwd(b, c):
        return (jnp.where(c < cc, ctx0 + b * cc + (cc - 1 - c), b * cx + (nc - 1 - c)), 0)

    sf, sb = pl.BlockSpec((C, D), fwd), pl.BlockSpec((C, D), bwd)
    return pl.pallas_call(
        _rwkv_scan_kernel,
        grid=(B, nc),
        in_specs=[sf] * 6 + [sb] * 6,
        out_specs=[sf, sb],
        out_shape=[jax.ShapeDtypeStruct((lay.N, D), F32)] * 2,
        scratch_shapes=[pltpu.VMEM((2, D // RW_GROUP, RW_GROUP, RW_GROUP), F32)],
        compiler_params=_cparams(("parallel", "arbitrary")),
        name="rwkv_scan",
    )(r, v, kn, lw0, k0, b0, r, v, kn, lw1, k1, b1)


def _rwkv_out_kernel(h_ref, yf_ref, yb_ref, bonus_ref, g_ref, mod_ref, lnw_ref, lnb_ref, wo_ref, bd_ref, o_ref):
    bd = bd_ref[...]
    y = yf_ref[...] + yb_ref[...]
    mu = _seg_sum(y, bd) * (1.0 / 64)
    yc = y - mu
    var = _seg_sum(yc * yc, bd) * (1.0 / 64)
    yn = yc * lax.rsqrt(var + GN_EPS) * lnw_ref[...] + lnb_ref[...]
    out = (yn + bonus_ref[...]) * g_ref[...]
    res = jnp.dot(out.astype(BF16), wo_ref[...], preferred_element_type=F32)
    o_ref[...] = h_ref[...] + mod_ref[0][:, 2 * D:3 * D] * res


def _rwkv_out(lay, h, yf, yb, bonus, g, mod, ln_w, ln_b, w_o):
    return pl.pallas_call(
        _rwkv_out_kernel,
        grid=(lay.n_t,),
        in_specs=[lay.row_spec()] * 5 + [lay.mod_spec(), _const_spec((1, D)), _const_spec((1, D)),
                                          _const_spec((D, D)), _const_spec((RW_GROUP, RW_GROUP))],
        out_specs=lay.row_spec(),
        out_shape=jax.ShapeDtypeStruct((lay.N, D), F32),
        compiler_params=_cparams(("parallel",)),
        name="rwkv_out",
    )(h, yf, yb, bonus, g, mod, ln_w.reshape(1, D), ln_b.reshape(1, D), w_o.astype(BF16), _block_ones())


RUN_ALIGN = 8


def _router_kernel(h_ref, mod_ref, g_ref, wr_ref, br_ref, upper_ref, gate_ref, lpos_ref, cnt_ref):
    m = mod_ref[0]
    f = _norm_mod(h_ref[...], g_ref[...], m[:, 3 * D:4 * D], m[:, 4 * D:5 * D])
    logits = lax.dot_general(wr_ref[...], f, (((1,), (1,)), ((), ())), preferred_element_type=F32,
                             precision=lax.Precision.HIGHEST) + br_ref[...]
    erow = lax.broadcasted_iota(jnp.int32, logits.shape, 0)
    tops, hots = [], []
    l = logits
    for _ in range(TOP_K):
        mx = jnp.max(l, axis=0, keepdims=True)
        ix = jnp.min(jnp.where(l == mx, erow, N_EXPERTS), axis=0, keepdims=True)
        tops.append(mx)
        hots.append(erow == ix)
        l = jnp.where(erow == ix, -jnp.inf, l)
    es = [jnp.exp(t - tops[0]) for t in tops]
    tot = es[0] + es[1] + es[2] + es[3]
    gate_ref[...] = jnp.concatenate([e / tot for e in es], axis=0)

    ohs = [jnp.where(hh, 1.0, 0.0) for hh in hots]
    cnts = [jnp.sum(oh, axis=1, keepdims=True) for oh in ohs]
    cnt = cnts[0] + cnts[1] + cnts[2] + cnts[3]
    run = jnp.ceil(cnt * (1.0 / RUN_ALIGN)) * RUN_ALIGN
    ee = lax.broadcasted_iota(jnp.int32, (N_EXPERTS, N_EXPERTS), 0)
    ec = lax.broadcasted_iota(jnp.int32, (N_EXPERTS, N_EXPERTS), 1)
    run_row = jnp.sum(jnp.where(ee == ec, jnp.broadcast_to(run, (N_EXPERTS, N_EXPERTS)), 0.0), axis=0, keepdims=True)
    offs = jnp.sum(jnp.where(ec < ee, jnp.broadcast_to(run_row, (N_EXPERTS, N_EXPERTS)), 0.0), axis=1, keepdims=True)
    base = offs
    lpos = []
    for k in range(TOP_K):
        before = jnp.dot(ohs[k].astype(BF16), upper_ref[...], preferred_element_type=F32)
        lpos.append(jnp.sum(ohs[k] * (before + base), axis=0, keepdims=True))
        base = base + cnts[k]
    lpos_ref[...] = jnp.concatenate(lpos, axis=0).astype(jnp.int32)
    cnt_ref[0] = cnt.astype(jnp.int32)


def _moe_router(lay, h, mod, g, w_router, b_router, n_rows):
    tm = lay.tm
    nt = n_rows // tm
    upper = jnp.asarray(np.triu(np.ones((tm, tm), np.float32), 1), dtype=BF16)
    return pl.pallas_call(
        _router_kernel,
        grid=(nt,),
        in_specs=[lay.row_spec(), lay.mod_spec(), _const_spec((1, D)),
                  _const_spec((N_EXPERTS, D)), _const_spec((N_EXPERTS, 1)), _const_spec((tm, tm))],
        out_specs=[pl.BlockSpec((TOP_K, tm), lambda i: (0, i)),
                   pl.BlockSpec((TOP_K, tm), lambda i: (0, i)),
                   pl.BlockSpec((1, N_EXPERTS, 1), lambda i: (i, 0, 0))],
        out_shape=[jax.ShapeDtypeStruct((TOP_K, n_rows), F32),
                   jax.ShapeDtypeStruct((TOP_K, n_rows), jnp.int32),
                   jax.ShapeDtypeStruct((nt, N_EXPERTS, 1), jnp.int32)],
        compiler_params=_cparams(("parallel",)),
        name="moe_router",
    )(h, mod, g.reshape(1, D), w_router.T, b_router.reshape(N_EXPERTS, 1), upper)


def _comp_rows(tm):
    return tm * TOP_K + N_EXPERTS * RUN_ALIGN


def _moe_plan(cnt, n, tm):
    nt = n // tm
    def excl_sum(a, axis):
        k = a.shape[axis]
        before = np.arange(k)[:, None] > np.arange(k)[None, :]
        if axis == 0:
            return jnp.sum(jnp.where(before[:, :, None], a[None, :, :], 0), axis=1)
        return jnp.sum(jnp.where(before[None, :, :], a[:, None, :], 0), axis=2)

    run = (cnt + RUN_ALIGN - 1) // RUN_ALIGN * RUN_ALIGN
    offs = excl_sum(run, 1)
    tot = jnp.sum(run, axis=0)
    padded = (tot + MOE_BLOCK - 1) // MOE_BLOCK * MOE_BLOCK
    pad_start = excl_sum(padded[None, :], 1)[0]
    pad_end = pad_start + padded
    base = pad_start[None, :] + excl_sum(run, 0)
    rows_max = n * TOP_K + nt * N_EXPERTS * (RUN_ALIGN - 1) + N_EXPERTS * MOE_BLOCK
    n_blocks = -(-rows_max // MOE_BLOCK)
    rows = n_blocks * MOE_BLOCK
    blk_row = jnp.arange(n_blocks, dtype=jnp.int32) * MOE_BLOCK
    block_expert = jnp.minimum(jnp.sum((pad_end[None, :] <= blk_row[:, None]).astype(jnp.int32), axis=1),
                               N_EXPERTS - 1).astype(jnp.int32)
    fill_base = jnp.concatenate([pad_start + tot, pad_end[-1:]]).astype(jnp.int32)
    fill_cnt = (jnp.concatenate([padded - tot, rows - pad_end[-1:]]) // RUN_ALIGN).astype(jnp.int32)
    n_used = (pad_end[-1:] // MOE_BLOCK).astype(jnp.int32)
    as3 = lambda a: a.astype(jnp.int32).reshape(nt, 1, N_EXPERTS)
    return as3(base), as3(offs), as3(run // RUN_ALIGN), block_expert, fill_base, fill_cnt, n_used, n_blocks


def _run_copies(groups_ref, src_of, dst_of, sem, wait):
    for e in range(N_EXPERTS):
        def body(c, carry):
            cp = pltpu.make_async_copy(src_of(e, c), dst_of(e, c), sem)
            cp.wait() if wait else cp.start()
            return carry

        lax.fori_loop(0, groups_ref[0, 0, e], body, 0)


def _rows_at(ref, start):
    return ref.at[pl.ds(pl.multiple_of(start, RUN_ALIGN), RUN_ALIGN), :]


def _compact_kernel(fill_base_ref, fill_cnt_ref, base_ref, offs_ref, groups_ref, pbase_ref, poffs_ref, pgroups_ref,
                    h_ref, mod_ref, g_ref, lpos_ref, xs_hbm, comp_ref, zero_ref, sems):
    i = pl.program_id(0)
    s = i % 2
    m = mod_ref[0]
    f = _norm_mod(h_ref[...], g_ref[...], m[:, 3 * D:4 * D], m[:, 4 * D:5 * D]).astype(BF16)
    cr, tm = comp_ref.shape[1], h_ref.shape[0]
    q = lax.broadcasted_iota(jnp.int32, (cr, tm), 0)
    lp = lpos_ref[...]
    hit = (lp[0:1, :] == q) | (lp[1:2, :] == q) | (lp[2:3, :] == q) | (lp[3:4, :] == q)
    comp_ref[s] = jnp.dot(jnp.where(hit, 1.0, 0.0).astype(BF16), f, preferred_element_type=F32)

    def copies(slot, b_ref, o_ref, g_ref_, wait):
        src = lambda e, c: _rows_at(comp_ref.at[slot], o_ref[0, 0, e] + c * RUN_ALIGN)
        dst = lambda e, c: _rows_at(xs_hbm, b_ref[0, 0, e] + c * RUN_ALIGN)
        _run_copies(g_ref_, src, dst, sems.at[slot], wait)

    @pl.when(i > 0)
    def _():
        copies(1 - s, pbase_ref, poffs_ref, pgroups_ref, True)

    copies(s, base_ref, offs_ref, groups_ref, False)

    @pl.when(i == 0)
    def _():
        zero_ref[...] = jnp.zeros_like(zero_ref)
        for e in range(N_EXPERTS + 1):
            def fill(c, carry, wait=False, e=e):
                cp = pltpu.make_async_copy(zero_ref, _rows_at(xs_hbm, fill_base_ref[e] + c * RUN_ALIGN), sems.at[0])
                cp.wait() if wait else cp.start()
                return carry

            lax.fori_loop(0, fill_cnt_ref[e], fill, 0)
            lax.fori_loop(0, fill_cnt_ref[e], functools.partial(fill, wait=True), 0)

    @pl.when(i == pl.num_programs(0) - 1)
    def _():
        copies(s, base_ref, offs_ref, groups_ref, True)


def _tile_tables(shift, nt):
    def idx(i, *_):
        return (jnp.clip(i + shift, 0, nt - 1), 0, 0)

    return [pl.BlockSpec((1, 1, N_EXPERTS), idx, memory_space=pltpu.SMEM)] * 3


def _moe_compact(lay, h, mod, g, lpos, base, offs, groups, fill_base, fill_cnt, n_rows, n_blocks):
    tm = lay.tm
    grid_spec = pltpu.PrefetchScalarGridSpec(
        num_scalar_prefetch=2,
        grid=(n_rows // tm,),
        in_specs=_tile_tables(0, n_rows // tm) + _tile_tables(-1, n_rows // tm)
                 + [pl.BlockSpec((tm, D), lambda i, *_: (i, 0)),
                    pl.BlockSpec((1, 1, 6 * D), lambda i, *_: (lay.mod_idx(i), 0, 0)),
                    pl.BlockSpec((1, D), lambda i, *_: (0, 0)),
                    pl.BlockSpec((TOP_K, tm), lambda i, *_: (0, i))],
        out_specs=pl.BlockSpec(memory_space=pl.ANY),
        scratch_shapes=[pltpu.VMEM((2, _comp_rows(tm), D), F32), pltpu.VMEM((RUN_ALIGN, D), F32),
                        pltpu.SemaphoreType.DMA((2,))],
    )
    return pl.pallas_call(
        _compact_kernel,
        grid_spec=grid_spec,
        out_shape=jax.ShapeDtypeStruct((n_blocks * MOE_BLOCK, D), F32),
        compiler_params=_cparams(("arbitrary",)),
        name="moe_compact",
    )(fill_base, fill_cnt, base, offs, groups, base, offs, groups, h, mod, g.reshape(1, D), lpos)


def _expert_kernel(be_ref, nu_ref, x_ref, w1_ref, b1_ref, w2_ref, b2_ref, o_ref):
    del be_ref
    i = pl.program_id(0)

    @pl.when(i < nu_ref[0])
    def _():
        hcat = jnp.dot(x_ref[...].astype(BF16), w1_ref[0], preferred_element_type=F32) + b1_ref[0]
        h_glu = jnp.minimum(hcat[:, :D_FF], SWIGLU_LIMIT)
        h_lin = jnp.clip(hcat[:, D_FF:], -SWIGLU_LIMIT, SWIGLU_LIMIT)
        act = h_glu * jax.nn.sigmoid(SWIGLU_ALPHA * h_glu) * (h_lin + 1.0)
        o_ref[...] = jnp.dot(act.astype(BF16), w2_ref[0], preferred_element_type=F32) + b2_ref[0]

    @pl.when(i >= nu_ref[0])
    def _():
        o_ref[...] = jnp.zeros_like(o_ref)


def _moe_experts(xs, block_expert, n_used, n_blocks, w1, b1, w2, b2):
    def xrow(i, be, nu):
        return (jnp.minimum(i, nu[0] - 1), 0)

    grid_spec = pltpu.PrefetchScalarGridSpec(
        num_scalar_prefetch=2,
        grid=(n_blocks,),
        in_specs=[pl.BlockSpec((MOE_BLOCK, D), xrow),
                  pl.BlockSpec((1, D, 2 * D_FF), lambda i, be, nu: (be[i], 0, 0)),
                  pl.BlockSpec((1, 1, 2 * D_FF), lambda i, be, nu: (be[i], 0, 0)),
                  pl.BlockSpec((1, D_FF, D), lambda i, be, nu: (be[i], 0, 0)),
                  pl.BlockSpec((1, 1, D), lambda i, be, nu: (be[i], 0, 0))],
        out_specs=pl.BlockSpec((MOE_BLOCK, D), lambda i, be, nu: (i, 0)),
    )
    return pl.pallas_call(
        _expert_kernel,
        grid_spec=grid_spec,
        out_shape=jax.ShapeDtypeStruct((n_blocks * MOE_BLOCK, D), F32),
        compiler_params=_cparams(("arbitrary",)),
        name="moe_experts",
    )(block_expert, n_used, xs, w1.astype(BF16), b1.reshape(N_EXPERTS, 1, 2 * D_FF),
      w2.astype(BF16), b2.reshape(N_EXPERTS, 1, D))


def _combine_kernel(base_ref, offs_ref, groups_ref, nbase_ref, noffs_ref, ngroups_ref, h_ref, gate_ref, lpos_ref,
                    mod_ref, y_hbm, o_ref, comp_ref, sems):
    i = pl.program_id(0)
    s = i % 2

    def copies(slot, b_ref, o_ref_, g_ref_, wait):
        src = lambda e, c: _rows_at(y_hbm, b_ref[0, 0, e] + c * RUN_ALIGN)
        dst = lambda e, c: _rows_at(comp_ref.at[slot], o_ref_[0, 0, e] + c * RUN_ALIGN)
        _run_copies(g_ref_, src, dst, sems.at[slot], wait)

    @pl.when(i == 0)
    def _():
        comp_ref[...] = jnp.zeros_like(comp_ref)
        copies(0, base_ref, offs_ref, groups_ref, False)

    @pl.when(i + 1 < pl.num_programs(0))
    def _():
        copies(1 - s, nbase_ref, noffs_ref, ngroups_ref, False)

    cr, tm = comp_ref.shape[1], h_ref.shape[0]
    q = lax.broadcasted_iota(jnp.int32, (tm, cr), 1)
    gates, lp = gate_ref[...], lpos_ref[...]
    pg = jnp.where(lp[:, 0:1] == q, gates[:, 0:1], 0.0)
    for k in range(1, TOP_K):
        pg = pg + jnp.where(lp[:, k:k + 1] == q, gates[:, k:k + 1], 0.0)
    used = (offs_ref[0, 0, N_EXPERTS - 1] + groups_ref[0, 0, N_EXPERTS - 1] * RUN_ALIGN)
    copies(s, base_ref, offs_ref, groups_ref, True)
    row = lax.broadcasted_iota(jnp.int32, (cr, 1), 0)
    yc = jnp.where(row < used, comp_ref[s], 0.0).astype(BF16)
    y = jnp.dot(pg.astype(BF16), yc, preferred_element_type=F32)
    o_ref[...] = h_ref[...] + mod_ref[0][:, 5 * D:6 * D] * y


def _moe_combine(lay, h, mod, yb, gates_t, lpos_t, base, offs, groups, n_rows):
    tm = lay.tm
    nt = n_rows // tm
    return pl.pallas_call(
        _combine_kernel,
        grid=(nt,),
        in_specs=_tile_tables(0, nt) + _tile_tables(1, nt)
                 + [lay.row_spec(), lay.row_spec(TOP_K), lay.row_spec(TOP_K), lay.mod_spec(),
                    pl.BlockSpec(memory_space=pl.ANY)],
        out_specs=lay.row_spec(),
        out_shape=jax.ShapeDtypeStruct((n_rows, D), F32),
        scratch_shapes=[pltpu.VMEM((2, _comp_rows(tm), D), F32), pltpu.SemaphoreType.DMA((2,))],
        compiler_params=_cparams(("arbitrary",)),
        name="moe_combine",
    )(base, offs, groups, base, offs, groups, h, gates_t, lpos_t, mod, yb)


def _moe_layer(lay, h, mod, g, w_router, b_router, w1, b1, w2, b2, n_rows):
    gates, lpos, cnt = _moe_router(lay, h, mod, g, w_router, b_router, n_rows)
    base, offs, groups, block_expert, fill_base, fill_cnt, n_used, n_blocks = _moe_plan(cnt[:, :, 0], n_rows, lay.tm)
    xs = _moe_compact(lay, h, mod, g, lpos, base, offs, groups, fill_base, fill_cnt, n_rows, n_blocks)
    yb = _moe_experts(xs, block_expert, n_used, n_blocks, w1, b1, w2, b2)
    return _moe_combine(lay, h, mod, yb, gates.T, lpos.T, base, offs, groups, n_rows)


def _final_kernel(h_ref, g_ref, o_ref):
    h = h_ref[...]
    ms = jnp.mean(h * h, axis=-1, keepdims=True)
    o_ref[...] = h * lax.rsqrt(ms + NORM_EPS) * g_ref[...]


def _final_norm(lay, h, g, n_rows):
    return pl.pallas_call(
        _final_kernel,
        grid=(n_rows // lay.tm,),
        in_specs=[lay.row_spec(), _const_spec((1, D))],
        out_specs=lay.row_spec(),
        out_shape=jax.ShapeDtypeStruct((n_rows, D), F32),
        compiler_params=_cparams(("parallel",)),
        name="final_norm",
    )(h, g.reshape(1, D))


def kernel(x, c, ctx, c_ctx, ada_w, ada_b, norm_g, attn_w_qkv, attn_b_qkv, attn_w_o, attn_b_o, attn_sinks, pool_w, pool_b, pool_scale, rwkv_mix, rwkv_w_rkv, rwkv_w0, rwkv_w1, rwkv_w2, rwkv_a0, rwkv_a1, rwkv_a2, rwkv_g1, rwkv_g2, rwkv_k_k, rwkv_k_a, rwkv_r_k, rwkv_ln_w, rwkv_ln_b, rwkv_w_o, moe_w_router, moe_b_router, moe_w1, moe_b1, moe_w2, moe_b2, final_g):
    B, T, _ = x.shape
    L = ctx.shape[1]
    depth = ada_w.shape[0]
    lay = _Layout(B, T, L)
    h = jnp.concatenate([x.reshape(B * T, D), ctx.reshape(B * L, D)], axis=0)

    n_c = B + 1
    n_c_pad = -(-n_c // 8) * 8
    c_all = jnp.concatenate([c, c_ctx[None, :], jnp.zeros((n_c_pad - n_c, D), F32)], axis=0)
    mods = _ada_mods(c_all, ada_w, ada_b)
    cos, sin = _rope_tables(T, L)

    for layer in range(depth):
        last = layer == depth - 1
        kind, j = layer % 3, layer // 3
        mod = mods[layer].reshape(n_c_pad, 1, 6 * D)
        n_rows = lay.rows_x if last else lay.N
        if kind == 0:
            q, k, v = _attn_qkv(lay, h, mod, norm_g[layer, 0], attn_w_qkv[j], attn_b_qkv[j], cos, sin)
            o = _attention(lay, q, k, v, attn_sinks[j], not last)
            h = _linear_residual(lay, h, o, mod, attn_w_o[j], attn_b_o[j], 2, n_rows)
        elif kind == 1:
            h = _pool_mixer(lay, h, mod, norm_g[layer, 0], pool_w[j], pool_b[j], pool_scale[j], n_rows)
        else:
            p = dict(mix=rwkv_mix[j], w_rkv=rwkv_w_rkv[j], w0=rwkv_w0[j], w1=rwkv_w1[j], w2=rwkv_w2[j],
                     a0=rwkv_a0[j], a1=rwkv_a1[j], a2=rwkv_a2[j], g1=rwkv_g1[j], g2=rwkv_g2[j],
                     k_k=rwkv_k_k[j], k_a=rwkv_k_a[j], r_k=rwkv_r_k[j])
            r, v, kn, g, bonus, lw0, k0, b0, lw1, k1, b1 = _rwkv_proj(lay, h, mod, norm_g[layer, 0], p)
            yf, yb = _rwkv_scan(lay, r, v, kn, lw0, k0, b0, lw1, k1, b1)
            h = _rwkv_out(lay, h, yf, yb, bonus, g, mod, rwkv_ln_w[j], rwkv_ln_b[j], rwkv_w_o[j])
        h = _moe_layer(lay, h, mod, norm_g[layer, 1], moe_w_router[layer], moe_b_router[layer],
                       moe_w1[layer], moe_b1[layer], moe_w2[layer], moe_b2[layer], n_rows)
    out = _final_norm(lay, h, final_g, lay.rows_x)
    return out.reshape(B, T, D)
```

```python
import functools

import jax
import jax.numpy as jnp
import numpy as np
from jax import lax
from jax.experimental import pallas as pl
from jax.experimental.pallas import tpu as pltpu

F32 = jnp.float32
BF16 = jnp.bfloat16

D = 1024
NORM_EPS = 1e-5
GRID_W = 64
HEAD_DIM = 64
N_HEADS = 16
N_KV = 4
Q_DIM = 1024
KV_DIM = 256
QK_DIM = Q_DIM + KV_DIM
KVD_DIM = 2 * KV_DIM
QKD_DIM = Q_DIM + KVD_DIM
WINDOW = 128
QB = 128
ROPE_BASE = 10000.0
POOL_SIZES = (2, 4, 8, 16)
POOL_GW = 256
HALO = 8
GN_EPS = 64e-5
N_EXPERTS = 32
TOP_K = 4
D_FF = 1024
SWIGLU_LIMIT = 7.0
SWIGLU_ALPHA = 1.702
MOE_BLOCK = 512
RW_CHUNK = 64
RW_GROUP = 256
VMEM_LIMIT = 56 * 1024 * 1024


def _cparams(sem):
    return pltpu.CompilerParams(dimension_semantics=sem, vmem_limit_bytes=VMEM_LIMIT)


def _norm_mod(h, g, shift, scale):
    ms = jnp.mean(h * h, axis=-1, keepdims=True)
    y = h * lax.rsqrt(ms + NORM_EPS) * g
    return y * (1.0 + scale) + shift


class _Layout:
    def __init__(self, B, T, L):
        self.B, self.T, self.L = B, T, L
        self.tm = 256 if L % 256 == 0 else 128
        self.rows_x, self.rows_c = B * T, B * L
        self.N = self.rows_x + self.rows_c
        self.tx, self.tc = T // self.tm, L // self.tm
        self.n_xt = self.rows_x // self.tm
        self.n_t = self.N // self.tm

    def mod_idx(self, i):
        return jnp.where(i < self.n_xt, i // self.tx, self.B)

    def mod_spec(self):
        return pl.BlockSpec((1, 1, 6 * D), lambda i: (self.mod_idx(i), 0, 0))

    def row_spec(self, width=D):
        return pl.BlockSpec((self.tm, width), lambda i: (i, 0))

    def seq_tile(self, i):
        j = jnp.where(i < self.n_xt, i % self.tx, (i - self.n_xt) % self.tc)
        n = jnp.where(i < self.n_xt, self.tx, self.tc)
        return j, n

    def halo_specs(self):
        hb = self.tm // HALO
        last = self.N // HALO - 1
        prev = pl.BlockSpec((HALO, D), lambda i: (jnp.maximum(i * hb - 1, 0), 0))
        nxt = pl.BlockSpec((HALO, D), lambda i: (jnp.minimum((i + 1) * hb, last), 0))
        return prev, nxt


def _const_spec(shape):
    nd = len(shape)
    return pl.BlockSpec(shape, lambda *_: (0,) * nd)


def _ada_kernel(c_ref, w_ref, b_ref, o_ref):
    c = c_ref[...]
    s = c * jax.nn.sigmoid(c)
    o_ref[0] = jnp.dot(s, w_ref[0], preferred_element_type=F32, precision=lax.Precision.HIGHEST) + b_ref[0]


def _ada_mods(c_all, ada_w, ada_b):
    depth = ada_w.shape[0]
    R = c_all.shape[0]
    nt = 1536
    return pl.pallas_call(
        _ada_kernel,
        grid=(depth, 6 * D // nt),
        in_specs=[pl.BlockSpec((R, D), lambda l, j: (0, 0)),
                  pl.BlockSpec((1, D, nt), lambda l, j: (l, 0, j)),
                  pl.BlockSpec((1, 1, nt), lambda l, j: (l, 0, j))],
        out_specs=pl.BlockSpec((1, R, nt), lambda l, j: (l, 0, j)),
        out_shape=jax.ShapeDtypeStruct((depth, R, 6 * D), F32),
        compiler_params=_cparams(("parallel", "parallel")),
        name="ada_mod",
    )(c_all, ada_w, ada_b.reshape(depth, 1, 6 * D))


def _qkv_kernel(h_ref, mod_ref, g_ref, w_ref, b_ref, cos_ref, sin_ref, q_ref, k_ref, v_ref):
    m = mod_ref[0]
    a = _norm_mod(h_ref[...], g_ref[...], m[:, 0:D], m[:, D:2 * D])
    qkv = jnp.dot(a.astype(BF16), w_ref[...], preferred_element_type=F32) + b_ref[...]
    qk = qkv[:, :QKD_DIM]
    lane = lax.broadcasted_iota(jnp.int32, (1, QKD_DIM), 1)
    first = (lane % 32) < 16
    rot = jnp.where(first, -pltpu.roll(qk, QKD_DIM - 16, 1), pltpu.roll(qk, 16, 1))
    qk = qk * cos_ref[...] + rot * sin_ref[...]
    q_ref[...] = (qk[:, :Q_DIM] * (HEAD_DIM ** -0.5)).astype(BF16)
    k_ref[...] = qk[:, Q_DIM:].astype(BF16)
    v_ref[...] = qkv[:, QKD_DIM:].astype(BF16)


def _dup_heads(w):
    w4 = w.reshape(w.shape[:-1] + (N_KV, 1, HEAD_DIM))
    return jnp.broadcast_to(w4, w.shape[:-1] + (N_KV, 2, HEAD_DIM)).reshape(w.shape[:-1] + (KVD_DIM,))


def _rope_tables(T, L):
    rows = T // GRID_W
    n_freq = HEAD_DIM // 4
    inv = ROPE_BASE ** (-jnp.arange(n_freq, dtype=F32) / n_freq)
    row_ang = jnp.arange(rows, dtype=F32)[:, None] * inv
    col_ang = jnp.arange(GRID_W, dtype=F32)[:, None] * inv
    ang_r = jnp.broadcast_to(row_ang[:, None, :], (rows, GRID_W, n_freq)).reshape(T, n_freq)
    ang_c = jnp.broadcast_to(col_ang[None, :, :], (rows, GRID_W, n_freq)).reshape(T, n_freq)
    ang = jnp.concatenate([ang_r, ang_r, ang_c, ang_c], axis=-1)
    cos = jnp.concatenate([jnp.cos(ang), jnp.ones((L, HEAD_DIM), F32)], axis=0)
    sin = jnp.concatenate([jnp.sin(ang), jnp.zeros((L, HEAD_DIM), F32)], axis=0)
    reps = QKD_DIM // HEAD_DIM
    return jnp.tile(cos, (1, reps)), jnp.tile(sin, (1, reps))


def _attn_qkv(lay, h, mod, g, w_qkv, b_qkv, cos, sin):
    tm = lay.tm

    def tab_idx(i):
        return (jnp.where(i < lay.n_xt, i % lay.tx, lay.tx + (i - lay.n_xt) % lay.tc), 0)

    wq, wk, wv = w_qkv[:, :Q_DIM], w_qkv[:, Q_DIM:QK_DIM], w_qkv[:, QK_DIM:]
    bq, bk, bv = b_qkv[:Q_DIM], b_qkv[Q_DIM:QK_DIM], b_qkv[QK_DIM:]
    w = jnp.concatenate([wq, _dup_heads(wk), _dup_heads(wv)], axis=1).astype(BF16)
    b = jnp.concatenate([bq, _dup_heads(bk), _dup_heads(bv)]).reshape(1, -1)
    width = Q_DIM + 2 * KVD_DIM
    return pl.pallas_call(
        _qkv_kernel,
        grid=(lay.n_t,),
        in_specs=[lay.row_spec(), lay.mod_spec(), _const_spec((1, D)),
                  _const_spec((D, width)), _const_spec((1, width)),
                  pl.BlockSpec((tm, QKD_DIM), tab_idx), pl.BlockSpec((tm, QKD_DIM), tab_idx)],
        out_specs=[lay.row_spec(Q_DIM), lay.row_spec(KVD_DIM), lay.row_spec(KVD_DIM)],
        out_shape=[jax.ShapeDtypeStruct((lay.N, Q_DIM), BF16),
                   jax.ShapeDtypeStruct((lay.N, KVD_DIM), BF16),
                   jax.ShapeDtypeStruct((lay.N, KVD_DIM), BF16)],
        compiler_params=_cparams(("parallel",)),
        name="attn_qkv",
    )(h, mod, g.reshape(1, D), w, b, cos, sin)


def _attn_heads(q, kk, vv, mask, sink_ref, o_ref):
    R = q.shape[0]
    G = N_HEADS // N_KV
    TILE = 2 * HEAD_DIM
    lo = lax.broadcasted_iota(jnp.int32, (1, TILE), 1) < HEAD_DIM
    order = (0, 2, 1, 3)
    scores = []
    for g in range(N_KV):
        pieces = []
        for j in order:
            hq = g * G + j
            qt = q[:, (hq // 2) * TILE:(hq // 2 + 1) * TILE]
            pieces.append(jnp.where(lo if hq % 2 == 0 else ~lo, qt, jnp.zeros_like(qt)))
        qs = jnp.concatenate(pieces, axis=0)
        scores.append(lax.dot_general(qs, kk[:, g * TILE:(g + 1) * TILE], (((1,), (1,)), ((), ())),
                                      preferred_element_type=F32))
    probs, esinks = [], []
    for g in range(N_KV):
        s = scores[g]
        if mask is not None:
            s = jnp.where(jnp.concatenate([mask] * G, axis=0), s, -jnp.inf)
        sink = jnp.concatenate([jnp.full((R, 1), sink_ref[g * G + j], F32) for j in order], axis=0)
        m = jnp.maximum(jnp.max(s, axis=-1, keepdims=True), sink)
        probs.append(jnp.exp(s - m).astype(BF16))
        esinks.append(jnp.exp(sink - m))
    for g in range(N_KV):
        vd = vv[:, g * TILE:(g + 1) * TILE]
        one = jnp.ones_like(vd)
        p, es = probs[g], esinks[g]
        half = (G // 2) * R
        oe = jnp.dot(p[:half], jnp.where(lo, vd, one), preferred_element_type=F32)
        oo = jnp.dot(p[half:], jnp.where(lo, one, vd), preferred_element_type=F32)
        re = oe / (pltpu.roll(oe, HEAD_DIM, 1) + es[:half])
        ro = oo / (pltpu.roll(oo, HEAD_DIM, 1) + es[half:])
        for t in range(G // 2):
            tile = jnp.where(lo, re[t * R:(t + 1) * R], ro[t * R:(t + 1) * R])
            c = g * (G // 2) + t
            o_ref[:, c * TILE:(c + 1) * TILE] = tile.astype(o_ref.dtype)


def _attn_kernel(sink_ref, q_ref, kp_ref, kc_ref, kn_ref, vp_ref, vc_ref, vn_ref, kx_ref, vx_ref, o_ref, *, T, L):
    i = pl.program_id(1)
    nqb = T // QB

    @pl.when(i < nqb)
    def _():
        kk = jnp.concatenate([kx_ref[...], kp_ref[...], kc_ref[...], kn_ref[...]], axis=0)
        vv = jnp.concatenate([vx_ref[...], vp_ref[...], vc_ref[...], vn_ref[...]], axis=0)
        S = L + 3 * QB
        col = lax.broadcasted_iota(jnp.int32, (QB, S), 1)
        row = lax.broadcasted_iota(jnp.int32, (QB, S), 0)
        rel = col - L - QB - row
        kpos = i * QB - QB + (col - L)
        local_ok = (jnp.abs(rel) <= WINDOW) & (kpos >= 0) & (kpos < T)
        mask = (col < L) | local_ok
        _attn_heads(q_ref[...], kk, vv, mask, sink_ref, o_ref)

    @pl.when(i >= nqb)
    def _():
        _attn_heads(q_ref[...], kx_ref[...], vx_ref[...], None, sink_ref, o_ref)


def _attention(lay, q, k, v, sinks, need_ctx):
    B, T, L = lay.B, lay.T, lay.L
    nqb, ncb = T // QB, L // QB
    cq = lay.rows_x // QB
    cb = lay.rows_x // L
    smem = pl.BlockSpec(memory_space=pltpu.SMEM)

    def cur(b, i):
        return (jnp.where(i < nqb, b * nqb + i, cq + b * ncb + (i - nqb)), 0)

    def prev(b, i):
        return (b * nqb + jnp.clip(i - 1, 0, nqb - 1), 0)

    def mid(b, i):
        return (b * nqb + jnp.minimum(i, nqb - 1), 0)

    def nxt(b, i):
        return (b * nqb + jnp.minimum(i + 1, nqb - 1), 0)

    def ctxb(b, i):
        return (cb + b, 0)

    kv_specs = [pl.BlockSpec((QB, KVD_DIM), prev), pl.BlockSpec((QB, KVD_DIM), mid), pl.BlockSpec((QB, KVD_DIM), nxt)]
    n_out = lay.N if need_ctx else lay.rows_x
    return pl.pallas_call(
        functools.partial(_attn_kernel, T=T, L=L),
        grid=(B, nqb + (ncb if need_ctx else 0)),
        in_specs=[smem, pl.BlockSpec((QB, Q_DIM), cur)] + kv_specs + kv_specs
                 + [pl.BlockSpec((L, KVD_DIM), ctxb), pl.BlockSpec((L, KVD_DIM), ctxb)],
        out_specs=pl.BlockSpec((QB, Q_DIM), cur),
        out_shape=jax.ShapeDtypeStruct((n_out, Q_DIM), BF16),
        compiler_params=_cparams(("parallel", "arbitrary")),
        name="attention",
    )(sinks, q, k, k, k, v, v, v, k, v)


def _linres_kernel(h_ref, x_ref, mod_ref, w_ref, b_ref, o_ref, *, gate_slot):
    gate = mod_ref[0][:, gate_slot * D:(gate_slot + 1) * D]
    y = jnp.dot(x_ref[...], w_ref[...], preferred_element_type=F32) + b_ref[...]
    o_ref[...] = h_ref[...] + gate * y


def _linear_residual(lay, h, x, mod, w, b, gate_slot, n_rows):
    return pl.pallas_call(
        functools.partial(_linres_kernel, gate_slot=gate_slot),
        grid=(n_rows // lay.tm,),
        in_specs=[lay.row_spec(), lay.row_spec(x.shape[1]), lay.mod_spec(),
                  _const_spec(w.shape), _const_spec((1, D))],
        out_specs=lay.row_spec(),
        out_shape=jax.ShapeDtypeStruct((n_rows, D), F32),
        compiler_params=_cparams(("parallel",)),
        name="linear_residual",
    )(h, x, mod, w.astype(BF16), b.reshape(1, D))


def _fill_ext(ext_ref, h_ref, hp_ref, hn_ref, g, shift, scale, j, n, tm):
    a = _norm_mod(h_ref[...], g, shift, scale)
    ap = _norm_mod(hp_ref[...], g, shift, scale)
    an = _norm_mod(hn_ref[...], g, shift, scale)
    ext_ref[0:HALO, :] = jnp.where(j == 0, 0.0, ap)
    ext_ref[HALO:HALO + tm, :] = a
    ext_ref[HALO + tm:2 * HALO + tm, :] = jnp.where(j == n - 1, 0.0, an)
    return a


def _pool_kernel(h_ref, hp_ref, hn_ref, mod_ref, g_ref, w_ref, b_ref, ls_ref, o_ref, ext_ref, *, lay):
    tm = lay.tm
    i = pl.program_id(0)
    j, n = lay.seq_tile(i)
    m = mod_ref[0]
    a = _fill_ext(ext_ref, h_ref, hp_ref, hn_ref, g_ref[...], m[:, 0:D], m[:, D:2 * D], j, n, tm)
    t = j * tm + lax.broadcasted_iota(jnp.int32, (tm, 1), 0)
    n_tok = n * tm
    outs = []
    for gi, size in enumerate(POOL_SIZES):
        sl = slice(gi * POOL_GW, (gi + 1) * POOL_GW)
        acc = ext_ref[HALO - size // 2:HALO - size // 2 + tm, sl]
        for o in range(-size // 2 + 1, size - size // 2):
            acc = acc + ext_ref[HALO + o:HALO + o + tm, sl]
        lo = jnp.maximum(t - size // 2, 0)
        hi = jnp.minimum(t + size - size // 2, n_tok)
        d = acc / (hi - lo).astype(F32) - a[:, sl]
        outs.append(jnp.dot(d.astype(BF16), w_ref[gi], preferred_element_type=F32) + b_ref[gi])
    y = jnp.concatenate(outs, axis=-1) * ls_ref[...]
    o_ref[...] = h_ref[...] + m[:, 2 * D:3 * D] * y


def _pool_mixer(lay, h, mod, g, w, b, ls, n_rows):
    prev, nxt = lay.halo_specs()
    return pl.pallas_call(
        functools.partial(_pool_kernel, lay=lay),
        grid=(n_rows // lay.tm,),
        in_specs=[lay.row_spec(), prev, nxt, lay.mod_spec(), _const_spec((1, D)),
                  _const_spec(w.shape), _const_spec((4, 1, POOL_GW)), _const_spec((1, D))],
        out_specs=lay.row_spec(),
        out_shape=jax.ShapeDtypeStruct((n_rows, D), F32),
        scratch_shapes=[pltpu.VMEM((lay.tm + 2 * HALO, D), F32)],
        compiler_params=_cparams(("parallel",)),
        name="pool_mixer",
    )(h, h, h, mod, g.reshape(1, D), w.astype(BF16), b.reshape(4, 1, POOL_GW), ls.reshape(1, D))


def _seg_sum(x, bd, split=True):
    outs = []
    for q in range(D // RW_GROUP):
        xg = x[:, q * RW_GROUP:(q + 1) * RW_GROUP]
        hi = xg.astype(BF16)
        acc = jnp.dot(hi, bd, preferred_element_type=F32)
        if split:
            acc = acc + jnp.dot((xg - hi.astype(F32)).astype(BF16), bd, preferred_element_type=F32)
        outs.append(acc)
    return jnp.concatenate(outs, axis=-1)


def _rwkv_proj_kernel(h_ref, hp_ref, hn_ref, mod_ref, g_ref, mix_ref, wr_ref, wk_ref, wv_ref, g1_ref, g2_ref,
                      w1_ref, w2_ref, w0_ref, a1_ref, a2_ref, a0_ref, kk_ref, ka_ref, rk_ref, bd_ref,
                      r_out, v_out, kn_out, g_out, bonus_out, lw0_out, k0_out, b0_out, lw1_out, k1_out, b1_out,
                      ext_ref, *, lay):
    tm = lay.tm
    i = pl.program_id(0)
    j, n = lay.seq_tile(i)
    m = mod_ref[0]
    a = _fill_ext(ext_ref, h_ref, hp_ref, hn_ref, g_ref[...], m[:, 0:D], m[:, D:2 * D], j, n, tm)
    prev = ext_ref[HALO - 1:HALO - 1 + tm, :]
    nxt = ext_ref[HALO + 1:HALO + 1 + tm, :]
    xx = 0.5 * (prev + nxt) - a
    x_r, x_w, x_k, x_v, x_a, x_g = [(a + xx * mix_ref[q:q + 1, :]).astype(BF16) for q in range(6)]
    bd = bd_ref[...]
    r = jnp.dot(x_r, wr_ref[...], preferred_element_type=F32)
    k = jnp.dot(x_k, wk_ref[...], preferred_element_type=F32)
    v = jnp.dot(x_v, wv_ref[...], preferred_element_type=F32)
    gl = jnp.dot(x_g, g1_ref[...], preferred_element_type=F32)
    g = jnp.dot(jax.nn.sigmoid(gl).astype(BF16), g2_ref[...], preferred_element_type=F32)
    kk = k * kk_ref[...]
    kn = kk / jnp.maximum(jnp.sqrt(_seg_sum(kk * kk, bd)), 1e-12)
    tw = jnp.tanh(jnp.dot(x_w, w1_ref[...], preferred_element_type=F32))
    aa = jnp.dot(x_a, a1_ref[...], preferred_element_type=F32)
    lane = lax.broadcasted_iota(jnp.int32, (1, tw.shape[1]), 1)
    r_out[...] = r.astype(r_out.dtype)
    v_out[...] = v.astype(v_out.dtype)
    kn_out[...] = kn.astype(kn_out.dtype)
    g_out[...] = g.astype(g_out.dtype)
    rk = rk_ref[...]
    bonus = jnp.zeros_like(v)
    outs = ((lw0_out, k0_out, b0_out), (lw1_out, k1_out, b1_out))
    half = tw.shape[1] // 2
    for d in range(2):
        sel = (lane >= d * half) & (lane < (d + 1) * half)
        wl = w0_ref[d:d + 1, :] + jnp.dot(jnp.where(sel, tw, 0.0).astype(BF16), w2_ref[...], preferred_element_type=F32)
        z = -wl
        w_log = -(jnp.maximum(z, 0.0) + jnp.log(1.0 + jnp.exp(-jnp.abs(z)))) - 0.5
        asig = jax.nn.sigmoid(a0_ref[d:d + 1, :] + jnp.dot(jnp.where(sel, aa, 0.0).astype(BF16), a2_ref[...],
                                                           preferred_element_type=F32))
        k_d = k * (1.0 + (asig - 1.0) * ka_ref[...])
        lw_o, k_o, b_o = outs[d]
        lw_o[...] = -jnp.exp(w_log)
        k_o[...] = k_d.astype(k_o.dtype)
        b_o[...] = (kn * asig).astype(b_o.dtype)
        bonus = bonus + _seg_sum(r * k_d * rk, bd, split=False) * v
    bonus_out[...] = bonus.astype(bonus_out.dtype)


def _block_ones():
    idx = np.arange(RW_GROUP) // 64
    return jnp.asarray(idx[:, None] == idx[None, :], dtype=BF16)


def _rwkv_proj(lay, h, mod, g, p):
    prev, nxt = lay.halo_specs()
    cat = lambda w: jnp.concatenate([w[0], w[1]], axis=-1).astype(BF16)
    stack = lambda w: jnp.concatenate([w[0], w[1]], axis=0).astype(BF16)
    args = [h, h, h, mod, g.reshape(1, D), p['mix'],
            p['w_rkv'][0].astype(BF16), p['w_rkv'][1].astype(BF16), p['w_rkv'][2].astype(BF16),
            p['g1'].astype(BF16), p['g2'].astype(BF16),
            cat(p['w1']), stack(p['w2']), p['w0'], cat(p['a1']), stack(p['a2']), p['a0'],
            p['k_k'].reshape(1, D), p['k_a'].reshape(1, D), p['r_k'].reshape(1, D), _block_ones()]
    in_specs = [lay.row_spec(), prev, nxt, lay.mod_spec()] + [_const_spec(x.shape) for x in args[4:]]
    dtypes = [BF16] * 5 + [F32, BF16, BF16] * 2
    return pl.pallas_call(
        functools.partial(_rwkv_proj_kernel, lay=lay),
        grid=(lay.n_t,),
        in_specs=in_specs,
        out_specs=[lay.row_spec()] * len(dtypes),
        out_shape=[jax.ShapeDtypeStruct((lay.N, D), dt) for dt in dtypes],
        scratch_shapes=[pltpu.VMEM((lay.tm + 2 * HALO, D), F32)],
        compiler_params=_cparams(("parallel",)),
        name="rwkv_proj",
    )(*args)


def _rwkv_scan_kernel(rf, vf, nf, lwf, kf, bf, rb, vb, nb, lwb, kb, bb, yf_out, yb_out, z_ref):
    C, G = RW_CHUNK, RW_GROUP
    c = pl.program_id(1)

    @pl.when(c == 0)
    def _():
        z_ref[...] = jnp.zeros_like(z_ref)

    trow = lax.broadcasted_iota(jnp.int32, (C, C), 0)
    tcol = lax.broadcasted_iota(jnp.int32, (C, C), 1)
    ti = lax.broadcasted_iota(jnp.int32, (C, G), 0)
    ii = lax.broadcasted_iota(jnp.int32, (C, G), 1) % C
    brow = lax.broadcasted_iota(jnp.int32, (G, G), 0)
    bcol = lax.broadcasted_iota(jnp.int32, (G, G), 1)
    m_bd_t = (brow // C) == (bcol // 64)
    m_bd_f = (brow // 64) == (bcol // 64)
    eye_g = brow == bcol

    def bd(x):
        return jnp.where(m_bd_t, jnp.concatenate([x] * (G // C), axis=0), 0.0).astype(BF16)

    def mm(a, b):
        return jnp.dot(a, b, preferred_element_type=F32)

    dirs = ((rf, vf, nf, lwf, kf, bf, yf_out, False), (rb, vb, nb, lwb, kb, bb, yb_out, True))
    ch = []
    for d, (r_ref, v_ref, n_ref, lw_ref, k_ref, b_ref, y_out, rev) in enumerate(dirs):
        tri = (tcol >= trow) if rev else (tcol <= trow)
        strict = (ii > ti) if rev else (ii < ti)
        incl = (ii >= ti) if rev else (ii <= ti)
        lw = lw_ref[...]
        lc = jnp.dot(tri.astype(F32), lw, preferred_element_type=F32, precision=lax.Precision.HIGHEST)
        ltot = lc[0:1, :] if rev else lc[C - 1:C, :]
        e_lc = jnp.exp(lc)
        ie_lc = jnp.exp(-lc)
        e_le = jnp.exp(lc - lw)
        e_bar = jnp.exp(ltot - lc)
        e_tot = jnp.exp(ltot)
        kvec, bvec = k_ref[...], b_ref[...]
        a_t = -n_ref[...] * e_le
        r_t = r_ref[...] * e_lc
        b_t = bvec * ie_lc
        k_t = kvec * ie_lc
        b_b = bvec * e_bar
        k_b = kvec * e_bar
        vall = v_ref[...]
        for g in range(D // G):
            sl = slice(g * G, (g + 1) * G)
            ch.append(dict(
                d=d, g=g, sl=sl, y_out=y_out, strict=strict, incl=incl, v=vall[:, sl], e_tot=e_tot[:, sl],
                ar=jnp.concatenate([a_t[:, sl], r_t[:, sl]], axis=0).astype(BF16),
                bk=jnp.concatenate([bd(b_t[:, sl]), bd(k_t[:, sl])], axis=0),
                lhs_t=jnp.concatenate([b_b[:, sl], k_b[:, sl]], axis=0).astype(BF16)))

    for q in ch:
        aa = lax.dot_general(q['ar'], q['bk'], (((1,), (1,)), ((), ())), preferred_element_type=F32)
        q['a_ab'] = jnp.where(q['strict'], aa[:C, :G], 0.0)
        q['a_ak'] = jnp.where(q['strict'], aa[:C, G:], 0.0)
        q['m_rb'] = jnp.where(q['incl'], aa[C:, :G], 0.0).astype(BF16)
        q['m_rk'] = jnp.where(q['incl'], aa[C:, G:], 0.0)
    for q in ch:
        q['x'] = jnp.where(ii == ti, 1.0, 0.0) + q['a_ab']
        q['p'] = mm(q['a_ab'].astype(BF16), bd(q['a_ab']))
    for q in ch:
        q['wv'] = mm(jnp.concatenate([q['a_ak'], q['m_rk']], axis=0).astype(BF16), bd(q['v']))
        q['z'] = z_ref[q['d'], q['g']]
        q['az'] = mm(q['ar'], q['z'].astype(BF16))
    n_sq = int(np.log2(C)) - 1
    for s in range(n_sq):
        last = s == n_sq - 1
        for q in ch:
            lhs = q['x'] if last else jnp.concatenate([q['x'], q['p']], axis=0)
            xp = mm(lhs.astype(BF16), bd(q['p']))
            q['x'] = q['x'] + xp[:C]
            if not last:
                q['p'] = xp[C:]
    for q in ch:
        q['u'] = mm(q['x'].astype(BF16), bd(q['az'][:C] + q['wv'][:C]))
    for q in ch:
        y = q['az'][C:] + q['wv'][C:] + mm(q['m_rb'], bd(q['u']))
        q['y_out'][:, q['sl']] = y
        rhs_t = jnp.concatenate([q['u'], q['v']], axis=0).astype(BF16)
        zu = lax.dot_general(q['lhs_t'], rhs_t, (((0,), (0,)), ((), ())), preferred_element_type=F32)
        decay = jnp.sum(jnp.where(eye_g, jnp.broadcast_to(q['e_tot'], (G, G)), 0.0), axis=1, keepdims=True)
        z_ref[q['d'], q['g']] = jnp.where(m_bd_f, decay * q['z'] + zu, 0.0)


def _rwkv_scan(lay, r, v, kn, lw0, k0, b0, lw1, k1, b1):
    B, T, L = lay.B, lay.T, lay.L
    C = RW_CHUNK
    cc, cx = L // C, T // C
    nc = cc + cx
    ctx0 = lay.rows_x // C

    def fwd(b, c):
        return (jnp.where(c < cc, ctx0 + b * cc + c, b * cx + c - cc), 0)

    def bwd(b, c):
        return (jnp.where(c < cc, ctx0 + b * cc + (cc - 1 - c), b * cx + (nc - 1 - c)), 0)

    sf, sb = pl.BlockSpec((C, D), fwd), pl.BlockSpec((C, D), bwd)
    return pl.pallas_call(
        _rwkv_scan_kernel,
        grid=(B, nc),
        in_specs=[sf] * 6 + [sb] * 6,
        out_specs=[sf, sb],
        out_shape=[jax.ShapeDtypeStruct((lay.N, D), F32)] * 2,
        scratch_shapes=[pltpu.VMEM((2, D // RW_GROUP, RW_GROUP, RW_GROUP), F32)],
        compiler_params=_cparams(("parallel", "arbitrary")),
        name="rwkv_scan",
    )(r, v, kn, lw0, k0, b0, r, v, kn, lw1, k1, b1)


def _rwkv_out_kernel(h_ref, yf_ref, yb_ref, bonus_ref, g_ref, mod_ref, lnw_ref, lnb_ref, wo_ref, bd_ref, o_ref):
    bd = bd_ref[...]
    y = yf_ref[...] + yb_ref[...]
    mu = _seg_sum(y, bd) * (1.0 / 64)
    yc = y - mu
    var = _seg_sum(yc * yc, bd) * (1.0 / 64)
    yn = yc * lax.rsqrt(var + GN_EPS) * lnw_ref[...] + lnb_ref[...]
    out = (yn + bonus_ref[...]) * g_ref[...]
    res = jnp.dot(out.astype(BF16), wo_ref[...], preferred_element_type=F32)
    o_ref[...] = h_ref[...] + mod_ref[0][:, 2 * D:3 * D] * res


def _rwkv_out(lay, h, yf, yb, bonus, g, mod, ln_w, ln_b, w_o):
    return pl.pallas_call(
        _rwkv_out_kernel,
        grid=(lay.n_t,),
        in_specs=[lay.row_spec()] * 5 + [lay.mod_spec(), _const_spec((1, D)), _const_spec((1, D)),
                                          _const_spec((D, D)), _const_spec((RW_GROUP, RW_GROUP))],
        out_specs=lay.row_spec(),
        out_shape=jax.ShapeDtypeStruct((lay.N, D), F32),
        compiler_params=_cparams(("parallel",)),
        name="rwkv_out",
    )(h, yf, yb, bonus, g, mod, ln_w.reshape(1, D), ln_b.reshape(1, D), w_o.astype(BF16), _block_ones())


RUN_ALIGN = 8


def _router_kernel(h_ref, mod_ref, g_ref, wr_ref, br_ref, upper_ref, gate_ref, lpos_ref, cnt_ref):
    m = mod_ref[0]
    f = _norm_mod(h_ref[...], g_ref[...], m[:, 3 * D:4 * D], m[:, 4 * D:5 * D])
    logits = lax.dot_general(wr_ref[...], f, (((1,), (1,)), ((), ())), preferred_element_type=F32,
                             precision=lax.Precision.HIGHEST) + br_ref[...]
    erow = lax.broadcasted_iota(jnp.int32, logits.shape, 0)
    tops, hots = [], []
    l = logits
    for _ in range(TOP_K):
        mx = jnp.max(l, axis=0, keepdims=True)
        ix = jnp.min(jnp.where(l == mx, erow, N_EXPERTS), axis=0, keepdims=True)
        tops.append(mx)
        hots.append(erow == ix)
        l = jnp.where(erow == ix, -jnp.inf, l)
    es = [jnp.exp(t - tops[0]) for t in tops]
    tot = es[0] + es[1] + es[2] + es[3]
    gate_ref[...] = jnp.concatenate([e / tot for e in es], axis=0)

    ohs = [jnp.where(hh, 1.0, 0.0) for hh in hots]
    cnts = [jnp.sum(oh, axis=1, keepdims=True) for oh in ohs]
    cnt = cnts[0] + cnts[1] + cnts[2] + cnts[3]
    run = jnp.ceil(cnt * (1.0 / RUN_ALIGN)) * RUN_ALIGN
    ee = lax.broadcasted_iota(jnp.int32, (N_EXPERTS, N_EXPERTS), 0)
    ec = lax.broadcasted_iota(jnp.int32, (N_EXPERTS, N_EXPERTS), 1)
    run_row = jnp.sum(jnp.where(ee == ec, jnp.broadcast_to(run, (N_EXPERTS, N_EXPERTS)), 0.0), axis=0, keepdims=True)
    offs = jnp.sum(jnp.where(ec < ee, jnp.broadcast_to(run_row, (N_EXPERTS, N_EXPERTS)), 0.0), axis=1, keepdims=True)
    base = offs
    lpos = []
    for k in range(TOP_K):
        before = jnp.dot(ohs[k].astype(BF16), upper_ref[...], preferred_element_type=F32)
        lpos.append(jnp.sum(ohs[k] * (before + base), axis=0, keepdims=True))
        base = base + cnts[k]
    lpos_ref[...] = jnp.concatenate(lpos, axis=0).astype(jnp.int32)
    cnt_ref[0] = cnt.astype(jnp.int32)


def _moe_router(lay, h, mod, g, w_router, b_router, n_rows):
    tm = lay.tm
    nt = n_rows // tm
    upper = jnp.asarray(np.triu(np.ones((tm, tm), np.float32), 1), dtype=BF16)
    return pl.pallas_call(
        _router_kernel,
        grid=(nt,),
        in_specs=[lay.row_spec(), lay.mod_spec(), _const_spec((1, D)),
                  _const_spec((N_EXPERTS, D)), _const_spec((N_EXPERTS, 1)), _const_spec((tm, tm))],
        out_specs=[pl.BlockSpec((TOP_K, tm), lambda i: (0, i)),
                   pl.BlockSpec((TOP_K, tm), lambda i: (0, i)),
                   pl.BlockSpec((1, N_EXPERTS, 1), lambda i: (i, 0, 0))],
        out_shape=[jax.ShapeDtypeStruct((TOP_K, n_rows), F32),
                   jax.ShapeDtypeStruct((TOP_K, n_rows), jnp.int32),
                   jax.ShapeDtypeStruct((nt, N_EXPERTS, 1), jnp.int32)],
        compiler_params=_cparams(("parallel",)),
        name="moe_router",
    )(h, mod, g.reshape(1, D), w_router.T, b_router.reshape(N_EXPERTS, 1), upper)


def _comp_rows(tm):
    return tm * TOP_K + N_EXPERTS * RUN_ALIGN


def _moe_plan(cnt, n, tm):
    nt = n // tm
    def excl_sum(a, axis):
        k = a.shape[axis]
        before = np.arange(k)[:, None] > np.arange(k)[None, :]
        if axis == 0:
            return jnp.sum(jnp.where(before[:, :, None], a[None, :, :], 0), axis=1)
        return jnp.sum(jnp.where(before[None, :, :], a[:, None, :], 0), axis=2)

    run = (cnt + RUN_ALIGN - 1) // RUN_ALIGN * RUN_ALIGN
    tot = jnp.sum(run, axis=0)
    padded = (tot + MOE_BLOCK - 1) // MOE_BLOCK * MOE_BLOCK
    pad_start = excl_sum(padded[None, :], 1)[0]
    pad_end = pad_start + padded
    base = pad_start[None, :] + excl_sum(run, 0)
    rows_max = n * TOP_K + nt * N_EXPERTS * (RUN_ALIGN - 1) + N_EXPERTS * MOE_BLOCK
    n_blocks = -(-rows_max // MOE_BLOCK)
    rows = n_blocks * MOE_BLOCK
    blk_row = jnp.arange(n_blocks, dtype=jnp.int32) * MOE_BLOCK
    block_expert = jnp.minimum(jnp.sum((pad_end[None, :] <= blk_row[:, None]).astype(jnp.int32), axis=1),
                               N_EXPERTS - 1).astype(jnp.int32)
    spare_rows = 2 * _comp_rows(tm)
    fill_base = jnp.concatenate([pad_start + tot, pad_end[-1:], jnp.full((1,), rows)]).astype(jnp.int32)
    fill_cnt = (jnp.concatenate([padded - tot, rows - pad_end[-1:], jnp.full((1,), spare_rows)])
                // RUN_ALIGN).astype(jnp.int32)
    n_used = (pad_end[-1:] // MOE_BLOCK).astype(jnp.int32)
    cr = _comp_rows(tm)
    ngrp = cr // RUN_ALIGN
    run_g = run // RUN_ALIGN
    goffs = excl_sum(run_g, 1)
    gend = goffs + run_g
    ng = gend[:, -1]
    j = jnp.arange(ngrp, dtype=jnp.int32)
    e_of = jnp.minimum(jnp.sum((gend[:, None, :] <= j[None, :, None]).astype(jnp.int32), axis=2), N_EXPERTS - 1)
    oh = e_of[:, :, None] == jnp.arange(N_EXPERTS, dtype=jnp.int32)[None, None, :]
    pick = lambda a: jnp.sum(jnp.where(oh, a[:, None, :], 0), axis=2)
    row = pick(base) + RUN_ALIGN * (j[None, :] - pick(goffs))
    live = j[None, :] < ng[:, None]
    spare = rows + (jnp.arange(nt, dtype=jnp.int32)[:, None] % 2) * cr + RUN_ALIGN * j[None, :]
    put_rows = jnp.where(live, row, spare).astype(jnp.int32).reshape(nt, 1, ngrp)
    get_rows = jnp.where(live, row, 0).astype(jnp.int32).reshape(nt, 1, ngrp)
    return put_rows, get_rows, ng.astype(jnp.int32), block_expert, fill_base, fill_cnt, n_used, n_blocks


def _group_copies(n_groups, src_of, dst_of, sem, wait):
    def body(j, carry):
        cp = pltpu.make_async_copy(src_of(j), dst_of(j), sem)
        cp.wait() if wait else cp.start()
        return carry

    lax.fori_loop(0, n_groups, body, 0, unroll=8)


def _rows_at(ref, start):
    return ref.at[pl.ds(pl.multiple_of(start, RUN_ALIGN), RUN_ALIGN), :]


def _compact_kernel(fill_base_ref, fill_cnt_ref, put_ref, pput_ref,
                    h_ref, mod_ref, g_ref, lpos_ref, xs_hbm, comp_ref, zero_ref, sems):
    i = pl.program_id(0)
    s = i % 2
    m = mod_ref[0]
    f = _norm_mod(h_ref[...], g_ref[...], m[:, 3 * D:4 * D], m[:, 4 * D:5 * D]).astype(BF16)
    cr, tm = comp_ref.shape[1], h_ref.shape[0]
    q = lax.broadcasted_iota(jnp.int32, (cr, tm), 0)
    lp = lpos_ref[...]
    hit = (lp[0:1, :] == q) | (lp[1:2, :] == q) | (lp[2:3, :] == q) | (lp[3:4, :] == q)
    comp_ref[s] = jnp.dot(jnp.where(hit, 1.0, 0.0).astype(BF16), f, preferred_element_type=F32)

    def copies(slot, rows_ref, wait):
        src = lambda j: _rows_at(comp_ref.at[slot], j * RUN_ALIGN)
        dst = lambda j: _rows_at(xs_hbm, rows_ref[0, 0, j])
        _group_copies(cr // RUN_ALIGN, src, dst, sems.at[slot], wait)

    @pl.when(i == 0)
    def _():
        zero_ref[...] = jnp.zeros_like(zero_ref)
        for e in range(fill_cnt_ref.shape[0]):
            def fill(c, carry, wait=False, e=e):
                cp = pltpu.make_async_copy(zero_ref, _rows_at(xs_hbm, fill_base_ref[e] + c * RUN_ALIGN), sems.at[0])
                cp.wait() if wait else cp.start()
                return carry

            lax.fori_loop(0, fill_cnt_ref[e], fill, 0)
            lax.fori_loop(0, fill_cnt_ref[e], functools.partial(fill, wait=True), 0)

    @pl.when(i > 0)
    def _():
        copies(1 - s, pput_ref, True)

    copies(s, put_ref, False)

    @pl.when(i == pl.num_programs(0) - 1)
    def _():
        copies(s, put_ref, True)


def _group_table(shift, nt, ngrp):
    def idx(i, *_):
        return (jnp.clip(i + shift, 0, nt - 1), 0, 0)

    return pl.BlockSpec((1, 1, ngrp), idx, memory_space=pltpu.SMEM)


def _moe_compact(lay, h, mod, g, lpos, put_rows, fill_base, fill_cnt, n_rows, n_blocks):
    tm = lay.tm
    nt, ngrp = n_rows // tm, _comp_rows(tm) // RUN_ALIGN
    grid_spec = pltpu.PrefetchScalarGridSpec(
        num_scalar_prefetch=2,
        grid=(nt,),
        in_specs=[_group_table(0, nt, ngrp), _group_table(-1, nt, ngrp)]
                 + [pl.BlockSpec((tm, D), lambda i, *_: (i, 0)),
                    pl.BlockSpec((1, 1, 6 * D), lambda i, *_: (lay.mod_idx(i), 0, 0)),
                    pl.BlockSpec((1, D), lambda i, *_: (0, 0)),
                    pl.BlockSpec((TOP_K, tm), lambda i, *_: (0, i))],
        out_specs=pl.BlockSpec(memory_space=pl.ANY),
        scratch_shapes=[pltpu.VMEM((2, _comp_rows(tm), D), F32), pltpu.VMEM((RUN_ALIGN, D), F32),
                        pltpu.SemaphoreType.DMA((2,))],
    )
    return pl.pallas_call(
        _compact_kernel,
        grid_spec=grid_spec,
        out_shape=jax.ShapeDtypeStruct((n_blocks * MOE_BLOCK + 2 * _comp_rows(tm), D), F32),
        compiler_params=_cparams(("arbitrary",)),
        name="moe_compact",
    )(fill_base, fill_cnt, put_rows, put_rows, h, mod, g.reshape(1, D), lpos)


def _expert_kernel(be_ref, nu_ref, x_ref, w1_ref, b1_ref, w2_ref, b2_ref, o_ref):
    del be_ref
    i = pl.program_id(0)

    @pl.when(i < nu_ref[0])
    def _():
        hcat = jnp.dot(x_ref[...].astype(BF16), w1_ref[0], preferred_element_type=F32) + b1_ref[0]
        h_glu = jnp.minimum(hcat[:, :D_FF], SWIGLU_LIMIT)
        h_lin = jnp.clip(hcat[:, D_FF:], -SWIGLU_LIMIT, SWIGLU_LIMIT)
        act = h_glu * jax.nn.sigmoid(SWIGLU_ALPHA * h_glu) * (h_lin + 1.0)
        o_ref[...] = jnp.dot(act.astype(BF16), w2_ref[0], preferred_element_type=F32) + b2_ref[0]

    @pl.when(i >= nu_ref[0])
    def _():
        o_ref[...] = jnp.zeros_like(o_ref)


def _moe_experts(xs, block_expert, n_used, n_blocks, w1, b1, w2, b2):
    def xrow(i, be, nu):
        return (jnp.minimum(i, nu[0] - 1), 0)

    grid_spec = pltpu.PrefetchScalarGridSpec(
        num_scalar_prefetch=2,
        grid=(n_blocks,),
        in_specs=[pl.BlockSpec((MOE_BLOCK, D), xrow),
                  pl.BlockSpec((1, D, 2 * D_FF), lambda i, be, nu: (be[i], 0, 0)),
                  pl.BlockSpec((1, 1, 2 * D_FF), lambda i, be, nu: (be[i], 0, 0)),
                  pl.BlockSpec((1, D_FF, D), lambda i, be, nu: (be[i], 0, 0)),
                  pl.BlockSpec((1, 1, D), lambda i, be, nu: (be[i], 0, 0))],
        out_specs=pl.BlockSpec((MOE_BLOCK, D), lambda i, be, nu: (i, 0)),
    )
    return pl.pallas_call(
        _expert_kernel,
        grid_spec=grid_spec,
        out_shape=jax.ShapeDtypeStruct((n_blocks * MOE_BLOCK, D), F32),
        compiler_params=_cparams(("arbitrary",)),
        name="moe_experts",
    )(block_expert, n_used, xs, w1.astype(BF16), b1.reshape(N_EXPERTS, 1, 2 * D_FF),
      w2.astype(BF16), b2.reshape(N_EXPERTS, 1, D))


def _combine_kernel(ng_ref, get_ref, nget_ref, h_ref, gate_ref, lpos_ref, mod_ref, y_hbm, o_ref, comp_ref, sems):
    i = pl.program_id(0)
    s = i % 2
    cr, tm = comp_ref.shape[1], h_ref.shape[0]

    def copies(slot, rows_ref, wait):
        src = lambda j: _rows_at(y_hbm, rows_ref[0, 0, j])
        dst = lambda j: _rows_at(comp_ref.at[slot], j * RUN_ALIGN)
        _group_copies(cr // RUN_ALIGN, src, dst, sems.at[slot], wait)

    @pl.when(i == 0)
    def _():
        copies(0, get_ref, False)

    @pl.when(i + 1 < pl.num_programs(0))
    def _():
        copies(1 - s, nget_ref, False)

    q = lax.broadcasted_iota(jnp.int32, (tm, cr), 1)
    gates, lp = gate_ref[...], lpos_ref[...]
    pg = jnp.where(lp[:, 0:1] == q, gates[:, 0:1], 0.0)
    for k in range(1, TOP_K):
        pg = pg + jnp.where(lp[:, k:k + 1] == q, gates[:, k:k + 1], 0.0)
    copies(s, get_ref, True)
    row = lax.broadcasted_iota(jnp.int32, (cr, 1), 0)
    yc = jnp.where(row < ng_ref[i] * RUN_ALIGN, comp_ref[s], 0.0).astype(BF16)
    y = jnp.dot(pg.astype(BF16), yc, preferred_element_type=F32)
    o_ref[...] = h_ref[...] + mod_ref[0][:, 5 * D:6 * D] * y


def _moe_combine(lay, h, mod, yb, gates_t, lpos_t, get_rows, ng, n_rows):
    tm = lay.tm
    nt, ngrp = n_rows // tm, _comp_rows(tm) // RUN_ALIGN
    grid_spec = pltpu.PrefetchScalarGridSpec(
        num_scalar_prefetch=1,
        grid=(nt,),
        in_specs=[_group_table(0, nt, ngrp), _group_table(1, nt, ngrp),
                  pl.BlockSpec((tm, D), lambda i, *_: (i, 0)),
                  pl.BlockSpec((tm, TOP_K), lambda i, *_: (i, 0)),
                  pl.BlockSpec((tm, TOP_K), lambda i, *_: (i, 0)),
                  pl.BlockSpec((1, 1, 6 * D), lambda i, *_: (lay.mod_idx(i), 0, 0)),
                  pl.BlockSpec(memory_space=pl.ANY)],
        out_specs=pl.BlockSpec((tm, D), lambda i, *_: (i, 0)),
        scratch_shapes=[pltpu.VMEM((2, _comp_rows(tm), D), F32), pltpu.SemaphoreType.DMA((2,))],
    )
    return pl.pallas_call(
        _combine_kernel,
        grid_spec=grid_spec,
        out_shape=jax.ShapeDtypeStruct((n_rows, D), F32),
        compiler_params=_cparams(("arbitrary",)),
        name="moe_combine",
    )(ng, get_rows, get_rows, h, gates_t, lpos_t, mod, yb)


def _moe_layer(lay, h, mod, g, w_router, b_router, w1, b1, w2, b2, n_rows):
    gates, lpos, cnt = _moe_router(lay, h, mod, g, w_router, b_router, n_rows)
    put_rows, get_rows, ng, block_expert, fill_base, fill_cnt, n_used, n_blocks = _moe_plan(cnt[:, :, 0], n_rows, lay.tm)
    xs = _moe_compact(lay, h, mod, g, lpos, put_rows, fill_base, fill_cnt, n_rows, n_blocks)
    yb = _moe_experts(xs, block_expert, n_used, n_blocks, w1, b1, w2, b2)
    return _moe_combine(lay, h, mod, yb, gates.T, lpos.T, get_rows, ng, n_rows)


def _final_kernel(h_ref, g_ref, o_ref):
    h = h_ref[...]
    ms = jnp.mean(h * h, axis=-1, keepdims=True)
    o_ref[...] = h * lax.rsqrt(ms + NORM_EPS) * g_ref[...]


def _final_norm(lay, h, g, n_rows):
    return pl.pallas_call(
        _final_kernel,
        grid=(n_rows // lay.tm,),
        in_specs=[lay.row_spec(), _const_spec((1, D))],
        out_specs=lay.row_spec(),
        out_shape=jax.ShapeDtypeStruct((n_rows, D), F32),
        compiler_params=_cparams(("parallel",)),
        name="final_norm",
    )(h, g.reshape(1, D))


def kernel(x, c, ctx, c_ctx, ada_w, ada_b, norm_g, attn_w_qkv, attn_b_qkv, attn_w_o, attn_b_o, attn_sinks, pool_w, pool_b, pool_scale, rwkv_mix, rwkv_w_rkv, rwkv_w0, rwkv_w1, rwkv_w2, rwkv_a0, rwkv_a1, rwkv_a2, rwkv_g1, rwkv_g2, rwkv_k_k, rwkv_k_a, rwkv_r_k, rwkv_ln_w, rwkv_ln_b, rwkv_w_o, moe_w_router, moe_b_router, moe_w1, moe_b1, moe_w2, moe_b2, final_g):
    B, T, _ = x.shape
    L = ctx.shape[1]
    depth = ada_w.shape[0]
    lay = _Layout(B, T, L)
    h = jnp.concatenate([x.reshape(B * T, D), ctx.reshape(B * L, D)], axis=0)

    n_c = B + 1
    n_c_pad = -(-n_c // 8) * 8
    c_all = jnp.concatenate([c, c_ctx[None, :], jnp.zeros((n_c_pad - n_c, D), F32)], axis=0)
    mods = _ada_mods(c_all, ada_w, ada_b)
    cos, sin = _rope_tables(T, L)

    for layer in range(depth):
        last = layer == depth - 1
        kind, j = layer % 3, layer // 3
        mod = mods[layer].reshape(n_c_pad, 1, 6 * D)
        n_rows = lay.rows_x if last else lay.N
        if kind == 0:
            q, k, v = _attn_qkv(lay, h, mod, norm_g[layer, 0], attn_w_qkv[j], attn_b_qkv[j], cos, sin)
            o = _attention(lay, q, k, v, attn_sinks[j], not last)
            h = _linear_residual(lay, h, o, mod, attn_w_o[j], attn_b_o[j], 2, n_rows)
        elif kind == 1:
            h = _pool_mixer(lay, h, mod, norm_g[layer, 0], pool_w[j], pool_b[j], pool_scale[j], n_rows)
        else:
            p = dict(mix=rwkv_mix[j], w_rkv=rwkv_w_rkv[j], w0=rwkv_w0[j], w1=rwkv_w1[j], w2=rwkv_w2[j],
                     a0=rwkv_a0[j], a1=rwkv_a1[j], a2=rwkv_a2[j], g1=rwkv_g1[j], g2=rwkv_g2[j],
                     k_k=rwkv_k_k[j], k_a=rwkv_k_a[j], r_k=rwkv_r_k[j])
            r, v, kn, g, bonus, lw0, k0, b0, lw1, k1, b1 = _rwkv_proj(lay, h, mod, norm_g[layer, 0], p)
            yf, yb = _rwkv_scan(lay, r, v, kn, lw0, k0, b0, lw1, k1, b1)
            h = _rwkv_out(lay, h, yf, yb, bonus, g, mod, rwkv_ln_w[j], rwkv_ln_b[j], rwkv_w_o[j])
        h = _moe_layer(lay, h, mod, norm_g[layer, 1], moe_w_router[layer], moe_b_router[layer],
                       moe_w1[layer], moe_b1[layer], moe_w2[layer], moe_b2[layer], n_rows)
    out = _final_norm(lay, h, final_g, lay.rows_x)
    return out.reshape(B, T, D)
```

```python
import functools

import jax
import jax.numpy as jnp
import numpy as np
from jax import lax
from jax.experimental import pallas as pl
from jax.experimental.pallas import tpu as pltpu

F32 = jnp.float32
BF16 = jnp.bfloat16

D = 1024
NORM_EPS = 1e-5
GRID_W = 64
HEAD_DIM = 64
N_HEADS = 16
N_KV = 4
Q_DIM = 1024
KV_DIM = 256
QK_DIM = Q_DIM + KV_DIM
KVD_DIM = 2 * KV_DIM
QKD_DIM = Q_DIM + KVD_DIM
WINDOW = 128
QB = 128
ROPE_BASE = 10000.0
POOL_SIZES = (2, 4, 8, 16)
POOL_GW = 256
HALO = 8
GN_EPS = 64e-5
N_EXPERTS = 32
TOP_K = 4
D_FF = 1024
SWIGLU_LIMIT = 7.0
SWIGLU_ALPHA = 1.702
MOE_BLOCK = 512
RW_CHUNK = 64
RW_GROUP = 256
VMEM_LIMIT = 56 * 1024 * 1024


def _cparams(sem):
    return pltpu.CompilerParams(dimension_semantics=sem, vmem_limit_bytes=VMEM_LIMIT)


def _norm_mod(h, g, shift, scale):
    ms = jnp.mean(h * h, axis=-1, keepdims=True)
    y = h * lax.rsqrt(ms + NORM_EPS) * g
    return y * (1.0 + scale) + shift


class _Layout:
    def __init__(self, B, T, L):
        self.B, self.T, self.L = B, T, L
        self.tm = 256 if L % 256 == 0 else 128
        self.rows_x, self.rows_c = B * T, B * L
        self.N = self.rows_x + self.rows_c
        self.tx, self.tc = T // self.tm, L // self.tm
        self.n_xt = self.rows_x // self.tm
        self.n_t = self.N // self.tm

    def mod_idx(self, i):
        return jnp.where(i < self.n_xt, i // self.tx, self.B)

    def mod_spec(self):
        return pl.BlockSpec((1, 1, 6 * D), lambda i: (self.mod_idx(i), 0, 0))

    def row_spec(self, width=D):
        return pl.BlockSpec((self.tm, width), lambda i: (i, 0))

    def seq_tile(self, i):
        j = jnp.where(i < self.n_xt, i % self.tx, (i - self.n_xt) % self.tc)
        n = jnp.where(i < self.n_xt, self.tx, self.tc)
        return j, n

    def halo_specs(self):
        hb = self.tm // HALO
        last = self.N // HALO - 1
        prev = pl.BlockSpec((HALO, D), lambda i: (jnp.maximum(i * hb - 1, 0), 0))
        nxt = pl.BlockSpec((HALO, D), lambda i: (jnp.minimum((i + 1) * hb, last), 0))
        return prev, nxt


def _const_spec(shape):
    nd = len(shape)
    return pl.BlockSpec(shape, lambda *_: (0,) * nd)


def _ada_kernel(c_ref, w_ref, b_ref, o_ref):
    c = c_ref[...]
    s = c * jax.nn.sigmoid(c)
    o_ref[0] = jnp.dot(s, w_ref[0], preferred_element_type=F32, precision=lax.Precision.HIGHEST) + b_ref[0]


def _ada_mods(c_all, ada_w, ada_b):
    depth = ada_w.shape[0]
    R = c_all.shape[0]
    nt = 1536
    return pl.pallas_call(
        _ada_kernel,
        grid=(depth, 6 * D // nt),
        in_specs=[pl.BlockSpec((R, D), lambda l, j: (0, 0)),
                  pl.BlockSpec((1, D, nt), lambda l, j: (l, 0, j)),
                  pl.BlockSpec((1, 1, nt), lambda l, j: (l, 0, j))],
        out_specs=pl.BlockSpec((1, R, nt), lambda l, j: (l, 0, j)),
        out_shape=jax.ShapeDtypeStruct((depth, R, 6 * D), F32),
        compiler_params=_cparams(("parallel", "parallel")),
        name="ada_mod",
    )(c_all, ada_w, ada_b.reshape(depth, 1, 6 * D))


def _qkv_kernel(h_ref, mod_ref, g_ref, w_ref, b_ref, cos_ref, sin_ref, q_ref, k_ref, v_ref):
    m = mod_ref[0]
    a = _norm_mod(h_ref[...], g_ref[...], m[:, 0:D], m[:, D:2 * D])
    qkv = jnp.dot(a.astype(BF16), w_ref[...], preferred_element_type=F32) + b_ref[...]
    qk = qkv[:, :QKD_DIM]
    lane = lax.broadcasted_iota(jnp.int32, (1, QKD_DIM), 1)
    first = (lane % 32) < 16
    rot = jnp.where(first, -pltpu.roll(qk, QKD_DIM - 16, 1), pltpu.roll(qk, 16, 1))
    qk = qk * cos_ref[...] + rot * sin_ref[...]
    q_ref[...] = (qk[:, :Q_DIM] * (HEAD_DIM ** -0.5)).astype(BF16)
    k_ref[...] = qk[:, Q_DIM:].astype(BF16)
    v_ref[...] = qkv[:, QKD_DIM:].astype(BF16)


def _dup_heads(w):
    w4 = w.reshape(w.shape[:-1] + (N_KV, 1, HEAD_DIM))
    return jnp.broadcast_to(w4, w.shape[:-1] + (N_KV, 2, HEAD_DIM)).reshape(w.shape[:-1] + (KVD_DIM,))


def _rope_tables(T, L):
    rows = T // GRID_W
    n_freq = HEAD_DIM // 4
    inv = ROPE_BASE ** (-jnp.arange(n_freq, dtype=F32) / n_freq)
    row_ang = jnp.arange(rows, dtype=F32)[:, None] * inv
    col_ang = jnp.arange(GRID_W, dtype=F32)[:, None] * inv
    ang_r = jnp.broadcast_to(row_ang[:, None, :], (rows, GRID_W, n_freq)).reshape(T, n_freq)
    ang_c = jnp.broadcast_to(col_ang[None, :, :], (rows, GRID_W, n_freq)).reshape(T, n_freq)
    ang = jnp.concatenate([ang_r, ang_r, ang_c, ang_c], axis=-1)
    cos = jnp.concatenate([jnp.cos(ang), jnp.ones((L, HEAD_DIM), F32)], axis=0)
    sin = jnp.concatenate([jnp.sin(ang), jnp.zeros((L, HEAD_DIM), F32)], axis=0)
    reps = QKD_DIM // HEAD_DIM
    return jnp.tile(cos, (1, reps)), jnp.tile(sin, (1, reps))


def _attn_qkv(lay, h, mod, g, w_qkv, b_qkv, cos, sin):
    tm = lay.tm

    def tab_idx(i):
        return (jnp.where(i < lay.n_xt, i % lay.tx, lay.tx + (i - lay.n_xt) % lay.tc), 0)

    wq, wk, wv = w_qkv[:, :Q_DIM], w_qkv[:, Q_DIM:QK_DIM], w_qkv[:, QK_DIM:]
    bq, bk, bv = b_qkv[:Q_DIM], b_qkv[Q_DIM:QK_DIM], b_qkv[QK_DIM:]
    w = jnp.concatenate([wq, _dup_heads(wk), _dup_heads(wv)], axis=1).astype(BF16)
    b = jnp.concatenate([bq, _dup_heads(bk), _dup_heads(bv)]).reshape(1, -1)
    width = Q_DIM + 2 * KVD_DIM
    return pl.pallas_call(
        _qkv_kernel,
        grid=(lay.n_t,),
        in_specs=[lay.row_spec(), lay.mod_spec(), _const_spec((1, D)),
                  _const_spec((D, width)), _const_spec((1, width)),
                  pl.BlockSpec((tm, QKD_DIM), tab_idx), pl.BlockSpec((tm, QKD_DIM), tab_idx)],
        out_specs=[lay.row_spec(Q_DIM), lay.row_spec(KVD_DIM), lay.row_spec(KVD_DIM)],
        out_shape=[jax.ShapeDtypeStruct((lay.N, Q_DIM), BF16),
                   jax.ShapeDtypeStruct((lay.N, KVD_DIM), BF16),
                   jax.ShapeDtypeStruct((lay.N, KVD_DIM), BF16)],
        compiler_params=_cparams(("parallel",)),
        name="attn_qkv",
    )(h, mod, g.reshape(1, D), w, b, cos, sin)


def _attn_heads(q, kk, vv, mask, sink_ref, o_ref):
    R = q.shape[0]
    G = N_HEADS // N_KV
    TILE = 2 * HEAD_DIM
    lo = lax.broadcasted_iota(jnp.int32, (1, TILE), 1) < HEAD_DIM
    order = (0, 2, 1, 3)
    scores = []
    for g in range(N_KV):
        pieces = []
        for j in order:
            hq = g * G + j
            qt = q[:, (hq // 2) * TILE:(hq // 2 + 1) * TILE]
            pieces.append(jnp.where(lo if hq % 2 == 0 else ~lo, qt, jnp.zeros_like(qt)))
        qs = jnp.concatenate(pieces, axis=0)
        scores.append(lax.dot_general(qs, kk[:, g * TILE:(g + 1) * TILE], (((1,), (1,)), ((), ())),
                                      preferred_element_type=F32))
    probs, esinks = [], []
    for g in range(N_KV):
        s = scores[g]
        if mask is not None:
            s = jnp.where(jnp.concatenate([mask] * G, axis=0), s, -jnp.inf)
        sink = jnp.concatenate([jnp.full((R, 1), sink_ref[g * G + j], F32) for j in order], axis=0)
        m = jnp.maximum(jnp.max(s, axis=-1, keepdims=True), sink)
        probs.append(jnp.exp((s - m).astype(BF16)))
        esinks.append(jnp.exp(sink - m))
    for g in range(N_KV):
        vd = vv[:, g * TILE:(g + 1) * TILE]
        one = jnp.ones_like(vd)
        p, es = probs[g], esinks[g]
        half = (G // 2) * R
        oe = jnp.dot(p[:half], jnp.where(lo, vd, one), preferred_element_type=F32)
        oo = jnp.dot(p[half:], jnp.where(lo, one, vd), preferred_element_type=F32)
        re = oe / (pltpu.roll(oe, HEAD_DIM, 1) + es[:half])
        ro = oo / (pltpu.roll(oo, HEAD_DIM, 1) + es[half:])
        for t in range(G // 2):
            tile = jnp.where(lo, re[t * R:(t + 1) * R], ro[t * R:(t + 1) * R])
            c = g * (G // 2) + t
            o_ref[:, c * TILE:(c + 1) * TILE] = tile.astype(o_ref.dtype)


def _attn_kernel(sink_ref, q_ref, kp_ref, kc_ref, kn_ref, vp_ref, vc_ref, vn_ref, kx_ref, vx_ref, o_ref, *, T, L):
    i = pl.program_id(1)
    nqb = T // QB

    @pl.when(i < nqb)
    def _():
        kk = jnp.concatenate([kx_ref[...], kp_ref[...], kc_ref[...], kn_ref[...]], axis=0)
        vv = jnp.concatenate([vx_ref[...], vp_ref[...], vc_ref[...], vn_ref[...]], axis=0)
        S = L + 3 * QB
        col = lax.broadcasted_iota(jnp.int32, (QB, S), 1)
        row = lax.broadcasted_iota(jnp.int32, (QB, S), 0)
        rel = col - L - QB - row
        kpos = i * QB - QB + (col - L)
        local_ok = (jnp.abs(rel) <= WINDOW) & (kpos >= 0) & (kpos < T)
        mask = (col < L) | local_ok
        _attn_heads(q_ref[...], kk, vv, mask, sink_ref, o_ref)

    @pl.when(i >= nqb)
    def _():
        _attn_heads(q_ref[...], kx_ref[...], vx_ref[...], None, sink_ref, o_ref)


def _attention(lay, q, k, v, sinks, need_ctx):
    B, T, L = lay.B, lay.T, lay.L
    nqb, ncb = T // QB, L // QB
    cq = lay.rows_x // QB
    cb = lay.rows_x // L
    smem = pl.BlockSpec(memory_space=pltpu.SMEM)

    def cur(b, i):
        return (jnp.where(i < nqb, b * nqb + i, cq + b * ncb + (i - nqb)), 0)

    def prev(b, i):
        return (b * nqb + jnp.clip(i - 1, 0, nqb - 1), 0)

    def mid(b, i):
        return (b * nqb + jnp.minimum(i, nqb - 1), 0)

    def nxt(b, i):
        return (b * nqb + jnp.minimum(i + 1, nqb - 1), 0)

    def ctxb(b, i):
        return (cb + b, 0)

    kv_specs = [pl.BlockSpec((QB, KVD_DIM), prev), pl.BlockSpec((QB, KVD_DIM), mid), pl.BlockSpec((QB, KVD_DIM), nxt)]
    n_out = lay.N if need_ctx else lay.rows_x
    return pl.pallas_call(
        functools.partial(_attn_kernel, T=T, L=L),
        grid=(B, nqb + (ncb if need_ctx else 0)),
        in_specs=[smem, pl.BlockSpec((QB, Q_DIM), cur)] + kv_specs + kv_specs
                 + [pl.BlockSpec((L, KVD_DIM), ctxb), pl.BlockSpec((L, KVD_DIM), ctxb)],
        out_specs=pl.BlockSpec((QB, Q_DIM), cur),
        out_shape=jax.ShapeDtypeStruct((n_out, Q_DIM), BF16),
        compiler_params=_cparams(("parallel", "arbitrary")),
        name="attention",
    )(sinks, q, k, k, k, v, v, v, k, v)


def _linres_kernel(h_ref, x_ref, mod_ref, w_ref, b_ref, o_ref, *, gate_slot):
    gate = mod_ref[0][:, gate_slot * D:(gate_slot + 1) * D]
    y = jnp.dot(x_ref[...], w_ref[...], preferred_element_type=F32) + b_ref[...]
    o_ref[...] = h_ref[...] + gate * y


def _linear_residual(lay, h, x, mod, w, b, gate_slot, n_rows):
    return pl.pallas_call(
        functools.partial(_linres_kernel, gate_slot=gate_slot),
        grid=(n_rows // lay.tm,),
        in_specs=[lay.row_spec(), lay.row_spec(x.shape[1]), lay.mod_spec(),
                  _const_spec(w.shape), _const_spec((1, D))],
        out_specs=lay.row_spec(),
        out_shape=jax.ShapeDtypeStruct((n_rows, D), F32),
        compiler_params=_cparams(("parallel",)),
        name="linear_residual",
    )(h, x, mod, w.astype(BF16), b.reshape(1, D))


def _fill_ext(ext_ref, h_ref, hp_ref, hn_ref, g, shift, scale, j, n, tm):
    a = _norm_mod(h_ref[...], g, shift, scale)
    ap = _norm_mod(hp_ref[...], g, shift, scale)
    an = _norm_mod(hn_ref[...], g, shift, scale)
    ext_ref[0:HALO, :] = jnp.where(j == 0, 0.0, ap)
    ext_ref[HALO:HALO + tm, :] = a
    ext_ref[HALO + tm:2 * HALO + tm, :] = jnp.where(j == n - 1, 0.0, an)
    return a


def _pool_kernel(h_ref, hp_ref, hn_ref, mod_ref, g_ref, w_ref, b_ref, ls_ref, o_ref, ext_ref, *, lay):
    tm = lay.tm
    i = pl.program_id(0)
    j, n = lay.seq_tile(i)
    m = mod_ref[0]
    a = _fill_ext(ext_ref, h_ref, hp_ref, hn_ref, g_ref[...], m[:, 0:D], m[:, D:2 * D], j, n, tm)
    t = j * tm + lax.broadcasted_iota(jnp.int32, (tm, 1), 0)
    n_tok = n * tm
    outs = []
    for gi, size in enumerate(POOL_SIZES):
        sl = slice(gi * POOL_GW, (gi + 1) * POOL_GW)
        acc = ext_ref[HALO - size // 2:HALO - size // 2 + tm, sl]
        for o in range(-size // 2 + 1, size - size // 2):
            acc = acc + ext_ref[HALO + o:HALO + o + tm, sl]
        lo = jnp.maximum(t - size // 2, 0)
        hi = jnp.minimum(t + size - size // 2, n_tok)
        d = acc / (hi - lo).astype(F32) - a[:, sl]
        outs.append(jnp.dot(d.astype(BF16), w_ref[gi], preferred_element_type=F32) + b_ref[gi])
    y = jnp.concatenate(outs, axis=-1) * ls_ref[...]
    o_ref[...] = h_ref[...] + m[:, 2 * D:3 * D] * y


def _pool_mixer(lay, h, mod, g, w, b, ls, n_rows):
    prev, nxt = lay.halo_specs()
    return pl.pallas_call(
        functools.partial(_pool_kernel, lay=lay),
        grid=(n_rows // lay.tm,),
        in_specs=[lay.row_spec(), prev, nxt, lay.mod_spec(), _const_spec((1, D)),
                  _const_spec(w.shape), _const_spec((4, 1, POOL_GW)), _const_spec((1, D))],
        out_specs=lay.row_spec(),
        out_shape=jax.ShapeDtypeStruct((n_rows, D), F32),
        scratch_shapes=[pltpu.VMEM((lay.tm + 2 * HALO, D), F32)],
        compiler_params=_cparams(("parallel",)),
        name="pool_mixer",
    )(h, h, h, mod, g.reshape(1, D), w.astype(BF16), b.reshape(4, 1, POOL_GW), ls.reshape(1, D))


def _seg_sum(x, bd, split=True):
    outs = []
    for q in range(D // RW_GROUP):
        xg = x[:, q * RW_GROUP:(q + 1) * RW_GROUP]
        hi = xg.astype(BF16)
        acc = jnp.dot(hi, bd, preferred_element_type=F32)
        if split:
            acc = acc + jnp.dot((xg - hi.astype(F32)).astype(BF16), bd, preferred_element_type=F32)
        outs.append(acc)
    return jnp.concatenate(outs, axis=-1)


def _rwkv_proj_kernel(h_ref, hp_ref, hn_ref, mod_ref, g_ref, mix_ref, wr_ref, wk_ref, wv_ref, g1_ref, g2_ref,
                      w1_ref, w2_ref, w0_ref, a1_ref, a2_ref, a0_ref, kk_ref, ka_ref, rk_ref, bd_ref,
                      r_out, v_out, kn_out, g_out, bonus_out, lw0_out, k0_out, b0_out, lw1_out, k1_out, b1_out,
                      ext_ref, *, lay):
    tm = lay.tm
    i = pl.program_id(0)
    j, n = lay.seq_tile(i)
    m = mod_ref[0]
    a = _fill_ext(ext_ref, h_ref, hp_ref, hn_ref, g_ref[...], m[:, 0:D], m[:, D:2 * D], j, n, tm)
    prev = ext_ref[HALO - 1:HALO - 1 + tm, :]
    nxt = ext_ref[HALO + 1:HALO + 1 + tm, :]
    xx = 0.5 * (prev + nxt) - a
    x_r, x_w, x_k, x_v, x_a, x_g = [(a + xx * mix_ref[q:q + 1, :]).astype(BF16) for q in range(6)]
    bd = bd_ref[...]
    r = jnp.dot(x_r, wr_ref[...], preferred_element_type=F32)
    k = jnp.dot(x_k, wk_ref[...], preferred_element_type=F32)
    v = jnp.dot(x_v, wv_ref[...], preferred_element_type=F32)
    gl = jnp.dot(x_g, g1_ref[...], preferred_element_type=F32)
    g = jnp.dot(jax.nn.sigmoid(gl).astype(BF16), g2_ref[...], preferred_element_type=F32)
    kk = k * kk_ref[...]
    kn = kk / jnp.maximum(jnp.sqrt(_seg_sum(kk * kk, bd)), 1e-12)
    tw = jnp.tanh(jnp.dot(x_w, w1_ref[...], preferred_element_type=F32))
    aa = jnp.dot(x_a, a1_ref[...], preferred_element_type=F32)
    lane = lax.broadcasted_iota(jnp.int32, (1, tw.shape[1]), 1)
    r_out[...] = r.astype(r_out.dtype)
    v_out[...] = v.astype(v_out.dtype)
    kn_out[...] = kn.astype(kn_out.dtype)
    g_out[...] = g.astype(g_out.dtype)
    rk = rk_ref[...]
    bonus = jnp.zeros_like(v)
    outs = ((lw0_out, k0_out, b0_out), (lw1_out, k1_out, b1_out))
    half = tw.shape[1] // 2
    for d in range(2):
        sel = (lane >= d * half) & (lane < (d + 1) * half)
        wl = w0_ref[d:d + 1, :] + jnp.dot(jnp.where(sel, tw, 0.0).astype(BF16), w2_ref[...], preferred_element_type=F32)
        z = -wl
        w_log = -(jnp.maximum(z, 0.0) + jnp.log(1.0 + jnp.exp(-jnp.abs(z)))) - 0.5
        asig = jax.nn.sigmoid(a0_ref[d:d + 1, :] + jnp.dot(jnp.where(sel, aa, 0.0).astype(BF16), a2_ref[...],
                                                           preferred_element_type=F32))
        k_d = k * (1.0 + (asig - 1.0) * ka_ref[...])
        lw_o, k_o, b_o = outs[d]
        lw_o[...] = -jnp.exp(w_log)
        k_o[...] = k_d.astype(k_o.dtype)
        b_o[...] = (kn * asig).astype(b_o.dtype)
        bonus = bonus + _seg_sum(r * k_d * rk, bd, split=False) * v
    bonus_out[...] = bonus.astype(bonus_out.dtype)


def _block_ones():
    idx = np.arange(RW_GROUP) // 64
    return jnp.asarray(idx[:, None] == idx[None, :], dtype=BF16)


def _rwkv_proj(lay, h, mod, g, p):
    prev, nxt = lay.halo_specs()
    cat = lambda w: jnp.concatenate([w[0], w[1]], axis=-1).astype(BF16)
    stack = lambda w: jnp.concatenate([w[0], w[1]], axis=0).astype(BF16)
    args = [h, h, h, mod, g.reshape(1, D), p['mix'],
            p['w_rkv'][0].astype(BF16), p['w_rkv'][1].astype(BF16), p['w_rkv'][2].astype(BF16),
            p['g1'].astype(BF16), p['g2'].astype(BF16),
            cat(p['w1']), stack(p['w2']), p['w0'], cat(p['a1']), stack(p['a2']), p['a0'],
            p['k_k'].reshape(1, D), p['k_a'].reshape(1, D), p['r_k'].reshape(1, D), _block_ones()]
    in_specs = [lay.row_spec(), prev, nxt, lay.mod_spec()] + [_const_spec(x.shape) for x in args[4:]]
    dtypes = [BF16] * 5 + [F32, BF16, BF16] * 2
    return pl.pallas_call(
        functools.partial(_rwkv_proj_kernel, lay=lay),
        grid=(lay.n_t,),
        in_specs=in_specs,
        out_specs=[lay.row_spec()] * len(dtypes),
        out_shape=[jax.ShapeDtypeStruct((lay.N, D), dt) for dt in dtypes],
        scratch_shapes=[pltpu.VMEM((lay.tm + 2 * HALO, D), F32)],
        compiler_params=_cparams(("parallel",)),
        name="rwkv_proj",
    )(*args)


def _rwkv_scan_kernel(rf, vf, nf, lwf, kf, bf, rb, vb, nb, lwb, kb, bb, yf_out, yb_out, z_ref):
    C, G = RW_CHUNK, RW_GROUP
    c = pl.program_id(1)

    @pl.when(c == 0)
    def _():
        z_ref[...] = jnp.zeros_like(z_ref)

    trow = lax.broadcasted_iota(jnp.int32, (C, C), 0)
    tcol = lax.broadcasted_iota(jnp.int32, (C, C), 1)
    ti = lax.broadcasted_iota(jnp.int32, (C, G), 0)
    ii = lax.broadcasted_iota(jnp.int32, (C, G), 1) % C
    brow = lax.broadcasted_iota(jnp.int32, (G, G), 0)
    bcol = lax.broadcasted_iota(jnp.int32, (G, G), 1)
    m_bd_t = (brow // C) == (bcol // 64)
    m_bd_f = (brow // 64) == (bcol // 64)
    eye_g = brow == bcol

    def bd(x):
        return jnp.where(m_bd_t, jnp.concatenate([x] * (G // C), axis=0), 0.0).astype(BF16)

    def mm(a, b):
        return jnp.dot(a, b, preferred_element_type=F32)

    dirs = ((rf, vf, nf, lwf, kf, bf, yf_out, False), (rb, vb, nb, lwb, kb, bb, yb_out, True))
    ch = []
    for d, (r_ref, v_ref, n_ref, lw_ref, k_ref, b_ref, y_out, rev) in enumerate(dirs):
        tri = (tcol >= trow) if rev else (tcol <= trow)
        strict = (ii > ti) if rev else (ii < ti)
        incl = (ii >= ti) if rev else (ii <= ti)
        lw = lw_ref[...]
        lc = jnp.dot(tri.astype(F32), lw, preferred_element_type=F32, precision=lax.Precision.HIGHEST)
        ltot = lc[0:1, :] if rev else lc[C - 1:C, :]
        e_lc = jnp.exp(lc)
        ie_lc = jnp.exp(-lc)
        e_le = jnp.exp(lc - lw)
        e_bar = jnp.exp(ltot - lc)
        e_tot = jnp.exp(ltot)
        kvec, bvec = k_ref[...], b_ref[...]
        a_t = -n_ref[...] * e_le
        r_t = r_ref[...] * e_lc
        b_t = bvec * ie_lc
        k_t = kvec * ie_lc
        b_b = bvec * e_bar
        k_b = kvec * e_bar
        vall = v_ref[...]
        for g in range(D // G):
            sl = slice(g * G, (g + 1) * G)
            ch.append(dict(
                d=d, g=g, sl=sl, y_out=y_out, strict=strict, incl=incl, v=vall[:, sl], e_tot=e_tot[:, sl],
                ar=jnp.concatenate([a_t[:, sl], r_t[:, sl]], axis=0).astype(BF16),
                bk=jnp.concatenate([bd(b_t[:, sl]), bd(k_t[:, sl])], axis=0),
                lhs_t=jnp.concatenate([b_b[:, sl], k_b[:, sl]], axis=0).astype(BF16)))

    for q in ch:
        aa = lax.dot_general(q['ar'], q['bk'], (((1,), (1,)), ((), ())), preferred_element_type=F32)
        q['a_ab'] = jnp.where(q['strict'], aa[:C, :G], 0.0)
        q['a_ak'] = jnp.where(q['strict'], aa[:C, G:], 0.0)
        q['m_rb'] = jnp.where(q['incl'], aa[C:, :G], 0.0).astype(BF16)
        q['m_rk'] = jnp.where(q['incl'], aa[C:, G:], 0.0)
    for q in ch:
        q['x'] = jnp.where(ii == ti, 1.0, 0.0) + q['a_ab']
        q['p'] = mm(q['a_ab'].astype(BF16), bd(q['a_ab']))
    for q in ch:
        q['wv'] = mm(jnp.concatenate([q['a_ak'], q['m_rk']], axis=0).astype(BF16), bd(q['v']))
        q['z'] = z_ref[q['d'], q['g']]
        q['az'] = mm(q['ar'], q['z'].astype(BF16))
    n_sq = int(np.log2(C)) - 1
    for s in range(n_sq):
        last = s == n_sq - 1
        for q in ch:
            lhs = q['x'] if last else jnp.concatenate([q['x'], q['p']], axis=0)
            xp = mm(lhs.astype(BF16), bd(q['p']))
            q['x'] = q['x'] + xp[:C]
            if not last:
                q['p'] = xp[C:]
    for q in ch:
        q['u'] = mm(q['x'].astype(BF16), bd(q['az'][:C] + q['wv'][:C]))
    for q in ch:
        y = q['az'][C:] + q['wv'][C:] + mm(q['m_rb'], bd(q['u']))
        q['y_out'][:, q['sl']] = y
        rhs_t = jnp.concatenate([q['u'], q['v']], axis=0).astype(BF16)
        zu = lax.dot_general(q['lhs_t'], rhs_t, (((0,), (0,)), ((), ())), preferred_element_type=F32)
        decay = jnp.sum(jnp.where(eye_g, jnp.broadcast_to(q['e_tot'], (G, G)), 0.0), axis=1, keepdims=True)
        z_ref[q['d'], q['g']] = jnp.where(m_bd_f, decay * q['z'] + zu, 0.0)


def _rwkv_scan(lay, r, v, kn, lw0, k0, b0, lw1, k1, b1):
    B, T, L = lay.B, lay.T, lay.L
    C = RW_CHUNK
    cc, cx = L // C, T // C
    nc = cc + cx
    ctx0 = lay.rows_x // C

    def fwd(b, c):
        return (jnp.where(c < cc, ctx0 + b * cc + c, b * cx + c - cc), 0)

    def bwd(b, c):
        return (jnp.where(c < cc, ctx0 + b * cc + (cc - 1 - c), b * cx + (nc - 1 - c)), 0)

    sf, sb = pl.BlockSpec((C, D), fwd), pl.BlockSpec((C, D), bwd)
    return pl.pallas_call(
        _rwkv_scan_kernel,
        grid=(B, nc),
        in_specs=[sf] * 6 + [sb] * 6,
        out_specs=[sf, sb],
        out_shape=[jax.ShapeDtypeStruct((lay.N, D), F32)] * 2,
        scratch_shapes=[pltpu.VMEM((2, D // RW_GROUP, RW_GROUP, RW_GROUP), F32)],
        compiler_params=_cparams(("parallel", "arbitrary")),
        name="rwkv_scan",
    )(r, v, kn, lw0, k0, b0, r, v, kn, lw1, k1, b1)


def _rwkv_out_kernel(h_ref, yf_ref, yb_ref, bonus_ref, g_ref, mod_ref, lnw_ref, lnb_ref, wo_ref, bd_ref, o_ref):
    bd = bd_ref[...]
    y = yf_ref[...] + yb_ref[...]
    mu = _seg_sum(y, bd) * (1.0 / 64)
    yc = y - mu
    var = _seg_sum(yc * yc, bd) * (1.0 / 64)
    yn = yc * lax.rsqrt(var + GN_EPS) * lnw_ref[...] + lnb_ref[...]
    out = (yn + bonus_ref[...]) * g_ref[...]
    res = jnp.dot(out.astype(BF16), wo_ref[...], preferred_element_type=F32)
    o_ref[...] = h_ref[...] + mod_ref[0][:, 2 * D:3 * D] * res


def _rwkv_out(lay, h, yf, yb, bonus, g, mod, ln_w, ln_b, w_o):
    return pl.pallas_call(
        _rwkv_out_kernel,
        grid=(lay.n_t,),
        in_specs=[lay.row_spec()] * 5 + [lay.mod_spec(), _const_spec((1, D)), _const_spec((1, D)),
                                          _const_spec((D, D)), _const_spec((RW_GROUP, RW_GROUP))],
        out_specs=lay.row_spec(),
        out_shape=jax.ShapeDtypeStruct((lay.N, D), F32),
        compiler_params=_cparams(("parallel",)),
        name="rwkv_out",
    )(h, yf, yb, bonus, g, mod, ln_w.reshape(1, D), ln_b.reshape(1, D), w_o.astype(BF16), _block_ones())


RUN_ALIGN = 8


def _router_kernel(h_ref, mod_ref, g_ref, wr_ref, br_ref, upper_ref, gate_ref, lpos_ref, cnt_ref):
    m = mod_ref[0]
    f = _norm_mod(h_ref[...], g_ref[...], m[:, 3 * D:4 * D], m[:, 4 * D:5 * D])
    logits = lax.dot_general(wr_ref[...], f, (((1,), (1,)), ((), ())), preferred_element_type=F32,
                             precision=lax.Precision.HIGHEST) + br_ref[...]
    erow = lax.broadcasted_iota(jnp.int32, logits.shape, 0)
    tops, hots = [], []
    l = logits
    for _ in range(TOP_K):
        mx = jnp.max(l, axis=0, keepdims=True)
        ix = jnp.min(jnp.where(l == mx, erow, N_EXPERTS), axis=0, keepdims=True)
        tops.append(mx)
        hots.append(erow == ix)
        l = jnp.where(erow == ix, -jnp.inf, l)
    es = [jnp.exp(t - tops[0]) for t in tops]
    tot = es[0] + es[1] + es[2] + es[3]
    gate_ref[...] = jnp.concatenate([e / tot for e in es], axis=0)

    ohs = [jnp.where(hh, 1.0, 0.0) for hh in hots]
    cnts = [jnp.sum(oh, axis=1, keepdims=True) for oh in ohs]
    cnt = cnts[0] + cnts[1] + cnts[2] + cnts[3]
    run = jnp.ceil(cnt * (1.0 / RUN_ALIGN)) * RUN_ALIGN
    ee = lax.broadcasted_iota(jnp.int32, (N_EXPERTS, N_EXPERTS), 0)
    ec = lax.broadcasted_iota(jnp.int32, (N_EXPERTS, N_EXPERTS), 1)
    run_row = jnp.sum(jnp.where(ee == ec, jnp.broadcast_to(run, (N_EXPERTS, N_EXPERTS)), 0.0), axis=0, keepdims=True)
    offs = jnp.sum(jnp.where(ec < ee, jnp.broadcast_to(run_row, (N_EXPERTS, N_EXPERTS)), 0.0), axis=1, keepdims=True)
    base = offs
    lpos = []
    for k in range(TOP_K):
        before = jnp.dot(ohs[k].astype(BF16), upper_ref[...], preferred_element_type=F32)
        lpos.append(jnp.sum(ohs[k] * (before + base), axis=0, keepdims=True))
        base = base + cnts[k]
    lpos_ref[...] = jnp.concatenate(lpos, axis=0).astype(jnp.int32)
    cnt_ref[0] = cnt.astype(jnp.int32)


def _moe_router(lay, h, mod, g, w_router, b_router, n_rows):
    tm = lay.tm
    nt = n_rows // tm
    upper = jnp.asarray(np.triu(np.ones((tm, tm), np.float32), 1), dtype=BF16)
    return pl.pallas_call(
        _router_kernel,
        grid=(nt,),
        in_specs=[lay.row_spec(), lay.mod_spec(), _const_spec((1, D)),
                  _const_spec((N_EXPERTS, D)), _const_spec((N_EXPERTS, 1)), _const_spec((tm, tm))],
        out_specs=[pl.BlockSpec((TOP_K, tm), lambda i: (0, i)),
                   pl.BlockSpec((TOP_K, tm), lambda i: (0, i)),
                   pl.BlockSpec((1, N_EXPERTS, 1), lambda i: (i, 0, 0))],
        out_shape=[jax.ShapeDtypeStruct((TOP_K, n_rows), F32),
                   jax.ShapeDtypeStruct((TOP_K, n_rows), jnp.int32),
                   jax.ShapeDtypeStruct((nt, N_EXPERTS, 1), jnp.int32)],
        compiler_params=_cparams(("parallel",)),
        name="moe_router",
    )(h, mod, g.reshape(1, D), w_router.T, b_router.reshape(N_EXPERTS, 1), upper)


def _comp_rows(tm):
    return tm * TOP_K + N_EXPERTS * RUN_ALIGN


def _moe_plan(cnt, n, tm):
    nt = n // tm
    def excl_sum(a, axis):
        k = a.shape[axis]
        before = np.arange(k)[:, None] > np.arange(k)[None, :]
        if axis == 0:
            return jnp.sum(jnp.where(before[:, :, None], a[None, :, :], 0), axis=1)
        return jnp.sum(jnp.where(before[None, :, :], a[:, None, :], 0), axis=2)

    run = (cnt + RUN_ALIGN - 1) // RUN_ALIGN * RUN_ALIGN
    tot = jnp.sum(run, axis=0)
    padded = (tot + MOE_BLOCK - 1) // MOE_BLOCK * MOE_BLOCK
    pad_start = excl_sum(padded[None, :], 1)[0]
    pad_end = pad_start + padded
    base = pad_start[None, :] + excl_sum(run, 0)
    rows_max = n * TOP_K + nt * N_EXPERTS * (RUN_ALIGN - 1) + N_EXPERTS * MOE_BLOCK
    n_blocks = -(-rows_max // MOE_BLOCK)
    rows = n_blocks * MOE_BLOCK
    blk_row = jnp.arange(n_blocks, dtype=jnp.int32) * MOE_BLOCK
    block_expert = jnp.minimum(jnp.sum((pad_end[None, :] <= blk_row[:, None]).astype(jnp.int32), axis=1),
                               N_EXPERTS - 1).astype(jnp.int32)
    spare_rows = 2 * _comp_rows(tm)
    fill_base = jnp.concatenate([pad_start + tot, pad_end[-1:], jnp.full((1,), rows)]).astype(jnp.int32)
    fill_cnt = (jnp.concatenate([padded - tot, rows - pad_end[-1:], jnp.full((1,), spare_rows)])
                // RUN_ALIGN).astype(jnp.int32)
    n_used = (pad_end[-1:] // MOE_BLOCK).astype(jnp.int32)
    cr = _comp_rows(tm)
    ngrp = cr // RUN_ALIGN
    run_g = run // RUN_ALIGN
    goffs = excl_sum(run_g, 1)
    gend = goffs + run_g
    ng = gend[:, -1]
    j = jnp.arange(ngrp, dtype=jnp.int32)
    e_of = jnp.minimum(jnp.sum((gend[:, None, :] <= j[None, :, None]).astype(jnp.int32), axis=2), N_EXPERTS - 1)
    oh = e_of[:, :, None] == jnp.arange(N_EXPERTS, dtype=jnp.int32)[None, None, :]
    pick = lambda a: jnp.sum(jnp.where(oh, a[:, None, :], 0), axis=2)
    row = pick(base) + RUN_ALIGN * (j[None, :] - pick(goffs))
    live = j[None, :] < ng[:, None]
    spare = rows + (jnp.arange(nt, dtype=jnp.int32)[:, None] % 2) * cr + RUN_ALIGN * j[None, :]
    put_rows = jnp.where(live, row, spare).astype(jnp.int32).reshape(nt, 1, ngrp)
    get_rows = jnp.where(live, row, 0).astype(jnp.int32).reshape(nt, 1, ngrp)
    return put_rows, get_rows, ng.astype(jnp.int32), block_expert, fill_base, fill_cnt, n_used, n_blocks


def _group_copies(n_groups, src_of, dst_of, sem, wait):
    def body(j, carry):
        cp = pltpu.make_async_copy(src_of(j), dst_of(j), sem)
        cp.wait() if wait else cp.start()
        return carry

    lax.fori_loop(0, n_groups, body, 0, unroll=8)


def _rows_at(ref, start):
    return ref.at[pl.ds(pl.multiple_of(start, RUN_ALIGN), RUN_ALIGN), :]


def _compact_kernel(fill_base_ref, fill_cnt_ref, put_ref, pput_ref,
                    h_ref, mod_ref, g_ref, lpos_ref, xs_hbm, comp_ref, zero_ref, sems):
    i = pl.program_id(0)
    s = i % 2
    m = mod_ref[0]
    f = _norm_mod(h_ref[...], g_ref[...], m[:, 3 * D:4 * D], m[:, 4 * D:5 * D]).astype(BF16)
    cr, tm = comp_ref.shape[1], h_ref.shape[0]
    q = lax.broadcasted_iota(jnp.int32, (cr, tm), 0)
    lp = lpos_ref[...]
    hit = (lp[0:1, :] == q) | (lp[1:2, :] == q) | (lp[2:3, :] == q) | (lp[3:4, :] == q)
    comp_ref[s] = jnp.dot(jnp.where(hit, 1.0, 0.0).astype(BF16), f, preferred_element_type=F32)

    def copies(slot, rows_ref, wait):
        src = lambda j: _rows_at(comp_ref.at[slot], j * RUN_ALIGN)
        dst = lambda j: _rows_at(xs_hbm, rows_ref[0, 0, j])
        _group_copies(cr // RUN_ALIGN, src, dst, sems.at[slot], wait)

    @pl.when(i == 0)
    def _():
        zero_ref[...] = jnp.zeros_like(zero_ref)
        for e in range(fill_cnt_ref.shape[0]):
            def fill(c, carry, wait=False, e=e):
                cp = pltpu.make_async_copy(zero_ref, _rows_at(xs_hbm, fill_base_ref[e] + c * RUN_ALIGN), sems.at[0])
                cp.wait() if wait else cp.start()
                return carry

            lax.fori_loop(0, fill_cnt_ref[e], fill, 0)
            lax.fori_loop(0, fill_cnt_ref[e], functools.partial(fill, wait=True), 0)

    def wait_slot(slot):
        pltpu.make_async_copy(comp_ref.at[slot], xs_hbm.at[pl.ds(0, cr), :], sems.at[slot]).wait()

    del pput_ref

    @pl.when(i > 0)
    def _():
        wait_slot(1 - s)

    copies(s, put_ref, False)

    @pl.when(i == pl.num_programs(0) - 1)
    def _():
        wait_slot(s)


def _group_table(shift, nt, ngrp):
    def idx(i, *_):
        return (jnp.clip(i + shift, 0, nt - 1), 0, 0)

    return pl.BlockSpec((1, 1, ngrp), idx, memory_space=pltpu.SMEM)


def _moe_compact(lay, h, mod, g, lpos, put_rows, fill_base, fill_cnt, n_rows, n_blocks):
    tm = lay.tm
    nt, ngrp = n_rows // tm, _comp_rows(tm) // RUN_ALIGN
    grid_spec = pltpu.PrefetchScalarGridSpec(
        num_scalar_prefetch=2,
        grid=(nt,),
        in_specs=[_group_table(0, nt, ngrp), _group_table(-1, nt, ngrp)]
                 + [pl.BlockSpec((tm, D), lambda i, *_: (i, 0)),
                    pl.BlockSpec((1, 1, 6 * D), lambda i, *_: (lay.mod_idx(i), 0, 0)),
                    pl.BlockSpec((1, D), lambda i, *_: (0, 0)),
                    pl.BlockSpec((TOP_K, tm), lambda i, *_: (0, i))],
        out_specs=pl.BlockSpec(memory_space=pl.ANY),
        scratch_shapes=[pltpu.VMEM((2, _comp_rows(tm), D), F32), pltpu.VMEM((RUN_ALIGN, D), F32),
                        pltpu.SemaphoreType.DMA((2,))],
    )
    return pl.pallas_call(
        _compact_kernel,
        grid_spec=grid_spec,
        out_shape=jax.ShapeDtypeStruct((n_blocks * MOE_BLOCK + 2 * _comp_rows(tm), D), F32),
        compiler_params=_cparams(("arbitrary",)),
        name="moe_compact",
    )(fill_base, fill_cnt, put_rows, put_rows, h, mod, g.reshape(1, D), lpos)


def _expert_kernel(be_ref, nu_ref, x_ref, w1_ref, b1_ref, w2_ref, b2_ref, o_ref):
    del be_ref
    i = pl.program_id(0)

    @pl.when(i < nu_ref[0])
    def _():
        hcat = jnp.dot(x_ref[...].astype(BF16), w1_ref[0], preferred_element_type=F32) + b1_ref[0]
        h_glu = jnp.minimum(hcat[:, :D_FF], SWIGLU_LIMIT)
        h_lin = jnp.clip(hcat[:, D_FF:], -SWIGLU_LIMIT, SWIGLU_LIMIT)
        act = h_glu * jax.nn.sigmoid(SWIGLU_ALPHA * h_glu) * (h_lin + 1.0)
        o_ref[...] = jnp.dot(act.astype(BF16), w2_ref[0], preferred_element_type=F32) + b2_ref[0]

    @pl.when(i >= nu_ref[0])
    def _():
        o_ref[...] = jnp.zeros_like(o_ref)


def _moe_experts(xs, block_expert, n_used, n_blocks, w1, b1, w2, b2):
    def xrow(i, be, nu):
        return (jnp.minimum(i, nu[0] - 1), 0)

    grid_spec = pltpu.PrefetchScalarGridSpec(
        num_scalar_prefetch=2,
        grid=(n_blocks,),
        in_specs=[pl.BlockSpec((MOE_BLOCK, D), xrow),
                  pl.BlockSpec((1, D, 2 * D_FF), lambda i, be, nu: (be[i], 0, 0)),
                  pl.BlockSpec((1, 1, 2 * D_FF), lambda i, be, nu: (be[i], 0, 0)),
                  pl.BlockSpec((1, D_FF, D), lambda i, be, nu: (be[i], 0, 0)),
                  pl.BlockSpec((1, 1, D), lambda i, be, nu: (be[i], 0, 0))],
        out_specs=pl.BlockSpec((MOE_BLOCK, D), lambda i, be, nu: (i, 0)),
    )
    return pl.pallas_call(
        _expert_kernel,
        grid_spec=grid_spec,
        out_shape=jax.ShapeDtypeStruct((n_blocks * MOE_BLOCK, D), F32),
        compiler_params=_cparams(("arbitrary",)),
        name="moe_experts",
    )(block_expert, n_used, xs, w1.astype(BF16), b1.reshape(N_EXPERTS, 1, 2 * D_FF),
      w2.astype(BF16), b2.reshape(N_EXPERTS, 1, D))


def _combine_kernel(ng_ref, get_ref, nget_ref, h_ref, gate_ref, lpos_ref, mod_ref, y_hbm, o_ref, comp_ref, sems):
    i = pl.program_id(0)
    s = i % 2
    cr, tm = comp_ref.shape[1], h_ref.shape[0]

    def copies(slot, rows_ref, wait):
        src = lambda j: _rows_at(y_hbm, rows_ref[0, 0, j])
        dst = lambda j: _rows_at(comp_ref.at[slot], j * RUN_ALIGN)
        _group_copies(cr // RUN_ALIGN, src, dst, sems.at[slot], wait)

    @pl.when(i == 0)
    def _():
        copies(0, get_ref, False)

    @pl.when(i + 1 < pl.num_programs(0))
    def _():
        copies(1 - s, nget_ref, False)

    q = lax.broadcasted_iota(jnp.int32, (tm, cr), 1)
    gates, lp = gate_ref[...], lpos_ref[...]
    pg = jnp.where(lp[:, 0:1] == q, gates[:, 0:1], 0.0)
    for k in range(1, TOP_K):
        pg = pg + jnp.where(lp[:, k:k + 1] == q, gates[:, k:k + 1], 0.0)
    pltpu.make_async_copy(y_hbm.at[pl.ds(0, cr), :], comp_ref.at[s], sems.at[s]).wait()
    row = lax.broadcasted_iota(jnp.int32, (cr, 1), 0)
    yc = jnp.where(row < ng_ref[i] * RUN_ALIGN, comp_ref[s], 0.0).astype(BF16)
    y = jnp.dot(pg.astype(BF16), yc, preferred_element_type=F32)
    o_ref[...] = h_ref[...] + mod_ref[0][:, 5 * D:6 * D] * y


def _moe_combine(lay, h, mod, yb, gates_t, lpos_t, get_rows, ng, n_rows):
    tm = lay.tm
    nt, ngrp = n_rows // tm, _comp_rows(tm) // RUN_ALIGN
    grid_spec = pltpu.PrefetchScalarGridSpec(
        num_scalar_prefetch=1,
        grid=(nt,),
        in_specs=[_group_table(0, nt, ngrp), _group_table(1, nt, ngrp),
                  pl.BlockSpec((tm, D), lambda i, *_: (i, 0)),
                  pl.BlockSpec((tm, TOP_K), lambda i, *_: (i, 0)),
                  pl.BlockSpec((tm, TOP_K), lambda i, *_: (i, 0)),
                  pl.BlockSpec((1, 1, 6 * D), lambda i, *_: (lay.mod_idx(i), 0, 0)),
                  pl.BlockSpec(memory_space=pl.ANY)],
        out_specs=pl.BlockSpec((tm, D), lambda i, *_: (i, 0)),
        scratch_shapes=[pltpu.VMEM((2, _comp_rows(tm), D), F32), pltpu.SemaphoreType.DMA((2,))],
    )
    return pl.pallas_call(
        _combine_kernel,
        grid_spec=grid_spec,
        out_shape=jax.ShapeDtypeStruct((n_rows, D), F32),
        compiler_params=_cparams(("arbitrary",)),
        name="moe_combine",
    )(ng, get_rows, get_rows, h, gates_t, lpos_t, mod, yb)


def _moe_layer(lay, h, mod, g, w_router, b_router, w1, b1, w2, b2, n_rows):
    gates, lpos, cnt = _moe_router(lay, h, mod, g, w_router, b_router, n_rows)
    put_rows, get_rows, ng, block_expert, fill_base, fill_cnt, n_used, n_blocks = _moe_plan(cnt[:, :, 0], n_rows, lay.tm)
    xs = _moe_compact(lay, h, mod, g, lpos, put_rows, fill_base, fill_cnt, n_rows, n_blocks)
    yb = _moe_experts(xs, block_expert, n_used, n_blocks, w1, b1, w2, b2)
    return _moe_combine(lay, h, mod, yb, gates.T, lpos.T, get_rows, ng, n_rows)


def _final_kernel(h_ref, g_ref, o_ref):
    h = h_ref[...]
    ms = jnp.mean(h * h, axis=-1, keepdims=True)
    o_ref[...] = h * lax.rsqrt(ms + NORM_EPS) * g_ref[...]


def _final_norm(lay, h, g, n_rows):
    return pl.pallas_call(
        _final_kernel,
        grid=(n_rows // lay.tm,),
        in_specs=[lay.row_spec(), _const_spec((1, D))],
        out_specs=lay.row_spec(),
        out_shape=jax.ShapeDtypeStruct((n_rows, D), F32),
        compiler_params=_cparams(("parallel",)),
        name="final_norm",
    )(h, g.reshape(1, D))


def kernel(x, c, ctx, c_ctx, ada_w, ada_b, norm_g, attn_w_qkv, attn_b_qkv, attn_w_o, attn_b_o, attn_sinks, pool_w, pool_b, pool_scale, rwkv_mix, rwkv_w_rkv, rwkv_w0, rwkv_w1, rwkv_w2, rwkv_a0, rwkv_a1, rwkv_a2, rwkv_g1, rwkv_g2, rwkv_k_k, rwkv_k_a, rwkv_r_k, rwkv_ln_w, rwkv_ln_b, rwkv_w_o, moe_w_router, moe_b_router, moe_w1, moe_b1, moe_w2, moe_b2, final_g):
    B, T, _ = x.shape
    L = ctx.shape[1]
    depth = ada_w.shape[0]
    lay = _Layout(B, T, L)
    h = jnp.concatenate([x.reshape(B * T, D), ctx.reshape(B * L, D)], axis=0)

    n_c = B + 1
    n_c_pad = -(-n_c // 8) * 8
    c_all = jnp.concatenate([c, c_ctx[None, :], jnp.zeros((n_c_pad - n_c, D), F32)], axis=0)
    mods = _ada_mods(c_all, ada_w, ada_b)
    cos, sin = _rope_tables(T, L)

    for layer in range(depth):
        last = layer == depth - 1
        kind, j = layer % 3, layer // 3
        mod = mods[layer].reshape(n_c_pad, 1, 6 * D)
        n_rows = lay.rows_x if last else lay.N
        if kind == 0:
            q, k, v = _attn_qkv(lay, h, mod, norm_g[layer, 0], attn_w_qkv[j], attn_b_qkv[j], cos, sin)
            o = _attention(lay, q, k, v, attn_sinks[j], not last)
            h = _linear_residual(lay, h, o, mod, attn_w_o[j], attn_b_o[j], 2, n_rows)
        elif kind == 1:
            h = _pool_mixer(lay, h, mod, norm_g[layer, 0], pool_w[j], pool_b[j], pool_scale[j], n_rows)
        else:
            p = dict(mix=rwkv_mix[j], w_rkv=rwkv_w_rkv[j], w0=rwkv_w0[j], w1=rwkv_w1[j], w2=rwkv_w2[j],
                     a0=rwkv_a0[j], a1=rwkv_a1[j], a2=rwkv_a2[j], g1=rwkv_g1[j], g2=rwkv_g2[j],
                     k_k=rwkv_k_k[j], k_a=rwkv_k_a[j], r_k=rwkv_r_k[j])
            r, v, kn, g, bonus, lw0, k0, b0, lw1, k1, b1 = _rwkv_proj(lay, h, mod, norm_g[layer, 0], p)
            yf, yb = _rwkv_scan(lay, r, v, kn, lw0, k0, b0, lw1, k1, b1)
            h = _rwkv_out(lay, h, yf, yb, bonus, g, mod, rwkv_ln_w[j], rwkv_ln_b[j], rwkv_w_o[j])
        h = _moe_layer(lay, h, mod, norm_g[layer, 1], moe_w_router[layer], moe_b_router[layer],
                       moe_w1[layer], moe_b1[layer], moe_w2[layer], moe_b2[layer], n_rows)
    out = _final_norm(lay, h, final_g, lay.rows_x)
    return out.reshape(B, T, D)
```

```python
import functools

import jax
import jax.numpy as jnp
import numpy as np
from jax import lax
from jax.experimental import pallas as pl
from jax.experimental.pallas import tpu as pltpu

F32 = jnp.float32
BF16 = jnp.bfloat16

D = 1024
NORM_EPS = 1e-5
GRID_W = 64
HEAD_DIM = 64
N_HEADS = 16
N_KV = 4
Q_DIM = 1024
KV_DIM = 256
QK_DIM = Q_DIM + KV_DIM
KVD_DIM = 2 * KV_DIM
QKD_DIM = Q_DIM + KVD_DIM
WINDOW = 128
QB = 128
ROPE_BASE = 10000.0
POOL_SIZES = (2, 4, 8, 16)
POOL_GW = 256
HALO = 8
GN_EPS = 64e-5
N_EXPERTS = 32
TOP_K = 4
D_FF = 1024
SWIGLU_LIMIT = 7.0
SWIGLU_ALPHA = 1.702
MOE_BLOCK = 512
RW_CHUNK = 64
RW_GROUP = 256
VMEM_LIMIT = 56 * 1024 * 1024


def _cparams(sem):
    return pltpu.CompilerParams(dimension_semantics=sem, vmem_limit_bytes=VMEM_LIMIT)


def _norm_mod(h, g, shift, scale):
    ms = jnp.mean(h * h, axis=-1, keepdims=True)
    y = h * lax.rsqrt(ms + NORM_EPS) * g
    return y * (1.0 + scale) + shift


class _Layout:
    def __init__(self, B, T, L):
        self.B, self.T, self.L = B, T, L
        self.tm = 256 if L % 256 == 0 else 128
        self.rows_x, self.rows_c = B * T, B * L
        self.N = self.rows_x + self.rows_c
        self.tx, self.tc = T // self.tm, L // self.tm
        self.n_xt = self.rows_x // self.tm
        self.n_t = self.N // self.tm

    def mod_idx(self, i):
        return jnp.where(i < self.n_xt, i // self.tx, self.B)

    def mod_spec(self):
        return pl.BlockSpec((1, 1, 6 * D), lambda i: (self.mod_idx(i), 0, 0))

    def row_spec(self, width=D):
        return pl.BlockSpec((self.tm, width), lambda i: (i, 0))

    def seq_tile(self, i):
        j = jnp.where(i < self.n_xt, i % self.tx, (i - self.n_xt) % self.tc)
        n = jnp.where(i < self.n_xt, self.tx, self.tc)
        return j, n

    def halo_specs(self):
        hb = self.tm // HALO
        last = self.N // HALO - 1
        prev = pl.BlockSpec((HALO, D), lambda i: (jnp.maximum(i * hb - 1, 0), 0))
        nxt = pl.BlockSpec((HALO, D), lambda i: (jnp.minimum((i + 1) * hb, last), 0))
        return prev, nxt


def _const_spec(shape):
    nd = len(shape)
    return pl.BlockSpec(shape, lambda *_: (0,) * nd)


def _ada_kernel(c_ref, w_ref, b_ref, o_ref):
    c = c_ref[...]
    s = c * jax.nn.sigmoid(c)
    o_ref[0] = jnp.dot(s, w_ref[0], preferred_element_type=F32, precision=lax.Precision.HIGHEST) + b_ref[0]


def _ada_mods(c_all, ada_w, ada_b):
    depth = ada_w.shape[0]
    R = c_all.shape[0]
    nt = 1536
    return pl.pallas_call(
        _ada_kernel,
        grid=(depth, 6 * D // nt),
        in_specs=[pl.BlockSpec((R, D), lambda l, j: (0, 0)),
                  pl.BlockSpec((1, D, nt), lambda l, j: (l, 0, j)),
                  pl.BlockSpec((1, 1, nt), lambda l, j: (l, 0, j))],
        out_specs=pl.BlockSpec((1, R, nt), lambda l, j: (l, 0, j)),
        out_shape=jax.ShapeDtypeStruct((depth, R, 6 * D), F32),
        compiler_params=_cparams(("parallel", "parallel")),
        name="ada_mod",
    )(c_all, ada_w, ada_b.reshape(depth, 1, 6 * D))


def _qkv_kernel(h_ref, mod_ref, g_ref, w_ref, b_ref, cos_ref, sin_ref, q_ref, k_ref, v_ref):
    m = mod_ref[0]
    a = _norm_mod(h_ref[...], g_ref[...], m[:, 0:D], m[:, D:2 * D])
    qkv = jnp.dot(a.astype(BF16), w_ref[...], preferred_element_type=F32) + b_ref[...]
    qk = qkv[:, :QK_DIM]
    lane = lax.broadcasted_iota(jnp.int32, (1, QK_DIM), 1)
    first = (lane % 32) < 16
    rot = jnp.where(first, -pltpu.roll(qk, QK_DIM - 16, 1), pltpu.roll(qk, 16, 1))
    qk = qk * cos_ref[...] + rot * sin_ref[...]
    q_ref[...] = (qk[:, :Q_DIM] * (HEAD_DIM ** -0.5)).astype(BF16)
    k_ref[...] = _dup_heads(qk[:, Q_DIM:]).astype(BF16)
    v_ref[...] = _dup_heads(qkv[:, QK_DIM:]).astype(BF16)


def _dup_heads(x):
    heads = [x[:, g * HEAD_DIM:(g + 1) * HEAD_DIM] for g in range(N_KV)]
    return jnp.concatenate([hh for hd in heads for hh in (hd, hd)], axis=1)


def _rope_tables(T, L):
    rows = T // GRID_W
    n_freq = HEAD_DIM // 4
    inv = ROPE_BASE ** (-jnp.arange(n_freq, dtype=F32) / n_freq)
    row_ang = jnp.arange(rows, dtype=F32)[:, None] * inv
    col_ang = jnp.arange(GRID_W, dtype=F32)[:, None] * inv
    ang_r = jnp.broadcast_to(row_ang[:, None, :], (rows, GRID_W, n_freq)).reshape(T, n_freq)
    ang_c = jnp.broadcast_to(col_ang[None, :, :], (rows, GRID_W, n_freq)).reshape(T, n_freq)
    ang = jnp.concatenate([ang_r, ang_r, ang_c, ang_c], axis=-1)
    cos = jnp.concatenate([jnp.cos(ang), jnp.ones((L, HEAD_DIM), F32)], axis=0)
    sin = jnp.concatenate([jnp.sin(ang), jnp.zeros((L, HEAD_DIM), F32)], axis=0)
    reps = QK_DIM // HEAD_DIM
    return jnp.tile(cos, (1, reps)), jnp.tile(sin, (1, reps))


def _attn_qkv(lay, h, mod, g, w_qkv, b_qkv, cos, sin):
    tm = lay.tm

    def tab_idx(i):
        return (jnp.where(i < lay.n_xt, i % lay.tx, lay.tx + (i - lay.n_xt) % lay.tc), 0)

    w = w_qkv.astype(BF16)
    b = b_qkv.reshape(1, -1)
    width = Q_DIM + 2 * KV_DIM
    return pl.pallas_call(
        _qkv_kernel,
        grid=(lay.n_t,),
        in_specs=[lay.row_spec(), lay.mod_spec(), _const_spec((1, D)),
                  _const_spec((D, width)), _const_spec((1, width)),
                  pl.BlockSpec((tm, QK_DIM), tab_idx), pl.BlockSpec((tm, QK_DIM), tab_idx)],
        out_specs=[lay.row_spec(Q_DIM), lay.row_spec(KVD_DIM), lay.row_spec(KVD_DIM)],
        out_shape=[jax.ShapeDtypeStruct((lay.N, Q_DIM), BF16),
                   jax.ShapeDtypeStruct((lay.N, KVD_DIM), BF16),
                   jax.ShapeDtypeStruct((lay.N, KVD_DIM), BF16)],
        compiler_params=_cparams(("parallel",)),
        name="attn_qkv",
    )(h, mod, g.reshape(1, D), w, b, cos, sin)


def _attn_heads(q, kk, vv, mask, sink_ref, o_ref):
    R = q.shape[0]
    G = N_HEADS // N_KV
    TILE = 2 * HEAD_DIM
    lo = lax.broadcasted_iota(jnp.int32, (1, TILE), 1) < HEAD_DIM
    order = (0, 2, 1, 3)
    scores = []
    for g in range(N_KV):
        pieces = []
        for j in order:
            hq = g * G + j
            qt = q[:, (hq // 2) * TILE:(hq // 2 + 1) * TILE]
            pieces.append(jnp.where(lo if hq % 2 == 0 else ~lo, qt, jnp.zeros_like(qt)))
        qs = jnp.concatenate(pieces, axis=0)
        scores.append(lax.dot_general(qs, kk[:, g * TILE:(g + 1) * TILE], (((1,), (1,)), ((), ())),
                                      preferred_element_type=F32))
    probs, esinks = [], []
    for g in range(N_KV):
        s = scores[g]
        if mask is not None:
            s = jnp.where(jnp.concatenate([mask] * G, axis=0), s, -jnp.inf)
        sink = jnp.concatenate([jnp.full((R, 1), sink_ref[g * G + j], F32) for j in order], axis=0)
        m = jnp.maximum(jnp.max(s, axis=-1, keepdims=True), sink)
        probs.append(jnp.exp((s - m).astype(BF16)))
        esinks.append(jnp.exp(sink - m))
    for g in range(N_KV):
        vd = vv[:, g * TILE:(g + 1) * TILE]
        one = jnp.ones_like(vd)
        p, es = probs[g], esinks[g]
        half = (G // 2) * R
        oe = jnp.dot(p[:half], jnp.where(lo, vd, one), preferred_element_type=F32)
        oo = jnp.dot(p[half:], jnp.where(lo, one, vd), preferred_element_type=F32)
        re = oe / (pltpu.roll(oe, HEAD_DIM, 1) + es[:half])
        ro = oo / (pltpu.roll(oo, HEAD_DIM, 1) + es[half:])
        for t in range(G // 2):
            tile = jnp.where(lo, re[t * R:(t + 1) * R], ro[t * R:(t + 1) * R])
            c = g * (G // 2) + t
            o_ref[:, c * TILE:(c + 1) * TILE] = tile.astype(o_ref.dtype)


def _attn_kernel(sink_ref, q_ref, kp_ref, kc_ref, kn_ref, vp_ref, vc_ref, vn_ref, kx_ref, vx_ref, o_ref, *, T, L):
    i = pl.program_id(1)
    nqb = T // QB

    @pl.when(i < nqb)
    def _():
        kk = jnp.concatenate([kx_ref[...], kp_ref[...], kc_ref[...], kn_ref[...]], axis=0)
        vv = jnp.concatenate([vx_ref[...], vp_ref[...], vc_ref[...], vn_ref[...]], axis=0)
        S = L + 3 * QB
        col = lax.broadcasted_iota(jnp.int32, (QB, S), 1)
        row = lax.broadcasted_iota(jnp.int32, (QB, S), 0)
        rel = col - L - QB - row
        kpos = i * QB - QB + (col - L)
        local_ok = (jnp.abs(rel) <= WINDOW) & (kpos >= 0) & (kpos < T)
        mask = (col < L) | local_ok
        _attn_heads(q_ref[...], kk, vv, mask, sink_ref, o_ref)

    @pl.when(i >= nqb)
    def _():
        _attn_heads(q_ref[...], kx_ref[...], vx_ref[...], None, sink_ref, o_ref)


def _attention(lay, q, k, v, sinks, need_ctx):
    B, T, L = lay.B, lay.T, lay.L
    nqb, ncb = T // QB, L // QB
    cq = lay.rows_x // QB
    cb = lay.rows_x // L
    smem = pl.BlockSpec(memory_space=pltpu.SMEM)

    def cur(b, i):
        return (jnp.where(i < nqb, b * nqb + i, cq + b * ncb + (i - nqb)), 0)

    def prev(b, i):
        return (b * nqb + jnp.clip(i - 1, 0, nqb - 1), 0)

    def mid(b, i):
        return (b * nqb + jnp.minimum(i, nqb - 1), 0)

    def nxt(b, i):
        return (b * nqb + jnp.minimum(i + 1, nqb - 1), 0)

    def ctxb(b, i):
        return (cb + b, 0)

    kv_specs = [pl.BlockSpec((QB, KVD_DIM), prev), pl.BlockSpec((QB, KVD_DIM), mid), pl.BlockSpec((QB, KVD_DIM), nxt)]
    n_out = lay.N if need_ctx else lay.rows_x
    return pl.pallas_call(
        functools.partial(_attn_kernel, T=T, L=L),
        grid=(B, nqb + (ncb if need_ctx else 0)),
        in_specs=[smem, pl.BlockSpec((QB, Q_DIM), cur)] + kv_specs + kv_specs
                 + [pl.BlockSpec((L, KVD_DIM), ctxb), pl.BlockSpec((L, KVD_DIM), ctxb)],
        out_specs=pl.BlockSpec((QB, Q_DIM), cur),
        out_shape=jax.ShapeDtypeStruct((n_out, Q_DIM), BF16),
        compiler_params=_cparams(("parallel", "arbitrary")),
        name="attention",
    )(sinks, q, k, k, k, v, v, v, k, v)


def _linres_kernel(h_ref, x_ref, mod_ref, w_ref, b_ref, o_ref, *, gate_slot):
    gate = mod_ref[0][:, gate_slot * D:(gate_slot + 1) * D]
    y = jnp.dot(x_ref[...], w_ref[...], preferred_element_type=F32) + b_ref[...]
    o_ref[...] = h_ref[...] + gate * y


def _linear_residual(lay, h, x, mod, w, b, gate_slot, n_rows):
    return pl.pallas_call(
        functools.partial(_linres_kernel, gate_slot=gate_slot),
        grid=(n_rows // lay.tm,),
        in_specs=[lay.row_spec(), lay.row_spec(x.shape[1]), lay.mod_spec(),
                  _const_spec(w.shape), _const_spec((1, D))],
        out_specs=lay.row_spec(),
        out_shape=jax.ShapeDtypeStruct((n_rows, D), F32),
        compiler_params=_cparams(("parallel",)),
        name="linear_residual",
    )(h, x, mod, w.astype(BF16), b.reshape(1, D))


def _fill_ext(ext_ref, h_ref, hp_ref, hn_ref, g, shift, scale, j, n, tm):
    a = _norm_mod(h_ref[...], g, shift, scale)
    ap = _norm_mod(hp_ref[...], g, shift, scale)
    an = _norm_mod(hn_ref[...], g, shift, scale)
    ext_ref[0:HALO, :] = jnp.where(j == 0, 0.0, ap)
    ext_ref[HALO:HALO + tm, :] = a
    ext_ref[HALO + tm:2 * HALO + tm, :] = jnp.where(j == n - 1, 0.0, an)
    return a


def _pool_kernel(h_ref, hp_ref, hn_ref, mod_ref, g_ref, w_ref, b_ref, ls_ref, o_ref, ext_ref, *, lay):
    tm = lay.tm
    i = pl.program_id(0)
    j, n = lay.seq_tile(i)
    m = mod_ref[0]
    a = _fill_ext(ext_ref, h_ref, hp_ref, hn_ref, g_ref[...], m[:, 0:D], m[:, D:2 * D], j, n, tm)
    t = j * tm + lax.broadcasted_iota(jnp.int32, (tm, 1), 0)
    n_tok = n * tm
    outs = []
    for gi, size in enumerate(POOL_SIZES):
        sl = slice(gi * POOL_GW, (gi + 1) * POOL_GW)
        acc = ext_ref[HALO - size // 2:HALO - size // 2 + tm, sl]
        for o in range(-size // 2 + 1, size - size // 2):
            acc = acc + ext_ref[HALO + o:HALO + o + tm, sl]
        lo = jnp.maximum(t - size // 2, 0)
        hi = jnp.minimum(t + size - size // 2, n_tok)
        d = acc / (hi - lo).astype(F32) - a[:, sl]
        outs.append(jnp.dot(d.astype(BF16), w_ref[gi], preferred_element_type=F32) + b_ref[gi])
    y = jnp.concatenate(outs, axis=-1) * ls_ref[...]
    o_ref[...] = h_ref[...] + m[:, 2 * D:3 * D] * y


def _pool_mixer(lay, h, mod, g, w, b, ls, n_rows):
    prev, nxt = lay.halo_specs()
    return pl.pallas_call(
        functools.partial(_pool_kernel, lay=lay),
        grid=(n_rows // lay.tm,),
        in_specs=[lay.row_spec(), prev, nxt, lay.mod_spec(), _const_spec((1, D)),
                  _const_spec(w.shape), _const_spec((4, 1, POOL_GW)), _const_spec((1, D))],
        out_specs=lay.row_spec(),
        out_shape=jax.ShapeDtypeStruct((n_rows, D), F32),
        scratch_shapes=[pltpu.VMEM((lay.tm + 2 * HALO, D), F32)],
        compiler_params=_cparams(("parallel",)),
        name="pool_mixer",
    )(h, h, h, mod, g.reshape(1, D), w.astype(BF16), b.reshape(4, 1, POOL_GW), ls.reshape(1, D))


def _seg_sum(x, bd, split=True):
    outs = []
    for q in range(D // RW_GROUP):
        xg = x[:, q * RW_GROUP:(q + 1) * RW_GROUP]
        hi = xg.astype(BF16)
        acc = jnp.dot(hi, bd, preferred_element_type=F32)
        if split:
            acc = acc + jnp.dot((xg - hi.astype(F32)).astype(BF16), bd, preferred_element_type=F32)
        outs.append(acc)
    return jnp.concatenate(outs, axis=-1)


def _rwkv_proj_kernel(h_ref, hp_ref, hn_ref, mod_ref, g_ref, mix_ref, wr_ref, wk_ref, wv_ref, g1_ref, g2_ref,
                      w1_ref, w2_ref, w0_ref, a1_ref, a2_ref, a0_ref, kk_ref, ka_ref, rk_ref, bd_ref,
                      r_out, v_out, kn_out, g_out, bonus_out, lw0_out, k0_out, b0_out, lw1_out, k1_out, b1_out,
                      ext_ref, *, lay):
    tm = lay.tm
    i = pl.program_id(0)
    j, n = lay.seq_tile(i)
    m = mod_ref[0]
    a = _fill_ext(ext_ref, h_ref, hp_ref, hn_ref, g_ref[...], m[:, 0:D], m[:, D:2 * D], j, n, tm)
    prev = ext_ref[HALO - 1:HALO - 1 + tm, :]
    nxt = ext_ref[HALO + 1:HALO + 1 + tm, :]
    xx = 0.5 * (prev + nxt) - a
    x_r, x_w, x_k, x_v, x_a, x_g = [(a + xx * mix_ref[q:q + 1, :]).astype(BF16) for q in range(6)]
    bd = bd_ref[...]
    r = jnp.dot(x_r, wr_ref[...], preferred_element_type=F32)
    k = jnp.dot(x_k, wk_ref[...], preferred_element_type=F32)
    v = jnp.dot(x_v, wv_ref[...], preferred_element_type=F32)
    gl = jnp.dot(x_g, g1_ref[...], preferred_element_type=F32)
    g = jnp.dot(jax.nn.sigmoid(gl).astype(BF16), g2_ref[...], preferred_element_type=F32)
    kk = k * kk_ref[...]
    kn = kk / jnp.maximum(jnp.sqrt(_seg_sum(kk * kk, bd)), 1e-12)
    tw = jnp.tanh(jnp.dot(x_w, w1_ref[...], preferred_element_type=F32))
    aa = jnp.dot(x_a, a1_ref[...], preferred_element_type=F32)
    lane = lax.broadcasted_iota(jnp.int32, (1, tw.shape[1]), 1)
    r_out[...] = r.astype(r_out.dtype)
    v_out[...] = v.astype(v_out.dtype)
    kn_out[...] = kn.astype(kn_out.dtype)
    g_out[...] = g.astype(g_out.dtype)
    rk = rk_ref[...]
    bonus = jnp.zeros_like(v)
    outs = ((lw0_out, k0_out, b0_out), (lw1_out, k1_out, b1_out))
    half = tw.shape[1] // 2
    for d in range(2):
        sel = (lane >= d * half) & (lane < (d + 1) * half)
        wl = w0_ref[d:d + 1, :] + jnp.dot(jnp.where(sel, tw, 0.0).astype(BF16), w2_ref[...], preferred_element_type=F32)
        z = -wl
        w_log = -(jnp.maximum(z, 0.0) + jnp.log(1.0 + jnp.exp(-jnp.abs(z)))) - 0.5
        asig = jax.nn.sigmoid(a0_ref[d:d + 1, :] + jnp.dot(jnp.where(sel, aa, 0.0).astype(BF16), a2_ref[...],
                                                           preferred_element_type=F32))
        k_d = k * (1.0 + (asig - 1.0) * ka_ref[...])
        lw_o, k_o, b_o = outs[d]
        lw_o[...] = -jnp.exp(w_log)
        k_o[...] = k_d.astype(k_o.dtype)
        b_o[...] = (kn * asig).astype(b_o.dtype)
        bonus = bonus + _seg_sum(r * k_d * rk, bd, split=False) * v
    bonus_out[...] = bonus.astype(bonus_out.dtype)


def _block_ones():
    idx = np.arange(RW_GROUP) // 64
    return jnp.asarray(idx[:, None] == idx[None, :], dtype=BF16)


def _rwkv_proj(lay, h, mod, g, p):
    prev, nxt = lay.halo_specs()
    cat = lambda w: jnp.concatenate([w[0], w[1]], axis=-1).astype(BF16)
    stack = lambda w: jnp.concatenate([w[0], w[1]], axis=0).astype(BF16)
    args = [h, h, h, mod, g.reshape(1, D), p['mix'],
            p['w_rkv'][0].astype(BF16), p['w_rkv'][1].astype(BF16), p['w_rkv'][2].astype(BF16),
            p['g1'].astype(BF16), p['g2'].astype(BF16),
            cat(p['w1']), stack(p['w2']), p['w0'], cat(p['a1']), stack(p['a2']), p['a0'],
            p['k_k'].reshape(1, D), p['k_a'].reshape(1, D), p['r_k'].reshape(1, D), _block_ones()]
    in_specs = [lay.row_spec(), prev, nxt, lay.mod_spec()] + [_const_spec(x.shape) for x in args[4:]]
    dtypes = [BF16] * 5 + [F32, BF16, BF16] * 2
    return pl.pallas_call(
        functools.partial(_rwkv_proj_kernel, lay=lay),
        grid=(lay.n_t,),
        in_specs=in_specs,
        out_specs=[lay.row_spec()] * len(dtypes),
        out_shape=[jax.ShapeDtypeStruct((lay.N, D), dt) for dt in dtypes],
        scratch_shapes=[pltpu.VMEM((lay.tm + 2 * HALO, D), F32)],
        compiler_params=_cparams(("parallel",)),
        name="rwkv_proj",
    )(*args)


def _rwkv_scan_kernel(rf, vf, nf, lwf, kf, bf, rb, vb, nb, lwb, kb, bb, yf_out, yb_out, z_ref):
    C, G = RW_CHUNK, RW_GROUP
    c = pl.program_id(1)

    @pl.when(c == 0)
    def _():
        z_ref[...] = jnp.zeros_like(z_ref)

    trow = lax.broadcasted_iota(jnp.int32, (C, C), 0)
    tcol = lax.broadcasted_iota(jnp.int32, (C, C), 1)
    ti = lax.broadcasted_iota(jnp.int32, (C, G), 0)
    ii = lax.broadcasted_iota(jnp.int32, (C, G), 1) % C
    brow = lax.broadcasted_iota(jnp.int32, (G, G), 0)
    bcol = lax.broadcasted_iota(jnp.int32, (G, G), 1)
    m_bd_t = (brow // C) == (bcol // 64)
    m_bd_f = (brow // 64) == (bcol // 64)
    eye_g = brow == bcol

    def bd(x):
        return jnp.where(m_bd_t, jnp.concatenate([x] * (G // C), axis=0), 0.0).astype(BF16)

    def mm(a, b):
        return jnp.dot(a, b, preferred_element_type=F32)

    dirs = ((rf, vf, nf, lwf, kf, bf, yf_out, False), (rb, vb, nb, lwb, kb, bb, yb_out, True))
    ch = []
    for d, (r_ref, v_ref, n_ref, lw_ref, k_ref, b_ref, y_out, rev) in enumerate(dirs):
        tri = (tcol >= trow) if rev else (tcol <= trow)
        strict = (ii > ti) if rev else (ii < ti)
        incl = (ii >= ti) if rev else (ii <= ti)
        lw = lw_ref[...]
        lc = jnp.dot(tri.astype(F32), lw, preferred_element_type=F32, precision=lax.Precision.HIGHEST)
        ltot = lc[0:1, :] if rev else lc[C - 1:C, :]
        e_lc = jnp.exp(lc)
        ie_lc = jnp.exp(-lc)
        e_le = jnp.exp(lc - lw)
        e_bar = jnp.exp(ltot - lc)
        e_tot = jnp.exp(ltot)
        kvec, bvec = k_ref[...], b_ref[...]
        a_t = -n_ref[...] * e_le
        r_t = r_ref[...] * e_lc
        b_t = bvec * ie_lc
        k_t = kvec * ie_lc
        b_b = bvec * e_bar
        k_b = kvec * e_bar
        vall = v_ref[...]
        for g in range(D // G):
            sl = slice(g * G, (g + 1) * G)
            ch.append(dict(
                d=d, g=g, sl=sl, y_out=y_out, strict=strict, incl=incl, v=vall[:, sl], e_tot=e_tot[:, sl],
                ar=jnp.concatenate([a_t[:, sl], r_t[:, sl]], axis=0).astype(BF16),
                bk=jnp.concatenate([bd(b_t[:, sl]), bd(k_t[:, sl])], axis=0),
                lhs_t=jnp.concatenate([b_b[:, sl], k_b[:, sl]], axis=0).astype(BF16)))

    for q in ch:
        aa = lax.dot_general(q['ar'], q['bk'], (((1,), (1,)), ((), ())), preferred_element_type=F32)
        q['a_ab'] = jnp.where(q['strict'], aa[:C, :G], 0.0)
        q['a_ak'] = jnp.where(q['strict'], aa[:C, G:], 0.0)
        q['m_rb'] = jnp.where(q['incl'], aa[C:, :G], 0.0).astype(BF16)
        q['m_rk'] = jnp.where(q['incl'], aa[C:, G:], 0.0)
    for q in ch:
        q['x'] = jnp.where(ii == ti, 1.0, 0.0) + q['a_ab']
        q['p'] = mm(q['a_ab'].astype(BF16), bd(q['a_ab']))
    for q in ch:
        q['wv'] = mm(jnp.concatenate([q['a_ak'], q['m_rk']], axis=0).astype(BF16), bd(q['v']))
        q['z'] = z_ref[q['d'], q['g']]
        q['az'] = mm(q['ar'], q['z'].astype(BF16))
    n_sq = int(np.log2(C)) - 1
    for s in range(n_sq):
        last = s == n_sq - 1
        for q in ch:
            lhs = q['x'] if last else jnp.concatenate([q['x'], q['p']], axis=0)
            xp = mm(lhs.astype(BF16), bd(q['p']))
            q['x'] = q['x'] + xp[:C]
            if not last:
                q['p'] = xp[C:]
    for q in ch:
        q['u'] = mm(q['x'].astype(BF16), bd(q['az'][:C] + q['wv'][:C]))
    for q in ch:
        y = q['az'][C:] + q['wv'][C:] + mm(q['m_rb'], bd(q['u']))
        q['y_out'][:, q['sl']] = y
        rhs_t = jnp.concatenate([q['u'], q['v']], axis=0).astype(BF16)
        zu = lax.dot_general(q['lhs_t'], rhs_t, (((0,), (0,)), ((), ())), preferred_element_type=F32)
        decay = jnp.sum(jnp.where(eye_g, jnp.broadcast_to(q['e_tot'], (G, G)), 0.0), axis=1, keepdims=True)
        z_ref[q['d'], q['g']] = jnp.where(m_bd_f, decay * q['z'] + zu, 0.0)


def _rwkv_scan(lay, r, v, kn, lw0, k0, b0, lw1, k1, b1):
    B, T, L = lay.B, lay.T, lay.L
    C = RW_CHUNK
    cc, cx = L // C, T // C
    nc = cc + cx
    ctx0 = lay.rows_x // C

    def fwd(b, c):
        return (jnp.where(c < cc, ctx0 + b * cc + c, b * cx + c - cc), 0)

    def bwd(b, c):
        return (jnp.where(c < cc, ctx0 + b * cc + (cc - 1 - c), b * cx + (nc - 1 - c)), 0)

    sf, sb = pl.BlockSpec((C, D), fwd), pl.BlockSpec((C, D), bwd)
    return pl.pallas_call(
        _rwkv_scan_kernel,
        grid=(B, nc),
        in_specs=[sf] * 6 + [sb] * 6,
        out_specs=[sf, sb],
        out_shape=[jax.ShapeDtypeStruct((lay.N, D), F32)] * 2,
        scratch_shapes=[pltpu.VMEM((2, D // RW_GROUP, RW_GROUP, RW_GROUP), F32)],
        compiler_params=_cparams(("parallel", "arbitrary")),
        name="rwkv_scan",
    )(r, v, kn, lw0, k0, b0, r, v, kn, lw1, k1, b1)


def _rwkv_out_kernel(h_ref, yf_ref, yb_ref, bonus_ref, g_ref, mod_ref, lnw_ref, lnb_ref, wo_ref, bd_ref, o_ref):
    bd = bd_ref[...]
    y = yf_ref[...] + yb_ref[...]
    mu = _seg_sum(y, bd) * (1.0 / 64)
    yc = y - mu
    var = _seg_sum(yc * yc, bd) * (1.0 / 64)
    yn = yc * lax.rsqrt(var + GN_EPS) * lnw_ref[...] + lnb_ref[...]
    out = (yn + bonus_ref[...]) * g_ref[...]
    res = jnp.dot(out.astype(BF16), wo_ref[...], preferred_element_type=F32)
    o_ref[...] = h_ref[...] + mod_ref[0][:, 2 * D:3 * D] * res


def _rwkv_out(lay, h, yf, yb, bonus, g, mod, ln_w, ln_b, w_o):
    return pl.pallas_call(
        _rwkv_out_kernel,
        grid=(lay.n_t,),
        in_specs=[lay.row_spec()] * 5 + [lay.mod_spec(), _const_spec((1, D)), _const_spec((1, D)),
                                          _const_spec((D, D)), _const_spec((RW_GROUP, RW_GROUP))],
        out_specs=lay.row_spec(),
        out_shape=jax.ShapeDtypeStruct((lay.N, D), F32),
        compiler_params=_cparams(("parallel",)),
        name="rwkv_out",
    )(h, yf, yb, bonus, g, mod, ln_w.reshape(1, D), ln_b.reshape(1, D), w_o.astype(BF16), _block_ones())


RUN_ALIGN = 8


def _router_kernel(h_ref, mod_ref, g_ref, wr_ref, br_ref, upper_ref, gate_ref, lpos_ref, cnt_ref):
    m = mod_ref[0]
    f = _norm_mod(h_ref[...], g_ref[...], m[:, 3 * D:4 * D], m[:, 4 * D:5 * D])
    logits = lax.dot_general(wr_ref[...], f, (((1,), (1,)), ((), ())), preferred_element_type=F32,
                             precision=lax.Precision.HIGHEST) + br_ref[...]
    erow = lax.broadcasted_iota(jnp.int32, logits.shape, 0)
    tops, hots = [], []
    l = logits
    for _ in range(TOP_K):
        mx = jnp.max(l, axis=0, keepdims=True)
        ix = jnp.min(jnp.where(l == mx, erow, N_EXPERTS), axis=0, keepdims=True)
        tops.append(mx)
        hots.append(erow == ix)
        l = jnp.where(erow == ix, -jnp.inf, l)
    es = [jnp.exp(t - tops[0]) for t in tops]
    tot = es[0] + es[1] + es[2] + es[3]
    gate_ref[...] = jnp.concatenate([e / tot for e in es], axis=0)

    ohs = [jnp.where(hh, 1.0, 0.0) for hh in hots]
    cnts = [jnp.sum(oh, axis=1, keepdims=True) for oh in ohs]
    cnt = cnts[0] + cnts[1] + cnts[2] + cnts[3]
    run = jnp.ceil(cnt * (1.0 / RUN_ALIGN)) * RUN_ALIGN
    ee = lax.broadcasted_iota(jnp.int32, (N_EXPERTS, N_EXPERTS), 0)
    ec = lax.broadcasted_iota(jnp.int32, (N_EXPERTS, N_EXPERTS), 1)
    run_row = jnp.sum(jnp.where(ee == ec, jnp.broadcast_to(run, (N_EXPERTS, N_EXPERTS)), 0.0), axis=0, keepdims=True)
    offs = jnp.sum(jnp.where(ec < ee, jnp.broadcast_to(run_row, (N_EXPERTS, N_EXPERTS)), 0.0), axis=1, keepdims=True)
    base = offs
    lpos = []
    for k in range(TOP_K):
        before = jnp.dot(ohs[k].astype(BF16), upper_ref[...], preferred_element_type=F32)
        lpos.append(jnp.sum(ohs[k] * (before + base), axis=0, keepdims=True))
        base = base + cnts[k]
    lpos_ref[...] = jnp.concatenate(lpos, axis=0).astype(jnp.int32)
    cnt_ref[0] = cnt.astype(jnp.int32)


def _moe_router(lay, h, mod, g, w_router, b_router, n_rows):
    tm = lay.tm
    nt = n_rows // tm
    upper = jnp.asarray(np.triu(np.ones((tm, tm), np.float32), 1), dtype=BF16)
    return pl.pallas_call(
        _router_kernel,
        grid=(nt,),
        in_specs=[lay.row_spec(), lay.mod_spec(), _const_spec((1, D)),
                  _const_spec((N_EXPERTS, D)), _const_spec((N_EXPERTS, 1)), _const_spec((tm, tm))],
        out_specs=[pl.BlockSpec((TOP_K, tm), lambda i: (0, i)),
                   pl.BlockSpec((TOP_K, tm), lambda i: (0, i)),
                   pl.BlockSpec((1, N_EXPERTS, 1), lambda i: (i, 0, 0))],
        out_shape=[jax.ShapeDtypeStruct((TOP_K, n_rows), F32),
                   jax.ShapeDtypeStruct((TOP_K, n_rows), jnp.int32),
                   jax.ShapeDtypeStruct((nt, N_EXPERTS, 1), jnp.int32)],
        compiler_params=_cparams(("parallel",)),
        name="moe_router",
    )(h, mod, g.reshape(1, D), w_router.T, b_router.reshape(N_EXPERTS, 1), upper)


def _comp_rows(tm):
    return tm * TOP_K + N_EXPERTS * RUN_ALIGN


def _moe_plan(cnt, n, tm):
    nt = n // tm
    def excl_sum(a, axis):
        k = a.shape[axis]
        before = np.arange(k)[:, None] > np.arange(k)[None, :]
        if axis == 0:
            return jnp.sum(jnp.where(before[:, :, None], a[None, :, :], 0), axis=1)
        return jnp.sum(jnp.where(before[None, :, :], a[:, None, :], 0), axis=2)

    run = (cnt + RUN_ALIGN - 1) // RUN_ALIGN * RUN_ALIGN
    tot = jnp.sum(run, axis=0)
    padded = (tot + MOE_BLOCK - 1) // MOE_BLOCK * MOE_BLOCK
    pad_start = excl_sum(padded[None, :], 1)[0]
    pad_end = pad_start + padded
    base = pad_start[None, :] + excl_sum(run, 0)
    rows_max = n * TOP_K + nt * N_EXPERTS * (RUN_ALIGN - 1) + N_EXPERTS * MOE_BLOCK
    n_blocks = -(-rows_max // MOE_BLOCK)
    rows = n_blocks * MOE_BLOCK
    blk_row = jnp.arange(n_blocks, dtype=jnp.int32) * MOE_BLOCK
    block_expert = jnp.minimum(jnp.sum((pad_end[None, :] <= blk_row[:, None]).astype(jnp.int32), axis=1),
                               N_EXPERTS - 1).astype(jnp.int32)
    spare_rows = 2 * _comp_rows(tm)
    fill_base = jnp.concatenate([pad_start + tot, pad_end[-1:], jnp.full((1,), rows)]).astype(jnp.int32)
    fill_cnt = (jnp.concatenate([padded - tot, rows - pad_end[-1:], jnp.full((1,), spare_rows)])
                // RUN_ALIGN).astype(jnp.int32)
    n_used = (pad_end[-1:] // MOE_BLOCK).astype(jnp.int32)
    cr = _comp_rows(tm)
    ngrp = cr // RUN_ALIGN
    run_g = run // RUN_ALIGN
    goffs = excl_sum(run_g, 1)
    gend = goffs + run_g
    ng = gend[:, -1]
    j = jnp.arange(ngrp, dtype=jnp.int32)
    e_of = jnp.minimum(jnp.sum((gend[:, None, :] <= j[None, :, None]).astype(jnp.int32), axis=2), N_EXPERTS - 1)
    oh = e_of[:, :, None] == jnp.arange(N_EXPERTS, dtype=jnp.int32)[None, None, :]
    pick = lambda a: jnp.sum(jnp.where(oh, a[:, None, :], 0), axis=2)
    row = pick(base) + RUN_ALIGN * (j[None, :] - pick(goffs))
    live = j[None, :] < ng[:, None]
    spare = rows + (jnp.arange(nt, dtype=jnp.int32)[:, None] % 2) * cr + RUN_ALIGN * j[None, :]
    put_rows = jnp.where(live, row, spare).astype(jnp.int32).reshape(nt, 1, ngrp)
    get_rows = jnp.where(live, row, 0).astype(jnp.int32).reshape(nt, 1, ngrp)
    return put_rows, get_rows, ng.astype(jnp.int32), block_expert, fill_base, fill_cnt, n_used, n_blocks


def _group_copies(n_groups, src_of, dst_of, sem, wait):
    def body(j, carry):
        cp = pltpu.make_async_copy(src_of(j), dst_of(j), sem)
        cp.wait() if wait else cp.start()
        return carry

    lax.fori_loop(0, n_groups, body, 0, unroll=8)


def _rows_at(ref, start):
    return ref.at[pl.ds(pl.multiple_of(start, RUN_ALIGN), RUN_ALIGN), :]


def _compact_kernel(fill_base_ref, fill_cnt_ref, put_ref, pput_ref,
                    h_ref, mod_ref, g_ref, lpos_ref, xs_hbm, comp_ref, zero_ref, sems):
    i = pl.program_id(0)
    s = i % 2
    m = mod_ref[0]
    f = _norm_mod(h_ref[...], g_ref[...], m[:, 3 * D:4 * D], m[:, 4 * D:5 * D]).astype(BF16)
    cr, tm = comp_ref.shape[1], h_ref.shape[0]
    q = lax.broadcasted_iota(jnp.int32, (cr, tm), 0)
    lp = lpos_ref[...]
    hit = (lp[0:1, :] == q) | (lp[1:2, :] == q) | (lp[2:3, :] == q) | (lp[3:4, :] == q)
    comp_ref[s] = jnp.dot(jnp.where(hit, 1.0, 0.0).astype(BF16), f, preferred_element_type=F32)

    def copies(slot, rows_ref, wait):
        src = lambda j: _rows_at(comp_ref.at[slot], j * RUN_ALIGN)
        dst = lambda j: _rows_at(xs_hbm, rows_ref[0, 0, j])
        _group_copies(cr // RUN_ALIGN, src, dst, sems.at[slot], wait)

    @pl.when(i == 0)
    def _():
        zero_ref[...] = jnp.zeros_like(zero_ref)
        for e in range(fill_cnt_ref.shape[0]):
            def fill(c, carry, wait=False, e=e):
                cp = pltpu.make_async_copy(zero_ref, _rows_at(xs_hbm, fill_base_ref[e] + c * RUN_ALIGN), sems.at[0])
                cp.wait() if wait else cp.start()
                return carry

            lax.fori_loop(0, fill_cnt_ref[e], fill, 0)
            lax.fori_loop(0, fill_cnt_ref[e], functools.partial(fill, wait=True), 0)

    def wait_slot(slot):
        pltpu.make_async_copy(comp_ref.at[slot], xs_hbm.at[pl.ds(0, cr), :], sems.at[slot]).wait()

    del pput_ref

    @pl.when(i > 0)
    def _():
        wait_slot(1 - s)

    copies(s, put_ref, False)

    @pl.when(i == pl.num_programs(0) - 1)
    def _():
        wait_slot(s)


def _group_table(shift, nt, ngrp):
    def idx(i, *_):
        return (jnp.clip(i + shift, 0, nt - 1), 0, 0)

    return pl.BlockSpec((1, 1, ngrp), idx, memory_space=pltpu.SMEM)


def _moe_compact(lay, h, mod, g, lpos, put_rows, fill_base, fill_cnt, n_rows, n_blocks):
    tm = lay.tm
    nt, ngrp = n_rows // tm, _comp_rows(tm) // RUN_ALIGN
    grid_spec = pltpu.PrefetchScalarGridSpec(
        num_scalar_prefetch=2,
        grid=(nt,),
        in_specs=[_group_table(0, nt, ngrp), _group_table(-1, nt, ngrp)]
                 + [pl.BlockSpec((tm, D), lambda i, *_: (i, 0)),
                    pl.BlockSpec((1, 1, 6 * D), lambda i, *_: (lay.mod_idx(i), 0, 0)),
                    pl.BlockSpec((1, D), lambda i, *_: (0, 0)),
                    pl.BlockSpec((TOP_K, tm), lambda i, *_: (0, i))],
        out_specs=pl.BlockSpec(memory_space=pl.ANY),
        scratch_shapes=[pltpu.VMEM((2, _comp_rows(tm), D), F32), pltpu.VMEM((RUN_ALIGN, D), F32),
                        pltpu.SemaphoreType.DMA((2,))],
    )
    return pl.pallas_call(
        _compact_kernel,
        grid_spec=grid_spec,
        out_shape=jax.ShapeDtypeStruct((n_blocks * MOE_BLOCK + 2 * _comp_rows(tm), D), F32),
        compiler_params=_cparams(("arbitrary",)),
        name="moe_compact",
    )(fill_base, fill_cnt, put_rows, put_rows, h, mod, g.reshape(1, D), lpos)


def _expert_kernel(be_ref, nu_ref, x_ref, w1_ref, b1_ref, w2_ref, b2_ref, o_ref, w1b_ref, w2b_ref):
    i = pl.program_id(0)

    @pl.when((i == 0) | (be_ref[i] != be_ref[jnp.maximum(i - 1, 0)]))
    def _():
        w1b_ref[...] = w1_ref[0, 0].astype(BF16)
        w2b_ref[...] = w2_ref[0, 0].astype(BF16)

    @pl.when(i < nu_ref[0])
    def _():
        hcat = jnp.dot(x_ref[...].astype(BF16), w1b_ref[...], preferred_element_type=F32) + b1_ref[0]
        h_glu = jnp.minimum(hcat[:, :D_FF], SWIGLU_LIMIT)
        h_lin = jnp.clip(hcat[:, D_FF:], -SWIGLU_LIMIT, SWIGLU_LIMIT)
        act = h_glu * jax.nn.sigmoid(SWIGLU_ALPHA * h_glu) * (h_lin + 1.0)
        o_ref[...] = jnp.dot(act.astype(BF16), w2b_ref[...], preferred_element_type=F32) + b2_ref[0]

    @pl.when(i >= nu_ref[0])
    def _():
        o_ref[...] = jnp.zeros_like(o_ref)


def _moe_experts(xs, block_expert, n_used, n_blocks, layer, w1_all, b1, w2_all, b2):
    def xrow(i, be, nu):
        return (jnp.minimum(i, nu[0] - 1), 0)

    grid_spec = pltpu.PrefetchScalarGridSpec(
        num_scalar_prefetch=2,
        grid=(n_blocks,),
        in_specs=[pl.BlockSpec((MOE_BLOCK, D), xrow),
                  pl.BlockSpec((1, 1, D, 2 * D_FF), lambda i, be, nu: (layer, be[i], 0, 0)),
                  pl.BlockSpec((1, 1, 2 * D_FF), lambda i, be, nu: (be[i], 0, 0)),
                  pl.BlockSpec((1, 1, D_FF, D), lambda i, be, nu: (layer, be[i], 0, 0)),
                  pl.BlockSpec((1, 1, D), lambda i, be, nu: (be[i], 0, 0))],
        out_specs=pl.BlockSpec((MOE_BLOCK, D), lambda i, be, nu: (i, 0)),
        scratch_shapes=[pltpu.VMEM((D, 2 * D_FF), BF16), pltpu.VMEM((D_FF, D), BF16)],
    )
    return pl.pallas_call(
        _expert_kernel,
        grid_spec=grid_spec,
        out_shape=jax.ShapeDtypeStruct((n_blocks * MOE_BLOCK, D), F32),
        compiler_params=_cparams(("arbitrary",)),
        name="moe_experts",
    )(block_expert, n_used, xs, w1_all, b1.reshape(N_EXPERTS, 1, 2 * D_FF), w2_all, b2.reshape(N_EXPERTS, 1, D))


def _combine_kernel(ng_ref, get_ref, nget_ref, h_ref, gate_ref, lpos_ref, mod_ref, y_hbm, *rest, final):
    fg_ref = rest[0] if final else None
    o_ref, comp_ref, sems = rest[-3:]
    i = pl.program_id(0)
    s = i % 2
    cr, tm = comp_ref.shape[1], h_ref.shape[0]

    def copies(slot, rows_ref, wait):
        src = lambda j: _rows_at(y_hbm, rows_ref[0, 0, j])
        dst = lambda j: _rows_at(comp_ref.at[slot], j * RUN_ALIGN)
        _group_copies(cr // RUN_ALIGN, src, dst, sems.at[slot], wait)

    @pl.when(i == 0)
    def _():
        copies(0, get_ref, False)

    @pl.when(i + 1 < pl.num_programs(0))
    def _():
        copies(1 - s, nget_ref, False)

    q = lax.broadcasted_iota(jnp.int32, (tm, cr), 1)
    gates, lp = gate_ref[...], lpos_ref[...]
    pg = jnp.where(lp[:, 0:1] == q, gates[:, 0:1], 0.0)
    for k in range(1, TOP_K):
        pg = pg + jnp.where(lp[:, k:k + 1] == q, gates[:, k:k + 1], 0.0)
    pltpu.make_async_copy(y_hbm.at[pl.ds(0, cr), :], comp_ref.at[s], sems.at[s]).wait()
    row = lax.broadcasted_iota(jnp.int32, (cr, 1), 0)
    yc = jnp.where(row < ng_ref[i] * RUN_ALIGN, comp_ref[s], 0.0).astype(BF16)
    y = jnp.dot(pg.astype(BF16), yc, preferred_element_type=F32)
    h_new = h_ref[...] + mod_ref[0][:, 5 * D:6 * D] * y
    if final:
        ms = jnp.mean(h_new * h_new, axis=-1, keepdims=True)
        h_new = h_new * lax.rsqrt(ms + NORM_EPS) * fg_ref[...]
    o_ref[...] = h_new


def _moe_combine(lay, h, mod, yb, gates_t, lpos_t, get_rows, ng, n_rows, final_g=None):
    tm = lay.tm
    nt, ngrp = n_rows // tm, _comp_rows(tm) // RUN_ALIGN
    final = final_g is not None
    extra_specs = [pl.BlockSpec((1, D), lambda i, *_: (0, 0))] if final else []
    extra_args = [final_g.reshape(1, D)] if final else []
    grid_spec = pltpu.PrefetchScalarGridSpec(
        num_scalar_prefetch=1,
        grid=(nt,),
        in_specs=[_group_table(0, nt, ngrp), _group_table(1, nt, ngrp),
                  pl.BlockSpec((tm, D), lambda i, *_: (i, 0)),
                  pl.BlockSpec((tm, TOP_K), lambda i, *_: (i, 0)),
                  pl.BlockSpec((tm, TOP_K), lambda i, *_: (i, 0)),
                  pl.BlockSpec((1, 1, 6 * D), lambda i, *_: (lay.mod_idx(i), 0, 0)),
                  pl.BlockSpec(memory_space=pl.ANY)] + extra_specs,
        out_specs=pl.BlockSpec((tm, D), lambda i, *_: (i, 0)),
        scratch_shapes=[pltpu.VMEM((2, _comp_rows(tm), D), F32), pltpu.SemaphoreType.DMA((2,))],
    )
    return pl.pallas_call(
        functools.partial(_combine_kernel, final=final),
        grid_spec=grid_spec,
        out_shape=jax.ShapeDtypeStruct((n_rows, D), F32),
        compiler_params=_cparams(("arbitrary",)),
        name="moe_combine",
    )(ng, get_rows, get_rows, h, gates_t, lpos_t, mod, yb, *extra_args)


def _moe_layer(lay, h, mod, g, w_router, b_router, layer, w1_all, b1, w2_all, b2, n_rows, final_g=None):
    gates, lpos, cnt = _moe_router(lay, h, mod, g, w_router, b_router, n_rows)
    put_rows, get_rows, ng, block_expert, fill_base, fill_cnt, n_used, n_blocks = _moe_plan(cnt[:, :, 0], n_rows, lay.tm)
    xs = _moe_compact(lay, h, mod, g, lpos, put_rows, fill_base, fill_cnt, n_rows, n_blocks)
    yb = _moe_experts(xs, block_expert, n_used, n_blocks, layer, w1_all, b1, w2_all, b2)
    return _moe_combine(lay, h, mod, yb, gates.T, lpos.T, get_rows, ng, n_rows, final_g)


def kernel(x, c, ctx, c_ctx, ada_w, ada_b, norm_g, attn_w_qkv, attn_b_qkv, attn_w_o, attn_b_o, attn_sinks, pool_w, pool_b, pool_scale, rwkv_mix, rwkv_w_rkv, rwkv_w0, rwkv_w1, rwkv_w2, rwkv_a0, rwkv_a1, rwkv_a2, rwkv_g1, rwkv_g2, rwkv_k_k, rwkv_k_a, rwkv_r_k, rwkv_ln_w, rwkv_ln_b, rwkv_w_o, moe_w_router, moe_b_router, moe_w1, moe_b1, moe_w2, moe_b2, final_g):
    B, T, _ = x.shape
    L = ctx.shape[1]
    depth = ada_w.shape[0]
    lay = _Layout(B, T, L)
    h = jnp.concatenate([x.reshape(B * T, D), ctx.reshape(B * L, D)], axis=0)

    n_c = B + 1
    n_c_pad = -(-n_c // 8) * 8
    c_all = jnp.concatenate([c, c_ctx[None, :], jnp.zeros((n_c_pad - n_c, D), F32)], axis=0)
    mods = _ada_mods(c_all, ada_w, ada_b)
    cos, sin = _rope_tables(T, L)

    for layer in range(depth):
        last = layer == depth - 1
        kind, j = layer % 3, layer // 3
        mod = mods[layer].reshape(n_c_pad, 1, 6 * D)
        n_rows = lay.rows_x if last else lay.N
        if kind == 0:
            q, k, v = _attn_qkv(lay, h, mod, norm_g[layer, 0], attn_w_qkv[j], attn_b_qkv[j], cos, sin)
            o = _attention(lay, q, k, v, attn_sinks[j], not last)
            h = _linear_residual(lay, h, o, mod, attn_w_o[j], attn_b_o[j], 2, n_rows)
        elif kind == 1:
            h = _pool_mixer(lay, h, mod, norm_g[layer, 0], pool_w[j], pool_b[j], pool_scale[j], n_rows)
        else:
            p = dict(mix=rwkv_mix[j], w_rkv=rwkv_w_rkv[j], w0=rwkv_w0[j], w1=rwkv_w1[j], w2=rwkv_w2[j],
                     a0=rwkv_a0[j], a1=rwkv_a1[j], a2=rwkv_a2[j], g1=rwkv_g1[j], g2=rwkv_g2[j],
                     k_k=rwkv_k_k[j], k_a=rwkv_k_a[j], r_k=rwkv_r_k[j])
            r, v, kn, g, bonus, lw0, k0, b0, lw1, k1, b1 = _rwkv_proj(lay, h, mod, norm_g[layer, 0], p)
            yf, yb = _rwkv_scan(lay, r, v, kn, lw0, k0, b0, lw1, k1, b1)
            h = _rwkv_out(lay, h, yf, yb, bonus, g, mod, rwkv_ln_w[j], rwkv_ln_b[j], rwkv_w_o[j])
        h = _moe_layer(lay, h, mod, norm_g[layer, 1], moe_w_router[layer], moe_b_router[layer],
                       layer, moe_w1, moe_b1[layer], moe_w2, moe_b2[layer], n_rows,
                       final_g if last else None)
    return h.reshape(B, T, D)
```

```python
import functools

import jax
import jax.numpy as jnp
import numpy as np
from jax import lax
from jax.experimental import pallas as pl
from jax.experimental.pallas import tpu as pltpu

F32 = jnp.float32
BF16 = jnp.bfloat16

D = 1024
NORM_EPS = 1e-5
GRID_W = 64
HEAD_DIM = 64
N_HEADS = 16
N_KV = 4
Q_DIM = 1024
KV_DIM = 256
QK_DIM = Q_DIM + KV_DIM
KVD_DIM = 2 * KV_DIM
QKD_DIM = Q_DIM + KVD_DIM
WINDOW = 128
QB = 128
ROPE_BASE = 10000.0
POOL_SIZES = (2, 4, 8, 16)
POOL_GW = 256
HALO = 8
GN_EPS = 64e-5
N_EXPERTS = 32
TOP_K = 4
D_FF = 1024
SWIGLU_LIMIT = 7.0
SWIGLU_ALPHA = 1.702
MOE_BLOCK = 512
RW_CHUNK = 64
RW_GROUP = 256
VMEM_LIMIT = 56 * 1024 * 1024


def _cparams(sem):
    return pltpu.CompilerParams(dimension_semantics=sem, vmem_limit_bytes=VMEM_LIMIT)


def _norm_mod(h, g, shift, scale):
    ms = jnp.mean(h * h, axis=-1, keepdims=True)
    y = h * lax.rsqrt(ms + NORM_EPS) * g
    return y * (1.0 + scale) + shift


class _Layout:
    def __init__(self, B, T, L):
        self.B, self.T, self.L = B, T, L
        self.tm = 256 if L % 256 == 0 else 128
        self.rows_x, self.rows_c = B * T, B * L
        self.N = self.rows_x + self.rows_c
        self.tx, self.tc = T // self.tm, L // self.tm
        self.n_xt = self.rows_x // self.tm
        self.n_t = self.N // self.tm

    def mod_idx(self, i):
        return jnp.where(i < self.n_xt, i // self.tx, self.B)

    def mod_spec(self):
        return pl.BlockSpec((1, 1, 6 * D), lambda i: (self.mod_idx(i), 0, 0))

    def row_spec(self, width=D):
        return pl.BlockSpec((self.tm, width), lambda i: (i, 0))

    def seq_tile(self, i):
        j = jnp.where(i < self.n_xt, i % self.tx, (i - self.n_xt) % self.tc)
        n = jnp.where(i < self.n_xt, self.tx, self.tc)
        return j, n

    def halo_specs(self):
        hb = self.tm // HALO
        last = self.N // HALO - 1
        prev = pl.BlockSpec((HALO, D), lambda i: (jnp.maximum(i * hb - 1, 0), 0))
        nxt = pl.BlockSpec((HALO, D), lambda i: (jnp.minimum((i + 1) * hb, last), 0))
        return prev, nxt


def _const_spec(shape):
    nd = len(shape)
    return pl.BlockSpec(shape, lambda *_: (0,) * nd)


def _ada_kernel(c_ref, w_ref, b_ref, o_ref):
    c = c_ref[...]
    s = c * jax.nn.sigmoid(c)
    o_ref[0] = jnp.dot(s, w_ref[0], preferred_element_type=F32, precision=lax.Precision.HIGHEST) + b_ref[0]


def _ada_mods(c_all, ada_w, ada_b):
    depth = ada_w.shape[0]
    R = c_all.shape[0]
    nt = 1536
    return pl.pallas_call(
        _ada_kernel,
        grid=(depth, 6 * D // nt),
        in_specs=[pl.BlockSpec((R, D), lambda l, j: (0, 0)),
                  pl.BlockSpec((1, D, nt), lambda l, j: (l, 0, j)),
                  pl.BlockSpec((1, 1, nt), lambda l, j: (l, 0, j))],
        out_specs=pl.BlockSpec((1, R, nt), lambda l, j: (l, 0, j)),
        out_shape=jax.ShapeDtypeStruct((depth, R, 6 * D), F32),
        compiler_params=_cparams(("parallel", "parallel")),
        name="ada_mod",
    )(c_all, ada_w, ada_b.reshape(depth, 1, 6 * D))


def _qkv_kernel(h_ref, mod_ref, g_ref, w_ref, b_ref, cos_ref, sin_ref, q_ref, k_ref, v_ref):
    m = mod_ref[0]
    a = _norm_mod(h_ref[...], g_ref[...], m[:, 0:D], m[:, D:2 * D])
    qkv = jnp.dot(a.astype(BF16), w_ref[...], preferred_element_type=F32) + b_ref[...]
    qk = qkv[:, :QK_DIM]
    lane = lax.broadcasted_iota(jnp.int32, (1, QK_DIM), 1)
    first = (lane % 32) < 16
    rot = jnp.where(first, -pltpu.roll(qk, QK_DIM - 16, 1), pltpu.roll(qk, 16, 1))
    qk = qk * cos_ref[...] + rot * sin_ref[...]
    q_ref[...] = (qk[:, :Q_DIM] * (HEAD_DIM ** -0.5)).astype(BF16)
    k_ref[...] = _dup_heads(qk[:, Q_DIM:]).astype(BF16)
    v_ref[...] = _dup_heads(qkv[:, QK_DIM:]).astype(BF16)


def _dup_heads(x):
    heads = [x[:, g * HEAD_DIM:(g + 1) * HEAD_DIM] for g in range(N_KV)]
    return jnp.concatenate([hh for hd in heads for hh in (hd, hd)], axis=1)


def _rope_tables(T, L):
    rows = T // GRID_W
    n_freq = HEAD_DIM // 4
    inv = ROPE_BASE ** (-jnp.arange(n_freq, dtype=F32) / n_freq)
    row_ang = jnp.arange(rows, dtype=F32)[:, None] * inv
    col_ang = jnp.arange(GRID_W, dtype=F32)[:, None] * inv
    ang_r = jnp.broadcast_to(row_ang[:, None, :], (rows, GRID_W, n_freq)).reshape(T, n_freq)
    ang_c = jnp.broadcast_to(col_ang[None, :, :], (rows, GRID_W, n_freq)).reshape(T, n_freq)
    ang = jnp.concatenate([ang_r, ang_r, ang_c, ang_c], axis=-1)
    cos = jnp.concatenate([jnp.cos(ang), jnp.ones((L, HEAD_DIM), F32)], axis=0)
    sin = jnp.concatenate([jnp.sin(ang), jnp.zeros((L, HEAD_DIM), F32)], axis=0)
    reps = QK_DIM // HEAD_DIM
    return jnp.tile(cos, (1, reps)), jnp.tile(sin, (1, reps))


def _attn_qkv(lay, h, mod, g, w_qkv, b_qkv, cos, sin):
    tm = lay.tm

    def tab_idx(i):
        return (jnp.where(i < lay.n_xt, i % lay.tx, lay.tx + (i - lay.n_xt) % lay.tc), 0)

    w = w_qkv.astype(BF16)
    b = b_qkv.reshape(1, -1)
    width = Q_DIM + 2 * KV_DIM
    return pl.pallas_call(
        _qkv_kernel,
        grid=(lay.n_t,),
        in_specs=[lay.row_spec(), lay.mod_spec(), _const_spec((1, D)),
                  _const_spec((D, width)), _const_spec((1, width)),
                  pl.BlockSpec((tm, QK_DIM), tab_idx), pl.BlockSpec((tm, QK_DIM), tab_idx)],
        out_specs=[lay.row_spec(Q_DIM), lay.row_spec(KVD_DIM), lay.row_spec(KVD_DIM)],
        out_shape=[jax.ShapeDtypeStruct((lay.N, Q_DIM), BF16),
                   jax.ShapeDtypeStruct((lay.N, KVD_DIM), BF16),
                   jax.ShapeDtypeStruct((lay.N, KVD_DIM), BF16)],
        compiler_params=_cparams(("parallel",)),
        name="attn_qkv",
    )(h, mod, g.reshape(1, D), w, b, cos, sin)


def _attn_heads(q, kk, vv, mask, sink_ref, o_ref):
    R = q.shape[0]
    G = N_HEADS // N_KV
    TILE = 2 * HEAD_DIM
    lo = lax.broadcasted_iota(jnp.int32, (1, TILE), 1) < HEAD_DIM
    order = (0, 2, 1, 3)
    scores = []
    for g in range(N_KV):
        pieces = []
        for j in order:
            hq = g * G + j
            qt = q[:, (hq // 2) * TILE:(hq // 2 + 1) * TILE]
            pieces.append(jnp.where(lo if hq % 2 == 0 else ~lo, qt, jnp.zeros_like(qt)))
        qs = jnp.concatenate(pieces, axis=0)
        scores.append(lax.dot_general(qs, kk[:, g * TILE:(g + 1) * TILE], (((1,), (1,)), ((), ())),
                                      preferred_element_type=F32))
    probs, esinks = [], []
    for g in range(N_KV):
        s = scores[g]
        if mask is not None:
            s = jnp.where(jnp.concatenate([mask] * G, axis=0), s, -jnp.inf)
        sink = jnp.concatenate([jnp.full((R, 1), sink_ref[g * G + j], F32) for j in order], axis=0)
        m = jnp.maximum(jnp.max(s, axis=-1, keepdims=True), sink)
        probs.append(jnp.exp((s - m).astype(BF16)))
        esinks.append(jnp.exp(sink - m))
    for g in range(N_KV):
        vd = vv[:, g * TILE:(g + 1) * TILE]
        one = jnp.ones_like(vd)
        p, es = probs[g], esinks[g]
        half = (G // 2) * R
        oe = jnp.dot(p[:half], jnp.where(lo, vd, one), preferred_element_type=F32)
        oo = jnp.dot(p[half:], jnp.where(lo, one, vd), preferred_element_type=F32)
        re = oe / (pltpu.roll(oe, HEAD_DIM, 1) + es[:half])
        ro = oo / (pltpu.roll(oo, HEAD_DIM, 1) + es[half:])
        for t in range(G // 2):
            tile = jnp.where(lo, re[t * R:(t + 1) * R], ro[t * R:(t + 1) * R])
            c = g * (G // 2) + t
            o_ref[:, c * TILE:(c + 1) * TILE] = tile.astype(o_ref.dtype)


def _attn_kernel(sink_ref, q_ref, kp_ref, kc_ref, kn_ref, vp_ref, vc_ref, vn_ref, kx_ref, vx_ref, o_ref, *, T, L):
    i = pl.program_id(1)
    nqb = T // QB

    @pl.when(i < nqb)
    def _():
        kk = jnp.concatenate([kx_ref[...], kp_ref[...], kc_ref[...], kn_ref[...]], axis=0)
        vv = jnp.concatenate([vx_ref[...], vp_ref[...], vc_ref[...], vn_ref[...]], axis=0)
        S = L + 3 * QB
        col = lax.broadcasted_iota(jnp.int32, (QB, S), 1)
        row = lax.broadcasted_iota(jnp.int32, (QB, S), 0)
        rel = col - L - QB - row
        kpos = i * QB - QB + (col - L)
        local_ok = (jnp.abs(rel) <= WINDOW) & (kpos >= 0) & (kpos < T)
        mask = (col < L) | local_ok
        _attn_heads(q_ref[...], kk, vv, mask, sink_ref, o_ref)

    @pl.when(i >= nqb)
    def _():
        _attn_heads(q_ref[...], kx_ref[...], vx_ref[...], None, sink_ref, o_ref)


def _attention(lay, q, k, v, sinks, need_ctx):
    B, T, L = lay.B, lay.T, lay.L
    nqb, ncb = T // QB, L // QB
    cq = lay.rows_x // QB
    cb = lay.rows_x // L
    smem = pl.BlockSpec(memory_space=pltpu.SMEM)

    def cur(b, i):
        return (jnp.where(i < nqb, b * nqb + i, cq + b * ncb + (i - nqb)), 0)

    def prev(b, i):
        return (b * nqb + jnp.clip(i - 1, 0, nqb - 1), 0)

    def mid(b, i):
        return (b * nqb + jnp.minimum(i, nqb - 1), 0)

    def nxt(b, i):
        return (b * nqb + jnp.minimum(i + 1, nqb - 1), 0)

    def ctxb(b, i):
        return (cb + b, 0)

    kv_specs = [pl.BlockSpec((QB, KVD_DIM), prev), pl.BlockSpec((QB, KVD_DIM), mid), pl.BlockSpec((QB, KVD_DIM), nxt)]
    n_out = lay.N if need_ctx else lay.rows_x
    return pl.pallas_call(
        functools.partial(_attn_kernel, T=T, L=L),
        grid=(B, nqb + (ncb if need_ctx else 0)),
        in_specs=[smem, pl.BlockSpec((QB, Q_DIM), cur)] + kv_specs + kv_specs
                 + [pl.BlockSpec((L, KVD_DIM), ctxb), pl.BlockSpec((L, KVD_DIM), ctxb)],
        out_specs=pl.BlockSpec((QB, Q_DIM), cur),
        out_shape=jax.ShapeDtypeStruct((n_out, Q_DIM), BF16),
        compiler_params=_cparams(("parallel", "arbitrary")),
        name="attention",
    )(sinks, q, k, k, k, v, v, v, k, v)


def _linres_kernel(h_ref, x_ref, mod_ref, w_ref, b_ref, o_ref, *, gate_slot):
    gate = mod_ref[0][:, gate_slot * D:(gate_slot + 1) * D]
    y = jnp.dot(x_ref[...], w_ref[...], preferred_element_type=F32) + b_ref[...]
    o_ref[...] = h_ref[...] + gate * y


def _linear_residual(lay, h, x, mod, w, b, gate_slot, n_rows):
    return pl.pallas_call(
        functools.partial(_linres_kernel, gate_slot=gate_slot),
        grid=(n_rows // lay.tm,),
        in_specs=[lay.row_spec(), lay.row_spec(x.shape[1]), lay.mod_spec(),
                  _const_spec(w.shape), _const_spec((1, D))],
        out_specs=lay.row_spec(),
        out_shape=jax.ShapeDtypeStruct((n_rows, D), F32),
        compiler_params=_cparams(("parallel",)),
        name="linear_residual",
    )(h, x, mod, w.astype(BF16), b.reshape(1, D))


def _fill_ext(ext_ref, h_ref, hp_ref, hn_ref, g, shift, scale, j, n, tm):
    a = _norm_mod(h_ref[...], g, shift, scale)
    ap = _norm_mod(hp_ref[...], g, shift, scale)
    an = _norm_mod(hn_ref[...], g, shift, scale)
    ext_ref[0:HALO, :] = jnp.where(j == 0, 0.0, ap)
    ext_ref[HALO:HALO + tm, :] = a
    ext_ref[HALO + tm:2 * HALO + tm, :] = jnp.where(j == n - 1, 0.0, an)
    return a


def _pool_kernel(h_ref, hp_ref, hn_ref, mod_ref, g_ref, w_ref, b_ref, ls_ref, o_ref, ext_ref, *, lay):
    tm = lay.tm
    i = pl.program_id(0)
    j, n = lay.seq_tile(i)
    m = mod_ref[0]
    a = _fill_ext(ext_ref, h_ref, hp_ref, hn_ref, g_ref[...], m[:, 0:D], m[:, D:2 * D], j, n, tm)
    t = j * tm + lax.broadcasted_iota(jnp.int32, (tm, 1), 0)
    n_tok = n * tm
    outs = []
    for gi, size in enumerate(POOL_SIZES):
        sl = slice(gi * POOL_GW, (gi + 1) * POOL_GW)
        acc = ext_ref[HALO - size // 2:HALO - size // 2 + tm, sl]
        for o in range(-size // 2 + 1, size - size // 2):
            acc = acc + ext_ref[HALO + o:HALO + o + tm, sl]
        lo = jnp.maximum(t - size // 2, 0)
        hi = jnp.minimum(t + size - size // 2, n_tok)
        d = acc / (hi - lo).astype(F32) - a[:, sl]
        outs.append(jnp.dot(d.astype(BF16), w_ref[gi], preferred_element_type=F32) + b_ref[gi])
    y = jnp.concatenate(outs, axis=-1) * ls_ref[...]
    o_ref[...] = h_ref[...] + m[:, 2 * D:3 * D] * y


def _pool_mixer(lay, h, mod, g, w, b, ls, n_rows):
    prev, nxt = lay.halo_specs()
    return pl.pallas_call(
        functools.partial(_pool_kernel, lay=lay),
        grid=(n_rows // lay.tm,),
        in_specs=[lay.row_spec(), prev, nxt, lay.mod_spec(), _const_spec((1, D)),
                  _const_spec(w.shape), _const_spec((4, 1, POOL_GW)), _const_spec((1, D))],
        out_specs=lay.row_spec(),
        out_shape=jax.ShapeDtypeStruct((n_rows, D), F32),
        scratch_shapes=[pltpu.VMEM((lay.tm + 2 * HALO, D), F32)],
        compiler_params=_cparams(("parallel",)),
        name="pool_mixer",
    )(h, h, h, mod, g.reshape(1, D), w.astype(BF16), b.reshape(4, 1, POOL_GW), ls.reshape(1, D))


def _seg_sum(x, bd, split=True):
    outs = []
    for q in range(D // RW_GROUP):
        xg = x[:, q * RW_GROUP:(q + 1) * RW_GROUP]
        hi = xg.astype(BF16)
        acc = jnp.dot(hi, bd, preferred_element_type=F32)
        if split:
            acc = acc + jnp.dot((xg - hi.astype(F32)).astype(BF16), bd, preferred_element_type=F32)
        outs.append(acc)
    return jnp.concatenate(outs, axis=-1)


def _rwkv_proj_kernel(h_ref, hp_ref, hn_ref, mod_ref, g_ref, mix_ref, wr_ref, wk_ref, wv_ref, g1_ref, g2_ref,
                      w1_ref, w2_ref, w0_ref, a1_ref, a2_ref, a0_ref, kk_ref, ka_ref, rk_ref, bd_ref,
                      r_out, v_out, kn_out, g_out, bonus_out, lw0_out, k0_out, b0_out, lw1_out, k1_out, b1_out,
                      ext_ref, *, lay):
    tm = lay.tm
    i = pl.program_id(0)
    j, n = lay.seq_tile(i)
    m = mod_ref[0]
    a = _fill_ext(ext_ref, h_ref, hp_ref, hn_ref, g_ref[...], m[:, 0:D], m[:, D:2 * D], j, n, tm)
    prev = ext_ref[HALO - 1:HALO - 1 + tm, :]
    nxt = ext_ref[HALO + 1:HALO + 1 + tm, :]
    xx = 0.5 * (prev + nxt) - a
    x_r, x_w, x_k, x_v, x_a, x_g = [(a + xx * mix_ref[q:q + 1, :]).astype(BF16) for q in range(6)]
    bd = bd_ref[...]
    r = jnp.dot(x_r, wr_ref[...], preferred_element_type=F32)
    k = jnp.dot(x_k, wk_ref[...], preferred_element_type=F32)
    v = jnp.dot(x_v, wv_ref[...], preferred_element_type=F32)
    gl = jnp.dot(x_g, g1_ref[...], preferred_element_type=F32)
    g = jnp.dot(jax.nn.sigmoid(gl).astype(BF16), g2_ref[...], preferred_element_type=F32)
    kk = k * kk_ref[...]
    kn = kk / jnp.maximum(jnp.sqrt(_seg_sum(kk * kk, bd)), 1e-12)
    tw = jnp.tanh(jnp.dot(x_w, w1_ref[...], preferred_element_type=F32))
    aa = jnp.dot(x_a, a1_ref[...], preferred_element_type=F32)
    lane = lax.broadcasted_iota(jnp.int32, (1, tw.shape[1]), 1)
    r_out[...] = r.astype(r_out.dtype)
    v_out[...] = v.astype(v_out.dtype)
    kn_out[...] = kn.astype(kn_out.dtype)
    g_out[...] = g.astype(g_out.dtype)
    rk = rk_ref[...]
    bonus = jnp.zeros_like(v)
    outs = ((lw0_out, k0_out, b0_out), (lw1_out, k1_out, b1_out))
    half = tw.shape[1] // 2
    for d in range(2):
        sel = (lane >= d * half) & (lane < (d + 1) * half)
        wl = w0_ref[d:d + 1, :] + jnp.dot(jnp.where(sel, tw, 0.0).astype(BF16), w2_ref[...], preferred_element_type=F32)
        z = -wl
        w_log = -(jnp.maximum(z, 0.0) + jnp.log(1.0 + jnp.exp(-jnp.abs(z)))) - 0.5
        asig = jax.nn.sigmoid(a0_ref[d:d + 1, :] + jnp.dot(jnp.where(sel, aa, 0.0).astype(BF16), a2_ref[...],
                                                           preferred_element_type=F32))
        k_d = k * (1.0 + (asig - 1.0) * ka_ref[...])
        lw_o, k_o, b_o = outs[d]
        lw_o[...] = -jnp.exp(w_log)
        k_o[...] = k_d.astype(k_o.dtype)
        b_o[...] = (kn * asig).astype(b_o.dtype)
        bonus = bonus + _seg_sum(r * k_d * rk, bd, split=False) * v
    bonus_out[...] = bonus.astype(bonus_out.dtype)


def _block_ones():
    idx = np.arange(RW_GROUP) // 64
    return jnp.asarray(idx[:, None] == idx[None, :], dtype=BF16)


def _rwkv_proj(lay, h, mod, g, p):
    prev, nxt = lay.halo_specs()
    cat = lambda w: jnp.concatenate([w[0], w[1]], axis=-1).astype(BF16)
    stack = lambda w: jnp.concatenate([w[0], w[1]], axis=0).astype(BF16)
    args = [h, h, h, mod, g.reshape(1, D), p['mix'],
            p['w_rkv'][0].astype(BF16), p['w_rkv'][1].astype(BF16), p['w_rkv'][2].astype(BF16),
            p['g1'].astype(BF16), p['g2'].astype(BF16),
            cat(p['w1']), stack(p['w2']), p['w0'], cat(p['a1']), stack(p['a2']), p['a0'],
            p['k_k'].reshape(1, D), p['k_a'].reshape(1, D), p['r_k'].reshape(1, D), _block_ones()]
    in_specs = [lay.row_spec(), prev, nxt, lay.mod_spec()] + [_const_spec(x.shape) for x in args[4:]]
    dtypes = [BF16] * 5 + [F32, BF16, BF16] * 2
    return pl.pallas_call(
        functools.partial(_rwkv_proj_kernel, lay=lay),
        grid=(lay.n_t,),
        in_specs=in_specs,
        out_specs=[lay.row_spec()] * len(dtypes),
        out_shape=[jax.ShapeDtypeStruct((lay.N, D), dt) for dt in dtypes],
        scratch_shapes=[pltpu.VMEM((lay.tm + 2 * HALO, D), F32)],
        compiler_params=_cparams(("parallel",)),
        name="rwkv_proj",
    )(*args)


def _rwkv_scan_kernel(rf, vf, nf, lwf, kf, bf, rb, vb, nb, lwb, kb, bb, yf_out, yb_out, z_ref):
    C, G = RW_CHUNK, RW_GROUP
    c = pl.program_id(1)

    @pl.when(c == 0)
    def _():
        z_ref[...] = jnp.zeros_like(z_ref)

    trow = lax.broadcasted_iota(jnp.int32, (C, C), 0)
    tcol = lax.broadcasted_iota(jnp.int32, (C, C), 1)
    ti = lax.broadcasted_iota(jnp.int32, (C, G), 0)
    ii = lax.broadcasted_iota(jnp.int32, (C, G), 1) % C
    brow = lax.broadcasted_iota(jnp.int32, (G, G), 0)
    bcol = lax.broadcasted_iota(jnp.int32, (G, G), 1)
    m_bd_t = (brow // C) == (bcol // 64)
    m_bd_f = (brow // 64) == (bcol // 64)
    eye_g = brow == bcol

    def bd(x):
        return jnp.where(m_bd_t, jnp.concatenate([x] * (G // C), axis=0), 0.0).astype(BF16)

    def mm(a, b):
        return jnp.dot(a, b, preferred_element_type=F32)

    dirs = ((rf, vf, nf, lwf, kf, bf, yf_out, False), (rb, vb, nb, lwb, kb, bb, yb_out, True))
    ch = []
    for d, (r_ref, v_ref, n_ref, lw_ref, k_ref, b_ref, y_out, rev) in enumerate(dirs):
        tri = (tcol >= trow) if rev else (tcol <= trow)
        strict = (ii > ti) if rev else (ii < ti)
        incl = (ii >= ti) if rev else (ii <= ti)
        lw = lw_ref[...]
        lc = jnp.dot(tri.astype(F32), lw, preferred_element_type=F32, precision=lax.Precision.HIGHEST)
        ltot = lc[0:1, :] if rev else lc[C - 1:C, :]
        e_lc = jnp.exp(lc)
        ie_lc = jnp.exp(-lc)
        e_le = jnp.exp(lc - lw)
        e_bar = jnp.exp(ltot - lc)
        e_tot = jnp.exp(ltot)
        kvec, bvec = k_ref[...], b_ref[...]
        a_t = -n_ref[...] * e_le
        r_t = r_ref[...] * e_lc
        b_t = bvec * ie_lc
        k_t = kvec * ie_lc
        b_b = bvec * e_bar
        k_b = kvec * e_bar
        vall = v_ref[...]
        for g in range(D // G):
            sl = slice(g * G, (g + 1) * G)
            ch.append(dict(
                d=d, g=g, sl=sl, y_out=y_out, strict=strict, incl=incl, v=vall[:, sl], e_tot=e_tot[:, sl],
                ar=jnp.concatenate([a_t[:, sl], r_t[:, sl]], axis=0).astype(BF16),
                bk=jnp.concatenate([bd(b_t[:, sl]), bd(k_t[:, sl])], axis=0),
                lhs_t=jnp.concatenate([b_b[:, sl], k_b[:, sl]], axis=0).astype(BF16)))

    for q in ch:
        aa = lax.dot_general(q['ar'], q['bk'], (((1,), (1,)), ((), ())), preferred_element_type=F32)
        q['a_ab'] = jnp.where(q['strict'], aa[:C, :G], 0.0)
        q['a_ak'] = jnp.where(q['strict'], aa[:C, G:], 0.0)
        q['m_rb'] = jnp.where(q['incl'], aa[C:, :G], 0.0).astype(BF16)
        q['m_rk'] = jnp.where(q['incl'], aa[C:, G:], 0.0)
    for q in ch:
        q['x'] = jnp.where(ii == ti, 1.0, 0.0) + q['a_ab']
        q['p'] = mm(q['a_ab'].astype(BF16), bd(q['a_ab']))
    for q in ch:
        q['wv'] = mm(jnp.concatenate([q['a_ak'], q['m_rk']], axis=0).astype(BF16), bd(q['v']))
        q['z'] = z_ref[q['d'], q['g']]
        q['az'] = mm(q['ar'], q['z'].astype(BF16))
    n_sq = int(np.log2(C)) - 1
    for s in range(n_sq):
        last = s == n_sq - 1
        for q in ch:
            lhs = q['x'] if last else jnp.concatenate([q['x'], q['p']], axis=0)
            xp = mm(lhs.astype(BF16), bd(q['p']))
            q['x'] = q['x'] + xp[:C]
            if not last:
                q['p'] = xp[C:]
    for q in ch:
        q['u'] = mm(q['x'].astype(BF16), bd(q['az'][:C] + q['wv'][:C]))
    for q in ch:
        y = q['az'][C:] + q['wv'][C:] + mm(q['m_rb'], bd(q['u']))
        q['y_out'][:, q['sl']] = y
        rhs_t = jnp.concatenate([q['u'], q['v']], axis=0).astype(BF16)
        zu = lax.dot_general(q['lhs_t'], rhs_t, (((0,), (0,)), ((), ())), preferred_element_type=F32)
        decay = jnp.sum(jnp.where(eye_g, jnp.broadcast_to(q['e_tot'], (G, G)), 0.0), axis=1, keepdims=True)
        z_ref[q['d'], q['g']] = jnp.where(m_bd_f, decay * q['z'] + zu, 0.0)


def _rwkv_scan(lay, r, v, kn, lw0, k0, b0, lw1, k1, b1):
    B, T, L = lay.B, lay.T, lay.L
    C = RW_CHUNK
    cc, cx = L // C, T // C
    nc = cc + cx
    ctx0 = lay.rows_x // C

    def fwd(b, c):
        return (jnp.where(c < cc, ctx0 + b * cc + c, b * cx + c - cc), 0)

    def bwd(b, c):
        return (jnp.where(c < cc, ctx0 + b * cc + (cc - 1 - c), b * cx + (nc - 1 - c)), 0)

    sf, sb = pl.BlockSpec((C, D), fwd), pl.BlockSpec((C, D), bwd)
    return pl.pallas_call(
        _rwkv_scan_kernel,
        grid=(B, nc),
        in_specs=[sf] * 6 + [sb] * 6,
        out_specs=[sf, sb],
        out_shape=[jax.ShapeDtypeStruct((lay.N, D), F32)] * 2,
        scratch_shapes=[pltpu.VMEM((2, D // RW_GROUP, RW_GROUP, RW_GROUP), F32)],
        compiler_params=_cparams(("parallel", "arbitrary")),
        name="rwkv_scan",
    )(r, v, kn, lw0, k0, b0, r, v, kn, lw1, k1, b1)


def _rwkv_out_kernel(h_ref, yf_ref, yb_ref, bonus_ref, g_ref, mod_ref, lnw_ref, lnb_ref, wo_ref, bd_ref, o_ref):
    bd = bd_ref[...]
    y = yf_ref[...] + yb_ref[...]
    mu = _seg_sum(y, bd) * (1.0 / 64)
    yc = y - mu
    var = _seg_sum(yc * yc, bd) * (1.0 / 64)
    yn = yc * lax.rsqrt(var + GN_EPS) * lnw_ref[...] + lnb_ref[...]
    out = (yn + bonus_ref[...]) * g_ref[...]
    res = jnp.dot(out.astype(BF16), wo_ref[...], preferred_element_type=F32)
    o_ref[...] = h_ref[...] + mod_ref[0][:, 2 * D:3 * D] * res


def _rwkv_out(lay, h, yf, yb, bonus, g, mod, ln_w, ln_b, w_o):
    return pl.pallas_call(
        _rwkv_out_kernel,
        grid=(lay.n_t,),
        in_specs=[lay.row_spec()] * 5 + [lay.mod_spec(), _const_spec((1, D)), _const_spec((1, D)),
                                          _const_spec((D, D)), _const_spec((RW_GROUP, RW_GROUP))],
        out_specs=lay.row_spec(),
        out_shape=jax.ShapeDtypeStruct((lay.N, D), F32),
        compiler_params=_cparams(("parallel",)),
        name="rwkv_out",
    )(h, yf, yb, bonus, g, mod, ln_w.reshape(1, D), ln_b.reshape(1, D), w_o.astype(BF16), _block_ones())


RUN_ALIGN = 8


def _router_kernel(h_ref, mod_ref, g_ref, wr_ref, br_ref, upper_ref, gate_ref, lpos_ref, cnt_ref):
    m = mod_ref[0]
    f = _norm_mod(h_ref[...], g_ref[...], m[:, 3 * D:4 * D], m[:, 4 * D:5 * D])
    def split(x):
        hi = x.astype(BF16)
        return hi, (x - hi.astype(F32)).astype(BF16)

    def nt(a, b):
        return lax.dot_general(a, b, (((1,), (1,)), ((), ())), preferred_element_type=F32)

    (w_hi, w_lo), (f_hi, f_lo) = split(wr_ref[...]), split(f)
    logits = nt(w_hi, f_hi) + (nt(w_hi, f_lo) + nt(w_lo, f_hi)) + br_ref[...]
    erow = lax.broadcasted_iota(jnp.int32, logits.shape, 0)
    tops, hots = [], []
    l = logits
    for _ in range(TOP_K):
        mx = jnp.max(l, axis=0, keepdims=True)
        ix = jnp.min(jnp.where(l == mx, erow, N_EXPERTS), axis=0, keepdims=True)
        tops.append(mx)
        hots.append(erow == ix)
        l = jnp.where(erow == ix, -jnp.inf, l)
    es = [jnp.exp(t - tops[0]) for t in tops]
    tot = es[0] + es[1] + es[2] + es[3]
    gate_ref[...] = jnp.concatenate([e / tot for e in es], axis=0)

    ohs = [jnp.where(hh, 1.0, 0.0) for hh in hots]
    cnts = [jnp.sum(oh, axis=1, keepdims=True) for oh in ohs]
    cnt = cnts[0] + cnts[1] + cnts[2] + cnts[3]
    run = jnp.ceil(cnt * (1.0 / RUN_ALIGN)) * RUN_ALIGN
    ee = lax.broadcasted_iota(jnp.int32, (N_EXPERTS, N_EXPERTS), 0)
    ec = lax.broadcasted_iota(jnp.int32, (N_EXPERTS, N_EXPERTS), 1)
    run_row = jnp.sum(jnp.where(ee == ec, jnp.broadcast_to(run, (N_EXPERTS, N_EXPERTS)), 0.0), axis=0, keepdims=True)
    offs = jnp.sum(jnp.where(ec < ee, jnp.broadcast_to(run_row, (N_EXPERTS, N_EXPERTS)), 0.0), axis=1, keepdims=True)
    base = offs
    lpos = []
    for k in range(TOP_K):
        before = jnp.dot(ohs[k].astype(BF16), upper_ref[...], preferred_element_type=F32)
        lpos.append(jnp.sum(ohs[k] * (before + base), axis=0, keepdims=True))
        base = base + cnts[k]
    lpos_ref[...] = jnp.concatenate(lpos, axis=0).astype(jnp.int32)
    cnt_ref[0] = cnt.astype(jnp.int32)


def _moe_router(lay, h, mod, g, w_router, b_router, n_rows):
    tm = lay.tm
    nt = n_rows // tm
    upper = jnp.asarray(np.triu(np.ones((tm, tm), np.float32), 1), dtype=BF16)
    return pl.pallas_call(
        _router_kernel,
        grid=(nt,),
        in_specs=[lay.row_spec(), lay.mod_spec(), _const_spec((1, D)),
                  _const_spec((N_EXPERTS, D)), _const_spec((N_EXPERTS, 1)), _const_spec((tm, tm))],
        out_specs=[pl.BlockSpec((TOP_K, tm), lambda i: (0, i)),
                   pl.BlockSpec((TOP_K, tm), lambda i: (0, i)),
                   pl.BlockSpec((1, N_EXPERTS, 1), lambda i: (i, 0, 0))],
        out_shape=[jax.ShapeDtypeStruct((TOP_K, n_rows), F32),
                   jax.ShapeDtypeStruct((TOP_K, n_rows), jnp.int32),
                   jax.ShapeDtypeStruct((nt, N_EXPERTS, 1), jnp.int32)],
        compiler_params=_cparams(("parallel",)),
        name="moe_router",
    )(h, mod, g.reshape(1, D), w_router.T, b_router.reshape(N_EXPERTS, 1), upper)


def _comp_rows(tm):
    return tm * TOP_K + N_EXPERTS * RUN_ALIGN


def _moe_plan(cnt, n, tm):
    nt = n // tm
    def excl_sum(a, axis):
        k = a.shape[axis]
        before = np.arange(k)[:, None] > np.arange(k)[None, :]
        if axis == 0:
            return jnp.sum(jnp.where(before[:, :, None], a[None, :, :], 0), axis=1)
        return jnp.sum(jnp.where(before[None, :, :], a[:, None, :], 0), axis=2)

    run = (cnt + RUN_ALIGN - 1) // RUN_ALIGN * RUN_ALIGN
    tot = jnp.sum(run, axis=0)
    padded = (tot + MOE_BLOCK - 1) // MOE_BLOCK * MOE_BLOCK
    pad_start = excl_sum(padded[None, :], 1)[0]
    pad_end = pad_start + padded
    base = pad_start[None, :] + excl_sum(run, 0)
    rows_max = n * TOP_K + nt * N_EXPERTS * (RUN_ALIGN - 1) + N_EXPERTS * MOE_BLOCK
    n_blocks = -(-rows_max // MOE_BLOCK)
    rows = n_blocks * MOE_BLOCK
    blk_row = jnp.arange(n_blocks, dtype=jnp.int32) * MOE_BLOCK
    block_expert = jnp.minimum(jnp.sum((pad_end[None, :] <= blk_row[:, None]).astype(jnp.int32), axis=1),
                               N_EXPERTS - 1).astype(jnp.int32)
    spare_rows = 2 * _comp_rows(tm)
    fill_base = jnp.concatenate([pad_start + tot, pad_end[-1:], jnp.full((1,), rows)]).astype(jnp.int32)
    fill_cnt = (jnp.concatenate([padded - tot, rows - pad_end[-1:], jnp.full((1,), spare_rows)])
                // RUN_ALIGN).astype(jnp.int32)
    n_used = (pad_end[-1:] // MOE_BLOCK).astype(jnp.int32)
    cr = _comp_rows(tm)
    ngrp = cr // RUN_ALIGN
    run_g = run // RUN_ALIGN
    goffs = excl_sum(run_g, 1)
    gend = goffs + run_g
    ng = gend[:, -1]
    j = jnp.arange(ngrp, dtype=jnp.int32)
    e_of = jnp.minimum(jnp.sum((gend[:, None, :] <= j[None, :, None]).astype(jnp.int32), axis=2), N_EXPERTS - 1)
    oh = e_of[:, :, None] == jnp.arange(N_EXPERTS, dtype=jnp.int32)[None, None, :]
    pick = lambda a: jnp.sum(jnp.where(oh, a[:, None, :], 0), axis=2)
    row = pick(base) + RUN_ALIGN * (j[None, :] - pick(goffs))
    live = j[None, :] < ng[:, None]
    spare = rows + (jnp.arange(nt, dtype=jnp.int32)[:, None] % 2) * cr + RUN_ALIGN * j[None, :]
    put_rows = jnp.where(live, row, spare).astype(jnp.int32).reshape(nt, 1, ngrp)
    get_rows = jnp.where(live, row, 0).astype(jnp.int32).reshape(nt, 1, ngrp)
    return put_rows, get_rows, block_expert, fill_base, fill_cnt, n_used, n_blocks


def _group_copies(n_groups, src_of, dst_of, sem, wait):
    def body(j, carry):
        cp = pltpu.make_async_copy(src_of(j), dst_of(j), sem)
        cp.wait() if wait else cp.start()
        return carry

    lax.fori_loop(0, n_groups, body, 0, unroll=8)


def _rows_at(ref, start):
    return ref.at[pl.ds(pl.multiple_of(start, RUN_ALIGN), RUN_ALIGN), :]


def _compact_kernel(fill_base_ref, fill_cnt_ref, put_ref, pput_ref,
                    h_ref, mod_ref, g_ref, lpos_ref, xs_hbm, comp_ref, zero_ref, sems):
    i = pl.program_id(0)
    s = i % 2
    m = mod_ref[0]
    f = _norm_mod(h_ref[...], g_ref[...], m[:, 3 * D:4 * D], m[:, 4 * D:5 * D]).astype(BF16)
    cr, tm = comp_ref.shape[1], h_ref.shape[0]
    q = lax.broadcasted_iota(jnp.int32, (cr, tm), 0).astype(F32)
    lp = lpos_ref[...].astype(F32)
    miss = (lp[0:1, :] - q) * (lp[1:2, :] - q) * ((lp[2:3, :] - q) * (lp[3:4, :] - q))
    comp_ref[s] = jnp.dot(jnp.where(miss == 0.0, 1.0, 0.0).astype(BF16), f, preferred_element_type=F32)

    def copies(slot, rows_ref, wait):
        src = lambda j: _rows_at(comp_ref.at[slot], j * RUN_ALIGN)
        dst = lambda j: _rows_at(xs_hbm, rows_ref[0, 0, j])
        _group_copies(cr // RUN_ALIGN, src, dst, sems.at[slot], wait)

    @pl.when(i == 0)
    def _():
        zero_ref[...] = jnp.zeros_like(zero_ref)
        for e in range(fill_cnt_ref.shape[0]):
            def fill(c, carry, wait=False, e=e):
                cp = pltpu.make_async_copy(zero_ref, _rows_at(xs_hbm, fill_base_ref[e] + c * RUN_ALIGN), sems.at[0])
                cp.wait() if wait else cp.start()
                return carry

            lax.fori_loop(0, fill_cnt_ref[e], fill, 0)
            lax.fori_loop(0, fill_cnt_ref[e], functools.partial(fill, wait=True), 0)

    def wait_slot(slot):
        pltpu.make_async_copy(comp_ref.at[slot], xs_hbm.at[pl.ds(0, cr), :], sems.at[slot]).wait()

    del pput_ref

    @pl.when(i > 0)
    def _():
        wait_slot(1 - s)

    copies(s, put_ref, False)

    @pl.when(i == pl.num_programs(0) - 1)
    def _():
        wait_slot(s)


def _group_table(shift, nt, ngrp):
    def idx(i, *_):
        return (jnp.clip(i + shift, 0, nt - 1), 0, 0)

    return pl.BlockSpec((1, 1, ngrp), idx, memory_space=pltpu.SMEM)


def _moe_compact(lay, h, mod, g, lpos, put_rows, fill_base, fill_cnt, n_rows, n_blocks):
    tm = lay.tm
    nt, ngrp = n_rows // tm, _comp_rows(tm) // RUN_ALIGN
    grid_spec = pltpu.PrefetchScalarGridSpec(
        num_scalar_prefetch=2,
        grid=(nt,),
        in_specs=[_group_table(0, nt, ngrp), _group_table(-1, nt, ngrp)]
                 + [pl.BlockSpec((tm, D), lambda i, *_: (i, 0)),
                    pl.BlockSpec((1, 1, 6 * D), lambda i, *_: (lay.mod_idx(i), 0, 0)),
                    pl.BlockSpec((1, D), lambda i, *_: (0, 0)),
                    pl.BlockSpec((TOP_K, tm), lambda i, *_: (0, i))],
        out_specs=pl.BlockSpec(memory_space=pl.ANY),
        scratch_shapes=[pltpu.VMEM((2, _comp_rows(tm), D), F32), pltpu.VMEM((RUN_ALIGN, D), F32),
                        pltpu.SemaphoreType.DMA((2,))],
    )
    return pl.pallas_call(
        _compact_kernel,
        grid_spec=grid_spec,
        out_shape=jax.ShapeDtypeStruct((n_blocks * MOE_BLOCK + 2 * _comp_rows(tm), D), F32),
        compiler_params=_cparams(("arbitrary",)),
        name="moe_compact",
    )(fill_base, fill_cnt, put_rows, put_rows, h, mod, g.reshape(1, D), lpos)


def _expert_kernel(be_ref, nu_ref, x_ref, w1_ref, b1_ref, w2_ref, b2_ref, o_ref, w1b_ref, w2b_ref):
    i = pl.program_id(0)

    @pl.when((i == 0) | (be_ref[i] != be_ref[jnp.maximum(i - 1, 0)]))
    def _():
        w1b_ref[...] = w1_ref[0, 0].astype(BF16)
        w2b_ref[...] = w2_ref[0, 0].astype(BF16)

    @pl.when(i < nu_ref[0])
    def _():
        hcat = jnp.dot(x_ref[...].astype(BF16), w1b_ref[...], preferred_element_type=F32) + b1_ref[0]
        h_glu = jnp.minimum(hcat[:, :D_FF], SWIGLU_LIMIT)
        h_lin = jnp.clip(hcat[:, D_FF:], -SWIGLU_LIMIT, SWIGLU_LIMIT)
        act = h_glu * jax.nn.sigmoid(SWIGLU_ALPHA * h_glu) * (h_lin + 1.0)
        o_ref[...] = jnp.dot(act.astype(BF16), w2b_ref[...], preferred_element_type=F32) + b2_ref[0]

    @pl.when(i >= nu_ref[0])
    def _():
        o_ref[...] = jnp.zeros_like(o_ref)


def _moe_experts(xs, block_expert, n_used, n_blocks, layer, w1_all, b1, w2_all, b2):
    def xrow(i, be, nu):
        return (jnp.minimum(i, nu[0] - 1), 0)

    grid_spec = pltpu.PrefetchScalarGridSpec(
        num_scalar_prefetch=2,
        grid=(n_blocks,),
        in_specs=[pl.BlockSpec((MOE_BLOCK, D), xrow),
                  pl.BlockSpec((1, 1, D, 2 * D_FF), lambda i, be, nu: (layer, be[i], 0, 0)),
                  pl.BlockSpec((1, 1, 2 * D_FF), lambda i, be, nu: (be[i], 0, 0)),
                  pl.BlockSpec((1, 1, D_FF, D), lambda i, be, nu: (layer, be[i], 0, 0)),
                  pl.BlockSpec((1, 1, D), lambda i, be, nu: (be[i], 0, 0))],
        out_specs=pl.BlockSpec((MOE_BLOCK, D), lambda i, be, nu: (i, 0)),
        scratch_shapes=[pltpu.VMEM((D, 2 * D_FF), BF16), pltpu.VMEM((D_FF, D), BF16)],
    )
    return pl.pallas_call(
        _expert_kernel,
        grid_spec=grid_spec,
        out_shape=jax.ShapeDtypeStruct((n_blocks * MOE_BLOCK, D), F32),
        compiler_params=_cparams(("arbitrary",)),
        name="moe_experts",
    )(block_expert, n_used, xs, w1_all, b1.reshape(N_EXPERTS, 1, 2 * D_FF), w2_all, b2.reshape(N_EXPERTS, 1, D))


def _combine_kernel(get_ref, nget_ref, h_ref, gate_ref, lpos_ref, mod_ref, y_hbm, *rest, final):
    fg_ref = rest[0] if final else None
    o_ref, comp_ref, sems = rest[-3:]
    i = pl.program_id(0)
    s = i % 2
    cr, tm = comp_ref.shape[1], h_ref.shape[0]

    def copies(slot, rows_ref, wait):
        src = lambda j: _rows_at(y_hbm, rows_ref[0, 0, j])
        dst = lambda j: _rows_at(comp_ref.at[slot], j * RUN_ALIGN)
        _group_copies(cr // RUN_ALIGN, src, dst, sems.at[slot], wait)

    @pl.when(i == 0)
    def _():
        copies(0, get_ref, False)

    @pl.when(i + 1 < pl.num_programs(0))
    def _():
        copies(1 - s, nget_ref, False)

    q = lax.broadcasted_iota(jnp.int32, (tm, cr), 1)
    gates, lp = gate_ref[...], lpos_ref[...]
    pg = jnp.zeros((tm, cr), F32)
    for k in range(TOP_K):
        pg = jnp.where(lp[:, k:k + 1] == q, gates[:, k:k + 1], pg)
    pltpu.make_async_copy(y_hbm.at[pl.ds(0, cr), :], comp_ref.at[s], sems.at[s]).wait()
    y = jnp.dot(pg.astype(BF16), comp_ref[s].astype(BF16), preferred_element_type=F32)
    h_new = h_ref[...] + mod_ref[0][:, 5 * D:6 * D] * y
    if final:
        ms = jnp.mean(h_new * h_new, axis=-1, keepdims=True)
        h_new = h_new * lax.rsqrt(ms + NORM_EPS) * fg_ref[...]
    o_ref[...] = h_new


def _moe_combine(lay, h, mod, yb, gates_t, lpos_t, get_rows, n_rows, final_g=None):
    tm = lay.tm
    nt, ngrp = n_rows // tm, _comp_rows(tm) // RUN_ALIGN
    final = final_g is not None
    extra_specs = [pl.BlockSpec((1, D), lambda i, *_: (0, 0))] if final else []
    extra_args = [final_g.reshape(1, D)] if final else []
    grid_spec = pltpu.PrefetchScalarGridSpec(
        num_scalar_prefetch=0,
        grid=(nt,),
        in_specs=[_group_table(0, nt, ngrp), _group_table(1, nt, ngrp),
                  pl.BlockSpec((tm, D), lambda i, *_: (i, 0)),
                  pl.BlockSpec((tm, TOP_K), lambda i, *_: (i, 0)),
                  pl.BlockSpec((tm, TOP_K), lambda i, *_: (i, 0)),
                  pl.BlockSpec((1, 1, 6 * D), lambda i, *_: (lay.mod_idx(i), 0, 0)),
                  pl.BlockSpec(memory_space=pl.ANY)] + extra_specs,
        out_specs=pl.BlockSpec((tm, D), lambda i, *_: (i, 0)),
        scratch_shapes=[pltpu.VMEM((2, _comp_rows(tm), D), F32), pltpu.SemaphoreType.DMA((2,))],
    )
    return pl.pallas_call(
        functools.partial(_combine_kernel, final=final),
        grid_spec=grid_spec,
        out_shape=jax.ShapeDtypeStruct((n_rows, D), F32),
        compiler_params=_cparams(("arbitrary",)),
        name="moe_combine",
    )(get_rows, get_rows, h, gates_t, lpos_t, mod, yb, *extra_args)


def _moe_layer(lay, h, mod, g, w_router, b_router, layer, w1_all, b1, w2_all, b2, n_rows, final_g=None):
    gates, lpos, cnt = _moe_router(lay, h, mod, g, w_router, b_router, n_rows)
    put_rows, get_rows, block_expert, fill_base, fill_cnt, n_used, n_blocks = _moe_plan(cnt[:, :, 0], n_rows, lay.tm)
    xs = _moe_compact(lay, h, mod, g, lpos, put_rows, fill_base, fill_cnt, n_rows, n_blocks)
    yb = _moe_experts(xs, block_expert, n_used, n_blocks, layer, w1_all, b1, w2_all, b2)
    return _moe_combine(lay, h, mod, yb, gates.T, lpos.T, get_rows, n_rows, final_g)


def kernel(x, c, ctx, c_ctx, ada_w, ada_b, norm_g, attn_w_qkv, attn_b_qkv, attn_w_o, attn_b_o, attn_sinks, pool_w, pool_b, pool_scale, rwkv_mix, rwkv_w_rkv, rwkv_w0, rwkv_w1, rwkv_w2, rwkv_a0, rwkv_a1, rwkv_a2, rwkv_g1, rwkv_g2, rwkv_k_k, rwkv_k_a, rwkv_r_k, rwkv_ln_w, rwkv_ln_b, rwkv_w_o, moe_w_router, moe_b_router, moe_w1, moe_b1, moe_w2, moe_b2, final_g):
    B, T, _ = x.shape
    L = ctx.shape[1]
    depth = ada_w.shape[0]
    lay = _Layout(B, T, L)
    h = jnp.concatenate([x.reshape(B * T, D), ctx.reshape(B * L, D)], axis=0)

    n_c = B + 1
    n_c_pad = -(-n_c // 8) * 8
    c_all = jnp.concatenate([c, c_ctx[None, :], jnp.zeros((n_c_pad - n_c, D), F32)], axis=0)
    mods = _ada_mods(c_all, ada_w, ada_b)
    cos, sin = _rope_tables(T, L)

    for layer in range(depth):
        last = layer == depth - 1
        kind, j = layer % 3, layer // 3
        mod = mods[layer].reshape(n_c_pad, 1, 6 * D)
        n_rows = lay.rows_x if last else lay.N
        if kind == 0:
            q, k, v = _attn_qkv(lay, h, mod, norm_g[layer, 0], attn_w_qkv[j], attn_b_qkv[j], cos, sin)
            o = _attention(lay, q, k, v, attn_sinks[j], not last)
            h = _linear_residual(lay, h, o, mod, attn_w_o[j], attn_b_o[j], 2, n_rows)
        elif kind == 1:
            h = _pool_mixer(lay, h, mod, norm_g[layer, 0], pool_w[j], pool_b[j], pool_scale[j], n_rows)
        else:
            p = dict(mix=rwkv_mix[j], w_rkv=rwkv_w_rkv[j], w0=rwkv_w0[j], w1=rwkv_w1[j], w2=rwkv_w2[j],
                     a0=rwkv_a0[j], a1=rwkv_a1[j], a2=rwkv_a2[j], g1=rwkv_g1[j], g2=rwkv_g2[j],
                     k_k=rwkv_k_k[j], k_a=rwkv_k_a[j], r_k=rwkv_r_k[j])
            r, v, kn, g, bonus, lw0, k0, b0, lw1, k1, b1 = _rwkv_proj(lay, h, mod, norm_g[layer, 0], p)
            yf, yb = _rwkv_scan(lay, r, v, kn, lw0, k0, b0, lw1, k1, b1)
            h = _rwkv_out(lay, h, yf, yb, bonus, g, mod, rwkv_ln_w[j], rwkv_ln_b[j], rwkv_w_o[j])
        h = _moe_layer(lay, h, mod, norm_g[layer, 1], moe_w_router[layer], moe_b_router[layer],
                       layer, moe_w1, moe_b1[layer], moe_w2, moe_b2[layer], n_rows,
                       final_g if last else None)
    return h.reshape(B, T, D)
```

```python
import functools

import jax
import jax.numpy as jnp
import numpy as np
from jax import lax
from jax.experimental import pallas as pl
from jax.experimental.pallas import tpu as pltpu

F32 = jnp.float32
BF16 = jnp.bfloat16

D = 1024
NORM_EPS = 1e-5
GRID_W = 64
HEAD_DIM = 64
N_HEADS = 16
N_KV = 4
Q_DIM = 1024
KV_DIM = 256
QK_DIM = Q_DIM + KV_DIM
KVD_DIM = 2 * KV_DIM
QKD_DIM = Q_DIM + KVD_DIM
WINDOW = 128
QB = 128
ROPE_BASE = 10000.0
POOL_SIZES = (2, 4, 8, 16)
POOL_GW = 256
HALO = 8
GN_EPS = 64e-5
N_EXPERTS = 32
TOP_K = 4
D_FF = 1024
SWIGLU_LIMIT = 7.0
SWIGLU_ALPHA = 1.702
MOE_BLOCK = 512
RW_CHUNK = 64
RW_GROUP = 256
VMEM_LIMIT = 56 * 1024 * 1024


def _cparams(sem):
    return pltpu.CompilerParams(dimension_semantics=sem, vmem_limit_bytes=VMEM_LIMIT)


def _norm_mod(h, g, shift, scale):
    ms = jnp.mean(h * h, axis=-1, keepdims=True)
    y = h * lax.rsqrt(ms + NORM_EPS) * g
    return y * (1.0 + scale) + shift


class _Layout:
    def __init__(self, B, T, L):
        self.B, self.T, self.L = B, T, L
        self.tm = 256 if L % 256 == 0 else 128
        self.rows_x, self.rows_c = B * T, B * L
        self.N = self.rows_x + self.rows_c
        self.tx, self.tc = T // self.tm, L // self.tm
        self.n_xt = self.rows_x // self.tm
        self.n_t = self.N // self.tm

    def mod_idx(self, i):
        return jnp.where(i < self.n_xt, i // self.tx, self.B)

    def mod_spec(self):
        return pl.BlockSpec((1, 1, 6 * D), lambda i: (self.mod_idx(i), 0, 0))

    def row_spec(self, width=D):
        return pl.BlockSpec((self.tm, width), lambda i: (i, 0))

    def seq_tile(self, i):
        j = jnp.where(i < self.n_xt, i % self.tx, (i - self.n_xt) % self.tc)
        n = jnp.where(i < self.n_xt, self.tx, self.tc)
        return j, n

    def halo_specs(self):
        hb = self.tm // HALO
        last = self.N // HALO - 1
        prev = pl.BlockSpec((HALO, D), lambda i: (jnp.maximum(i * hb - 1, 0), 0))
        nxt = pl.BlockSpec((HALO, D), lambda i: (jnp.minimum((i + 1) * hb, last), 0))
        return prev, nxt


def _const_spec(shape):
    nd = len(shape)
    return pl.BlockSpec(shape, lambda *_: (0,) * nd)


def _ada_kernel(c_ref, w_ref, b_ref, o_ref):
    c = c_ref[...]
    s = c * jax.nn.sigmoid(c)
    o_ref[0] = jnp.dot(s, w_ref[0], preferred_element_type=F32, precision=lax.Precision.HIGHEST) + b_ref[0]


def _ada_mods(c_all, ada_w, ada_b):
    depth = ada_w.shape[0]
    R = c_all.shape[0]
    nt = 1536
    return pl.pallas_call(
        _ada_kernel,
        grid=(depth, 6 * D // nt),
        in_specs=[pl.BlockSpec((R, D), lambda l, j: (0, 0)),
                  pl.BlockSpec((1, D, nt), lambda l, j: (l, 0, j)),
                  pl.BlockSpec((1, 1, nt), lambda l, j: (l, 0, j))],
        out_specs=pl.BlockSpec((1, R, nt), lambda l, j: (l, 0, j)),
        out_shape=jax.ShapeDtypeStruct((depth, R, 6 * D), F32),
        compiler_params=_cparams(("parallel", "parallel")),
        name="ada_mod",
    )(c_all, ada_w, ada_b.reshape(depth, 1, 6 * D))


def _qkv_kernel(h_ref, mod_ref, g_ref, w_ref, b_ref, cos_ref, sin_ref, q_ref, k_ref, v_ref):
    m = mod_ref[0]
    a = _norm_mod(h_ref[...], g_ref[...], m[:, 0:D], m[:, D:2 * D])
    qkv = jnp.dot(a.astype(BF16), w_ref[...], preferred_element_type=F32) + b_ref[...]
    qk = qkv[:, :QK_DIM]
    lane = lax.broadcasted_iota(jnp.int32, (1, QK_DIM), 1)
    first = (lane % 32) < 16
    rot = jnp.where(first, -pltpu.roll(qk, QK_DIM - 16, 1), pltpu.roll(qk, 16, 1))
    qk = qk * cos_ref[...] + rot * sin_ref[...]
    q_ref[...] = (qk[:, :Q_DIM] * (HEAD_DIM ** -0.5)).astype(BF16)
    k_ref[...] = _dup_heads(qk[:, Q_DIM:]).astype(BF16)
    v_ref[...] = _dup_heads(qkv[:, QK_DIM:]).astype(BF16)


def _dup_heads(x):
    heads = [x[:, g * HEAD_DIM:(g + 1) * HEAD_DIM] for g in range(N_KV)]
    return jnp.concatenate([hh for hd in heads for hh in (hd, hd)], axis=1)


def _rope_tables(T, L):
    rows = T // GRID_W
    n_freq = HEAD_DIM // 4
    inv = ROPE_BASE ** (-jnp.arange(n_freq, dtype=F32) / n_freq)
    row_ang = jnp.arange(rows, dtype=F32)[:, None] * inv
    col_ang = jnp.arange(GRID_W, dtype=F32)[:, None] * inv
    ang_r = jnp.broadcast_to(row_ang[:, None, :], (rows, GRID_W, n_freq)).reshape(T, n_freq)
    ang_c = jnp.broadcast_to(col_ang[None, :, :], (rows, GRID_W, n_freq)).reshape(T, n_freq)
    ang = jnp.concatenate([ang_r, ang_r, ang_c, ang_c], axis=-1)
    cos = jnp.concatenate([jnp.cos(ang), jnp.ones((L, HEAD_DIM), F32)], axis=0)
    sin = jnp.concatenate([jnp.sin(ang), jnp.zeros((L, HEAD_DIM), F32)], axis=0)
    reps = QK_DIM // HEAD_DIM
    return jnp.tile(cos, (1, reps)), jnp.tile(sin, (1, reps))


def _attn_qkv(lay, h, mod, g, w_qkv, b_qkv, cos, sin):
    tm = lay.tm

    def tab_idx(i):
        return (jnp.where(i < lay.n_xt, i % lay.tx, lay.tx + (i - lay.n_xt) % lay.tc), 0)

    w = w_qkv.astype(BF16)
    b = b_qkv.reshape(1, -1)
    width = Q_DIM + 2 * KV_DIM
    return pl.pallas_call(
        _qkv_kernel,
        grid=(lay.n_t,),
        in_specs=[lay.row_spec(), lay.mod_spec(), _const_spec((1, D)),
                  _const_spec((D, width)), _const_spec((1, width)),
                  pl.BlockSpec((tm, QK_DIM), tab_idx), pl.BlockSpec((tm, QK_DIM), tab_idx)],
        out_specs=[lay.row_spec(Q_DIM), lay.row_spec(KVD_DIM), lay.row_spec(KVD_DIM)],
        out_shape=[jax.ShapeDtypeStruct((lay.N, Q_DIM), BF16),
                   jax.ShapeDtypeStruct((lay.N, KVD_DIM), BF16),
                   jax.ShapeDtypeStruct((lay.N, KVD_DIM), BF16)],
        compiler_params=_cparams(("parallel",)),
        name="attn_qkv",
    )(h, mod, g.reshape(1, D), w, b, cos, sin)


def _attn_heads(q, kk, vv, mask, sink_ref, o_ref):
    R = q.shape[0]
    G = N_HEADS // N_KV
    TILE = 2 * HEAD_DIM
    lo = lax.broadcasted_iota(jnp.int32, (1, TILE), 1) < HEAD_DIM
    order = (0, 2, 1, 3)
    scores = []
    for g in range(N_KV):
        pieces = []
        for j in order:
            hq = g * G + j
            qt = q[:, (hq // 2) * TILE:(hq // 2 + 1) * TILE]
            pieces.append(jnp.where(lo if hq % 2 == 0 else ~lo, qt, jnp.zeros_like(qt)))
        qs = jnp.concatenate(pieces, axis=0)
        scores.append(lax.dot_general(qs, kk[:, g * TILE:(g + 1) * TILE], (((1,), (1,)), ((), ())),
                                      preferred_element_type=F32))
    probs, esinks = [], []
    for g in range(N_KV):
        s = scores[g]
        if mask is not None:
            s = jnp.where(jnp.concatenate([mask] * G, axis=0), s, -jnp.inf)
        sink = jnp.concatenate([jnp.full((R, 1), sink_ref[g * G + j], F32) for j in order], axis=0)
        m = jnp.maximum(jnp.max(s, axis=-1, keepdims=True), sink)
        probs.append(jnp.exp((s - m).astype(BF16)))
        esinks.append(jnp.exp(sink - m))
    for g in range(N_KV):
        vd = vv[:, g * TILE:(g + 1) * TILE]
        one = jnp.ones_like(vd)
        p, es = probs[g], esinks[g]
        half = (G // 2) * R
        oe = jnp.dot(p[:half], jnp.where(lo, vd, one), preferred_element_type=F32)
        oo = jnp.dot(p[half:], jnp.where(lo, one, vd), preferred_element_type=F32)
        re = oe / (pltpu.roll(oe, HEAD_DIM, 1) + es[:half])
        ro = oo / (pltpu.roll(oo, HEAD_DIM, 1) + es[half:])
        for t in range(G // 2):
            tile = jnp.where(lo, re[t * R:(t + 1) * R], ro[t * R:(t + 1) * R])
            c = g * (G // 2) + t
            o_ref[:, c * TILE:(c + 1) * TILE] = tile.astype(o_ref.dtype)


def _attn_kernel(sink_ref, q_ref, kp_ref, kc_ref, kn_ref, vp_ref, vc_ref, vn_ref, kx_ref, vx_ref, o_ref, *, T, L):
    i = pl.program_id(1)
    nqb = T // QB

    @pl.when(i < nqb)
    def _():
        kk = jnp.concatenate([kx_ref[...], kp_ref[...], kc_ref[...], kn_ref[...]], axis=0)
        vv = jnp.concatenate([vx_ref[...], vp_ref[...], vc_ref[...], vn_ref[...]], axis=0)
        S = L + 3 * QB
        col = lax.broadcasted_iota(jnp.int32, (QB, S), 1)
        row = lax.broadcasted_iota(jnp.int32, (QB, S), 0)
        rel = col - L - QB - row
        kpos = i * QB - QB + (col - L)
        local_ok = (jnp.abs(rel) <= WINDOW) & (kpos >= 0) & (kpos < T)
        mask = (col < L) | local_ok
        _attn_heads(q_ref[...], kk, vv, mask, sink_ref, o_ref)

    @pl.when(i >= nqb)
    def _():
        _attn_heads(q_ref[...], kx_ref[...], vx_ref[...], None, sink_ref, o_ref)


def _attention(lay, q, k, v, sinks, need_ctx):
    B, T, L = lay.B, lay.T, lay.L
    nqb, ncb = T // QB, L // QB
    cq = lay.rows_x // QB
    cb = lay.rows_x // L
    smem = pl.BlockSpec(memory_space=pltpu.SMEM)

    def cur(b, i):
        return (jnp.where(i < nqb, b * nqb + i, cq + b * ncb + (i - nqb)), 0)

    def prev(b, i):
        return (b * nqb + jnp.clip(i - 1, 0, nqb - 1), 0)

    def mid(b, i):
        return (b * nqb + jnp.minimum(i, nqb - 1), 0)

    def nxt(b, i):
        return (b * nqb + jnp.minimum(i + 1, nqb - 1), 0)

    def ctxb(b, i):
        return (cb + b, 0)

    kv_specs = [pl.BlockSpec((QB, KVD_DIM), prev), pl.BlockSpec((QB, KVD_DIM), mid), pl.BlockSpec((QB, KVD_DIM), nxt)]
    n_out = lay.N if need_ctx else lay.rows_x
    return pl.pallas_call(
        functools.partial(_attn_kernel, T=T, L=L),
        grid=(B, nqb + (ncb if need_ctx else 0)),
        in_specs=[smem, pl.BlockSpec((QB, Q_DIM), cur)] + kv_specs + kv_specs
                 + [pl.BlockSpec((L, KVD_DIM), ctxb), pl.BlockSpec((L, KVD_DIM), ctxb)],
        out_specs=pl.BlockSpec((QB, Q_DIM), cur),
        out_shape=jax.ShapeDtypeStruct((n_out, Q_DIM), BF16),
        compiler_params=_cparams(("parallel", "arbitrary")),
        name="attention",
    )(sinks, q, k, k, k, v, v, v, k, v)


def _linres_kernel(h_ref, x_ref, mod_ref, w_ref, b_ref, o_ref, *, gate_slot):
    gate = mod_ref[0][:, gate_slot * D:(gate_slot + 1) * D]
    y = jnp.dot(x_ref[...], w_ref[...], preferred_element_type=F32) + b_ref[...]
    o_ref[...] = h_ref[...] + gate * y


def _linear_residual(lay, h, x, mod, w, b, gate_slot, n_rows):
    return pl.pallas_call(
        functools.partial(_linres_kernel, gate_slot=gate_slot),
        grid=(n_rows // lay.tm,),
        in_specs=[lay.row_spec(), lay.row_spec(x.shape[1]), lay.mod_spec(),
                  _const_spec(w.shape), _const_spec((1, D))],
        out_specs=lay.row_spec(),
        out_shape=jax.ShapeDtypeStruct((n_rows, D), F32),
        compiler_params=_cparams(("parallel",)),
        name="linear_residual",
    )(h, x, mod, w.astype(BF16), b.reshape(1, D))


def _fill_ext(ext_ref, h_ref, hp_ref, hn_ref, g, shift, scale, j, n, tm):
    a = _norm_mod(h_ref[...], g, shift, scale)
    ap = _norm_mod(hp_ref[...], g, shift, scale)
    an = _norm_mod(hn_ref[...], g, shift, scale)
    ext_ref[0:HALO, :] = jnp.where(j == 0, 0.0, ap)
    ext_ref[HALO:HALO + tm, :] = a
    ext_ref[HALO + tm:2 * HALO + tm, :] = jnp.where(j == n - 1, 0.0, an)
    return a


def _pool_kernel(h_ref, hp_ref, hn_ref, mod_ref, g_ref, w_ref, b_ref, ls_ref, o_ref, ext_ref, *, lay):
    tm = lay.tm
    i = pl.program_id(0)
    j, n = lay.seq_tile(i)
    m = mod_ref[0]
    a = _fill_ext(ext_ref, h_ref, hp_ref, hn_ref, g_ref[...], m[:, 0:D], m[:, D:2 * D], j, n, tm)
    t = j * tm + lax.broadcasted_iota(jnp.int32, (tm, 1), 0)
    n_tok = n * tm
    outs = []
    for gi, size in enumerate(POOL_SIZES):
        sl = slice(gi * POOL_GW, (gi + 1) * POOL_GW)
        acc = ext_ref[HALO - size // 2:HALO - size // 2 + tm, sl]
        for o in range(-size // 2 + 1, size - size // 2):
            acc = acc + ext_ref[HALO + o:HALO + o + tm, sl]
        lo = jnp.maximum(t - size // 2, 0)
        hi = jnp.minimum(t + size - size // 2, n_tok)
        d = acc / (hi - lo).astype(F32) - a[:, sl]
        outs.append(jnp.dot(d.astype(BF16), w_ref[gi], preferred_element_type=F32) + b_ref[gi])
    y = jnp.concatenate(outs, axis=-1) * ls_ref[...]
    o_ref[...] = h_ref[...] + m[:, 2 * D:3 * D] * y


def _pool_mixer(lay, h, mod, g, w, b, ls, n_rows):
    prev, nxt = lay.halo_specs()
    return pl.pallas_call(
        functools.partial(_pool_kernel, lay=lay),
        grid=(n_rows // lay.tm,),
        in_specs=[lay.row_spec(), prev, nxt, lay.mod_spec(), _const_spec((1, D)),
                  _const_spec(w.shape), _const_spec((4, 1, POOL_GW)), _const_spec((1, D))],
        out_specs=lay.row_spec(),
        out_shape=jax.ShapeDtypeStruct((n_rows, D), F32),
        scratch_shapes=[pltpu.VMEM((lay.tm + 2 * HALO, D), F32)],
        compiler_params=_cparams(("parallel",)),
        name="pool_mixer",
    )(h, h, h, mod, g.reshape(1, D), w.astype(BF16), b.reshape(4, 1, POOL_GW), ls.reshape(1, D))


def _seg_sum(x, bd, split=True):
    outs = []
    for q in range(D // RW_GROUP):
        xg = x[:, q * RW_GROUP:(q + 1) * RW_GROUP]
        hi = xg.astype(BF16)
        acc = jnp.dot(hi, bd, preferred_element_type=F32)
        if split:
            acc = acc + jnp.dot((xg - hi.astype(F32)).astype(BF16), bd, preferred_element_type=F32)
        outs.append(acc)
    return jnp.concatenate(outs, axis=-1)


def _rwkv_proj_kernel(h_ref, hp_ref, hn_ref, mod_ref, g_ref, mix_ref, wr_ref, wk_ref, wv_ref, g1_ref, g2_ref,
                      w1_ref, w2_ref, w0_ref, a1_ref, a2_ref, a0_ref, kk_ref, ka_ref, rk_ref, bd_ref,
                      r_out, v_out, kn_out, g_out, bonus_out, lw0_out, k0_out, b0_out, lw1_out, k1_out, b1_out,
                      ext_ref, *, lay):
    tm = lay.tm
    i = pl.program_id(0)
    j, n = lay.seq_tile(i)
    m = mod_ref[0]
    a = _fill_ext(ext_ref, h_ref, hp_ref, hn_ref, g_ref[...], m[:, 0:D], m[:, D:2 * D], j, n, tm)
    prev = ext_ref[HALO - 1:HALO - 1 + tm, :]
    nxt = ext_ref[HALO + 1:HALO + 1 + tm, :]
    xx = 0.5 * (prev + nxt) - a
    x_r, x_w, x_k, x_v, x_a, x_g = [(a + xx * mix_ref[q:q + 1, :]).astype(BF16) for q in range(6)]
    bd = bd_ref[...]
    r = jnp.dot(x_r, wr_ref[...], preferred_element_type=F32)
    k = jnp.dot(x_k, wk_ref[...], preferred_element_type=F32)
    v = jnp.dot(x_v, wv_ref[...], preferred_element_type=F32)
    gl = jnp.dot(x_g, g1_ref[...], preferred_element_type=F32)
    g = jnp.dot(jax.nn.sigmoid(gl).astype(BF16), g2_ref[...], preferred_element_type=F32)
    kk = k * kk_ref[...]
    kn = kk / jnp.maximum(jnp.sqrt(_seg_sum(kk * kk, bd)), 1e-12)
    tw = jnp.tanh(jnp.dot(x_w, w1_ref[...], preferred_element_type=F32))
    aa = jnp.dot(x_a, a1_ref[...], preferred_element_type=F32)
    lane = lax.broadcasted_iota(jnp.int32, (1, tw.shape[1]), 1)
    r_out[...] = r.astype(r_out.dtype)
    v_out[...] = v.astype(v_out.dtype)
    kn_out[...] = kn.astype(kn_out.dtype)
    g_out[...] = g.astype(g_out.dtype)
    rk = rk_ref[...]
    bonus = jnp.zeros_like(v)
    outs = ((lw0_out, k0_out, b0_out), (lw1_out, k1_out, b1_out))
    half = tw.shape[1] // 2
    for d in range(2):
        sel = (lane >= d * half) & (lane < (d + 1) * half)
        wl = w0_ref[d:d + 1, :] + jnp.dot(jnp.where(sel, tw, 0.0).astype(BF16), w2_ref[...], preferred_element_type=F32)
        z = -wl
        w_log = -(jnp.maximum(z, 0.0) + jnp.log(1.0 + jnp.exp(-jnp.abs(z)))) - 0.5
        asig = jax.nn.sigmoid(a0_ref[d:d + 1, :] + jnp.dot(jnp.where(sel, aa, 0.0).astype(BF16), a2_ref[...],
                                                           preferred_element_type=F32))
        k_d = k * (1.0 + (asig - 1.0) * ka_ref[...])
        lw_o, k_o, b_o = outs[d]
        lw_o[...] = -jnp.exp(w_log)
        k_o[...] = k_d.astype(k_o.dtype)
        b_o[...] = (kn * asig).astype(b_o.dtype)
        bonus = bonus + _seg_sum(r * k_d * rk, bd, split=False) * v
    bonus_out[...] = bonus.astype(bonus_out.dtype)


def _block_ones():
    idx = np.arange(RW_GROUP) // 64
    return jnp.asarray(idx[:, None] == idx[None, :], dtype=BF16)


def _rwkv_proj(lay, h, mod, g, p):
    prev, nxt = lay.halo_specs()
    cat = lambda w: jnp.concatenate([w[0], w[1]], axis=-1).astype(BF16)
    stack = lambda w: jnp.concatenate([w[0], w[1]], axis=0).astype(BF16)
    args = [h, h, h, mod, g.reshape(1, D), p['mix'],
            p['w_rkv'][0].astype(BF16), p['w_rkv'][1].astype(BF16), p['w_rkv'][2].astype(BF16),
            p['g1'].astype(BF16), p['g2'].astype(BF16),
            cat(p['w1']), stack(p['w2']), p['w0'], cat(p['a1']), stack(p['a2']), p['a0'],
            p['k_k'].reshape(1, D), p['k_a'].reshape(1, D), p['r_k'].reshape(1, D), _block_ones()]
    in_specs = [lay.row_spec(), prev, nxt, lay.mod_spec()] + [_const_spec(x.shape) for x in args[4:]]
    dtypes = [BF16] * 5 + [F32, BF16, BF16] * 2
    return pl.pallas_call(
        functools.partial(_rwkv_proj_kernel, lay=lay),
        grid=(lay.n_t,),
        in_specs=in_specs,
        out_specs=[lay.row_spec()] * len(dtypes),
        out_shape=[jax.ShapeDtypeStruct((lay.N, D), dt) for dt in dtypes],
        scratch_shapes=[pltpu.VMEM((lay.tm + 2 * HALO, D), F32)],
        compiler_params=_cparams(("parallel",)),
        name="rwkv_proj",
    )(*args)


def _rwkv_scan_kernel(rf, vf, nf, lwf, kf, bf, rb, vb, nb, lwb, kb, bb, yf_out, yb_out, z_ref):
    C, G = RW_CHUNK, RW_GROUP
    c = pl.program_id(1)

    @pl.when(c == 0)
    def _():
        z_ref[...] = jnp.zeros_like(z_ref)

    trow = lax.broadcasted_iota(jnp.int32, (C, C), 0)
    tcol = lax.broadcasted_iota(jnp.int32, (C, C), 1)
    ti = lax.broadcasted_iota(jnp.int32, (C, G), 0)
    ii = lax.broadcasted_iota(jnp.int32, (C, G), 1) % C
    brow = lax.broadcasted_iota(jnp.int32, (G, G), 0)
    bcol = lax.broadcasted_iota(jnp.int32, (G, G), 1)
    m_bd_t = (brow // C) == (bcol // 64)
    m_bd_f = (brow // 64) == (bcol // 64)
    eye_g = brow == bcol

    def bd(x):
        return jnp.where(m_bd_t, jnp.concatenate([x] * (G // C), axis=0), 0.0).astype(BF16)

    def mm(a, b):
        return jnp.dot(a, b, preferred_element_type=F32)

    dirs = ((rf, vf, nf, lwf, kf, bf, yf_out, False), (rb, vb, nb, lwb, kb, bb, yb_out, True))
    ch = []
    for d, (r_ref, v_ref, n_ref, lw_ref, k_ref, b_ref, y_out, rev) in enumerate(dirs):
        tri = (tcol >= trow) if rev else (tcol <= trow)
        strict = (ii > ti) if rev else (ii < ti)
        incl = (ii >= ti) if rev else (ii <= ti)
        lw = lw_ref[...]
        lc = jnp.dot(tri.astype(F32), lw, preferred_element_type=F32, precision=lax.Precision.HIGHEST)
        ltot = lc[0:1, :] if rev else lc[C - 1:C, :]
        e_lc = jnp.exp(lc)
        ie_lc = jnp.exp(-lc)
        e_le = jnp.exp(lc - lw)
        e_bar = jnp.exp(ltot - lc)
        e_tot = jnp.exp(ltot)
        kvec, bvec = k_ref[...], b_ref[...]
        a_t = -n_ref[...] * e_le
        r_t = r_ref[...] * e_lc
        b_t = bvec * ie_lc
        k_t = kvec * ie_lc
        b_b = bvec * e_bar
        k_b = kvec * e_bar
        vall = v_ref[...]
        for g in range(D // G):
            sl = slice(g * G, (g + 1) * G)
            ch.append(dict(
                d=d, g=g, sl=sl, y_out=y_out, strict=strict, incl=incl, v=vall[:, sl], e_tot=e_tot[:, sl],
                ar=jnp.concatenate([a_t[:, sl], r_t[:, sl]], axis=0).astype(BF16),
                bk=jnp.concatenate([bd(b_t[:, sl]), bd(k_t[:, sl])], axis=0),
                lhs_t=jnp.concatenate([b_b[:, sl], k_b[:, sl]], axis=0).astype(BF16)))

    for q in ch:
        aa = lax.dot_general(q['ar'], q['bk'], (((1,), (1,)), ((), ())), preferred_element_type=F32)
        q['a_ab'] = jnp.where(q['strict'], aa[:C, :G], 0.0)
        q['a_ak'] = jnp.where(q['strict'], aa[:C, G:], 0.0)
        q['m_rb'] = jnp.where(q['incl'], aa[C:, :G], 0.0).astype(BF16)
        q['m_rk'] = jnp.where(q['incl'], aa[C:, G:], 0.0)
    for q in ch:
        q['x'] = jnp.where(ii == ti, 1.0, 0.0) + q['a_ab']
        q['p'] = mm(q['a_ab'].astype(BF16), bd(q['a_ab']))
    for q in ch:
        q['wv'] = mm(jnp.concatenate([q['a_ak'], q['m_rk']], axis=0).astype(BF16), bd(q['v']))
        q['z'] = z_ref[q['d'], q['g']]
        q['az'] = mm(q['ar'], q['z'].astype(BF16))
    n_sq = int(np.log2(C)) - 1
    for s in range(n_sq):
        last = s == n_sq - 1
        for q in ch:
            lhs = q['x'] if last else jnp.concatenate([q['x'], q['p']], axis=0)
            xp = mm(lhs.astype(BF16), bd(q['p']))
            q['x'] = q['x'] + xp[:C]
            if not last:
                q['p'] = xp[C:]
    for q in ch:
        q['u'] = mm(q['x'].astype(BF16), bd(q['az'][:C] + q['wv'][:C]))
    for q in ch:
        y = q['az'][C:] + q['wv'][C:] + mm(q['m_rb'], bd(q['u']))
        q['y_out'][:, q['sl']] = y
        rhs_t = jnp.concatenate([q['u'], q['v']], axis=0).astype(BF16)
        zu = lax.dot_general(q['lhs_t'], rhs_t, (((0,), (0,)), ((), ())), preferred_element_type=F32)
        decay = jnp.sum(jnp.where(eye_g, jnp.broadcast_to(q['e_tot'], (G, G)), 0.0), axis=1, keepdims=True)
        z_ref[q['d'], q['g']] = jnp.where(m_bd_f, decay * q['z'] + zu, 0.0)


def _rwkv_scan(lay, r, v, kn, lw0, k0, b0, lw1, k1, b1):
    B, T, L = lay.B, lay.T, lay.L
    C = RW_CHUNK
    cc, cx = L // C, T // C
    nc = cc + cx
    ctx0 = lay.rows_x // C

    def fwd(b, c):
        return (jnp.where(c < cc, ctx0 + b * cc + c, b * cx + c - cc), 0)

    def bwd(b, c):
        return (jnp.where(c < cc, ctx0 + b * cc + (cc - 1 - c), b * cx + (nc - 1 - c)), 0)

    sf, sb = pl.BlockSpec((C, D), fwd), pl.BlockSpec((C, D), bwd)
    return pl.pallas_call(
        _rwkv_scan_kernel,
        grid=(B, nc),
        in_specs=[sf] * 6 + [sb] * 6,
        out_specs=[sf, sb],
        out_shape=[jax.ShapeDtypeStruct((lay.N, D), F32)] * 2,
        scratch_shapes=[pltpu.VMEM((2, D // RW_GROUP, RW_GROUP, RW_GROUP), F32)],
        compiler_params=_cparams(("parallel", "arbitrary")),
        name="rwkv_scan",
    )(r, v, kn, lw0, k0, b0, r, v, kn, lw1, k1, b1)


def _rwkv_out_kernel(h_ref, yf_ref, yb_ref, bonus_ref, g_ref, mod_ref, lnw_ref, lnb_ref, wo_ref, bd_ref, o_ref):
    bd = bd_ref[...]
    y = yf_ref[...] + yb_ref[...]
    mu = _seg_sum(y, bd) * (1.0 / 64)
    yc = y - mu
    var = _seg_sum(yc * yc, bd) * (1.0 / 64)
    yn = yc * lax.rsqrt(var + GN_EPS) * lnw_ref[...] + lnb_ref[...]
    out = (yn + bonus_ref[...]) * g_ref[...]
    res = jnp.dot(out.astype(BF16), wo_ref[...], preferred_element_type=F32)
    o_ref[...] = h_ref[...] + mod_ref[0][:, 2 * D:3 * D] * res


def _rwkv_out(lay, h, yf, yb, bonus, g, mod, ln_w, ln_b, w_o):
    return pl.pallas_call(
        _rwkv_out_kernel,
        grid=(lay.n_t,),
        in_specs=[lay.row_spec()] * 5 + [lay.mod_spec(), _const_spec((1, D)), _const_spec((1, D)),
                                          _const_spec((D, D)), _const_spec((RW_GROUP, RW_GROUP))],
        out_specs=lay.row_spec(),
        out_shape=jax.ShapeDtypeStruct((lay.N, D), F32),
        compiler_params=_cparams(("parallel",)),
        name="rwkv_out",
    )(h, yf, yb, bonus, g, mod, ln_w.reshape(1, D), ln_b.reshape(1, D), w_o.astype(BF16), _block_ones())


RUN_ALIGN = 8
DP = D // 2
_HI_MASK = 0xFFFF0000


def _pack_rows(x):
    lo = pltpu.bitcast(x[:, :DP], jnp.uint32)
    hi = pltpu.bitcast(x[:, DP:], jnp.uint32)
    return (lo >> 16) | (hi & jnp.uint32(_HI_MASK))


def _unpack_rows(u):
    lo = pltpu.bitcast(u << 16, F32)
    hi = pltpu.bitcast(u & jnp.uint32(_HI_MASK), F32)
    return lo.astype(BF16), hi.astype(BF16)


def _router_kernel(h_ref, mod_ref, g_ref, wr_ref, br_ref, upper_ref, gate_ref, lpos_ref, cnt_ref):
    m = mod_ref[0]
    f = _norm_mod(h_ref[...], g_ref[...], m[:, 3 * D:4 * D], m[:, 4 * D:5 * D])
    def split(x):
        hi = x.astype(BF16)
        return hi, (x - hi.astype(F32)).astype(BF16)

    def nt(a, b):
        return lax.dot_general(a, b, (((1,), (1,)), ((), ())), preferred_element_type=F32)

    (w_hi, w_lo), (f_hi, f_lo) = split(wr_ref[...]), split(f)
    logits = nt(w_hi, f_hi) + (nt(w_hi, f_lo) + nt(w_lo, f_hi)) + br_ref[...]
    erow = lax.broadcasted_iota(jnp.int32, logits.shape, 0)
    tops, hots = [], []
    l = logits
    for _ in range(TOP_K):
        mx = jnp.max(l, axis=0, keepdims=True)
        ix = jnp.min(jnp.where(l == mx, erow, N_EXPERTS), axis=0, keepdims=True)
        tops.append(mx)
        hots.append(erow == ix)
        l = jnp.where(erow == ix, -jnp.inf, l)
    es = [jnp.exp(t - tops[0]) for t in tops]
    tot = es[0] + es[1] + es[2] + es[3]
    gate_ref[...] = jnp.concatenate([e / tot for e in es], axis=0)

    ohs = [jnp.where(hh, 1.0, 0.0) for hh in hots]
    cnts = [jnp.sum(oh, axis=1, keepdims=True) for oh in ohs]
    cnt = cnts[0] + cnts[1] + cnts[2] + cnts[3]
    run = jnp.ceil(cnt * (1.0 / RUN_ALIGN)) * RUN_ALIGN
    ee = lax.broadcasted_iota(jnp.int32, (N_EXPERTS, N_EXPERTS), 0)
    ec = lax.broadcasted_iota(jnp.int32, (N_EXPERTS, N_EXPERTS), 1)
    run_row = jnp.sum(jnp.where(ee == ec, jnp.broadcast_to(run, (N_EXPERTS, N_EXPERTS)), 0.0), axis=0, keepdims=True)
    offs = jnp.sum(jnp.where(ec < ee, jnp.broadcast_to(run_row, (N_EXPERTS, N_EXPERTS)), 0.0), axis=1, keepdims=True)
    base = offs
    lpos = []
    for k in range(TOP_K):
        before = jnp.dot(ohs[k].astype(BF16), upper_ref[...], preferred_element_type=F32)
        lpos.append(jnp.sum(ohs[k] * (before + base), axis=0, keepdims=True))
        base = base + cnts[k]
    lpos_ref[...] = jnp.concatenate(lpos, axis=0).astype(jnp.int32)
    cnt_ref[0] = cnt.astype(jnp.int32)


def _moe_router(lay, h, mod, g, w_router, b_router, n_rows):
    tm = lay.tm
    nt = n_rows // tm
    upper = jnp.asarray(np.triu(np.ones((tm, tm), np.float32), 1), dtype=BF16)
    return pl.pallas_call(
        _router_kernel,
        grid=(nt,),
        in_specs=[lay.row_spec(), lay.mod_spec(), _const_spec((1, D)),
                  _const_spec((N_EXPERTS, D)), _const_spec((N_EXPERTS, 1)), _const_spec((tm, tm))],
        out_specs=[pl.BlockSpec((TOP_K, tm), lambda i: (0, i)),
                   pl.BlockSpec((TOP_K, tm), lambda i: (0, i)),
                   pl.BlockSpec((1, N_EXPERTS, 1), lambda i: (i, 0, 0))],
        out_shape=[jax.ShapeDtypeStruct((TOP_K, n_rows), F32),
                   jax.ShapeDtypeStruct((TOP_K, n_rows), jnp.int32),
                   jax.ShapeDtypeStruct((nt, N_EXPERTS, 1), jnp.int32)],
        compiler_params=_cparams(("parallel",)),
        name="moe_router",
    )(h, mod, g.reshape(1, D), w_router.T, b_router.reshape(N_EXPERTS, 1), upper)


def _comp_rows(tm):
    return tm * TOP_K + N_EXPERTS * RUN_ALIGN


def _moe_plan(cnt, n, tm):
    nt = n // tm
    def excl_sum(a, axis):
        k = a.shape[axis]
        before = np.arange(k)[:, None] > np.arange(k)[None, :]
        if axis == 0:
            return jnp.sum(jnp.where(before[:, :, None], a[None, :, :], 0), axis=1)
        return jnp.sum(jnp.where(before[None, :, :], a[:, None, :], 0), axis=2)

    run = (cnt + RUN_ALIGN - 1) // RUN_ALIGN * RUN_ALIGN
    tot = jnp.sum(run, axis=0)
    padded = (tot + MOE_BLOCK - 1) // MOE_BLOCK * MOE_BLOCK
    pad_start = excl_sum(padded[None, :], 1)[0]
    pad_end = pad_start + padded
    base = pad_start[None, :] + excl_sum(run, 0)
    rows_max = n * TOP_K + nt * N_EXPERTS * (RUN_ALIGN - 1) + N_EXPERTS * MOE_BLOCK
    n_blocks = -(-rows_max // MOE_BLOCK)
    rows = n_blocks * MOE_BLOCK
    blk_row = jnp.arange(n_blocks, dtype=jnp.int32) * MOE_BLOCK
    block_expert = jnp.minimum(jnp.sum((pad_end[None, :] <= blk_row[:, None]).astype(jnp.int32), axis=1),
                               N_EXPERTS - 1).astype(jnp.int32)
    spare_rows = 2 * _comp_rows(tm)
    fill_base = jnp.concatenate([pad_start + tot, pad_end[-1:], jnp.full((1,), rows)]).astype(jnp.int32)
    fill_cnt = (jnp.concatenate([padded - tot, rows - pad_end[-1:], jnp.full((1,), spare_rows)])
                // RUN_ALIGN).astype(jnp.int32)
    n_used = (pad_end[-1:] // MOE_BLOCK).astype(jnp.int32)
    cr = _comp_rows(tm)
    ngrp = cr // RUN_ALIGN
    run_g = run // RUN_ALIGN
    goffs = excl_sum(run_g, 1)
    gend = goffs + run_g
    ng = gend[:, -1]
    j = jnp.arange(ngrp, dtype=jnp.int32)
    e_of = jnp.minimum(jnp.sum((gend[:, None, :] <= j[None, :, None]).astype(jnp.int32), axis=2), N_EXPERTS - 1)
    oh = e_of[:, :, None] == jnp.arange(N_EXPERTS, dtype=jnp.int32)[None, None, :]
    pick = lambda a: jnp.sum(jnp.where(oh, a[:, None, :], 0), axis=2)
    row = pick(base) + RUN_ALIGN * (j[None, :] - pick(goffs))
    live = j[None, :] < ng[:, None]
    spare = rows + (jnp.arange(nt, dtype=jnp.int32)[:, None] % 2) * cr + RUN_ALIGN * j[None, :]
    put_rows = jnp.where(live, row, spare).astype(jnp.int32).reshape(nt, 1, ngrp)
    get_rows = jnp.where(live, row, 0).astype(jnp.int32).reshape(nt, 1, ngrp)
    return put_rows, get_rows, block_expert, fill_base, fill_cnt, n_used, n_blocks


def _group_copies(n_groups, src_of, dst_of, sem, wait):
    def body(j, carry):
        cp = pltpu.make_async_copy(src_of(j), dst_of(j), sem)
        cp.wait() if wait else cp.start()
        return carry

    lax.fori_loop(0, n_groups, body, 0, unroll=8)


def _rows_at(ref, start):
    return ref.at[pl.ds(pl.multiple_of(start, RUN_ALIGN), RUN_ALIGN), :]


def _compact_kernel(fill_base_ref, fill_cnt_ref, put_ref, pput_ref,
                    h_ref, mod_ref, g_ref, lpos_ref, xs_hbm, comp_ref, zero_ref, sems):
    i = pl.program_id(0)
    s = i % 2
    m = mod_ref[0]
    f = _norm_mod(h_ref[...], g_ref[...], m[:, 3 * D:4 * D], m[:, 4 * D:5 * D]).astype(BF16)
    cr, tm = comp_ref.shape[1], h_ref.shape[0]
    q = lax.broadcasted_iota(jnp.int32, (cr, tm), 0).astype(F32)
    lp = lpos_ref[...].astype(F32)
    miss = (lp[0:1, :] - q) * (lp[1:2, :] - q) * ((lp[2:3, :] - q) * (lp[3:4, :] - q))
    comp_ref[s] = _pack_rows(jnp.dot(jnp.where(miss == 0.0, 1.0, 0.0).astype(BF16), f, preferred_element_type=F32))

    def copies(slot, rows_ref, wait):
        src = lambda j: _rows_at(comp_ref.at[slot], j * RUN_ALIGN)
        dst = lambda j: _rows_at(xs_hbm, rows_ref[0, 0, j])
        _group_copies(cr // RUN_ALIGN, src, dst, sems.at[slot], wait)

    @pl.when(i == 0)
    def _():
        zero_ref[...] = jnp.zeros_like(zero_ref)
        for e in range(fill_cnt_ref.shape[0]):
            def fill(c, carry, wait=False, e=e):
                cp = pltpu.make_async_copy(zero_ref, _rows_at(xs_hbm, fill_base_ref[e] + c * RUN_ALIGN), sems.at[0])
                cp.wait() if wait else cp.start()
                return carry

            lax.fori_loop(0, fill_cnt_ref[e], fill, 0)
            lax.fori_loop(0, fill_cnt_ref[e], functools.partial(fill, wait=True), 0)

    def wait_slot(slot):
        pltpu.make_async_copy(comp_ref.at[slot], xs_hbm.at[pl.ds(0, cr), :], sems.at[slot]).wait()

    del pput_ref

    @pl.when(i > 0)
    def _():
        wait_slot(1 - s)

    copies(s, put_ref, False)

    @pl.when(i == pl.num_programs(0) - 1)
    def _():
        wait_slot(s)


def _group_table(shift, nt, ngrp):
    def idx(i, *_):
        return (jnp.clip(i + shift, 0, nt - 1), 0, 0)

    return pl.BlockSpec((1, 1, ngrp), idx, memory_space=pltpu.SMEM)


def _moe_compact(lay, h, mod, g, lpos, put_rows, fill_base, fill_cnt, n_rows, n_blocks):
    tm = lay.tm
    nt, ngrp = n_rows // tm, _comp_rows(tm) // RUN_ALIGN
    grid_spec = pltpu.PrefetchScalarGridSpec(
        num_scalar_prefetch=2,
        grid=(nt,),
        in_specs=[_group_table(0, nt, ngrp), _group_table(-1, nt, ngrp)]
                 + [pl.BlockSpec((tm, D), lambda i, *_: (i, 0)),
                    pl.BlockSpec((1, 1, 6 * D), lambda i, *_: (lay.mod_idx(i), 0, 0)),
                    pl.BlockSpec((1, D), lambda i, *_: (0, 0)),
                    pl.BlockSpec((TOP_K, tm), lambda i, *_: (0, i))],
        out_specs=pl.BlockSpec(memory_space=pl.ANY),
        scratch_shapes=[pltpu.VMEM((2, _comp_rows(tm), DP), jnp.uint32), pltpu.VMEM((RUN_ALIGN, DP), jnp.uint32),
                        pltpu.SemaphoreType.DMA((2,))],
    )
    return pl.pallas_call(
        _compact_kernel,
        grid_spec=grid_spec,
        out_shape=jax.ShapeDtypeStruct((n_blocks * MOE_BLOCK + 2 * _comp_rows(tm), DP), jnp.uint32),
        compiler_params=_cparams(("arbitrary",)),
        name="moe_compact",
    )(fill_base, fill_cnt, put_rows, put_rows, h, mod, g.reshape(1, D), lpos)


def _expert_kernel(be_ref, nu_ref, x_ref, w1_ref, b1_ref, w2_ref, b2_ref, o_ref, w1b_ref, w2b_ref):
    i = pl.program_id(0)

    @pl.when((i == 0) | (be_ref[i] != be_ref[jnp.maximum(i - 1, 0)]))
    def _():
        w1b_ref[...] = w1_ref[0, 0].astype(BF16)
        w2b_ref[...] = w2_ref[0, 0].astype(BF16)

    @pl.when(i < nu_ref[0])
    def _():
        x = jnp.concatenate(_unpack_rows(x_ref[...]), axis=1)
        hcat = jnp.dot(x, w1b_ref[...], preferred_element_type=F32) + b1_ref[0]
        h_glu = jnp.minimum(hcat[:, :D_FF], SWIGLU_LIMIT)
        h_lin = jnp.clip(hcat[:, D_FF:], -SWIGLU_LIMIT, SWIGLU_LIMIT)
        act = h_glu * jax.nn.sigmoid(SWIGLU_ALPHA * h_glu) * (h_lin + 1.0)
        y = jnp.dot(act.astype(BF16), w2b_ref[...], preferred_element_type=F32) + b2_ref[0]
        o_ref[...] = _pack_rows(y.astype(BF16).astype(F32))

    @pl.when(i >= nu_ref[0])
    def _():
        o_ref[...] = jnp.zeros_like(o_ref)


def _moe_experts(xs, block_expert, n_used, n_blocks, layer, w1_all, b1, w2_all, b2):
    def xrow(i, be, nu):
        return (jnp.minimum(i, nu[0] - 1), 0)

    grid_spec = pltpu.PrefetchScalarGridSpec(
        num_scalar_prefetch=2,
        grid=(n_blocks,),
        in_specs=[pl.BlockSpec((MOE_BLOCK, DP), xrow),
                  pl.BlockSpec((1, 1, D, 2 * D_FF), lambda i, be, nu: (layer, be[i], 0, 0)),
                  pl.BlockSpec((1, 1, 2 * D_FF), lambda i, be, nu: (be[i], 0, 0)),
                  pl.BlockSpec((1, 1, D_FF, D), lambda i, be, nu: (layer, be[i], 0, 0)),
                  pl.BlockSpec((1, 1, D), lambda i, be, nu: (be[i], 0, 0))],
        out_specs=pl.BlockSpec((MOE_BLOCK, DP), lambda i, be, nu: (i, 0)),
        scratch_shapes=[pltpu.VMEM((D, 2 * D_FF), BF16), pltpu.VMEM((D_FF, D), BF16)],
    )
    return pl.pallas_call(
        _expert_kernel,
        grid_spec=grid_spec,
        out_shape=jax.ShapeDtypeStruct((n_blocks * MOE_BLOCK, DP), jnp.uint32),
        compiler_params=_cparams(("arbitrary",)),
        name="moe_experts",
    )(block_expert, n_used, xs, w1_all, b1.reshape(N_EXPERTS, 1, 2 * D_FF), w2_all, b2.reshape(N_EXPERTS, 1, D))


def _combine_kernel(get_ref, nget_ref, h_ref, gate_ref, lpos_ref, mod_ref, y_hbm, *rest, final):
    fg_ref = rest[0] if final else None
    o_ref, comp_ref, sems = rest[-3:]
    i = pl.program_id(0)
    s = i % 2
    cr, tm = comp_ref.shape[1], h_ref.shape[0]

    def copies(slot, rows_ref, wait):
        src = lambda j: _rows_at(y_hbm, rows_ref[0, 0, j])
        dst = lambda j: _rows_at(comp_ref.at[slot], j * RUN_ALIGN)
        _group_copies(cr // RUN_ALIGN, src, dst, sems.at[slot], wait)

    @pl.when(i == 0)
    def _():
        copies(0, get_ref, False)

    @pl.when(i + 1 < pl.num_programs(0))
    def _():
        copies(1 - s, nget_ref, False)

    q = lax.broadcasted_iota(jnp.int32, (tm, cr), 1)
    gates, lp = gate_ref[...], lpos_ref[...]
    pg = jnp.zeros((tm, cr), F32)
    for k in range(TOP_K):
        pg = jnp.where(lp[:, k:k + 1] == q, gates[:, k:k + 1], pg)
    pltpu.make_async_copy(y_hbm.at[pl.ds(0, cr), :], comp_ref.at[s], sems.at[s]).wait()
    y_lo, y_hi = _unpack_rows(comp_ref[s])
    pgb = pg.astype(BF16)
    y = jnp.concatenate([jnp.dot(pgb, y_lo, preferred_element_type=F32),
                         jnp.dot(pgb, y_hi, preferred_element_type=F32)], axis=1)
    h_new = h_ref[...] + mod_ref[0][:, 5 * D:6 * D] * y
    if final:
        ms = jnp.mean(h_new * h_new, axis=-1, keepdims=True)
        h_new = h_new * lax.rsqrt(ms + NORM_EPS) * fg_ref[...]
    o_ref[...] = h_new


def _moe_combine(lay, h, mod, yb, gates_t, lpos_t, get_rows, n_rows, final_g=None):
    tm = lay.tm
    nt, ngrp = n_rows // tm, _comp_rows(tm) // RUN_ALIGN
    final = final_g is not None
    extra_specs = [pl.BlockSpec((1, D), lambda i, *_: (0, 0))] if final else []
    extra_args = [final_g.reshape(1, D)] if final else []
    grid_spec = pltpu.PrefetchScalarGridSpec(
        num_scalar_prefetch=0,
        grid=(nt,),
        in_specs=[_group_table(0, nt, ngrp), _group_table(1, nt, ngrp),
                  pl.BlockSpec((tm, D), lambda i, *_: (i, 0)),
                  pl.BlockSpec((tm, TOP_K), lambda i, *_: (i, 0)),
                  pl.BlockSpec((tm, TOP_K), lambda i, *_: (i, 0)),
                  pl.BlockSpec((1, 1, 6 * D), lambda i, *_: (lay.mod_idx(i), 0, 0)),
                  pl.BlockSpec(memory_space=pl.ANY)] + extra_specs,
        out_specs=pl.BlockSpec((tm, D), lambda i, *_: (i, 0)),
        scratch_shapes=[pltpu.VMEM((2, _comp_rows(tm), DP), jnp.uint32), pltpu.SemaphoreType.DMA((2,))],
    )
    return pl.pallas_call(
        functools.partial(_combine_kernel, final=final),
        grid_spec=grid_spec,
        out_shape=jax.ShapeDtypeStruct((n_rows, D), F32),
        compiler_params=_cparams(("arbitrary",)),
        name="moe_combine",
    )(get_rows, get_rows, h, gates_t, lpos_t, mod, yb, *extra_args)


def _moe_layer(lay, h, mod, g, w_router, b_router, layer, w1_all, b1, w2_all, b2, n_rows, final_g=None):
    gates, lpos, cnt = _moe_router(lay, h, mod, g, w_router, b_router, n_rows)
    put_rows, get_rows, block_expert, fill_base, fill_cnt, n_used, n_blocks = _moe_plan(cnt[:, :, 0], n_rows, lay.tm)
    xs = _moe_compact(lay, h, mod, g, lpos, put_rows, fill_base, fill_cnt, n_rows, n_blocks)
    yb = _moe_experts(xs, block_expert, n_used, n_blocks, layer, w1_all, b1, w2_all, b2)
    return _moe_combine(lay, h, mod, yb, gates.T, lpos.T, get_rows, n_rows, final_g)


def kernel(x, c, ctx, c_ctx, ada_w, ada_b, norm_g, attn_w_qkv, attn_b_qkv, attn_w_o, attn_b_o, attn_sinks, pool_w, pool_b, pool_scale, rwkv_mix, rwkv_w_rkv, rwkv_w0, rwkv_w1, rwkv_w2, rwkv_a0, rwkv_a1, rwkv_a2, rwkv_g1, rwkv_g2, rwkv_k_k, rwkv_k_a, rwkv_r_k, rwkv_ln_w, rwkv_ln_b, rwkv_w_o, moe_w_router, moe_b_router, moe_w1, moe_b1, moe_w2, moe_b2, final_g):
    B, T, _ = x.shape
    L = ctx.shape[1]
    depth = ada_w.shape[0]
    lay = _Layout(B, T, L)
    h = jnp.concatenate([x.reshape(B * T, D), ctx.reshape(B * L, D)], axis=0)

    n_c = B + 1
    n_c_pad = -(-n_c // 8) * 8
    c_all = jnp.concatenate([c, c_ctx[None, :], jnp.zeros((n_c_pad - n_c, D), F32)], axis=0)
    mods = _ada_mods(c_all, ada_w, ada_b)
    cos, sin = _rope_tables(T, L)

    for layer in range(depth):
        last = layer == depth - 1
        kind, j = layer % 3, layer // 3
        mod = mods[layer].reshape(n_c_pad, 1, 6 * D)
        n_rows = lay.rows_x if last else lay.N
        if kind == 0:
            q, k, v = _attn_qkv(lay, h, mod, norm_g[layer, 0], attn_w_qkv[j], attn_b_qkv[j], cos, sin)
            o = _attention(lay, q, k, v, attn_sinks[j], not last)
            h = _linear_residual(lay, h, o, mod, attn_w_o[j], attn_b_o[j], 2, n_rows)
        elif kind == 1:
            h = _pool_mixer(lay, h, mod, norm_g[layer, 0], pool_w[j], pool_b[j], pool_scale[j], n_rows)
        else:
            p = dict(mix=rwkv_mix[j], w_rkv=rwkv_w_rkv[j], w0=rwkv_w0[j], w1=rwkv_w1[j], w2=rwkv_w2[j],
                     a0=rwkv_a0[j], a1=rwkv_a1[j], a2=rwkv_a2[j], g1=rwkv_g1[j], g2=rwkv_g2[j],
                     k_k=rwkv_k_k[j], k_a=rwkv_k_a[j], r_k=rwkv_r_k[j])
            r, v, kn, g, bonus, lw0, k0, b0, lw1, k1, b1 = _rwkv_proj(lay, h, mod, norm_g[layer, 0], p)
            yf, yb = _rwkv_scan(lay, r, v, kn, lw0, k0, b0, lw1, k1, b1)
            h = _rwkv_out(lay, h, yf, yb, bonus, g, mod, rwkv_ln_w[j], rwkv_ln_b[j], rwkv_w_o[j])
        h = _moe_layer(lay, h, mod, norm_g[layer, 1], moe_w_router[layer], moe_b_router[layer],
                       layer, moe_w1, moe_b1[layer], moe_w2, moe_b2[layer], n_rows,
                       final_g if last else None)
    return h.reshape(B, T, D)
```

```python
import functools

import jax
import jax.numpy as jnp
import numpy as np
from jax import lax
from jax.experimental import pallas as pl
from jax.experimental.pallas import tpu as pltpu

F32 = jnp.float32
BF16 = jnp.bfloat16

D = 1024
NORM_EPS = 1e-5
GRID_W = 64
HEAD_DIM = 64
N_HEADS = 16
N_KV = 4
Q_DIM = 1024
KV_DIM = 256
QK_DIM = Q_DIM + KV_DIM
KVD_DIM = 2 * KV_DIM
QKD_DIM = Q_DIM + KVD_DIM
WINDOW = 128
QB = 128
ROPE_BASE = 10000.0
POOL_SIZES = (2, 4, 8, 16)
POOL_GW = 256
HALO = 8
GN_EPS = 64e-5
N_EXPERTS = 32
TOP_K = 4
D_FF = 1024
SWIGLU_LIMIT = 7.0
SWIGLU_ALPHA = 1.702
MOE_BLOCK = 512
RW_CHUNK = 64
RW_GROUP = 256
VMEM_LIMIT = 56 * 1024 * 1024


def _cparams(sem):
    return pltpu.CompilerParams(dimension_semantics=sem, vmem_limit_bytes=VMEM_LIMIT)


def _norm_mod(h, g, shift, scale):
    ms = jnp.mean(h * h, axis=-1, keepdims=True)
    y = h * lax.rsqrt(ms + NORM_EPS) * g
    return y * (1.0 + scale) + shift


class _Layout:
    def __init__(self, B, T, L):
        self.B, self.T, self.L = B, T, L
        self.tm = 256 if L % 256 == 0 else 128
        self.rows_x, self.rows_c = B * T, B * L
        self.N = self.rows_x + self.rows_c
        self.tx, self.tc = T // self.tm, L // self.tm
        self.n_xt = self.rows_x // self.tm
        self.n_t = self.N // self.tm

    def mod_idx(self, i):
        return jnp.where(i < self.n_xt, i // self.tx, self.B)

    def mod_spec(self):
        return pl.BlockSpec((1, 1, 6 * D), lambda i: (self.mod_idx(i), 0, 0))

    def row_spec(self, width=D):
        return pl.BlockSpec((self.tm, width), lambda i: (i, 0))

    def seq_tile(self, i):
        j = jnp.where(i < self.n_xt, i % self.tx, (i - self.n_xt) % self.tc)
        n = jnp.where(i < self.n_xt, self.tx, self.tc)
        return j, n

    def halo_specs(self):
        hb = self.tm // HALO
        last = self.N // HALO - 1
        prev = pl.BlockSpec((HALO, D), lambda i: (jnp.maximum(i * hb - 1, 0), 0))
        nxt = pl.BlockSpec((HALO, D), lambda i: (jnp.minimum((i + 1) * hb, last), 0))
        return prev, nxt


def _const_spec(shape):
    nd = len(shape)
    return pl.BlockSpec(shape, lambda *_: (0,) * nd)


def _ada_kernel(c_ref, w_ref, b_ref, o_ref):
    c = c_ref[...]
    s = c * jax.nn.sigmoid(c)
    o_ref[0] = jnp.dot(s, w_ref[0], preferred_element_type=F32, precision=lax.Precision.HIGHEST) + b_ref[0]


def _ada_mods(c_all, ada_w, ada_b):
    depth = ada_w.shape[0]
    R = c_all.shape[0]
    nt = 1536
    return pl.pallas_call(
        _ada_kernel,
        grid=(depth, 6 * D // nt),
        in_specs=[pl.BlockSpec((R, D), lambda l, j: (0, 0)),
                  pl.BlockSpec((1, D, nt), lambda l, j: (l, 0, j)),
                  pl.BlockSpec((1, 1, nt), lambda l, j: (l, 0, j))],
        out_specs=pl.BlockSpec((1, R, nt), lambda l, j: (l, 0, j)),
        out_shape=jax.ShapeDtypeStruct((depth, R, 6 * D), F32),
        compiler_params=_cparams(("parallel", "parallel")),
        name="ada_mod",
    )(c_all, ada_w, ada_b.reshape(depth, 1, 6 * D))


def _qkv_kernel(h_ref, mod_ref, g_ref, w_ref, b_ref, cos_ref, sin_ref, q_ref, k_ref, v_ref):
    m = mod_ref[0]
    a = _norm_mod(h_ref[...], g_ref[...], m[:, 0:D], m[:, D:2 * D])
    qkv = jnp.dot(a.astype(BF16), w_ref[...], preferred_element_type=F32) + b_ref[...]
    qk = qkv[:, :QK_DIM]
    lane = lax.broadcasted_iota(jnp.int32, (1, QK_DIM), 1)
    first = (lane % 32) < 16
    rot = jnp.where(first, -pltpu.roll(qk, QK_DIM - 16, 1), pltpu.roll(qk, 16, 1))
    qk = qk * cos_ref[...] + rot * sin_ref[...]
    q_ref[...] = (qk[:, :Q_DIM] * (HEAD_DIM ** -0.5)).astype(BF16)
    k_ref[...] = _dup_heads(qk[:, Q_DIM:]).astype(BF16)
    v_ref[...] = _dup_heads(qkv[:, QK_DIM:]).astype(BF16)


def _dup_heads(x):
    heads = [x[:, g * HEAD_DIM:(g + 1) * HEAD_DIM] for g in range(N_KV)]
    return jnp.concatenate([hh for hd in heads for hh in (hd, hd)], axis=1)


def _rope_tables(T, L):
    rows = T // GRID_W
    n_freq = HEAD_DIM // 4
    inv = ROPE_BASE ** (-jnp.arange(n_freq, dtype=F32) / n_freq)
    row_ang = jnp.arange(rows, dtype=F32)[:, None] * inv
    col_ang = jnp.arange(GRID_W, dtype=F32)[:, None] * inv
    ang_r = jnp.broadcast_to(row_ang[:, None, :], (rows, GRID_W, n_freq)).reshape(T, n_freq)
    ang_c = jnp.broadcast_to(col_ang[None, :, :], (rows, GRID_W, n_freq)).reshape(T, n_freq)
    ang = jnp.concatenate([ang_r, ang_r, ang_c, ang_c], axis=-1)
    cos = jnp.concatenate([jnp.cos(ang), jnp.ones((L, HEAD_DIM), F32)], axis=0)
    sin = jnp.concatenate([jnp.sin(ang), jnp.zeros((L, HEAD_DIM), F32)], axis=0)
    reps = QK_DIM // HEAD_DIM
    return jnp.tile(cos, (1, reps)), jnp.tile(sin, (1, reps))


def _attn_qkv(lay, h, mod, g, w_qkv, b_qkv, cos, sin):
    tm = lay.tm

    def tab_idx(i):
        return (jnp.where(i < lay.n_xt, i % lay.tx, lay.tx + (i - lay.n_xt) % lay.tc), 0)

    w = w_qkv.astype(BF16)
    b = b_qkv.reshape(1, -1)
    width = Q_DIM + 2 * KV_DIM
    return pl.pallas_call(
        _qkv_kernel,
        grid=(lay.n_t,),
        in_specs=[lay.row_spec(), lay.mod_spec(), _const_spec((1, D)),
                  _const_spec((D, width)), _const_spec((1, width)),
                  pl.BlockSpec((tm, QK_DIM), tab_idx), pl.BlockSpec((tm, QK_DIM), tab_idx)],
        out_specs=[lay.row_spec(Q_DIM), lay.row_spec(KVD_DIM), lay.row_spec(KVD_DIM)],
        out_shape=[jax.ShapeDtypeStruct((lay.N, Q_DIM), BF16),
                   jax.ShapeDtypeStruct((lay.N, KVD_DIM), BF16),
                   jax.ShapeDtypeStruct((lay.N, KVD_DIM), BF16)],
        compiler_params=_cparams(("parallel",)),
        name="attn_qkv",
    )(h, mod, g.reshape(1, D), w, b, cos, sin)


def _attn_heads(q, kk, vv, mask, sink_ref, o_ref):
    R = q.shape[0]
    G = N_HEADS // N_KV
    TILE = 2 * HEAD_DIM
    lo = lax.broadcasted_iota(jnp.int32, (1, TILE), 1) < HEAD_DIM
    order = (0, 2, 1, 3)
    scores = []
    for g in range(N_KV):
        pieces = []
        for j in order:
            hq = g * G + j
            qt = q[:, (hq // 2) * TILE:(hq // 2 + 1) * TILE]
            pieces.append(jnp.where(lo if hq % 2 == 0 else ~lo, qt, jnp.zeros_like(qt)))
        qs = jnp.concatenate(pieces, axis=0)
        scores.append(lax.dot_general(qs, kk[:, g * TILE:(g + 1) * TILE], (((1,), (1,)), ((), ())),
                                      preferred_element_type=F32))
    probs, esinks = [], []
    for g in range(N_KV):
        s = scores[g]
        if mask is not None:
            s = jnp.where(mask[None], s.reshape(G, R, -1), -jnp.inf).reshape(G * R, -1)
        sink = jnp.concatenate([jnp.full((R, 1), sink_ref[g * G + j], F32) for j in order], axis=0)
        m = jnp.maximum(jnp.max(s, axis=-1, keepdims=True), sink)
        probs.append(jnp.exp((s - m).astype(BF16)))
        esinks.append(jnp.exp(sink - m))
    for g in range(N_KV):
        vd = vv[:, g * TILE:(g + 1) * TILE]
        one = jnp.ones_like(vd)
        p, es = probs[g], esinks[g]
        half = (G // 2) * R
        oe = jnp.dot(p[:half], jnp.where(lo, vd, one), preferred_element_type=F32)
        oo = jnp.dot(p[half:], jnp.where(lo, one, vd), preferred_element_type=F32)
        re = oe / (pltpu.roll(oe, HEAD_DIM, 1) + es[:half])
        ro = oo / (pltpu.roll(oo, HEAD_DIM, 1) + es[half:])
        for t in range(G // 2):
            tile = jnp.where(lo, re[t * R:(t + 1) * R], ro[t * R:(t + 1) * R])
            c = g * (G // 2) + t
            o_ref[:, c * TILE:(c + 1) * TILE] = tile.astype(o_ref.dtype)


def _attn_kernel(sink_ref, q_ref, kp_ref, kc_ref, kn_ref, vp_ref, vc_ref, vn_ref, kx_ref, vx_ref, o_ref, *, T, L):
    i = pl.program_id(1)
    nqb = T // QB

    @pl.when(i < nqb)
    def _():
        kk = jnp.concatenate([kx_ref[...], kp_ref[...], kc_ref[...], kn_ref[...]], axis=0)
        vv = jnp.concatenate([vx_ref[...], vp_ref[...], vc_ref[...], vn_ref[...]], axis=0)
        S = L + 3 * QB
        col = lax.broadcasted_iota(jnp.int32, (QB, S), 1)
        row = lax.broadcasted_iota(jnp.int32, (QB, S), 0)
        rel = col - L - QB - row
        kpos = i * QB - QB + (col - L)
        local_ok = (jnp.abs(rel) <= WINDOW) & (kpos >= 0) & (kpos < T)
        mask = (col < L) | local_ok
        _attn_heads(q_ref[...], kk, vv, mask, sink_ref, o_ref)

    @pl.when(i >= nqb)
    def _():
        _attn_heads(q_ref[...], kx_ref[...], vx_ref[...], None, sink_ref, o_ref)


def _attention(lay, q, k, v, sinks, need_ctx):
    B, T, L = lay.B, lay.T, lay.L
    nqb, ncb = T // QB, L // QB
    cq = lay.rows_x // QB
    cb = lay.rows_x // L
    smem = pl.BlockSpec(memory_space=pltpu.SMEM)

    def cur(b, i):
        return (jnp.where(i < nqb, b * nqb + i, cq + b * ncb + (i - nqb)), 0)

    def prev(b, i):
        return (b * nqb + jnp.clip(i - 1, 0, nqb - 1), 0)

    def mid(b, i):
        return (b * nqb + jnp.minimum(i, nqb - 1), 0)

    def nxt(b, i):
        return (b * nqb + jnp.minimum(i + 1, nqb - 1), 0)

    def ctxb(b, i):
        return (cb + b, 0)

    kv_specs = [pl.BlockSpec((QB, KVD_DIM), prev), pl.BlockSpec((QB, KVD_DIM), mid), pl.BlockSpec((QB, KVD_DIM), nxt)]
    n_out = lay.N if need_ctx else lay.rows_x
    return pl.pallas_call(
        functools.partial(_attn_kernel, T=T, L=L),
        grid=(B, nqb + (ncb if need_ctx else 0)),
        in_specs=[smem, pl.BlockSpec((QB, Q_DIM), cur)] + kv_specs + kv_specs
                 + [pl.BlockSpec((L, KVD_DIM), ctxb), pl.BlockSpec((L, KVD_DIM), ctxb)],
        out_specs=pl.BlockSpec((QB, Q_DIM), cur),
        out_shape=jax.ShapeDtypeStruct((n_out, Q_DIM), BF16),
        compiler_params=_cparams(("parallel", "arbitrary")),
        name="attention",
    )(sinks, q, k, k, k, v, v, v, k, v)


def _linres_kernel(h_ref, x_ref, mod_ref, w_ref, b_ref, o_ref, *, gate_slot):
    gate = mod_ref[0][:, gate_slot * D:(gate_slot + 1) * D]
    y = jnp.dot(x_ref[...], w_ref[...], preferred_element_type=F32) + b_ref[...]
    o_ref[...] = h_ref[...] + gate * y


def _linear_residual(lay, h, x, mod, w, b, gate_slot, n_rows):
    return pl.pallas_call(
        functools.partial(_linres_kernel, gate_slot=gate_slot),
        grid=(n_rows // lay.tm,),
        in_specs=[lay.row_spec(), lay.row_spec(x.shape[1]), lay.mod_spec(),
                  _const_spec(w.shape), _const_spec((1, D))],
        out_specs=lay.row_spec(),
        out_shape=jax.ShapeDtypeStruct((n_rows, D), F32),
        compiler_params=_cparams(("parallel",)),
        name="linear_residual",
    )(h, x, mod, w.astype(BF16), b.reshape(1, D))


def _fill_ext(ext_ref, h_ref, hp_ref, hn_ref, g, shift, scale, j, n, tm):
    a = _norm_mod(h_ref[...], g, shift, scale)
    ap = _norm_mod(hp_ref[...], g, shift, scale)
    an = _norm_mod(hn_ref[...], g, shift, scale)
    ext_ref[0:HALO, :] = jnp.where(j == 0, 0.0, ap)
    ext_ref[HALO:HALO + tm, :] = a
    ext_ref[HALO + tm:2 * HALO + tm, :] = jnp.where(j == n - 1, 0.0, an)
    return a


def _pool_kernel(h_ref, hp_ref, hn_ref, mod_ref, g_ref, w_ref, b_ref, ls_ref, o_ref, ext_ref, *, lay):
    tm = lay.tm
    i = pl.program_id(0)
    j, n = lay.seq_tile(i)
    m = mod_ref[0]
    a = _fill_ext(ext_ref, h_ref, hp_ref, hn_ref, g_ref[...], m[:, 0:D], m[:, D:2 * D], j, n, tm)
    t = j * tm + lax.broadcasted_iota(jnp.int32, (tm, 1), 0)
    n_tok = n * tm
    outs = []
    for gi, size in enumerate(POOL_SIZES):
        sl = slice(gi * POOL_GW, (gi + 1) * POOL_GW)
        acc = ext_ref[HALO - size // 2:HALO - size // 2 + tm, sl]
        for o in range(-size // 2 + 1, size - size // 2):
            acc = acc + ext_ref[HALO + o:HALO + o + tm, sl]
        lo = jnp.maximum(t - size // 2, 0)
        hi = jnp.minimum(t + size - size // 2, n_tok)
        d = acc / (hi - lo).astype(F32) - a[:, sl]
        outs.append(jnp.dot(d.astype(BF16), w_ref[gi], preferred_element_type=F32) + b_ref[gi])
    y = jnp.concatenate(outs, axis=-1) * ls_ref[...]
    o_ref[...] = h_ref[...] + m[:, 2 * D:3 * D] * y


def _pool_mixer(lay, h, mod, g, w, b, ls, n_rows):
    prev, nxt = lay.halo_specs()
    return pl.pallas_call(
        functools.partial(_pool_kernel, lay=lay),
        grid=(n_rows // lay.tm,),
        in_specs=[lay.row_spec(), prev, nxt, lay.mod_spec(), _const_spec((1, D)),
                  _const_spec(w.shape), _const_spec((4, 1, POOL_GW)), _const_spec((1, D))],
        out_specs=lay.row_spec(),
        out_shape=jax.ShapeDtypeStruct((n_rows, D), F32),
        scratch_shapes=[pltpu.VMEM((lay.tm + 2 * HALO, D), F32)],
        compiler_params=_cparams(("parallel",)),
        name="pool_mixer",
    )(h, h, h, mod, g.reshape(1, D), w.astype(BF16), b.reshape(4, 1, POOL_GW), ls.reshape(1, D))


def _seg_sum(x, bd, split=True):
    outs = []
    for q in range(D // RW_GROUP):
        xg = x[:, q * RW_GROUP:(q + 1) * RW_GROUP]
        hi = xg.astype(BF16)
        acc = jnp.dot(hi, bd, preferred_element_type=F32)
        if split:
            acc = acc + jnp.dot((xg - hi.astype(F32)).astype(BF16), bd, preferred_element_type=F32)
        outs.append(acc)
    return jnp.concatenate(outs, axis=-1)


def _rwkv_proj_kernel(h_ref, hp_ref, hn_ref, mod_ref, g_ref, mix_ref, wr_ref, wk_ref, wv_ref, g1_ref, g2_ref,
                      w1_ref, w2_ref, w0_ref, a1_ref, a2_ref, a0_ref, kk_ref, ka_ref, rk_ref, bd_ref,
                      r_out, v_out, kn_out, g_out, bonus_out, lw0_out, k0_out, b0_out, lw1_out, k1_out, b1_out,
                      ext_ref, *, lay):
    tm = lay.tm
    i = pl.program_id(0)
    j, n = lay.seq_tile(i)
    m = mod_ref[0]
    a = _fill_ext(ext_ref, h_ref, hp_ref, hn_ref, g_ref[...], m[:, 0:D], m[:, D:2 * D], j, n, tm)
    prev = ext_ref[HALO - 1:HALO - 1 + tm, :]
    nxt = ext_ref[HALO + 1:HALO + 1 + tm, :]
    xx = 0.5 * (prev + nxt) - a
    x_r, x_w, x_k, x_v, x_a, x_g = [(a + xx * mix_ref[q:q + 1, :]).astype(BF16) for q in range(6)]
    bd = bd_ref[...]
    r = jnp.dot(x_r, wr_ref[...], preferred_element_type=F32)
    k = jnp.dot(x_k, wk_ref[...], preferred_element_type=F32)
    v = jnp.dot(x_v, wv_ref[...], preferred_element_type=F32)
    gl = jnp.dot(x_g, g1_ref[...], preferred_element_type=F32)
    g = jnp.dot(jax.nn.sigmoid(gl).astype(BF16), g2_ref[...], preferred_element_type=F32)
    kk = k * kk_ref[...]
    kn = kk / jnp.maximum(jnp.sqrt(_seg_sum(kk * kk, bd)), 1e-12)
    tw = jnp.tanh(jnp.dot(x_w, w1_ref[...], preferred_element_type=F32))
    aa = jnp.dot(x_a, a1_ref[...], preferred_element_type=F32)
    lane = lax.broadcasted_iota(jnp.int32, (1, tw.shape[1]), 1)
    r_out[...] = r.astype(r_out.dtype)
    v_out[...] = v.astype(v_out.dtype)
    kn_out[...] = kn.astype(kn_out.dtype)
    g_out[...] = g.astype(g_out.dtype)
    rk = rk_ref[...]
    bonus = jnp.zeros_like(v)
    outs = ((lw0_out, k0_out, b0_out), (lw1_out, k1_out, b1_out))
    half = tw.shape[1] // 2
    for d in range(2):
        sel = (lane >= d * half) & (lane < (d + 1) * half)
        wl = w0_ref[d:d + 1, :] + jnp.dot(jnp.where(sel, tw, 0.0).astype(BF16), w2_ref[...], preferred_element_type=F32)
        asig = jax.nn.sigmoid(a0_ref[d:d + 1, :] + jnp.dot(jnp.where(sel, aa, 0.0).astype(BF16), a2_ref[...],
                                                           preferred_element_type=F32))
        k_d = k * (1.0 + (asig - 1.0) * ka_ref[...])
        lw_o, k_o, b_o = outs[d]
        lw_o[...] = -float(np.exp(-0.5)) * jax.nn.sigmoid(wl)
        k_o[...] = k_d.astype(k_o.dtype)
        b_o[...] = (kn * asig).astype(b_o.dtype)
        bonus = bonus + _seg_sum(r * k_d * rk, bd, split=False) * v
    bonus_out[...] = bonus.astype(bonus_out.dtype)


def _block_ones():
    idx = np.arange(RW_GROUP) // 64
    return jnp.asarray(idx[:, None] == idx[None, :], dtype=BF16)


def _rwkv_proj(lay, h, mod, g, p):
    prev, nxt = lay.halo_specs()
    cat = lambda w: jnp.concatenate([w[0], w[1]], axis=-1).astype(BF16)
    stack = lambda w: jnp.concatenate([w[0], w[1]], axis=0).astype(BF16)
    args = [h, h, h, mod, g.reshape(1, D), p['mix'],
            p['w_rkv'][0].astype(BF16), p['w_rkv'][1].astype(BF16), p['w_rkv'][2].astype(BF16),
            p['g1'].astype(BF16), p['g2'].astype(BF16),
            cat(p['w1']), stack(p['w2']), p['w0'], cat(p['a1']), stack(p['a2']), p['a0'],
            p['k_k'].reshape(1, D), p['k_a'].reshape(1, D), p['r_k'].reshape(1, D), _block_ones()]
    in_specs = [lay.row_spec(), prev, nxt, lay.mod_spec()] + [_const_spec(x.shape) for x in args[4:]]
    dtypes = [BF16] * 5 + [F32, BF16, BF16] * 2
    return pl.pallas_call(
        functools.partial(_rwkv_proj_kernel, lay=lay),
        grid=(lay.n_t,),
        in_specs=in_specs,
        out_specs=[lay.row_spec()] * len(dtypes),
        out_shape=[jax.ShapeDtypeStruct((lay.N, D), dt) for dt in dtypes],
        scratch_shapes=[pltpu.VMEM((lay.tm + 2 * HALO, D), F32)],
        compiler_params=_cparams(("parallel",)),
        name="rwkv_proj",
    )(*args)


def _rwkv_scan_kernel(rf, vf, nf, lwf, kf, bf, rb, vb, nb, lwb, kb, bb, yf_out, yb_out, z_ref):
    C, G = RW_CHUNK, RW_GROUP
    c = pl.program_id(1)

    @pl.when(c == 0)
    def _():
        z_ref[...] = jnp.zeros_like(z_ref)

    trow = lax.broadcasted_iota(jnp.int32, (C, C), 0)
    tcol = lax.broadcasted_iota(jnp.int32, (C, C), 1)
    ti = lax.broadcasted_iota(jnp.int32, (C, G), 0)
    ii = lax.broadcasted_iota(jnp.int32, (C, G), 1) % C
    brow = lax.broadcasted_iota(jnp.int32, (G, G), 0)
    bcol = lax.broadcasted_iota(jnp.int32, (G, G), 1)
    m_bd_t = (brow // C) == (bcol // 64)
    m_bd_f = (brow // 64) == (bcol // 64)
    eye_g = brow == bcol

    def bd(x):
        return jnp.where(m_bd_t, jnp.concatenate([x] * (G // C), axis=0), 0.0).astype(BF16)

    def mm(a, b):
        return jnp.dot(a, b, preferred_element_type=F32)

    dirs = ((rf, vf, nf, lwf, kf, bf, yf_out, False), (rb, vb, nb, lwb, kb, bb, yb_out, True))
    ch = []
    for d, (r_ref, v_ref, n_ref, lw_ref, k_ref, b_ref, y_out, rev) in enumerate(dirs):
        tri = (tcol >= trow) if rev else (tcol <= trow)
        strict = (ii > ti) if rev else (ii < ti)
        incl = (ii >= ti) if rev else (ii <= ti)
        lw = lw_ref[...]
        lc = jnp.dot(tri.astype(F32), lw, preferred_element_type=F32, precision=lax.Precision.HIGHEST)
        ltot = lc[0:1, :] if rev else lc[C - 1:C, :]
        e_lc = jnp.exp(lc)
        ie_lc = jnp.exp(-lc)
        e_le = jnp.exp(lc - lw)
        e_bar = jnp.exp(ltot - lc)
        e_tot = jnp.exp(ltot)
        kvec, bvec = k_ref[...], b_ref[...]
        a_t = -n_ref[...] * e_le
        r_t = r_ref[...] * e_lc
        b_t = bvec * ie_lc
        k_t = kvec * ie_lc
        b_b = bvec * e_bar
        k_b = kvec * e_bar
        vall = v_ref[...]
        for g in range(D // G):
            sl = slice(g * G, (g + 1) * G)
            ch.append(dict(
                d=d, g=g, sl=sl, y_out=y_out, strict=strict, incl=incl, v=vall[:, sl], e_tot=e_tot[:, sl],
                ar=jnp.concatenate([a_t[:, sl], r_t[:, sl]], axis=0).astype(BF16),
                bk=jnp.concatenate([bd(b_t[:, sl]), bd(k_t[:, sl])], axis=0),
                lhs_t=jnp.concatenate([b_b[:, sl], k_b[:, sl]], axis=0).astype(BF16)))

    for q in ch:
        aa = lax.dot_general(q['ar'], q['bk'], (((1,), (1,)), ((), ())), preferred_element_type=F32)
        q['a_ab'] = jnp.where(q['strict'], aa[:C, :G], 0.0)
        q['a_ak'] = jnp.where(q['strict'], aa[:C, G:], 0.0)
        q['m_rb'] = jnp.where(q['incl'], aa[C:, :G], 0.0).astype(BF16)
        q['m_rk'] = jnp.where(q['incl'], aa[C:, G:], 0.0)
    for q in ch:
        q['x'] = jnp.where(ii == ti, 1.0, 0.0) + q['a_ab']
        q['p'] = mm(q['a_ab'].astype(BF16), bd(q['a_ab']))
    for q in ch:
        q['wv'] = mm(jnp.concatenate([q['a_ak'], q['m_rk']], axis=0).astype(BF16), bd(q['v']))
        q['z'] = z_ref[q['d'], q['g']]
        q['az'] = mm(q['ar'], q['z'].astype(BF16))
    n_sq = int(np.log2(C)) - 1
    for s in range(n_sq):
        last = s == n_sq - 1
        for q in ch:
            lhs = q['x'] if last else jnp.concatenate([q['x'], q['p']], axis=0)
            xp = mm(lhs.astype(BF16), bd(q['p']))
            q['x'] = q['x'] + xp[:C]
            if not last:
                q['p'] = xp[C:]
    for q in ch:
        q['u'] = mm(q['x'].astype(BF16), bd(q['az'][:C] + q['wv'][:C]))
    for q in ch:
        y = q['az'][C:] + q['wv'][C:] + mm(q['m_rb'], bd(q['u']))
        q['y_out'][:, q['sl']] = y
        rhs_t = jnp.concatenate([q['u'], q['v']], axis=0).astype(BF16)
        zu = lax.dot_general(q['lhs_t'], rhs_t, (((0,), (0,)), ((), ())), preferred_element_type=F32)
        decay = jnp.sum(jnp.where(eye_g, jnp.broadcast_to(q['e_tot'], (G, G)), 0.0), axis=1, keepdims=True)
        z_ref[q['d'], q['g']] = jnp.where(m_bd_f, decay * q['z'] + zu, 0.0)


def _rwkv_scan(lay, r, v, kn, lw0, k0, b0, lw1, k1, b1):
    B, T, L = lay.B, lay.T, lay.L
    C = RW_CHUNK
    cc, cx = L // C, T // C
    nc = cc + cx
    ctx0 = lay.rows_x // C

    def fwd(b, c):
        return (jnp.where(c < cc, ctx0 + b * cc + c, b * cx + c - cc), 0)

    def bwd(b, c):
        return (jnp.where(c < cc, ctx0 + b * cc + (cc - 1 - c), b * cx + (nc - 1 - c)), 0)

    sf, sb = pl.BlockSpec((C, D), fwd), pl.BlockSpec((C, D), bwd)
    return pl.pallas_call(
        _rwkv_scan_kernel,
        grid=(B, nc),
        in_specs=[sf] * 6 + [sb] * 6,
        out_specs=[sf, sb],
        out_shape=[jax.ShapeDtypeStruct((lay.N, D), F32)] * 2,
        scratch_shapes=[pltpu.VMEM((2, D // RW_GROUP, RW_GROUP, RW_GROUP), F32)],
        compiler_params=_cparams(("parallel", "arbitrary")),
        name="rwkv_scan",
    )(r, v, kn, lw0, k0, b0, r, v, kn, lw1, k1, b1)


def _rwkv_out_kernel(h_ref, yf_ref, yb_ref, bonus_ref, g_ref, mod_ref, lnw_ref, lnb_ref, wo_ref, bd_ref, o_ref):
    bd = bd_ref[...]
    y = yf_ref[...] + yb_ref[...]
    mu = _seg_sum(y, bd) * (1.0 / 64)
    yc = y - mu
    var = _seg_sum(yc * yc, bd) * (1.0 / 64)
    yn = yc * lax.rsqrt(var + GN_EPS) * lnw_ref[...] + lnb_ref[...]
    out = (yn + bonus_ref[...]) * g_ref[...]
    res = jnp.dot(out.astype(BF16), wo_ref[...], preferred_element_type=F32)
    o_ref[...] = h_ref[...] + mod_ref[0][:, 2 * D:3 * D] * res


def _rwkv_out(lay, h, yf, yb, bonus, g, mod, ln_w, ln_b, w_o):
    return pl.pallas_call(
        _rwkv_out_kernel,
        grid=(lay.n_t,),
        in_specs=[lay.row_spec()] * 5 + [lay.mod_spec(), _const_spec((1, D)), _const_spec((1, D)),
                                          _const_spec((D, D)), _const_spec((RW_GROUP, RW_GROUP))],
        out_specs=lay.row_spec(),
        out_shape=jax.ShapeDtypeStruct((lay.N, D), F32),
        compiler_params=_cparams(("parallel",)),
        name="rwkv_out",
    )(h, yf, yb, bonus, g, mod, ln_w.reshape(1, D), ln_b.reshape(1, D), w_o.astype(BF16), _block_ones())


RUN_ALIGN = 8
DP = D // 2
_HI_MASK = 0xFFFF0000


def _pack_rows(x):
    lo = pltpu.bitcast(x[:, :DP], jnp.uint32)
    hi = pltpu.bitcast(x[:, DP:], jnp.uint32)
    return (lo >> 16) | (hi & jnp.uint32(_HI_MASK))


def _unpack_rows(u):
    lo = pltpu.bitcast(u << 16, F32)
    hi = pltpu.bitcast(u & jnp.uint32(_HI_MASK), F32)
    return lo.astype(BF16), hi.astype(BF16)


def _router_kernel(h_ref, mod_ref, g_ref, wr_ref, br_ref, upper_ref, gate_ref, lpos_ref, cnt_ref):
    m = mod_ref[0]
    f = _norm_mod(h_ref[...], g_ref[...], m[:, 3 * D:4 * D], m[:, 4 * D:5 * D])
    def split(x):
        hi = x.astype(BF16)
        return hi, (x - hi.astype(F32)).astype(BF16)

    def nt(a, b):
        return lax.dot_general(a, b, (((1,), (1,)), ((), ())), preferred_element_type=F32)

    (w_hi, w_lo), (f_hi, f_lo) = split(wr_ref[...]), split(f)
    logits = nt(w_hi, f_hi) + (nt(w_hi, f_lo) + nt(w_lo, f_hi)) + br_ref[...]
    erow = lax.broadcasted_iota(jnp.int32, logits.shape, 0)
    tops, hots = [], []
    l = logits
    for _ in range(TOP_K):
        mx = jnp.max(l, axis=0, keepdims=True)
        ix = jnp.min(jnp.where(l == mx, erow, N_EXPERTS), axis=0, keepdims=True)
        tops.append(mx)
        hots.append(erow == ix)
        l = jnp.where(erow == ix, -jnp.inf, l)
    es = [jnp.exp(t - tops[0]) for t in tops]
    tot = es[0] + es[1] + es[2] + es[3]
    gate_ref[...] = jnp.concatenate([e / tot for e in es], axis=0)

    ohs = [jnp.where(hh, 1.0, 0.0) for hh in hots]
    cnts = [jnp.sum(oh, axis=1, keepdims=True) for oh in ohs]
    cnt = cnts[0] + cnts[1] + cnts[2] + cnts[3]
    run = jnp.ceil(cnt * (1.0 / RUN_ALIGN)) * RUN_ALIGN
    ee = lax.broadcasted_iota(jnp.int32, (N_EXPERTS, N_EXPERTS), 0)
    ec = lax.broadcasted_iota(jnp.int32, (N_EXPERTS, N_EXPERTS), 1)
    run_row = jnp.sum(jnp.where(ee == ec, jnp.broadcast_to(run, (N_EXPERTS, N_EXPERTS)), 0.0), axis=0, keepdims=True)
    offs = jnp.sum(jnp.where(ec < ee, jnp.broadcast_to(run_row, (N_EXPERTS, N_EXPERTS)), 0.0), axis=1, keepdims=True)
    base = offs
    lpos = []
    for k in range(TOP_K):
        before = jnp.dot(ohs[k].astype(BF16), upper_ref[...], preferred_element_type=F32)
        lpos.append(jnp.sum(ohs[k] * (before + base), axis=0, keepdims=True))
        base = base + cnts[k]
    lpos_ref[...] = jnp.concatenate(lpos, axis=0).astype(jnp.int32)
    cnt_ref[0] = cnt.astype(jnp.int32)


def _moe_router(lay, h, mod, g, w_router, b_router, n_rows):
    tm = lay.tm
    nt = n_rows // tm
    upper = jnp.asarray(np.triu(np.ones((tm, tm), np.float32), 1), dtype=BF16)
    return pl.pallas_call(
        _router_kernel,
        grid=(nt,),
        in_specs=[lay.row_spec(), lay.mod_spec(), _const_spec((1, D)),
                  _const_spec((N_EXPERTS, D)), _const_spec((N_EXPERTS, 1)), _const_spec((tm, tm))],
        out_specs=[pl.BlockSpec((TOP_K, tm), lambda i: (0, i)),
                   pl.BlockSpec((TOP_K, tm), lambda i: (0, i)),
                   pl.BlockSpec((1, N_EXPERTS, 1), lambda i: (i, 0, 0))],
        out_shape=[jax.ShapeDtypeStruct((TOP_K, n_rows), F32),
                   jax.ShapeDtypeStruct((TOP_K, n_rows), jnp.int32),
                   jax.ShapeDtypeStruct((nt, N_EXPERTS, 1), jnp.int32)],
        compiler_params=_cparams(("parallel",)),
        name="moe_router",
    )(h, mod, g.reshape(1, D), w_router.T, b_router.reshape(N_EXPERTS, 1), upper)


def _comp_rows(tm):
    return tm * TOP_K + N_EXPERTS * RUN_ALIGN


def _moe_plan(cnt, n, tm):
    nt = n // tm
    def excl_sum(a, axis):
        k = a.shape[axis]
        before = np.arange(k)[:, None] > np.arange(k)[None, :]
        if axis == 0:
            return jnp.sum(jnp.where(before[:, :, None], a[None, :, :], 0), axis=1)
        return jnp.sum(jnp.where(before[None, :, :], a[:, None, :], 0), axis=2)

    run = (cnt + RUN_ALIGN - 1) // RUN_ALIGN * RUN_ALIGN
    tot = jnp.sum(run, axis=0)
    padded = (tot + MOE_BLOCK - 1) // MOE_BLOCK * MOE_BLOCK
    pad_start = excl_sum(padded[None, :], 1)[0]
    pad_end = pad_start + padded
    base = pad_start[None, :] + excl_sum(run, 0)
    rows_max = n * TOP_K + nt * N_EXPERTS * (RUN_ALIGN - 1) + N_EXPERTS * MOE_BLOCK
    n_blocks = -(-rows_max // MOE_BLOCK)
    rows = n_blocks * MOE_BLOCK
    blk_row = jnp.arange(n_blocks, dtype=jnp.int32) * MOE_BLOCK
    block_expert = jnp.minimum(jnp.sum((pad_end[None, :] <= blk_row[:, None]).astype(jnp.int32), axis=1),
                               N_EXPERTS - 1).astype(jnp.int32)
    spare_rows = 2 * _comp_rows(tm)
    fill_base = jnp.concatenate([pad_start + tot, pad_end[-1:], jnp.full((1,), rows)]).astype(jnp.int32)
    fill_cnt = (jnp.concatenate([padded - tot, rows - pad_end[-1:], jnp.full((1,), spare_rows)])
                // RUN_ALIGN).astype(jnp.int32)
    n_used = (pad_end[-1:] // MOE_BLOCK).astype(jnp.int32)
    cr = _comp_rows(tm)
    ngrp = cr // RUN_ALIGN
    run_g = run // RUN_ALIGN
    goffs = excl_sum(run_g, 1)
    gend = goffs + run_g
    ng = gend[:, -1]
    j = jnp.arange(ngrp, dtype=jnp.int32)
    e_of = jnp.minimum(jnp.sum((gend[:, None, :] <= j[None, :, None]).astype(jnp.int32), axis=2), N_EXPERTS - 1)
    oh = e_of[:, :, None] == jnp.arange(N_EXPERTS, dtype=jnp.int32)[None, None, :]
    pick = lambda a: jnp.sum(jnp.where(oh, a[:, None, :], 0), axis=2)
    row = pick(base) + RUN_ALIGN * (j[None, :] - pick(goffs))
    live = j[None, :] < ng[:, None]
    spare = rows + (jnp.arange(nt, dtype=jnp.int32)[:, None] % 2) * cr + RUN_ALIGN * j[None, :]
    put_rows = jnp.where(live, row, spare).astype(jnp.int32).reshape(nt, 1, ngrp)
    get_rows = jnp.where(live, row, 0).astype(jnp.int32).reshape(nt, 1, ngrp)
    return put_rows, get_rows, block_expert, fill_base, fill_cnt, n_used, n_blocks


def _group_copies(n_groups, src_of, dst_of, sem, wait):
    def body(j, carry):
        cp = pltpu.make_async_copy(src_of(j), dst_of(j), sem)
        cp.wait() if wait else cp.start()
        return carry

    lax.fori_loop(0, n_groups, body, 0, unroll=8)


def _rows_at(ref, start):
    return ref.at[pl.ds(pl.multiple_of(start, RUN_ALIGN), RUN_ALIGN), :]


def _compact_kernel(fill_base_ref, fill_cnt_ref, put_ref, pput_ref,
                    h_ref, mod_ref, g_ref, lpos_ref, xs_hbm, comp_ref, zero_ref, sems):
    i = pl.program_id(0)
    s = i % 2
    m = mod_ref[0]
    f = _norm_mod(h_ref[...], g_ref[...], m[:, 3 * D:4 * D], m[:, 4 * D:5 * D]).astype(BF16)
    cr, tm = comp_ref.shape[1], h_ref.shape[0]
    q = lax.broadcasted_iota(jnp.int32, (cr, tm), 0).astype(F32)
    lp = lpos_ref[...].astype(F32)
    miss = (lp[0:1, :] - q) * (lp[1:2, :] - q) * ((lp[2:3, :] - q) * (lp[3:4, :] - q))
    comp_ref[s] = _pack_rows(jnp.dot(jnp.where(miss == 0.0, 1.0, 0.0).astype(BF16), f, preferred_element_type=F32))

    def copies(slot, rows_ref, wait):
        src = lambda j: _rows_at(comp_ref.at[slot], j * RUN_ALIGN)
        dst = lambda j: _rows_at(xs_hbm, rows_ref[0, 0, j])
        _group_copies(cr // RUN_ALIGN, src, dst, sems.at[slot], wait)

    @pl.when(i == 0)
    def _():
        zero_ref[...] = jnp.zeros_like(zero_ref)
        for e in range(fill_cnt_ref.shape[0]):
            def fill(c, carry, wait=False, e=e):
                cp = pltpu.make_async_copy(zero_ref, _rows_at(xs_hbm, fill_base_ref[e] + c * RUN_ALIGN), sems.at[0])
                cp.wait() if wait else cp.start()
                return carry

            lax.fori_loop(0, fill_cnt_ref[e], fill, 0)
            lax.fori_loop(0, fill_cnt_ref[e], functools.partial(fill, wait=True), 0)

    def wait_slot(slot):
        pltpu.make_async_copy(comp_ref.at[slot], xs_hbm.at[pl.ds(0, cr), :], sems.at[slot]).wait()

    del pput_ref

    @pl.when(i > 0)
    def _():
        wait_slot(1 - s)

    copies(s, put_ref, False)

    @pl.when(i == pl.num_programs(0) - 1)
    def _():
        wait_slot(s)


def _group_table(shift, nt, ngrp):
    def idx(i, *_):
        return (jnp.clip(i + shift, 0, nt - 1), 0, 0)

    return pl.BlockSpec((1, 1, ngrp), idx, memory_space=pltpu.SMEM)


def _moe_compact(lay, h, mod, g, lpos, put_rows, fill_base, fill_cnt, n_rows, n_blocks):
    tm = lay.tm
    nt, ngrp = n_rows // tm, _comp_rows(tm) // RUN_ALIGN
    grid_spec = pltpu.PrefetchScalarGridSpec(
        num_scalar_prefetch=2,
        grid=(nt,),
        in_specs=[_group_table(0, nt, ngrp), _group_table(-1, nt, ngrp)]
                 + [pl.BlockSpec((tm, D), lambda i, *_: (i, 0)),
                    pl.BlockSpec((1, 1, 6 * D), lambda i, *_: (lay.mod_idx(i), 0, 0)),
                    pl.BlockSpec((1, D), lambda i, *_: (0, 0)),
                    pl.BlockSpec((TOP_K, tm), lambda i, *_: (0, i))],
        out_specs=pl.BlockSpec(memory_space=pl.ANY),
        scratch_shapes=[pltpu.VMEM((2, _comp_rows(tm), DP), jnp.uint32), pltpu.VMEM((RUN_ALIGN, DP), jnp.uint32),
                        pltpu.SemaphoreType.DMA((2,))],
    )
    return pl.pallas_call(
        _compact_kernel,
        grid_spec=grid_spec,
        out_shape=jax.ShapeDtypeStruct((n_blocks * MOE_BLOCK + 2 * _comp_rows(tm), DP), jnp.uint32),
        compiler_params=_cparams(("arbitrary",)),
        name="moe_compact",
    )(fill_base, fill_cnt, put_rows, put_rows, h, mod, g.reshape(1, D), lpos)


def _expert_kernel(be_ref, nu_ref, x_ref, w1_ref, b1_ref, w2_ref, b2_ref, o_ref, w1b_ref, w2b_ref):
    i = pl.program_id(0)

    @pl.when((i == 0) | (be_ref[i] != be_ref[jnp.maximum(i - 1, 0)]))
    def _():
        w1b_ref[...] = w1_ref[0, 0].astype(BF16)
        w2b_ref[...] = w2_ref[0, 0].astype(BF16)

    @pl.when(i < nu_ref[0])
    def _():
        x = jnp.concatenate(_unpack_rows(x_ref[...]), axis=1)
        hcat = jnp.dot(x, w1b_ref[...], preferred_element_type=F32) + b1_ref[0]
        h_glu = jnp.minimum(hcat[:, :D_FF], SWIGLU_LIMIT)
        h_lin = jnp.clip(hcat[:, D_FF:], -SWIGLU_LIMIT, SWIGLU_LIMIT)
        act = h_glu * jax.nn.sigmoid(SWIGLU_ALPHA * h_glu) * (h_lin + 1.0)
        y = jnp.dot(act.astype(BF16), w2b_ref[...], preferred_element_type=F32) + b2_ref[0]
        o_ref[...] = _pack_rows(y.astype(BF16).astype(F32))

    @pl.when(i >= nu_ref[0])
    def _():
        o_ref[...] = jnp.zeros_like(o_ref)


def _moe_experts(xs, block_expert, n_used, n_blocks, layer, w1_all, b1, w2_all, b2):
    def xrow(i, be, nu):
        return (jnp.minimum(i, nu[0] - 1), 0)

    grid_spec = pltpu.PrefetchScalarGridSpec(
        num_scalar_prefetch=2,
        grid=(n_blocks,),
        in_specs=[pl.BlockSpec((MOE_BLOCK, DP), xrow),
                  pl.BlockSpec((1, 1, D, 2 * D_FF), lambda i, be, nu: (layer, be[i], 0, 0)),
                  pl.BlockSpec((1, 1, 2 * D_FF), lambda i, be, nu: (be[i], 0, 0)),
                  pl.BlockSpec((1, 1, D_FF, D), lambda i, be, nu: (layer, be[i], 0, 0)),
                  pl.BlockSpec((1, 1, D), lambda i, be, nu: (be[i], 0, 0))],
        out_specs=pl.BlockSpec((MOE_BLOCK, DP), lambda i, be, nu: (i, 0)),
        scratch_shapes=[pltpu.VMEM((D, 2 * D_FF), BF16), pltpu.VMEM((D_FF, D), BF16)],
    )
    return pl.pallas_call(
        _expert_kernel,
        grid_spec=grid_spec,
        out_shape=jax.ShapeDtypeStruct((n_blocks * MOE_BLOCK, DP), jnp.uint32),
        compiler_params=_cparams(("arbitrary",)),
        name="moe_experts",
    )(block_expert, n_used, xs, w1_all, b1.reshape(N_EXPERTS, 1, 2 * D_FF), w2_all, b2.reshape(N_EXPERTS, 1, D))


def _combine_kernel(get_ref, nget_ref, h_ref, gate_ref, lpos_ref, mod_ref, y_hbm, *rest, final):
    fg_ref = rest[0] if final else None
    o_ref, comp_ref, sems = rest[-3:]
    i = pl.program_id(0)
    s = i % 2
    cr, tm = comp_ref.shape[1], h_ref.shape[0]

    def copies(slot, rows_ref, wait):
        src = lambda j: _rows_at(y_hbm, rows_ref[0, 0, j])
        dst = lambda j: _rows_at(comp_ref.at[slot], j * RUN_ALIGN)
        _group_copies(cr // RUN_ALIGN, src, dst, sems.at[slot], wait)

    @pl.when(i == 0)
    def _():
        copies(0, get_ref, False)

    @pl.when(i + 1 < pl.num_programs(0))
    def _():
        copies(1 - s, nget_ref, False)

    q = lax.broadcasted_iota(jnp.int32, (tm, cr), 1)
    gates, lp = gate_ref[...], lpos_ref[...]
    pg = jnp.zeros((tm, cr), F32)
    for k in range(TOP_K):
        pg = jnp.where(lp[:, k:k + 1] == q, gates[:, k:k + 1], pg)
    pltpu.make_async_copy(y_hbm.at[pl.ds(0, cr), :], comp_ref.at[s], sems.at[s]).wait()
    y_lo, y_hi = _unpack_rows(comp_ref[s])
    pgb = pg.astype(BF16)
    y = jnp.concatenate([jnp.dot(pgb, y_lo, preferred_element_type=F32),
                         jnp.dot(pgb, y_hi, preferred_element_type=F32)], axis=1)
    h_new = h_ref[...] + mod_ref[0][:, 5 * D:6 * D] * y
    if final:
        ms = jnp.mean(h_new * h_new, axis=-1, keepdims=True)
        h_new = h_new * lax.rsqrt(ms + NORM_EPS) * fg_ref[...]
    o_ref[...] = h_new


def _moe_combine(lay, h, mod, yb, gates_t, lpos_t, get_rows, n_rows, final_g=None):
    tm = lay.tm
    nt, ngrp = n_rows // tm, _comp_rows(tm) // RUN_ALIGN
    final = final_g is not None
    extra_specs = [pl.BlockSpec((1, D), lambda i, *_: (0, 0))] if final else []
    extra_args = [final_g.reshape(1, D)] if final else []
    grid_spec = pltpu.PrefetchScalarGridSpec(
        num_scalar_prefetch=0,
        grid=(nt,),
        in_specs=[_group_table(0, nt, ngrp), _group_table(1, nt, ngrp),
                  pl.BlockSpec((tm, D), lambda i, *_: (i, 0)),
                  pl.BlockSpec((tm, TOP_K), lambda i, *_: (i, 0)),
                  pl.BlockSpec((tm, TOP_K), lambda i, *_: (i, 0)),
                  pl.BlockSpec((1, 1, 6 * D), lambda i, *_: (lay.mod_idx(i), 0, 0)),
                  pl.BlockSpec(memory_space=pl.ANY)] + extra_specs,
        out_specs=pl.BlockSpec((tm, D), lambda i, *_: (i, 0)),
        scratch_shapes=[pltpu.VMEM((2, _comp_rows(tm), DP), jnp.uint32), pltpu.SemaphoreType.DMA((2,))],
    )
    return pl.pallas_call(
        functools.partial(_combine_kernel, final=final),
        grid_spec=grid_spec,
        out_shape=jax.ShapeDtypeStruct((n_rows, D), F32),
        compiler_params=_cparams(("arbitrary",)),
        name="moe_combine",
    )(get_rows, get_rows, h, gates_t, lpos_t, mod, yb, *extra_args)


def _moe_layer(lay, h, mod, g, w_router, b_router, layer, w1_all, b1, w2_all, b2, n_rows, final_g=None):
    gates, lpos, cnt = _moe_router(lay, h, mod, g, w_router, b_router, n_rows)
    put_rows, get_rows, block_expert, fill_base, fill_cnt, n_used, n_blocks = _moe_plan(cnt[:, :, 0], n_rows, lay.tm)
    xs = _moe_compact(lay, h, mod, g, lpos, put_rows, fill_base, fill_cnt, n_rows, n_blocks)
    yb = _moe_experts(xs, block_expert, n_used, n_blocks, layer, w1_all, b1, w2_all, b2)
    return _moe_combine(lay, h, mod, yb, gates.T, lpos.T, get_rows, n_rows, final_g)


def kernel(x, c, ctx, c_ctx, ada_w, ada_b, norm_g, attn_w_qkv, attn_b_qkv, attn_w_o, attn_b_o, attn_sinks, pool_w, pool_b, pool_scale, rwkv_mix, rwkv_w_rkv, rwkv_w0, rwkv_w1, rwkv_w2, rwkv_a0, rwkv_a1, rwkv_a2, rwkv_g1, rwkv_g2, rwkv_k_k, rwkv_k_a, rwkv_r_k, rwkv_ln_w, rwkv_ln_b, rwkv_w_o, moe_w_router, moe_b_router, moe_w1, moe_b1, moe_w2, moe_b2, final_g):
    B, T, _ = x.shape
    L = ctx.shape[1]
    depth = ada_w.shape[0]
    lay = _Layout(B, T, L)
    h = jnp.concatenate([x.reshape(B * T, D), ctx.reshape(B * L, D)], axis=0)

    n_c = B + 1
    n_c_pad = -(-n_c // 8) * 8
    c_all = jnp.concatenate([c, c_ctx[None, :], jnp.zeros((n_c_pad - n_c, D), F32)], axis=0)
    mods = _ada_mods(c_all, ada_w, ada_b)
    cos, sin = _rope_tables(T, L)

    for layer in range(depth):
        last = layer == depth - 1
        kind, j = layer % 3, layer // 3
        mod = mods[layer].reshape(n_c_pad, 1, 6 * D)
        n_rows = lay.rows_x if last else lay.N
        if kind == 0:
            q, k, v = _attn_qkv(lay, h, mod, norm_g[layer, 0], attn_w_qkv[j], attn_b_qkv[j], cos, sin)
            o = _attention(lay, q, k, v, attn_sinks[j], not last)
            h = _linear_residual(lay, h, o, mod, attn_w_o[j], attn_b_o[j], 2, n_rows)
        elif kind == 1:
            h = _pool_mixer(lay, h, mod, norm_g[layer, 0], pool_w[j], pool_b[j], pool_scale[j], n_rows)
        else:
            p = dict(mix=rwkv_mix[j], w_rkv=rwkv_w_rkv[j], w0=rwkv_w0[j], w1=rwkv_w1[j], w2=rwkv_w2[j],
                     a0=rwkv_a0[j], a1=rwkv_a1[j], a2=rwkv_a2[j], g1=rwkv_g1[j], g2=rwkv_g2[j],
                     k_k=rwkv_k_k[j], k_a=rwkv_k_a[j], r_k=rwkv_r_k[j])
            r, v, kn, g, bonus, lw0, k0, b0, lw1, k1, b1 = _rwkv_proj(lay, h, mod, norm_g[layer, 0], p)
            yf, yb = _rwkv_scan(lay, r, v, kn, lw0, k0, b0, lw1, k1, b1)
            h = _rwkv_out(lay, h, yf, yb, bonus, g, mod, rwkv_ln_w[j], rwkv_ln_b[j], rwkv_w_o[j])
        h = _moe_layer(lay, h, mod, norm_g[layer, 1], moe_w_router[layer], moe_b_router[layer],
                       layer, moe_w1, moe_b1[layer], moe_w2, moe_b2[layer], n_rows,
                       final_g if last else None)
    return h.reshape(B, T, D)
```

```python
import functools

import jax
import jax.numpy as jnp
import numpy as np
from jax import lax
from jax.experimental import pallas as pl
from jax.experimental.pallas import tpu as pltpu

F32 = jnp.float32
BF16 = jnp.bfloat16

D = 1024
NORM_EPS = 1e-5
GRID_W = 64
HEAD_DIM = 64
N_HEADS = 16
N_KV = 4
Q_DIM = 1024
KV_DIM = 256
QK_DIM = Q_DIM + KV_DIM
KVD_DIM = 2 * KV_DIM
WINDOW = 128
QB = 128
ROPE_BASE = 10000.0
POOL_SIZES = (2, 4, 8, 16)
POOL_GW = 256
HALO = 8
GN_EPS = 64e-5
N_EXPERTS = 32
TOP_K = 4
D_FF = 1024
SWIGLU_LIMIT = 7.0
SWIGLU_ALPHA = 1.702
MOE_BLOCK = 512
RW_CHUNK = 64
RW_HEAD = 64
RW_GROUP = 256
VMEM_LIMIT = 56 * 1024 * 1024


def _cparams(sem):
    return pltpu.CompilerParams(dimension_semantics=sem, vmem_limit_bytes=VMEM_LIMIT)


def _norm_mod(h, g, shift, scale):
    ms = jnp.mean(h * h, axis=-1, keepdims=True)
    y = h * lax.rsqrt(ms + NORM_EPS) * g
    return y * (1.0 + scale) + shift


class _Layout:
    def __init__(self, B, T, L):
        self.B, self.T, self.L = B, T, L
        self.tm = 256 if L % 256 == 0 else 128
        self.rows_x, self.rows_c = B * T, B * L
        self.N = self.rows_x + self.rows_c
        self.tx, self.tc = T // self.tm, L // self.tm
        self.n_xt = self.rows_x // self.tm
        self.n_t = self.N // self.tm

    def mod_idx(self, i):
        return jnp.where(i < self.n_xt, i // self.tx, self.B)

    def mod_spec(self):
        return pl.BlockSpec((1, 1, 6 * D), lambda i: (self.mod_idx(i), 0, 0))

    def row_spec(self, width=D):
        return pl.BlockSpec((self.tm, width), lambda i: (i, 0))

    def seq_tile(self, i):
        j = jnp.where(i < self.n_xt, i % self.tx, (i - self.n_xt) % self.tc)
        n = jnp.where(i < self.n_xt, self.tx, self.tc)
        return j, n

    def halo_specs(self):
        hb = self.tm // HALO
        last = self.N // HALO - 1
        prev = pl.BlockSpec((HALO, D), lambda i: (jnp.maximum(i * hb - 1, 0), 0))
        nxt = pl.BlockSpec((HALO, D), lambda i: (jnp.minimum((i + 1) * hb, last), 0))
        return prev, nxt


def _const_spec(shape):
    nd = len(shape)
    return pl.BlockSpec(shape, lambda *_: (0,) * nd)


def _ada_kernel(c_ref, w_ref, b_ref, o_ref):
    c = c_ref[...]
    s = c * jax.nn.sigmoid(c)
    o_ref[0] = jnp.dot(s, w_ref[0], preferred_element_type=F32, precision=lax.Precision.HIGHEST) + b_ref[0]


def _ada_mods(c_all, ada_w, ada_b):
    depth = ada_w.shape[0]
    R = c_all.shape[0]
    nt = 1536
    return pl.pallas_call(
        _ada_kernel,
        grid=(depth, 6 * D // nt),
        in_specs=[pl.BlockSpec((R, D), lambda l, j: (0, 0)),
                  pl.BlockSpec((1, D, nt), lambda l, j: (l, 0, j)),
                  pl.BlockSpec((1, 1, nt), lambda l, j: (l, 0, j))],
        out_specs=pl.BlockSpec((1, R, nt), lambda l, j: (l, 0, j)),
        out_shape=jax.ShapeDtypeStruct((depth, R, 6 * D), F32),
        compiler_params=_cparams(("parallel", "parallel")),
        name="ada_mod",
    )(c_all, ada_w, ada_b.reshape(depth, 1, 6 * D))


def _qkv_kernel(h_ref, mod_ref, g_ref, w_ref, b_ref, cos_ref, sin_ref, q_ref, k_ref, v_ref):
    m = mod_ref[0]
    a = _norm_mod(h_ref[...], g_ref[...], m[:, 0:D], m[:, D:2 * D])
    qkv = jnp.dot(a.astype(BF16), w_ref[...], preferred_element_type=F32) + b_ref[...]
    qk = qkv[:, :QK_DIM]
    lane = lax.broadcasted_iota(jnp.int32, (1, QK_DIM), 1)
    first = (lane % 32) < 16
    rot = jnp.where(first, -pltpu.roll(qk, QK_DIM - 16, 1), pltpu.roll(qk, 16, 1))
    qk = qk * cos_ref[...] + rot * sin_ref[...]
    q_ref[...] = (qk[:, :Q_DIM] * (HEAD_DIM ** -0.5)).astype(BF16)
    k_ref[...] = _dup_heads(qk[:, Q_DIM:]).astype(BF16)
    v_ref[...] = _dup_heads(qkv[:, QK_DIM:]).astype(BF16)


def _dup_heads(x):
    heads = [x[:, g * HEAD_DIM:(g + 1) * HEAD_DIM] for g in range(N_KV)]
    return jnp.concatenate([hh for hd in heads for hh in (hd, hd)], axis=1)


def _rope_tables(T, L):
    rows = T // GRID_W
    n_freq = HEAD_DIM // 4
    inv = ROPE_BASE ** (-jnp.arange(n_freq, dtype=F32) / n_freq)
    row_ang = jnp.arange(rows, dtype=F32)[:, None] * inv
    col_ang = jnp.arange(GRID_W, dtype=F32)[:, None] * inv
    ang_r = jnp.broadcast_to(row_ang[:, None, :], (rows, GRID_W, n_freq)).reshape(T, n_freq)
    ang_c = jnp.broadcast_to(col_ang[None, :, :], (rows, GRID_W, n_freq)).reshape(T, n_freq)
    ang = jnp.concatenate([ang_r, ang_r, ang_c, ang_c], axis=-1)
    cos = jnp.concatenate([jnp.cos(ang), jnp.ones((L, HEAD_DIM), F32)], axis=0)
    sin = jnp.concatenate([jnp.sin(ang), jnp.zeros((L, HEAD_DIM), F32)], axis=0)
    reps = QK_DIM // HEAD_DIM
    return jnp.tile(cos, (1, reps)), jnp.tile(sin, (1, reps))


def _attn_qkv(lay, h, mod, g, w_qkv, b_qkv, cos, sin):
    tm = lay.tm

    def tab_idx(i):
        return (jnp.where(i < lay.n_xt, i % lay.tx, lay.tx + (i - lay.n_xt) % lay.tc), 0)

    w = w_qkv.astype(BF16)
    b = b_qkv.reshape(1, -1)
    width = Q_DIM + 2 * KV_DIM
    return pl.pallas_call(
        _qkv_kernel,
        grid=(lay.n_t,),
        in_specs=[lay.row_spec(), lay.mod_spec(), _const_spec((1, D)),
                  _const_spec((D, width)), _const_spec((1, width)),
                  pl.BlockSpec((tm, QK_DIM), tab_idx), pl.BlockSpec((tm, QK_DIM), tab_idx)],
        out_specs=[lay.row_spec(Q_DIM), lay.row_spec(KVD_DIM), lay.row_spec(KVD_DIM)],
        out_shape=[jax.ShapeDtypeStruct((lay.N, Q_DIM), BF16),
                   jax.ShapeDtypeStruct((lay.N, KVD_DIM), BF16),
                   jax.ShapeDtypeStruct((lay.N, KVD_DIM), BF16)],
        compiler_params=_cparams(("parallel",)),
        name="attn_qkv",
    )(h, mod, g.reshape(1, D), w, b, cos, sin)


def _attn_heads(q, kk, vv, mask, sink_ref, o_ref):
    R = q.shape[0]
    G = N_HEADS // N_KV
    TILE = 2 * HEAD_DIM
    lo = lax.broadcasted_iota(jnp.int32, (1, TILE), 1) < HEAD_DIM
    order = (0, 2, 1, 3)
    scores = []
    for g in range(N_KV):
        pieces = []
        for j in order:
            hq = g * G + j
            qt = q[:, (hq // 2) * TILE:(hq // 2 + 1) * TILE]
            pieces.append(jnp.where(lo if hq % 2 == 0 else ~lo, qt, jnp.zeros_like(qt)))
        qs = jnp.concatenate(pieces, axis=0)
        scores.append(lax.dot_general(qs, kk[:, g * TILE:(g + 1) * TILE], (((1,), (1,)), ((), ())),
                                      preferred_element_type=F32))
    probs, esinks = [], []
    for g in range(N_KV):
        s = scores[g]
        if mask is not None:
            s = jnp.where(mask[None], s.reshape(G, R, -1), -jnp.inf).reshape(G * R, -1)
        sink = jnp.concatenate([jnp.full((R, 1), sink_ref[g * G + j], F32) for j in order], axis=0)
        m = jnp.maximum(jnp.max(s, axis=-1, keepdims=True), sink)
        probs.append(jnp.exp((s - m).astype(BF16)))
        esinks.append(jnp.exp(sink - m))
    for g in range(N_KV):
        vd = vv[:, g * TILE:(g + 1) * TILE]
        one = jnp.ones_like(vd)
        p, es = probs[g], esinks[g]
        half = (G // 2) * R
        oe = jnp.dot(p[:half], jnp.where(lo, vd, one), preferred_element_type=F32)
        oo = jnp.dot(p[half:], jnp.where(lo, one, vd), preferred_element_type=F32)
        re = oe / (pltpu.roll(oe, HEAD_DIM, 1) + es[:half])
        ro = oo / (pltpu.roll(oo, HEAD_DIM, 1) + es[half:])
        for t in range(G // 2):
            tile = jnp.where(lo, re[t * R:(t + 1) * R], ro[t * R:(t + 1) * R])
            c = g * (G // 2) + t
            o_ref[:, c * TILE:(c + 1) * TILE] = tile.astype(o_ref.dtype)


def _attn_kernel(sink_ref, q_ref, kp_ref, kc_ref, kn_ref, vp_ref, vc_ref, vn_ref, kx_ref, vx_ref, o_ref, *, T, L):
    i = pl.program_id(1)
    nqb = T // QB

    @pl.when(i < nqb)
    def _():
        kk = jnp.concatenate([kx_ref[...], kp_ref[...], kc_ref[...], kn_ref[...]], axis=0)
        vv = jnp.concatenate([vx_ref[...], vp_ref[...], vc_ref[...], vn_ref[...]], axis=0)
        S = L + 3 * QB
        col = lax.broadcasted_iota(jnp.int32, (QB, S), 1)
        row = lax.broadcasted_iota(jnp.int32, (QB, S), 0)
        rel = col - L - QB - row
        kpos = i * QB - QB + (col - L)
        local_ok = (jnp.abs(rel) <= WINDOW) & (kpos >= 0) & (kpos < T)
        mask = (col < L) | local_ok
        _attn_heads(q_ref[...], kk, vv, mask, sink_ref, o_ref)

    @pl.when(i >= nqb)
    def _():
        _attn_heads(q_ref[...], kx_ref[...], vx_ref[...], None, sink_ref, o_ref)


def _attention(lay, q, k, v, sinks, need_ctx):
    B, T, L = lay.B, lay.T, lay.L
    nqb, ncb = T // QB, L // QB
    cq = lay.rows_x // QB
    cb = lay.rows_x // L
    smem = pl.BlockSpec(memory_space=pltpu.SMEM)

    def cur(b, i):
        return (jnp.where(i < nqb, b * nqb + i, cq + b * ncb + (i - nqb)), 0)

    def prev(b, i):
        return (b * nqb + jnp.clip(i - 1, 0, nqb - 1), 0)

    def mid(b, i):
        return (b * nqb + jnp.minimum(i, nqb - 1), 0)

    def nxt(b, i):
        return (b * nqb + jnp.minimum(i + 1, nqb - 1), 0)

    def ctxb(b, i):
        return (cb + b, 0)

    kv_specs = [pl.BlockSpec((QB, KVD_DIM), prev), pl.BlockSpec((QB, KVD_DIM), mid), pl.BlockSpec((QB, KVD_DIM), nxt)]
    n_out = lay.N if need_ctx else lay.rows_x
    return pl.pallas_call(
        functools.partial(_attn_kernel, T=T, L=L),
        grid=(B, nqb + (ncb if need_ctx else 0)),
        in_specs=[smem, pl.BlockSpec((QB, Q_DIM), cur)] + kv_specs + kv_specs
                 + [pl.BlockSpec((L, KVD_DIM), ctxb), pl.BlockSpec((L, KVD_DIM), ctxb)],
        out_specs=pl.BlockSpec((QB, Q_DIM), cur),
        out_shape=jax.ShapeDtypeStruct((n_out, Q_DIM), BF16),
        compiler_params=_cparams(("parallel", "arbitrary")),
        name="attention",
    )(sinks, q, k, k, k, v, v, v, k, v)


def _linres_kernel(h_ref, x_ref, mod_ref, w_ref, b_ref, o_ref, *, gate_slot):
    gate = mod_ref[0][:, gate_slot * D:(gate_slot + 1) * D]
    y = jnp.dot(x_ref[...], w_ref[...], preferred_element_type=F32) + b_ref[...]
    o_ref[...] = h_ref[...] + gate * y


def _linear_residual(lay, h, x, mod, w, b, gate_slot, n_rows):
    return pl.pallas_call(
        functools.partial(_linres_kernel, gate_slot=gate_slot),
        grid=(n_rows // lay.tm,),
        in_specs=[lay.row_spec(), lay.row_spec(x.shape[1]), lay.mod_spec(),
                  _const_spec(w.shape), _const_spec((1, D))],
        out_specs=lay.row_spec(),
        out_shape=jax.ShapeDtypeStruct((n_rows, D), F32),
        compiler_params=_cparams(("parallel",)),
        name="linear_residual",
    )(h, x, mod, w.astype(BF16), b.reshape(1, D))


def _fill_ext(ext_ref, h_ref, hp_ref, hn_ref, g, shift, scale, j, n, tm):
    a = _norm_mod(h_ref[...], g, shift, scale)
    ap = _norm_mod(hp_ref[...], g, shift, scale)
    an = _norm_mod(hn_ref[...], g, shift, scale)
    ext_ref[0:HALO, :] = jnp.where(j == 0, 0.0, ap)
    ext_ref[HALO:HALO + tm, :] = a
    ext_ref[HALO + tm:2 * HALO + tm, :] = jnp.where(j == n - 1, 0.0, an)
    return a


def _pool_kernel(h_ref, hp_ref, hn_ref, mod_ref, g_ref, w_ref, b_ref, ls_ref, o_ref, ext_ref, *, lay):
    tm = lay.tm
    i = pl.program_id(0)
    j, n = lay.seq_tile(i)
    m = mod_ref[0]
    a = _fill_ext(ext_ref, h_ref, hp_ref, hn_ref, g_ref[...], m[:, 0:D], m[:, D:2 * D], j, n, tm)
    t = j * tm + lax.broadcasted_iota(jnp.int32, (tm, 1), 0)
    n_tok = n * tm
    outs = []
    for gi, size in enumerate(POOL_SIZES):
        sl = slice(gi * POOL_GW, (gi + 1) * POOL_GW)
        acc = ext_ref[HALO - size // 2:HALO - size // 2 + tm, sl]
        for o in range(-size // 2 + 1, size - size // 2):
            acc = acc + ext_ref[HALO + o:HALO + o + tm, sl]
        lo = jnp.maximum(t - size // 2, 0)
        hi = jnp.minimum(t + size - size // 2, n_tok)
        d = acc / (hi - lo).astype(F32) - a[:, sl]
        outs.append(jnp.dot(d.astype(BF16), w_ref[gi], preferred_element_type=F32) + b_ref[gi])
    y = jnp.concatenate(outs, axis=-1) * ls_ref[...]
    o_ref[...] = h_ref[...] + m[:, 2 * D:3 * D] * y


def _pool_mixer(lay, h, mod, g, w, b, ls, n_rows):
    prev, nxt = lay.halo_specs()
    return pl.pallas_call(
        functools.partial(_pool_kernel, lay=lay),
        grid=(n_rows // lay.tm,),
        in_specs=[lay.row_spec(), prev, nxt, lay.mod_spec(), _const_spec((1, D)),
                  _const_spec(w.shape), _const_spec((4, 1, POOL_GW)), _const_spec((1, D))],
        out_specs=lay.row_spec(),
        out_shape=jax.ShapeDtypeStruct((n_rows, D), F32),
        scratch_shapes=[pltpu.VMEM((lay.tm + 2 * HALO, D), F32)],
        compiler_params=_cparams(("parallel",)),
        name="pool_mixer",
    )(h, h, h, mod, g.reshape(1, D), w.astype(BF16), b.reshape(4, 1, POOL_GW), ls.reshape(1, D))


def _seg_sum(x, bd, split=True):
    outs = []
    for q in range(D // RW_GROUP):
        xg = x[:, q * RW_GROUP:(q + 1) * RW_GROUP]
        hi = xg.astype(BF16)
        acc = jnp.dot(hi, bd, preferred_element_type=F32)
        if split:
            acc = acc + jnp.dot((xg - hi.astype(F32)).astype(BF16), bd, preferred_element_type=F32)
        outs.append(acc)
    return jnp.concatenate(outs, axis=-1)


def _rwkv_proj_kernel(h_ref, hp_ref, hn_ref, mod_ref, g_ref, mix_ref, wr_ref, wk_ref, wv_ref, g1_ref, g2_ref,
                      w1_ref, w2_ref, w0_ref, a1_ref, a2_ref, a0_ref, kk_ref, ka_ref, rk_ref, bd_ref,
                      r_out, v_out, kn_out, g_out, bonus_out, lw0_out, k0_out, b0_out, lw1_out, k1_out, b1_out,
                      ext_ref, *, lay):
    tm = lay.tm
    i = pl.program_id(0)
    j, n = lay.seq_tile(i)
    m = mod_ref[0]
    a = _fill_ext(ext_ref, h_ref, hp_ref, hn_ref, g_ref[...], m[:, 0:D], m[:, D:2 * D], j, n, tm)
    prev = ext_ref[HALO - 1:HALO - 1 + tm, :]
    nxt = ext_ref[HALO + 1:HALO + 1 + tm, :]
    xx = 0.5 * (prev + nxt) - a
    x_r, x_w, x_k, x_v, x_a, x_g = [(a + xx * mix_ref[q:q + 1, :]).astype(BF16) for q in range(6)]
    bd = bd_ref[...]
    r = jnp.dot(x_r, wr_ref[...], preferred_element_type=F32)
    k = jnp.dot(x_k, wk_ref[...], preferred_element_type=F32)
    v = jnp.dot(x_v, wv_ref[...], preferred_element_type=F32)
    gl = jnp.dot(x_g, g1_ref[...], preferred_element_type=F32)
    g = jnp.dot(jax.nn.sigmoid(gl).astype(BF16), g2_ref[...], preferred_element_type=F32)
    kk = k * kk_ref[...]
    kn = kk / jnp.maximum(jnp.sqrt(_seg_sum(kk * kk, bd)), 1e-12)
    tw = jnp.tanh(jnp.dot(x_w, w1_ref[...], preferred_element_type=F32))
    aa = jnp.dot(x_a, a1_ref[...], preferred_element_type=F32)
    lane = lax.broadcasted_iota(jnp.int32, (1, tw.shape[1]), 1)
    r_out[...] = r.astype(r_out.dtype)
    v_out[...] = v.astype(v_out.dtype)
    kn_out[...] = kn.astype(kn_out.dtype)
    g_out[...] = g.astype(g_out.dtype)
    rk = rk_ref[...]
    bonus = jnp.zeros_like(v)
    outs = ((lw0_out, k0_out, b0_out), (lw1_out, k1_out, b1_out))
    half = tw.shape[1] // 2
    for d in range(2):
        sel = (lane >= d * half) & (lane < (d + 1) * half)
        wl = w0_ref[d:d + 1, :] + jnp.dot(jnp.where(sel, tw, 0.0).astype(BF16), w2_ref[...], preferred_element_type=F32)
        asig = jax.nn.sigmoid(a0_ref[d:d + 1, :] + jnp.dot(jnp.where(sel, aa, 0.0).astype(BF16), a2_ref[...],
                                                           preferred_element_type=F32))
        k_d = k * (1.0 + (asig - 1.0) * ka_ref[...])
        lw_o, k_o, b_o = outs[d]
        lw_o[...] = -float(np.exp(-0.5)) * jax.nn.sigmoid(wl)
        k_o[...] = k_d.astype(k_o.dtype)
        b_o[...] = (kn * asig).astype(b_o.dtype)
        bonus = bonus + _seg_sum(r * k_d * rk, bd, split=False) * v
    bonus_out[...] = bonus.astype(bonus_out.dtype)


def _block_ones():
    idx = np.arange(RW_GROUP) // RW_HEAD
    return jnp.asarray(idx[:, None] == idx[None, :], dtype=BF16)


def _rwkv_proj(lay, h, mod, g, p):
    prev, nxt = lay.halo_specs()
    cat = lambda w: jnp.concatenate([w[0], w[1]], axis=-1).astype(BF16)
    stack = lambda w: jnp.concatenate([w[0], w[1]], axis=0).astype(BF16)
    args = [h, h, h, mod, g.reshape(1, D), p['mix'],
            p['w_rkv'][0].astype(BF16), p['w_rkv'][1].astype(BF16), p['w_rkv'][2].astype(BF16),
            p['g1'].astype(BF16), p['g2'].astype(BF16),
            cat(p['w1']), stack(p['w2']), p['w0'], cat(p['a1']), stack(p['a2']), p['a0'],
            p['k_k'].reshape(1, D), p['k_a'].reshape(1, D), p['r_k'].reshape(1, D), _block_ones()]
    in_specs = [lay.row_spec(), prev, nxt, lay.mod_spec()] + [_const_spec(x.shape) for x in args[4:]]
    dtypes = [BF16] * 5 + [F32, BF16, BF16] * 2
    return pl.pallas_call(
        functools.partial(_rwkv_proj_kernel, lay=lay),
        grid=(lay.n_t,),
        in_specs=in_specs,
        out_specs=[lay.row_spec()] * len(dtypes),
        out_shape=[jax.ShapeDtypeStruct((lay.N, D), dt) for dt in dtypes],
        scratch_shapes=[pltpu.VMEM((lay.tm + 2 * HALO, D), F32)],
        compiler_params=_cparams(("parallel",)),
        name="rwkv_proj",
    )(*args)


def _rwkv_scan_kernel(rf, vf, nf, lwf, kf, bf, rb, vb, nb, lwb, kb, bb, yf_out, yb_out, z_ref):
    C, G = RW_CHUNK, RW_GROUP
    c = pl.program_id(1)

    @pl.when(c == 0)
    def _():
        z_ref[...] = jnp.zeros_like(z_ref)

    trow = lax.broadcasted_iota(jnp.int32, (C, C), 0)
    tcol = lax.broadcasted_iota(jnp.int32, (C, C), 1)
    ti = lax.broadcasted_iota(jnp.int32, (C, G), 0)
    ii = lax.broadcasted_iota(jnp.int32, (C, G), 1) % C
    brow = lax.broadcasted_iota(jnp.int32, (G, G), 0)
    bcol = lax.broadcasted_iota(jnp.int32, (G, G), 1)
    m_bd_t = (brow // C) == (bcol // RW_HEAD)
    m_bd_f = (brow // RW_HEAD) == (bcol // RW_HEAD)
    eye_g = brow == bcol

    def bd(x):
        return jnp.where(m_bd_t, jnp.concatenate([x] * (G // C), axis=0), 0.0).astype(BF16)

    def mm(a, b):
        return jnp.dot(a, b, preferred_element_type=F32)

    dirs = ((rf, vf, nf, lwf, kf, bf, yf_out, False), (rb, vb, nb, lwb, kb, bb, yb_out, True))
    ch = []
    for d, (r_ref, v_ref, n_ref, lw_ref, k_ref, b_ref, y_out, rev) in enumerate(dirs):
        tri = (tcol >= trow) if rev else (tcol <= trow)
        strict = (ii > ti) if rev else (ii < ti)
        incl = (ii >= ti) if rev else (ii <= ti)
        lw = lw_ref[...]
        tri_b = jnp.where(tri, 1.0, 0.0).astype(BF16)
        lw_hi = lw.astype(BF16)
        rest = lw - lw_hi.astype(F32)
        lw_mid = rest.astype(BF16)
        lw_lo = (rest - lw_mid.astype(F32)).astype(BF16)
        lc = mm(tri_b, lw_hi) + (mm(tri_b, lw_mid) + mm(tri_b, lw_lo))
        ltot = lc[0:1, :] if rev else lc[C - 1:C, :]
        e_lc = jnp.exp(lc)
        ie_lc = jnp.exp(-lc)
        e_le = jnp.exp(lc - lw)
        e_bar = jnp.exp(ltot - lc)
        e_tot = jnp.exp(ltot)
        kvec, bvec = k_ref[...], b_ref[...]
        a_t = -n_ref[...] * e_le
        r_t = r_ref[...] * e_lc
        b_t = bvec * ie_lc
        k_t = kvec * ie_lc
        b_b = bvec * e_bar
        k_b = kvec * e_bar
        vall = v_ref[...]
        for g in range(D // G):
            sl = slice(g * G, (g + 1) * G)
            ch.append(dict(
                d=d, g=g, sl=sl, y_out=y_out, strict=strict, incl=incl, v=vall[:, sl], e_tot=e_tot[:, sl],
                ar=jnp.concatenate([a_t[:, sl], r_t[:, sl]], axis=0).astype(BF16),
                bk=jnp.concatenate([bd(b_t[:, sl]), bd(k_t[:, sl])], axis=0),
                lhs_t=jnp.concatenate([b_b[:, sl], k_b[:, sl]], axis=0).astype(BF16)))

    for q in ch:
        aa = lax.dot_general(q['ar'], q['bk'], (((1,), (1,)), ((), ())), preferred_element_type=F32)
        q['a_ab'] = jnp.where(q['strict'], aa[:C, :G], 0.0)
        q['a_ak'] = jnp.where(q['strict'], aa[:C, G:], 0.0)
        q['m_rb'] = jnp.where(q['incl'], aa[C:, :G], 0.0).astype(BF16)
        q['m_rk'] = jnp.where(q['incl'], aa[C:, G:], 0.0)
    for q in ch:
        q['x'] = jnp.where(ii == ti, 1.0, 0.0) + q['a_ab']
        q['p'] = mm(q['a_ab'].astype(BF16), bd(q['a_ab']))
    for q in ch:
        q['wv'] = mm(jnp.concatenate([q['a_ak'], q['m_rk']], axis=0).astype(BF16), bd(q['v']))
        q['z'] = z_ref[q['d'], q['g']]
        q['az'] = mm(q['ar'], q['z'].astype(BF16))
    n_sq = int(np.log2(C)) - 1
    for s in range(n_sq):
        last = s == n_sq - 1
        for q in ch:
            lhs = q['x'] if last else jnp.concatenate([q['x'], q['p']], axis=0)
            xp = mm(lhs.astype(BF16), bd(q['p']))
            q['x'] = q['x'] + xp[:C]
            if not last:
                q['p'] = xp[C:]
    for q in ch:
        q['u'] = mm(q['x'].astype(BF16), bd(q['az'][:C] + q['wv'][:C]))
    for q in ch:
        y = q['az'][C:] + q['wv'][C:] + mm(q['m_rb'], bd(q['u']))
        q['y_out'][:, q['sl']] = y
        rhs_t = jnp.concatenate([q['u'], q['v']], axis=0).astype(BF16)
        zu = lax.dot_general(q['lhs_t'], rhs_t, (((0,), (0,)), ((), ())), preferred_element_type=F32)
        decay = jnp.sum(jnp.where(eye_g, jnp.broadcast_to(q['e_tot'], (G, G)), 0.0), axis=1, keepdims=True)
        z_ref[q['d'], q['g']] = jnp.where(m_bd_f, decay * q['z'] + zu, 0.0)


def _rwkv_scan(lay, r, v, kn, lw0, k0, b0, lw1, k1, b1):
    B, T, L = lay.B, lay.T, lay.L
    C = RW_CHUNK
    cc, cx = L // C, T // C
    nc = cc + cx
    ctx0 = lay.rows_x // C

    def fwd(b, c):
        return (jnp.where(c < cc, ctx0 + b * cc + c, b * cx + c - cc), 0)

    def bwd(b, c):
        return (jnp.where(c < cc, ctx0 + b * cc + (cc - 1 - c), b * cx + (nc - 1 - c)), 0)

    sf, sb = pl.BlockSpec((C, D), fwd), pl.BlockSpec((C, D), bwd)
    return pl.pallas_call(
        _rwkv_scan_kernel,
        grid=(B, nc),
        in_specs=[sf] * 6 + [sb] * 6,
        out_specs=[sf, sb],
        out_shape=[jax.ShapeDtypeStruct((lay.N, D), F32)] * 2,
        scratch_shapes=[pltpu.VMEM((2, D // RW_GROUP, RW_GROUP, RW_GROUP), F32)],
        compiler_params=_cparams(("parallel", "arbitrary")),
        name="rwkv_scan",
    )(r, v, kn, lw0, k0, b0, r, v, kn, lw1, k1, b1)


def _rwkv_out_kernel(h_ref, yf_ref, yb_ref, bonus_ref, g_ref, mod_ref, lnw_ref, lnb_ref, wo_ref, bd_ref, o_ref):
    bd = bd_ref[...]
    y = yf_ref[...] + yb_ref[...]
    mu = _seg_sum(y, bd) * (1.0 / RW_HEAD)
    yc = y - mu
    var = _seg_sum(yc * yc, bd) * (1.0 / RW_HEAD)
    yn = yc * lax.rsqrt(var + GN_EPS) * lnw_ref[...] + lnb_ref[...]
    out = (yn + bonus_ref[...]) * g_ref[...]
    res = jnp.dot(out.astype(BF16), wo_ref[...], preferred_element_type=F32)
    o_ref[...] = h_ref[...] + mod_ref[0][:, 2 * D:3 * D] * res


def _rwkv_out(lay, h, yf, yb, bonus, g, mod, ln_w, ln_b, w_o):
    return pl.pallas_call(
        _rwkv_out_kernel,
        grid=(lay.n_t,),
        in_specs=[lay.row_spec()] * 5 + [lay.mod_spec(), _const_spec((1, D)), _const_spec((1, D)),
                                          _const_spec((D, D)), _const_spec((RW_GROUP, RW_GROUP))],
        out_specs=lay.row_spec(),
        out_shape=jax.ShapeDtypeStruct((lay.N, D), F32),
        compiler_params=_cparams(("parallel",)),
        name="rwkv_out",
    )(h, yf, yb, bonus, g, mod, ln_w.reshape(1, D), ln_b.reshape(1, D), w_o.astype(BF16), _block_ones())


RUN_ALIGN = 8
DP = D // 2
_HI_MASK = 0xFFFF0000


def _pack_rows(x):
    lo = pltpu.bitcast(x[:, :DP], jnp.uint32)
    hi = pltpu.bitcast(x[:, DP:], jnp.uint32)
    return (lo >> 16) | (hi & jnp.uint32(_HI_MASK))


def _unpack_rows(u):
    lo = pltpu.bitcast(u << 16, F32)
    hi = pltpu.bitcast(u & jnp.uint32(_HI_MASK), F32)
    return lo.astype(BF16), hi.astype(BF16)


def _router_kernel(h_ref, mod_ref, g_ref, wr_ref, br_ref, upper_ref, gate_ref, lpos_ref, cnt_ref):
    m = mod_ref[0]
    f = _norm_mod(h_ref[...], g_ref[...], m[:, 3 * D:4 * D], m[:, 4 * D:5 * D])
    def split(x):
        hi = x.astype(BF16)
        return hi, (x - hi.astype(F32)).astype(BF16)

    def nt(a, b):
        return lax.dot_general(a, b, (((1,), (1,)), ((), ())), preferred_element_type=F32)

    (w_hi, w_lo), (f_hi, f_lo) = split(wr_ref[...]), split(f)
    logits = nt(w_hi, f_hi) + (nt(w_hi, f_lo) + nt(w_lo, f_hi)) + br_ref[...]
    erow = lax.broadcasted_iota(jnp.int32, logits.shape, 0)
    tops, hots = [], []
    l = logits
    for _ in range(TOP_K):
        mx = jnp.max(l, axis=0, keepdims=True)
        ix = jnp.min(jnp.where(l == mx, erow, N_EXPERTS), axis=0, keepdims=True)
        tops.append(mx)
        hots.append(erow == ix)
        l = jnp.where(erow == ix, -jnp.inf, l)
    es = [jnp.exp(t - tops[0]) for t in tops]
    tot = es[0] + es[1] + es[2] + es[3]
    gate_ref[...] = jnp.concatenate([e / tot for e in es], axis=0)

    ohs = [jnp.where(hh, 1.0, 0.0) for hh in hots]
    cnts = [jnp.sum(oh, axis=1, keepdims=True) for oh in ohs]
    cnt = cnts[0] + cnts[1] + cnts[2] + cnts[3]
    run = jnp.ceil(cnt * (1.0 / RUN_ALIGN)) * RUN_ALIGN
    ee = lax.broadcasted_iota(jnp.int32, (N_EXPERTS, N_EXPERTS), 0)
    ec = lax.broadcasted_iota(jnp.int32, (N_EXPERTS, N_EXPERTS), 1)
    run_row = jnp.sum(jnp.where(ee == ec, jnp.broadcast_to(run, (N_EXPERTS, N_EXPERTS)), 0.0), axis=0, keepdims=True)
    offs = jnp.sum(jnp.where(ec < ee, jnp.broadcast_to(run_row, (N_EXPERTS, N_EXPERTS)), 0.0), axis=1, keepdims=True)
    base = offs
    lpos = []
    for k in range(TOP_K):
        before = jnp.dot(ohs[k].astype(BF16), upper_ref[...], preferred_element_type=F32)
        lpos.append(jnp.sum(ohs[k] * (before + base), axis=0, keepdims=True))
        base = base + cnts[k]
    lpos_ref[...] = jnp.concatenate(lpos, axis=0).astype(jnp.int32)
    cnt_ref[0] = cnt.astype(jnp.int32)


def _moe_router(lay, h, mod, g, w_router, b_router, n_rows):
    tm = lay.tm
    nt = n_rows // tm
    upper = jnp.asarray(np.triu(np.ones((tm, tm), np.float32), 1), dtype=BF16)
    return pl.pallas_call(
        _router_kernel,
        grid=(nt,),
        in_specs=[lay.row_spec(), lay.mod_spec(), _const_spec((1, D)),
                  _const_spec((N_EXPERTS, D)), _const_spec((N_EXPERTS, 1)), _const_spec((tm, tm))],
        out_specs=[pl.BlockSpec((TOP_K, tm), lambda i: (0, i)),
                   pl.BlockSpec((TOP_K, tm), lambda i: (0, i)),
                   pl.BlockSpec((1, N_EXPERTS, 1), lambda i: (i, 0, 0))],
        out_shape=[jax.ShapeDtypeStruct((TOP_K, n_rows), F32),
                   jax.ShapeDtypeStruct((TOP_K, n_rows), jnp.int32),
                   jax.ShapeDtypeStruct((nt, N_EXPERTS, 1), jnp.int32)],
        compiler_params=_cparams(("parallel",)),
        name="moe_router",
    )(h, mod, g.reshape(1, D), w_router.T, b_router.reshape(N_EXPERTS, 1), upper)


def _comp_rows(tm):
    return tm * TOP_K + N_EXPERTS * RUN_ALIGN


def _moe_plan(cnt, n, tm):
    nt = n // tm
    def excl_sum(a, axis):
        k = a.shape[axis]
        before = np.arange(k)[:, None] > np.arange(k)[None, :]
        if axis == 0:
            return jnp.sum(jnp.where(before[:, :, None], a[None, :, :], 0), axis=1)
        return jnp.sum(jnp.where(before[None, :, :], a[:, None, :], 0), axis=2)

    run = (cnt + RUN_ALIGN - 1) // RUN_ALIGN * RUN_ALIGN
    tot = jnp.sum(run, axis=0)
    padded = (tot + MOE_BLOCK - 1) // MOE_BLOCK * MOE_BLOCK
    pad_start = excl_sum(padded[None, :], 1)[0]
    pad_end = pad_start + padded
    base = pad_start[None, :] + excl_sum(run, 0)
    rows_max = n * TOP_K + nt * N_EXPERTS * (RUN_ALIGN - 1) + N_EXPERTS * MOE_BLOCK
    n_blocks = -(-rows_max // MOE_BLOCK)
    rows = n_blocks * MOE_BLOCK
    blk_row = jnp.arange(n_blocks, dtype=jnp.int32) * MOE_BLOCK
    block_expert = jnp.minimum(jnp.sum((pad_end[None, :] <= blk_row[:, None]).astype(jnp.int32), axis=1),
                               N_EXPERTS - 1).astype(jnp.int32)
    spare_rows = 2 * _comp_rows(tm)
    fill_base = jnp.concatenate([pad_start + tot, pad_end[-1:], jnp.full((1,), rows)]).astype(jnp.int32)
    fill_cnt = (jnp.concatenate([padded - tot, rows - pad_end[-1:], jnp.full((1,), spare_rows)])
                // RUN_ALIGN).astype(jnp.int32)
    n_used = (pad_end[-1:] // MOE_BLOCK).astype(jnp.int32)
    cr = _comp_rows(tm)
    ngrp = cr // RUN_ALIGN
    run_g = run // RUN_ALIGN
    goffs = excl_sum(run_g, 1)
    gend = goffs + run_g
    ng = gend[:, -1]
    j = jnp.arange(ngrp, dtype=jnp.int32)
    e_of = jnp.minimum(jnp.sum((gend[:, None, :] <= j[None, :, None]).astype(jnp.int32), axis=2), N_EXPERTS - 1)
    oh = e_of[:, :, None] == jnp.arange(N_EXPERTS, dtype=jnp.int32)[None, None, :]
    pick = lambda a: jnp.sum(jnp.where(oh, a[:, None, :], 0), axis=2)
    row = pick(base) + RUN_ALIGN * (j[None, :] - pick(goffs))
    live = j[None, :] < ng[:, None]
    spare = rows + (jnp.arange(nt, dtype=jnp.int32)[:, None] % 2) * cr + RUN_ALIGN * j[None, :]
    put_rows = jnp.where(live, row, spare).astype(jnp.int32).reshape(nt, 1, ngrp)
    get_rows = jnp.where(live, row, 0).astype(jnp.int32).reshape(nt, 1, ngrp)
    return put_rows, get_rows, block_expert, fill_base, fill_cnt, n_used, n_blocks


def _group_copies(n_groups, src_of, dst_of, sem, wait):
    def body(j, carry):
        cp = pltpu.make_async_copy(src_of(j), dst_of(j), sem)
        cp.wait() if wait else cp.start()
        return carry

    lax.fori_loop(0, n_groups, body, 0, unroll=8)


def _rows_at(ref, start):
    return ref.at[pl.ds(pl.multiple_of(start, RUN_ALIGN), RUN_ALIGN), :]


def _compact_kernel(fill_base_ref, fill_cnt_ref, put_ref, pput_ref,
                    h_ref, mod_ref, g_ref, lpos_ref, xs_hbm, comp_ref, zero_ref, sems):
    i = pl.program_id(0)
    s = i % 2
    m = mod_ref[0]
    f = _norm_mod(h_ref[...], g_ref[...], m[:, 3 * D:4 * D], m[:, 4 * D:5 * D]).astype(BF16)
    cr, tm = comp_ref.shape[1], h_ref.shape[0]
    q = lax.broadcasted_iota(jnp.int32, (cr, tm), 0).astype(F32)
    lp = lpos_ref[...].astype(F32)
    miss = (lp[0:1, :] - q) * (lp[1:2, :] - q) * ((lp[2:3, :] - q) * (lp[3:4, :] - q))
    comp_ref[s] = _pack_rows(jnp.dot(jnp.where(miss == 0.0, 1.0, 0.0).astype(BF16), f, preferred_element_type=F32))

    def copies(slot, rows_ref, wait):
        src = lambda j: _rows_at(comp_ref.at[slot], j * RUN_ALIGN)
        dst = lambda j: _rows_at(xs_hbm, rows_ref[0, 0, j])
        _group_copies(cr // RUN_ALIGN, src, dst, sems.at[slot], wait)

    @pl.when(i == 0)
    def _():
        zero_ref[...] = jnp.zeros_like(zero_ref)
        for e in range(fill_cnt_ref.shape[0]):
            def fill(c, carry, wait=False, e=e):
                cp = pltpu.make_async_copy(zero_ref, _rows_at(xs_hbm, fill_base_ref[e] + c * RUN_ALIGN), sems.at[0])
                cp.wait() if wait else cp.start()
                return carry

            lax.fori_loop(0, fill_cnt_ref[e], fill, 0)
            lax.fori_loop(0, fill_cnt_ref[e], functools.partial(fill, wait=True), 0)

    def wait_slot(slot):
        pltpu.make_async_copy(comp_ref.at[slot], xs_hbm.at[pl.ds(0, cr), :], sems.at[slot]).wait()

    del pput_ref

    @pl.when(i > 0)
    def _():
        wait_slot(1 - s)

    copies(s, put_ref, False)

    @pl.when(i == pl.num_programs(0) - 1)
    def _():
        wait_slot(s)


def _group_table(shift, nt, ngrp):
    def idx(i, *_):
        return (jnp.clip(i + shift, 0, nt - 1), 0, 0)

    return pl.BlockSpec((1, 1, ngrp), idx, memory_space=pltpu.SMEM)


def _moe_compact(lay, h, mod, g, lpos, put_rows, fill_base, fill_cnt, n_rows, n_blocks):
    tm = lay.tm
    nt, ngrp = n_rows // tm, _comp_rows(tm) // RUN_ALIGN
    grid_spec = pltpu.PrefetchScalarGridSpec(
        num_scalar_prefetch=2,
        grid=(nt,),
        in_specs=[_group_table(0, nt, ngrp), _group_table(-1, nt, ngrp)]
                 + [pl.BlockSpec((tm, D), lambda i, *_: (i, 0)),
                    pl.BlockSpec((1, 1, 6 * D), lambda i, *_: (lay.mod_idx(i), 0, 0)),
                    pl.BlockSpec((1, D), lambda i, *_: (0, 0)),
                    pl.BlockSpec((TOP_K, tm), lambda i, *_: (0, i))],
        out_specs=pl.BlockSpec(memory_space=pl.ANY),
        scratch_shapes=[pltpu.VMEM((2, _comp_rows(tm), DP), jnp.uint32), pltpu.VMEM((RUN_ALIGN, DP), jnp.uint32),
                        pltpu.SemaphoreType.DMA((2,))],
    )
    return pl.pallas_call(
        _compact_kernel,
        grid_spec=grid_spec,
        out_shape=jax.ShapeDtypeStruct((n_blocks * MOE_BLOCK + 2 * _comp_rows(tm), DP), jnp.uint32),
        compiler_params=_cparams(("arbitrary",)),
        name="moe_compact",
    )(fill_base, fill_cnt, put_rows, put_rows, h, mod, g.reshape(1, D), lpos)


def _expert_kernel(be_ref, nu_ref, x_ref, w1_ref, b1_ref, w2_ref, b2_ref, o_ref, w1b_ref, w2b_ref):
    i = pl.program_id(0)

    @pl.when((i == 0) | (be_ref[i] != be_ref[jnp.maximum(i - 1, 0)]))
    def _():
        w1b_ref[...] = w1_ref[0, 0].astype(BF16)
        w2b_ref[...] = w2_ref[0, 0].astype(BF16)

    @pl.when(i < nu_ref[0])
    def _():
        x = jnp.concatenate(_unpack_rows(x_ref[...]), axis=1)
        hcat = jnp.dot(x, w1b_ref[...], preferred_element_type=F32) + b1_ref[0]
        h_glu = jnp.minimum(hcat[:, :D_FF], SWIGLU_LIMIT)
        h_lin = jnp.clip(hcat[:, D_FF:], -SWIGLU_LIMIT, SWIGLU_LIMIT)
        act = h_glu * jax.nn.sigmoid(SWIGLU_ALPHA * h_glu) * (h_lin + 1.0)
        y = jnp.dot(act.astype(BF16), w2b_ref[...], preferred_element_type=F32) + b2_ref[0]
        o_ref[...] = _pack_rows(y.astype(BF16).astype(F32))

    @pl.when(i >= nu_ref[0])
    def _():
        o_ref[...] = jnp.zeros_like(o_ref)


def _moe_experts(xs, block_expert, n_used, n_blocks, layer, w1_all, b1, w2_all, b2):
    def xrow(i, be, nu):
        return (jnp.minimum(i, nu[0] - 1), 0)

    grid_spec = pltpu.PrefetchScalarGridSpec(
        num_scalar_prefetch=2,
        grid=(n_blocks,),
        in_specs=[pl.BlockSpec((MOE_BLOCK, DP), xrow),
                  pl.BlockSpec((1, 1, D, 2 * D_FF), lambda i, be, nu: (layer, be[i], 0, 0)),
                  pl.BlockSpec((1, 1, 2 * D_FF), lambda i, be, nu: (be[i], 0, 0)),
                  pl.BlockSpec((1, 1, D_FF, D), lambda i, be, nu: (layer, be[i], 0, 0)),
                  pl.BlockSpec((1, 1, D), lambda i, be, nu: (be[i], 0, 0))],
        out_specs=pl.BlockSpec((MOE_BLOCK, DP), lambda i, be, nu: (i, 0)),
        scratch_shapes=[pltpu.VMEM((D, 2 * D_FF), BF16), pltpu.VMEM((D_FF, D), BF16)],
    )
    return pl.pallas_call(
        _expert_kernel,
        grid_spec=grid_spec,
        out_shape=jax.ShapeDtypeStruct((n_blocks * MOE_BLOCK, DP), jnp.uint32),
        compiler_params=_cparams(("arbitrary",)),
        name="moe_experts",
    )(block_expert, n_used, xs, w1_all, b1.reshape(N_EXPERTS, 1, 2 * D_FF), w2_all, b2.reshape(N_EXPERTS, 1, D))


def _combine_kernel(get_ref, nget_ref, h_ref, gate_ref, lpos_ref, mod_ref, y_hbm, *rest, final):
    fg_ref = rest[0] if final else None
    o_ref, comp_ref, sems = rest[-3:]
    i = pl.program_id(0)
    s = i % 2
    cr, tm = comp_ref.shape[1], h_ref.shape[0]

    def copies(slot, rows_ref, wait):
        src = lambda j: _rows_at(y_hbm, rows_ref[0, 0, j])
        dst = lambda j: _rows_at(comp_ref.at[slot], j * RUN_ALIGN)
        _group_copies(cr // RUN_ALIGN, src, dst, sems.at[slot], wait)

    @pl.when(i == 0)
    def _():
        copies(0, get_ref, False)

    @pl.when(i + 1 < pl.num_programs(0))
    def _():
        copies(1 - s, nget_ref, False)

    q = lax.broadcasted_iota(jnp.int32, (tm, cr), 1)
    gates, lp = gate_ref[...], lpos_ref[...]
    pg = jnp.zeros((tm, cr), F32)
    for k in range(TOP_K):
        pg = jnp.where(lp[:, k:k + 1] == q, gates[:, k:k + 1], pg)
    pltpu.make_async_copy(y_hbm.at[pl.ds(0, cr), :], comp_ref.at[s], sems.at[s]).wait()
    y_lo, y_hi = _unpack_rows(comp_ref[s])
    pgb = pg.astype(BF16)
    y = jnp.concatenate([jnp.dot(pgb, y_lo, preferred_element_type=F32),
                         jnp.dot(pgb, y_hi, preferred_element_type=F32)], axis=1)
    h_new = h_ref[...] + mod_ref[0][:, 5 * D:6 * D] * y
    if final:
        ms = jnp.mean(h_new * h_new, axis=-1, keepdims=True)
        h_new = h_new * lax.rsqrt(ms + NORM_EPS) * fg_ref[...]
    o_ref[...] = h_new


def _moe_combine(lay, h, mod, yb, gates_t, lpos_t, get_rows, n_rows, final_g=None):
    tm = lay.tm
    nt, ngrp = n_rows // tm, _comp_rows(tm) // RUN_ALIGN
    final = final_g is not None
    extra_specs = [pl.BlockSpec((1, D), lambda i, *_: (0, 0))] if final else []
    extra_args = [final_g.reshape(1, D)] if final else []
    grid_spec = pltpu.PrefetchScalarGridSpec(
        num_scalar_prefetch=0,
        grid=(nt,),
        in_specs=[_group_table(0, nt, ngrp), _group_table(1, nt, ngrp),
                  pl.BlockSpec((tm, D), lambda i, *_: (i, 0)),
                  pl.BlockSpec((tm, TOP_K), lambda i, *_: (i, 0)),
                  pl.BlockSpec((tm, TOP_K), lambda i, *_: (i, 0)),
                  pl.BlockSpec((1, 1, 6 * D), lambda i, *_: (lay.mod_idx(i), 0, 0)),
                  pl.BlockSpec(memory_space=pl.ANY)] + extra_specs,
        out_specs=pl.BlockSpec((tm, D), lambda i, *_: (i, 0)),
        scratch_shapes=[pltpu.VMEM((2, _comp_rows(tm), DP), jnp.uint32), pltpu.SemaphoreType.DMA((2,))],
    )
    return pl.pallas_call(
        functools.partial(_combine_kernel, final=final),
        grid_spec=grid_spec,
        out_shape=jax.ShapeDtypeStruct((n_rows, D), F32),
        compiler_params=_cparams(("arbitrary",)),
        name="moe_combine",
    )(get_rows, get_rows, h, gates_t, lpos_t, mod, yb, *extra_args)


def _moe_layer(lay, h, mod, g, w_router, b_router, layer, w1_all, b1, w2_all, b2, n_rows, final_g=None):
    gates, lpos, cnt = _moe_router(lay, h, mod, g, w_router, b_router, n_rows)
    put_rows, get_rows, block_expert, fill_base, fill_cnt, n_used, n_blocks = _moe_plan(cnt[:, :, 0], n_rows, lay.tm)
    xs = _moe_compact(lay, h, mod, g, lpos, put_rows, fill_base, fill_cnt, n_rows, n_blocks)
    yb = _moe_experts(xs, block_expert, n_used, n_blocks, layer, w1_all, b1, w2_all, b2)
    return _moe_combine(lay, h, mod, yb, gates.T, lpos.T, get_rows, n_rows, final_g)


def kernel(x, c, ctx, c_ctx, ada_w, ada_b, norm_g, attn_w_qkv, attn_b_qkv, attn_w_o, attn_b_o, attn_sinks, pool_w, pool_b, pool_scale, rwkv_mix, rwkv_w_rkv, rwkv_w0, rwkv_w1, rwkv_w2, rwkv_a0, rwkv_a1, rwkv_a2, rwkv_g1, rwkv_g2, rwkv_k_k, rwkv_k_a, rwkv_r_k, rwkv_ln_w, rwkv_ln_b, rwkv_w_o, moe_w_router, moe_b_router, moe_w1, moe_b1, moe_w2, moe_b2, final_g):
    B, T, _ = x.shape
    L = ctx.shape[1]
    depth = ada_w.shape[0]
    lay = _Layout(B, T, L)
    h = jnp.concatenate([x.reshape(B * T, D), ctx.reshape(B * L, D)], axis=0)

    n_c = B + 1
    n_c_pad = -(-n_c // 8) * 8
    c_all = jnp.concatenate([c, c_ctx[None, :], jnp.zeros((n_c_pad - n_c, D), F32)], axis=0)
    mods = _ada_mods(c_all, ada_w, ada_b)
    cos, sin = _rope_tables(T, L)

    for layer in range(depth):
        last = layer == depth - 1
        kind, j = layer % 3, layer // 3
        mod = mods[layer].reshape(n_c_pad, 1, 6 * D)
        n_rows = lay.rows_x if last else lay.N
        if kind == 0:
            q, k, v = _attn_qkv(lay, h, mod, norm_g[layer, 0], attn_w_qkv[j], attn_b_qkv[j], cos, sin)
            o = _attention(lay, q, k, v, attn_sinks[j], not last)
            h = _linear_residual(lay, h, o, mod, attn_w_o[j], attn_b_o[j], 2, n_rows)
        elif kind == 1:
            h = _pool_mixer(lay, h, mod, norm_g[layer, 0], pool_w[j], pool_b[j], pool_scale[j], n_rows)
        else:
            p = dict(mix=rwkv_mix[j], w_rkv=rwkv_w_rkv[j], w0=rwkv_w0[j], w1=rwkv_w1[j], w2=rwkv_w2[j],
                     a0=rwkv_a0[j], a1=rwkv_a1[j], a2=rwkv_a2[j], g1=rwkv_g1[j], g2=rwkv_g2[j],
                     k_k=rwkv_k_k[j], k_a=rwkv_k_a[j], r_k=rwkv_r_k[j])
            r, v, kn, g, bonus, lw0, k0, b0, lw1, k1, b1 = _rwkv_proj(lay, h, mod, norm_g[layer, 0], p)
            yf, yb = _rwkv_scan(lay, r, v, kn, lw0, k0, b0, lw1, k1, b1)
            h = _rwkv_out(lay, h, yf, yb, bonus, g, mod, rwkv_ln_w[j], rwkv_ln_b[j], rwkv_w_o[j])
        h = _moe_layer(lay, h, mod, norm_g[layer, 1], moe_w_router[layer], moe_b_router[layer],
                       layer, moe_w1, moe_b1[layer], moe_w2, moe_b2[layer], n_rows,
                       final_g if last else None)
    return h.reshape(B, T, D)
```

```python
import functools

import jax
import jax.numpy as jnp
import numpy as np
from jax import lax
from jax.experimental import pallas as pl
from jax.experimental.pallas import tpu as pltpu

F32 = jnp.float32
BF16 = jnp.bfloat16

D = 1024
NORM_EPS = 1e-5
GRID_W = 64
HEAD_DIM = 64
N_HEADS = 16
N_KV = 4
Q_DIM = 1024
KV_DIM = 256
QK_DIM = Q_DIM + KV_DIM
KVD_DIM = 2 * KV_DIM
QKD_DIM = Q_DIM + KVD_DIM
WINDOW = 128
QB = 128
ROPE_BASE = 10000.0
POOL_SIZES = (2, 4, 8, 16)
POOL_GW = 256
HALO = 8
GN_EPS = 64e-5
N_EXPERTS = 32
TOP_K = 4
D_FF = 1024
SWIGLU_LIMIT = 7.0
SWIGLU_ALPHA = 1.702
MOE_BLOCK = 512
RW_CHUNK = 64
RW_GROUP = 256
VMEM_LIMIT = 56 * 1024 * 1024


def _cparams(sem):
    return pltpu.CompilerParams(dimension_semantics=sem, vmem_limit_bytes=VMEM_LIMIT)


def _norm_mod(h, g, shift, scale):
    ms = jnp.mean(h * h, axis=-1, keepdims=True)
    y = h * lax.rsqrt(ms + NORM_EPS) * g
    return y * (1.0 + scale) + shift


class _Layout:
    def __init__(self, B, T, L):
        self.B, self.T, self.L = B, T, L
        self.tm = 256 if L % 256 == 0 else 128
        self.rows_x, self.rows_c = B * T, B * L
        self.N = self.rows_x + self.rows_c
        self.tx, self.tc = T // self.tm, L // self.tm
        self.n_xt = self.rows_x // self.tm
        self.n_t = self.N // self.tm

    def mod_idx(self, i):
        return jnp.where(i < self.n_xt, i // self.tx, self.B)

    def mod_spec(self):
        return pl.BlockSpec((1, 1, 6 * D), lambda i: (self.mod_idx(i), 0, 0))

    def row_spec(self, width=D):
        return pl.BlockSpec((self.tm, width), lambda i: (i, 0))

    def split_row_specs(self):
        return [pl.BlockSpec((self.tm, D), lambda i: (jnp.minimum(i, self.n_xt - 1), 0)),
                pl.BlockSpec((self.tm, D), lambda i: (jnp.maximum(i - self.n_xt, 0), 0))]

    def seq_tile(self, i):
        j = jnp.where(i < self.n_xt, i % self.tx, (i - self.n_xt) % self.tc)
        n = jnp.where(i < self.n_xt, self.tx, self.tc)
        return j, n

    def halo_specs(self):
        hb = self.tm // HALO
        last = self.N // HALO - 1
        prev = pl.BlockSpec((HALO, D), lambda i: (jnp.maximum(i * hb - 1, 0), 0))
        nxt = pl.BlockSpec((HALO, D), lambda i: (jnp.minimum((i + 1) * hb, last), 0))
        return prev, nxt


def _const_spec(shape):
    nd = len(shape)
    return pl.BlockSpec(shape, lambda *_: (0,) * nd)


def _ada_kernel(c_ref, w_ref, b_ref, o_ref):
    c = c_ref[...]
    s = c * jax.nn.sigmoid(c)
    o_ref[0] = jnp.dot(s, w_ref[0], preferred_element_type=F32, precision=lax.Precision.HIGHEST) + b_ref[0]


def _ada_mods(c_all, ada_w, ada_b):
    depth = ada_w.shape[0]
    R = c_all.shape[0]
    nt = 1536
    return pl.pallas_call(
        _ada_kernel,
        grid=(depth, 6 * D // nt),
        in_specs=[pl.BlockSpec((R, D), lambda l, j: (0, 0)),
                  pl.BlockSpec((1, D, nt), lambda l, j: (l, 0, j)),
                  pl.BlockSpec((1, 1, nt), lambda l, j: (l, 0, j))],
        out_specs=pl.BlockSpec((1, R, nt), lambda l, j: (l, 0, j)),
        out_shape=jax.ShapeDtypeStruct((depth, R, 6 * D), F32),
        compiler_params=_cparams(("parallel", "parallel")),
        name="ada_mod",
    )(c_all, ada_w, ada_b.reshape(depth, 1, 6 * D))


def _stream_rows(refs, n_split):
    if n_split is None:
        return refs[0][...], refs[1:]
    return jnp.where(pl.program_id(0) < n_split, refs[0][...], refs[1][...]), refs[2:]


def _qkv_kernel(*refs, n_split):
    h, (mod_ref, g_ref, w_ref, b_ref, cos_ref, sin_ref, q_ref, k_ref, v_ref) = _stream_rows(refs, n_split)
    m = mod_ref[0]
    a = _norm_mod(h, g_ref[...], m[:, 0:D], m[:, D:2 * D])
    qkv = jnp.dot(a.astype(BF16), w_ref[...], preferred_element_type=F32) + b_ref[...]
    qk = qkv[:, :QK_DIM]
    lane = lax.broadcasted_iota(jnp.int32, (1, QK_DIM), 1)
    first = (lane % 32) < 16
    rot = jnp.where(first, -pltpu.roll(qk, QK_DIM - 16, 1), pltpu.roll(qk, 16, 1))
    qk = qk * cos_ref[...] + rot * sin_ref[...]
    q_ref[...] = (qk[:, :Q_DIM] * (HEAD_DIM ** -0.5)).astype(BF16)
    k_ref[...] = _dup_heads(qk[:, Q_DIM:]).astype(BF16)
    v_ref[...] = _dup_heads(qkv[:, QK_DIM:]).astype(BF16)


def _dup_heads(x):
    heads = [x[:, g * HEAD_DIM:(g + 1) * HEAD_DIM] for g in range(N_KV)]
    return jnp.concatenate([hh for hd in heads for hh in (hd, hd)], axis=1)


def _rope_tables(T, L):
    rows = T // GRID_W
    n_freq = HEAD_DIM // 4
    inv = ROPE_BASE ** (-jnp.arange(n_freq, dtype=F32) / n_freq)
    row_ang = jnp.arange(rows, dtype=F32)[:, None] * inv
    col_ang = jnp.arange(GRID_W, dtype=F32)[:, None] * inv
    ang_r = jnp.broadcast_to(row_ang[:, None, :], (rows, GRID_W, n_freq)).reshape(T, n_freq)
    ang_c = jnp.broadcast_to(col_ang[None, :, :], (rows, GRID_W, n_freq)).reshape(T, n_freq)
    ang = jnp.concatenate([ang_r, ang_r, ang_c, ang_c], axis=-1)
    cos = jnp.concatenate([jnp.cos(ang), jnp.ones((L, HEAD_DIM), F32)], axis=0)
    sin = jnp.concatenate([jnp.sin(ang), jnp.zeros((L, HEAD_DIM), F32)], axis=0)
    reps = QK_DIM // HEAD_DIM
    return jnp.tile(cos, (1, reps)), jnp.tile(sin, (1, reps))


def _attn_qkv(lay, h, mod, g, w_qkv, b_qkv, cos, sin, h_ctx=None):
    tm = lay.tm
    split = h_ctx is not None
    stream = [h, h_ctx] if split else [h]

    def tab_idx(i):
        return (jnp.where(i < lay.n_xt, i % lay.tx, lay.tx + (i - lay.n_xt) % lay.tc), 0)

    w = w_qkv.astype(BF16)
    b = b_qkv.reshape(1, -1)
    width = Q_DIM + 2 * KV_DIM
    return pl.pallas_call(
        functools.partial(_qkv_kernel, n_split=lay.n_xt if split else None),
        grid=(lay.n_t,),
        in_specs=(lay.split_row_specs() if split else [lay.row_spec()])
                 + [lay.mod_spec(), _const_spec((1, D)), _const_spec((D, width)), _const_spec((1, width)),
                  pl.BlockSpec((tm, QK_DIM), tab_idx), pl.BlockSpec((tm, QK_DIM), tab_idx)],
        out_specs=[lay.row_spec(Q_DIM), lay.row_spec(KVD_DIM), lay.row_spec(KVD_DIM)],
        out_shape=[jax.ShapeDtypeStruct((lay.N, Q_DIM), BF16),
                   jax.ShapeDtypeStruct((lay.N, KVD_DIM), BF16),
                   jax.ShapeDtypeStruct((lay.N, KVD_DIM), BF16)],
        compiler_params=_cparams(("parallel",)),
        name="attn_qkv",
    )(*stream, mod, g.reshape(1, D), w, b, cos, sin)


def _attn_heads(q, kk, vv, mask, sink_ref, o_ref):
    R = q.shape[0]
    G = N_HEADS // N_KV
    TILE = 2 * HEAD_DIM
    lo = lax.broadcasted_iota(jnp.int32, (1, TILE), 1) < HEAD_DIM
    order = (0, 2, 1, 3)
    scores = []
    for g in range(N_KV):
        pieces = []
        for j in order:
            hq = g * G + j
            qt = q[:, (hq // 2) * TILE:(hq // 2 + 1) * TILE]
            pieces.append(jnp.where(lo if hq % 2 == 0 else ~lo, qt, jnp.zeros_like(qt)))
        qs = jnp.concatenate(pieces, axis=0)
        scores.append(lax.dot_general(qs, kk[:, g * TILE:(g + 1) * TILE], (((1,), (1,)), ((), ())),
                                      preferred_element_type=F32))
    probs, esinks = [], []
    for g in range(N_KV):
        s = scores[g]
        if mask is not None:
            s = jnp.where(mask[None], s.reshape(G, R, -1), -jnp.inf).reshape(G * R, -1)
        sink = jnp.concatenate([jnp.full((R, 1), sink_ref[g * G + j], F32) for j in order], axis=0)
        m = jnp.maximum(jnp.max(s, axis=-1, keepdims=True), sink)
        probs.append(jnp.exp((s - m).astype(BF16)))
        esinks.append(jnp.exp(sink - m))
    for g in range(N_KV):
        vd = vv[:, g * TILE:(g + 1) * TILE]
        one = jnp.ones_like(vd)
        p, es = probs[g], esinks[g]
        half = (G // 2) * R
        oe = jnp.dot(p[:half], jnp.where(lo, vd, one), preferred_element_type=F32)
        oo = jnp.dot(p[half:], jnp.where(lo, one, vd), preferred_element_type=F32)
        re = oe / (pltpu.roll(oe, HEAD_DIM, 1) + es[:half])
        ro = oo / (pltpu.roll(oo, HEAD_DIM, 1) + es[half:])
        for t in range(G // 2):
            tile = jnp.where(lo, re[t * R:(t + 1) * R], ro[t * R:(t + 1) * R])
            c = g * (G // 2) + t
            o_ref[:, c * TILE:(c + 1) * TILE] = tile.astype(o_ref.dtype)


def _attn_kernel(sink_ref, q_ref, kp_ref, kc_ref, kn_ref, vp_ref, vc_ref, vn_ref, kx_ref, vx_ref, o_ref, *, T, L):
    i = pl.program_id(1)
    nqb = T // QB

    @pl.when(i < nqb)
    def _():
        kk = jnp.concatenate([kx_ref[...], kp_ref[...], kc_ref[...], kn_ref[...]], axis=0)
        vv = jnp.concatenate([vx_ref[...], vp_ref[...], vc_ref[...], vn_ref[...]], axis=0)
        S = L + 3 * QB
        col = lax.broadcasted_iota(jnp.int32, (QB, S), 1)
        row = lax.broadcasted_iota(jnp.int32, (QB, S), 0)
        rel = col - L - QB - row
        kpos = i * QB - QB + (col - L)
        local_ok = (jnp.abs(rel) <= WINDOW) & (kpos >= 0) & (kpos < T)
        mask = (col < L) | local_ok
        _attn_heads(q_ref[...], kk, vv, mask, sink_ref, o_ref)

    @pl.when(i >= nqb)
    def _():
        _attn_heads(q_ref[...], kx_ref[...], vx_ref[...], None, sink_ref, o_ref)


def _attention(lay, q, k, v, sinks, need_ctx):
    B, T, L = lay.B, lay.T, lay.L
    nqb, ncb = T // QB, L // QB
    cq = lay.rows_x // QB
    cb = lay.rows_x // L
    smem = pl.BlockSpec(memory_space=pltpu.SMEM)

    def cur(b, i):
        return (jnp.where(i < nqb, b * nqb + i, cq + b * ncb + (i - nqb)), 0)

    def prev(b, i):
        return (b * nqb + jnp.clip(i - 1, 0, nqb - 1), 0)

    def mid(b, i):
        return (b * nqb + jnp.minimum(i, nqb - 1), 0)

    def nxt(b, i):
        return (b * nqb + jnp.minimum(i + 1, nqb - 1), 0)

    def ctxb(b, i):
        return (cb + b, 0)

    kv_specs = [pl.BlockSpec((QB, KVD_DIM), prev), pl.BlockSpec((QB, KVD_DIM), mid), pl.BlockSpec((QB, KVD_DIM), nxt)]
    n_out = lay.N if need_ctx else lay.rows_x
    return pl.pallas_call(
        functools.partial(_attn_kernel, T=T, L=L),
        grid=(B, nqb + (ncb if need_ctx else 0)),
        in_specs=[smem, pl.BlockSpec((QB, Q_DIM), cur)] + kv_specs + kv_specs
                 + [pl.BlockSpec((L, KVD_DIM), ctxb), pl.BlockSpec((L, KVD_DIM), ctxb)],
        out_specs=pl.BlockSpec((QB, Q_DIM), cur),
        out_shape=jax.ShapeDtypeStruct((n_out, Q_DIM), BF16),
        compiler_params=_cparams(("parallel", "arbitrary")),
        name="attention",
    )(sinks, q, k, k, k, v, v, v, k, v)


def _linres_kernel(*refs, gate_slot, n_split):
    h, (x_ref, mod_ref, w_ref, b_ref, o_ref) = _stream_rows(refs, n_split)
    gate = mod_ref[0][:, gate_slot * D:(gate_slot + 1) * D]
    y = jnp.dot(x_ref[...], w_ref[...], preferred_element_type=F32) + b_ref[...]
    o_ref[...] = h + gate * y


def _linear_residual(lay, h, x, mod, w, b, gate_slot, n_rows, h_ctx=None):
    split = h_ctx is not None
    stream = [h, h_ctx] if split else [h]
    return pl.pallas_call(
        functools.partial(_linres_kernel, gate_slot=gate_slot, n_split=lay.n_xt if split else None),
        grid=(n_rows // lay.tm,),
        in_specs=(lay.split_row_specs() if split else [lay.row_spec()])
                 + [lay.row_spec(x.shape[1]), lay.mod_spec(), _const_spec(w.shape), _const_spec((1, D))],
        out_specs=lay.row_spec(),
        out_shape=jax.ShapeDtypeStruct((n_rows, D), F32),
        compiler_params=_cparams(("parallel",)),
        name="linear_residual",
    )(*stream, x, mod, w.astype(BF16), b.reshape(1, D))


def _fill_ext(ext_ref, h_ref, hp_ref, hn_ref, g, shift, scale, j, n, tm):
    a = _norm_mod(h_ref[...], g, shift, scale)
    ap = _norm_mod(hp_ref[...], g, shift, scale)
    an = _norm_mod(hn_ref[...], g, shift, scale)
    ext_ref[0:HALO, :] = jnp.where(j == 0, 0.0, ap)
    ext_ref[HALO:HALO + tm, :] = a
    ext_ref[HALO + tm:2 * HALO + tm, :] = jnp.where(j == n - 1, 0.0, an)
    return a


def _pool_kernel(h_ref, hp_ref, hn_ref, mod_ref, g_ref, w_ref, b_ref, ls_ref, o_ref, ext_ref, *, lay):
    tm = lay.tm
    i = pl.program_id(0)
    j, n = lay.seq_tile(i)
    m = mod_ref[0]
    a = _fill_ext(ext_ref, h_ref, hp_ref, hn_ref, g_ref[...], m[:, 0:D], m[:, D:2 * D], j, n, tm)
    t = j * tm + lax.broadcasted_iota(jnp.int32, (tm, 1), 0)
    n_tok = n * tm
    outs = []
    for gi, size in enumerate(POOL_SIZES):
        sl = slice(gi * POOL_GW, (gi + 1) * POOL_GW)
        acc = ext_ref[HALO - size // 2:HALO - size // 2 + tm, sl]
        for o in range(-size // 2 + 1, size - size // 2):
            acc = acc + ext_ref[HALO + o:HALO + o + tm, sl]
        lo = jnp.maximum(t - size // 2, 0)
        hi = jnp.minimum(t + size - size // 2, n_tok)
        d = acc / (hi - lo).astype(F32) - a[:, sl]
        outs.append(jnp.dot(d.astype(BF16), w_ref[gi], preferred_element_type=F32) + b_ref[gi])
    y = jnp.concatenate(outs, axis=-1) * ls_ref[...]
    o_ref[...] = h_ref[...] + m[:, 2 * D:3 * D] * y


def _pool_mixer(lay, h, mod, g, w, b, ls, n_rows):
    prev, nxt = lay.halo_specs()
    return pl.pallas_call(
        functools.partial(_pool_kernel, lay=lay),
        grid=(n_rows // lay.tm,),
        in_specs=[lay.row_spec(), prev, nxt, lay.mod_spec(), _const_spec((1, D)),
                  _const_spec(w.shape), _const_spec((4, 1, POOL_GW)), _const_spec((1, D))],
        out_specs=lay.row_spec(),
        out_shape=jax.ShapeDtypeStruct((n_rows, D), F32),
        scratch_shapes=[pltpu.VMEM((lay.tm + 2 * HALO, D), F32)],
        compiler_params=_cparams(("parallel",)),
        name="pool_mixer",
    )(h, h, h, mod, g.reshape(1, D), w.astype(BF16), b.reshape(4, 1, POOL_GW), ls.reshape(1, D))


def _seg_sum(x, bd, split=True):
    outs = []
    for q in range(D // RW_GROUP):
        xg = x[:, q * RW_GROUP:(q + 1) * RW_GROUP]
        hi = xg.astype(BF16)
        acc = jnp.dot(hi, bd, preferred_element_type=F32)
        if split:
            acc = acc + jnp.dot((xg - hi.astype(F32)).astype(BF16), bd, preferred_element_type=F32)
        outs.append(acc)
    return jnp.concatenate(outs, axis=-1)


def _rwkv_proj_kernel(h_ref, hp_ref, hn_ref, mod_ref, g_ref, mix_ref, wr_ref, wk_ref, wv_ref, g1_ref, g2_ref,
                      w1_ref, w2_ref, w0_ref, a1_ref, a2_ref, a0_ref, kk_ref, ka_ref, rk_ref, bd_ref,
                      r_out, v_out, kn_out, g_out, bonus_out, lw0_out, k0_out, b0_out, lw1_out, k1_out, b1_out,
                      ext_ref, *, lay):
    tm = lay.tm
    i = pl.program_id(0)
    j, n = lay.seq_tile(i)
    m = mod_ref[0]
    a = _fill_ext(ext_ref, h_ref, hp_ref, hn_ref, g_ref[...], m[:, 0:D], m[:, D:2 * D], j, n, tm)
    prev = ext_ref[HALO - 1:HALO - 1 + tm, :]
    nxt = ext_ref[HALO + 1:HALO + 1 + tm, :]
    xx = 0.5 * (prev + nxt) - a
    x_r, x_w, x_k, x_v, x_a, x_g = [(a + xx * mix_ref[q:q + 1, :]).astype(BF16) for q in range(6)]
    bd = bd_ref[...]
    r = jnp.dot(x_r, wr_ref[...], preferred_element_type=F32)
    k = jnp.dot(x_k, wk_ref[...], preferred_element_type=F32)
    v = jnp.dot(x_v, wv_ref[...], preferred_element_type=F32)
    gl = jnp.dot(x_g, g1_ref[...], preferred_element_type=F32)
    g = jnp.dot(jax.nn.sigmoid(gl).astype(BF16), g2_ref[...], preferred_element_type=F32)
    kk = k * kk_ref[...]
    kn = kk / jnp.maximum(jnp.sqrt(_seg_sum(kk * kk, bd)), 1e-12)
    tw = jnp.tanh(jnp.dot(x_w, w1_ref[...], preferred_element_type=F32))
    aa = jnp.dot(x_a, a1_ref[...], preferred_element_type=F32)
    lane = lax.broadcasted_iota(jnp.int32, (1, tw.shape[1]), 1)
    r_out[...] = r.astype(r_out.dtype)
    v_out[...] = v.astype(v_out.dtype)
    kn_out[...] = kn.astype(kn_out.dtype)
    g_out[...] = g.astype(g_out.dtype)
    rk = rk_ref[...]
    bonus = jnp.zeros_like(v)
    outs = ((lw0_out, k0_out, b0_out), (lw1_out, k1_out, b1_out))
    half = tw.shape[1] // 2
    for d in range(2):
        sel = (lane >= d * half) & (lane < (d + 1) * half)
        wl = w0_ref[d:d + 1, :] + jnp.dot(jnp.where(sel, tw, 0.0).astype(BF16), w2_ref[...], preferred_element_type=F32)
        asig = jax.nn.sigmoid(a0_ref[d:d + 1, :] + jnp.dot(jnp.where(sel, aa, 0.0).astype(BF16), a2_ref[...],
                                                           preferred_element_type=F32))
        k_d = k * (1.0 + (asig - 1.0) * ka_ref[...])
        lw_o, k_o, b_o = outs[d]
        lw_o[...] = -float(np.exp(-0.5)) * jax.nn.sigmoid(wl)
        k_o[...] = k_d.astype(k_o.dtype)
        b_o[...] = (kn * asig).astype(b_o.dtype)
        bonus = bonus + _seg_sum(r * k_d * rk, bd, split=False) * v
    bonus_out[...] = bonus.astype(bonus_out.dtype)


def _block_ones():
    idx = np.arange(RW_GROUP) // 64
    return jnp.asarray(idx[:, None] == idx[None, :], dtype=BF16)


def _rwkv_proj(lay, h, mod, g, p):
    prev, nxt = lay.halo_specs()
    cat = lambda w: jnp.concatenate([w[0], w[1]], axis=-1).astype(BF16)
    stack = lambda w: jnp.concatenate([w[0], w[1]], axis=0).astype(BF16)
    args = [h, h, h, mod, g.reshape(1, D), p['mix'],
            p['w_rkv'][0].astype(BF16), p['w_rkv'][1].astype(BF16), p['w_rkv'][2].astype(BF16),
            p['g1'].astype(BF16), p['g2'].astype(BF16),
            cat(p['w1']), stack(p['w2']), p['w0'], cat(p['a1']), stack(p['a2']), p['a0'],
            p['k_k'].reshape(1, D), p['k_a'].reshape(1, D), p['r_k'].reshape(1, D), _block_ones()]
    in_specs = [lay.row_spec(), prev, nxt, lay.mod_spec()] + [_const_spec(x.shape) for x in args[4:]]
    dtypes = [BF16] * 5 + [F32, BF16, BF16] * 2
    return pl.pallas_call(
        functools.partial(_rwkv_proj_kernel, lay=lay),
        grid=(lay.n_t,),
        in_specs=in_specs,
        out_specs=[lay.row_spec()] * len(dtypes),
        out_shape=[jax.ShapeDtypeStruct((lay.N, D), dt) for dt in dtypes],
        scratch_shapes=[pltpu.VMEM((lay.tm + 2 * HALO, D), F32)],
        compiler_params=_cparams(("parallel",)),
        name="rwkv_proj",
    )(*args)


def _rwkv_scan_kernel(rf, vf, nf, lwf, kf, bf, rb, vb, nb, lwb, kb, bb, yf_out, yb_out, z_ref):
    C, G = RW_CHUNK, RW_GROUP
    c = pl.program_id(1)

    @pl.when(c == 0)
    def _():
        z_ref[...] = jnp.zeros_like(z_ref)

    trow = lax.broadcasted_iota(jnp.int32, (C, C), 0)
    tcol = lax.broadcasted_iota(jnp.int32, (C, C), 1)
    ti = lax.broadcasted_iota(jnp.int32, (C, G), 0)
    ii = lax.broadcasted_iota(jnp.int32, (C, G), 1) % C
    brow = lax.broadcasted_iota(jnp.int32, (G, G), 0)
    bcol = lax.broadcasted_iota(jnp.int32, (G, G), 1)
    m_bd_t = (brow // C) == (bcol // 64)
    m_bd_f = (brow // 64) == (bcol // 64)
    eye_g = brow == bcol

    def bd(x):
        return jnp.where(m_bd_t, jnp.concatenate([x] * (G // C), axis=0), 0.0).astype(BF16)

    def mm(a, b):
        return jnp.dot(a, b, preferred_element_type=F32)

    dirs = ((rf, vf, nf, lwf, kf, bf, yf_out, False), (rb, vb, nb, lwb, kb, bb, yb_out, True))
    ch = []
    for d, (r_ref, v_ref, n_ref, lw_ref, k_ref, b_ref, y_out, rev) in enumerate(dirs):
        tri = (tcol >= trow) if rev else (tcol <= trow)
        strict = (ii > ti) if rev else (ii < ti)
        incl = (ii >= ti) if rev else (ii <= ti)
        lw = lw_ref[...]
        lc = jnp.dot(tri.astype(F32), lw, preferred_element_type=F32, precision=lax.Precision.HIGHEST)
        ltot = lc[0:1, :] if rev else lc[C - 1:C, :]
        e_lc = jnp.exp(lc)
        ie_lc = jnp.exp(-lc)
        e_le = jnp.exp(lc - lw)
        e_bar = jnp.exp(ltot - lc)
        e_tot = jnp.exp(ltot)
        kvec, bvec = k_ref[...], b_ref[...]
        a_t = -n_ref[...] * e_le
        r_t = r_ref[...] * e_lc
        b_t = bvec * ie_lc
        k_t = kvec * ie_lc
        b_b = bvec * e_bar
        k_b = kvec * e_bar
        vall = v_ref[...]
        for g in range(D // G):
            sl = slice(g * G, (g + 1) * G)
            ch.append(dict(
                d=d, g=g, sl=sl, y_out=y_out, strict=strict, incl=incl, v=vall[:, sl], e_tot=e_tot[:, sl],
                ar=jnp.concatenate([a_t[:, sl], r_t[:, sl]], axis=0).astype(BF16),
                bk=jnp.concatenate([bd(b_t[:, sl]), bd(k_t[:, sl])], axis=0),
                lhs_t=jnp.concatenate([b_b[:, sl], k_b[:, sl]], axis=0).astype(BF16)))

    for q in ch:
        aa = lax.dot_general(q['ar'], q['bk'], (((1,), (1,)), ((), ())), preferred_element_type=F32)
        q['a_ab'] = jnp.where(q['strict'], aa[:C, :G], 0.0)
        q['a_ak'] = jnp.where(q['strict'], aa[:C, G:], 0.0)
        q['m_rb'] = jnp.where(q['incl'], aa[C:, :G], 0.0).astype(BF16)
        q['m_rk'] = jnp.where(q['incl'], aa[C:, G:], 0.0)
    for q in ch:
        q['x'] = jnp.where(ii == ti, 1.0, 0.0) + q['a_ab']
        q['p'] = mm(q['a_ab'].astype(BF16), bd(q['a_ab']))
    for q in ch:
        q['wv'] = mm(jnp.concatenate([q['a_ak'], q['m_rk']], axis=0).astype(BF16), bd(q['v']))
        q['z'] = z_ref[q['d'], q['g']]
        q['az'] = mm(q['ar'], q['z'].astype(BF16))
    n_sq = int(np.log2(C)) - 1
    for s in range(n_sq):
        last = s == n_sq - 1
        for q in ch:
            lhs = q['x'] if last else jnp.concatenate([q['x'], q['p']], axis=0)
            xp = mm(lhs.astype(BF16), bd(q['p']))
            q['x'] = q['x'] + xp[:C]
            if not last:
                q['p'] = xp[C:]
    for q in ch:
        q['u'] = mm(q['x'].astype(BF16), bd(q['az'][:C] + q['wv'][:C]))
    for q in ch:
        y = q['az'][C:] + q['wv'][C:] + mm(q['m_rb'], bd(q['u']))
        q['y_out'][:, q['sl']] = y
        rhs_t = jnp.concatenate([q['u'], q['v']], axis=0).astype(BF16)
        zu = lax.dot_general(q['lhs_t'], rhs_t, (((0,), (0,)), ((), ())), preferred_element_type=F32)
        decay = jnp.sum(jnp.where(eye_g, jnp.broadcast_to(q['e_tot'], (G, G)), 0.0), axis=1, keepdims=True)
        z_ref[q['d'], q['g']] = jnp.where(m_bd_f, decay * q['z'] + zu, 0.0)


def _rwkv_scan(lay, r, v, kn, lw0, k0, b0, lw1, k1, b1):
    B, T, L = lay.B, lay.T, lay.L
    C = RW_CHUNK
    cc, cx = L // C, T // C
    nc = cc + cx
    ctx0 = lay.rows_x // C

    def fwd(b, c):
        return (jnp.where(c < cc, ctx0 + b * cc + c, b * cx + c - cc), 0)

    def bwd(b, c):
        return (jnp.where(c < cc, ctx0 + b * cc + (cc - 1 - c), b * cx + (nc - 1 - c)), 0)

    sf, sb = pl.BlockSpec((C, D), fwd), pl.BlockSpec((C, D), bwd)
    return pl.pallas_call(
        _rwkv_scan_kernel,
        grid=(B, nc),
        in_specs=[sf] * 6 + [sb] * 6,
        out_specs=[sf, sb],
        out_shape=[jax.ShapeDtypeStruct((lay.N, D), F32)] * 2,
        scratch_shapes=[pltpu.VMEM((2, D // RW_GROUP, RW_GROUP, RW_GROUP), F32)],
        compiler_params=_cparams(("parallel", "arbitrary")),
        name="rwkv_scan",
    )(r, v, kn, lw0, k0, b0, r, v, kn, lw1, k1, b1)


def _rwkv_out_kernel(h_ref, yf_ref, yb_ref, bonus_ref, g_ref, mod_ref, lnw_ref, lnb_ref, wo_ref, bd_ref, o_ref):
    bd = bd_ref[...]
    y = yf_ref[...] + yb_ref[...]
    mu = _seg_sum(y, bd) * (1.0 / 64)
    yc = y - mu
    var = _seg_sum(yc * yc, bd) * (1.0 / 64)
    yn = yc * lax.rsqrt(var + GN_EPS) * lnw_ref[...] + lnb_ref[...]
    out = (yn + bonus_ref[...]) * g_ref[...]
    res = jnp.dot(out.astype(BF16), wo_ref[...], preferred_element_type=F32)
    o_ref[...] = h_ref[...] + mod_ref[0][:, 2 * D:3 * D] * res


def _rwkv_out(lay, h, yf, yb, bonus, g, mod, ln_w, ln_b, w_o):
    return pl.pallas_call(
        _rwkv_out_kernel,
        grid=(lay.n_t,),
        in_specs=[lay.row_spec()] * 5 + [lay.mod_spec(), _const_spec((1, D)), _const_spec((1, D)),
                                          _const_spec((D, D)), _const_spec((RW_GROUP, RW_GROUP))],
        out_specs=lay.row_spec(),
        out_shape=jax.ShapeDtypeStruct((lay.N, D), F32),
        compiler_params=_cparams(("parallel",)),
        name="rwkv_out",
    )(h, yf, yb, bonus, g, mod, ln_w.reshape(1, D), ln_b.reshape(1, D), w_o.astype(BF16), _block_ones())


RUN_ALIGN = 8
DP = D // 2
_HI_MASK = 0xFFFF0000


def _pack_rows(x):
    lo = pltpu.bitcast(x[:, :DP], jnp.uint32)
    hi = pltpu.bitcast(x[:, DP:], jnp.uint32)
    return (lo >> 16) | (hi & jnp.uint32(_HI_MASK))


def _unpack_rows(u):
    lo = pltpu.bitcast(u << 16, F32)
    hi = pltpu.bitcast(u & jnp.uint32(_HI_MASK), F32)
    return lo.astype(BF16), hi.astype(BF16)


def _router_kernel(h_ref, mod_ref, g_ref, wr_ref, br_ref, upper_ref, gate_ref, lpos_ref, cnt_ref):
    m = mod_ref[0]
    f = _norm_mod(h_ref[...], g_ref[...], m[:, 3 * D:4 * D], m[:, 4 * D:5 * D])
    def split(x):
        hi = x.astype(BF16)
        return hi, (x - hi.astype(F32)).astype(BF16)

    def nt(a, b):
        return lax.dot_general(a, b, (((1,), (1,)), ((), ())), preferred_element_type=F32)

    (w_hi, w_lo), (f_hi, f_lo) = split(wr_ref[...]), split(f)
    logits = nt(w_hi, f_hi) + (nt(w_hi, f_lo) + nt(w_lo, f_hi)) + br_ref[...]
    erow = lax.broadcasted_iota(jnp.int32, logits.shape, 0)
    tops, hots = [], []
    l = logits
    for _ in range(TOP_K):
        mx = jnp.max(l, axis=0, keepdims=True)
        ix = jnp.min(jnp.where(l == mx, erow, N_EXPERTS), axis=0, keepdims=True)
        tops.append(mx)
        hots.append(erow == ix)
        l = jnp.where(erow == ix, -jnp.inf, l)
    es = [jnp.exp(t - tops[0]) for t in tops]
    tot = es[0] + es[1] + es[2] + es[3]
    gate_ref[...] = jnp.concatenate([e / tot for e in es], axis=0)

    ohs = [jnp.where(hh, 1.0, 0.0) for hh in hots]
    cnts = [jnp.sum(oh, axis=1, keepdims=True) for oh in ohs]
    cnt = cnts[0] + cnts[1] + cnts[2] + cnts[3]
    run = jnp.ceil(cnt * (1.0 / RUN_ALIGN)) * RUN_ALIGN
    ee = lax.broadcasted_iota(jnp.int32, (N_EXPERTS, N_EXPERTS), 0)
    ec = lax.broadcasted_iota(jnp.int32, (N_EXPERTS, N_EXPERTS), 1)
    run_row = jnp.sum(jnp.where(ee == ec, jnp.broadcast_to(run, (N_EXPERTS, N_EXPERTS)), 0.0), axis=0, keepdims=True)
    offs = jnp.sum(jnp.where(ec < ee, jnp.broadcast_to(run_row, (N_EXPERTS, N_EXPERTS)), 0.0), axis=1, keepdims=True)
    base = offs
    lpos = []
    for k in range(TOP_K):
        before = jnp.dot(ohs[k].astype(BF16), upper_ref[...], preferred_element_type=F32)
        lpos.append(jnp.sum(ohs[k] * (before + base), axis=0, keepdims=True))
        base = base + cnts[k]
    lpos_ref[...] = jnp.concatenate(lpos, axis=0).astype(jnp.int32)
    cnt_ref[0] = cnt.astype(jnp.int32)


def _moe_router(lay, h, mod, g, w_router, b_router, n_rows):
    tm = lay.tm
    nt = n_rows // tm
    upper = jnp.asarray(np.triu(np.ones((tm, tm), np.float32), 1), dtype=BF16)
    return pl.pallas_call(
        _router_kernel,
        grid=(nt,),
        in_specs=[lay.row_spec(), lay.mod_spec(), _const_spec((1, D)),
                  _const_spec((N_EXPERTS, D)), _const_spec((N_EXPERTS, 1)), _const_spec((tm, tm))],
        out_specs=[pl.BlockSpec((TOP_K, tm), lambda i: (0, i)),
                   pl.BlockSpec((TOP_K, tm), lambda i: (0, i)),
                   pl.BlockSpec((1, N_EXPERTS, 1), lambda i: (i, 0, 0))],
        out_shape=[jax.ShapeDtypeStruct((TOP_K, n_rows), F32),
                   jax.ShapeDtypeStruct((TOP_K, n_rows), jnp.int32),
                   jax.ShapeDtypeStruct((nt, N_EXPERTS, 1), jnp.int32)],
        compiler_params=_cparams(("parallel",)),
        name="moe_router",
    )(h, mod, g.reshape(1, D), w_router.T, b_router.reshape(N_EXPERTS, 1), upper)


def _comp_rows(tm):
    return tm * TOP_K + N_EXPERTS * RUN_ALIGN


def _moe_plan(cnt, n, tm):
    nt = n // tm
    def excl_sum(a, axis):
        k = a.shape[axis]
        before = np.arange(k)[:, None] > np.arange(k)[None, :]
        if axis == 0:
            return jnp.sum(jnp.where(before[:, :, None], a[None, :, :], 0), axis=1)
        return jnp.sum(jnp.where(before[None, :, :], a[:, None, :], 0), axis=2)

    run = (cnt + RUN_ALIGN - 1) // RUN_ALIGN * RUN_ALIGN
    tot = jnp.sum(run, axis=0)
    padded = (tot + MOE_BLOCK - 1) // MOE_BLOCK * MOE_BLOCK
    pad_start = excl_sum(padded[None, :], 1)[0]
    pad_end = pad_start + padded
    base = pad_start[None, :] + excl_sum(run, 0)
    rows_max = n * TOP_K + nt * N_EXPERTS * (RUN_ALIGN - 1) + N_EXPERTS * MOE_BLOCK
    n_blocks = -(-rows_max // MOE_BLOCK)
    rows = n_blocks * MOE_BLOCK
    blk_row = jnp.arange(n_blocks, dtype=jnp.int32) * MOE_BLOCK
    block_expert = jnp.minimum(jnp.sum((pad_end[None, :] <= blk_row[:, None]).astype(jnp.int32), axis=1),
                               N_EXPERTS - 1).astype(jnp.int32)
    spare_rows = 2 * _comp_rows(tm)
    fill_base = jnp.concatenate([pad_start + tot, pad_end[-1:], jnp.full((1,), rows)]).astype(jnp.int32)
    fill_cnt = (jnp.concatenate([padded - tot, rows - pad_end[-1:], jnp.full((1,), spare_rows)])
                // RUN_ALIGN).astype(jnp.int32)
    n_used = (pad_end[-1:] // MOE_BLOCK).astype(jnp.int32)
    cr = _comp_rows(tm)
    ngrp = cr // RUN_ALIGN
    run_g = run // RUN_ALIGN
    goffs = excl_sum(run_g, 1)
    gend = goffs + run_g
    ng = gend[:, -1]
    j = jnp.arange(ngrp, dtype=jnp.int32)
    e_of = jnp.minimum(jnp.sum((gend[:, None, :] <= j[None, :, None]).astype(jnp.int32), axis=2), N_EXPERTS - 1)
    oh = e_of[:, :, None] == jnp.arange(N_EXPERTS, dtype=jnp.int32)[None, None, :]
    pick = lambda a: jnp.sum(jnp.where(oh, a[:, None, :], 0), axis=2)
    row = pick(base) + RUN_ALIGN * (j[None, :] - pick(goffs))
    live = j[None, :] < ng[:, None]
    spare = rows + (jnp.arange(nt, dtype=jnp.int32)[:, None] % 2) * cr + RUN_ALIGN * j[None, :]
    put_rows = jnp.where(live, row, spare).astype(jnp.int32).reshape(nt, 1, ngrp)
    get_rows = jnp.where(live, row, 0).astype(jnp.int32).reshape(nt, 1, ngrp)
    return put_rows, get_rows, block_expert, fill_base, fill_cnt, n_used, n_blocks


def _group_copies(n_groups, src_of, dst_of, sem, wait):
    def body(j, carry):
        cp = pltpu.make_async_copy(src_of(j), dst_of(j), sem)
        cp.wait() if wait else cp.start()
        return carry

    lax.fori_loop(0, n_groups, body, 0, unroll=8)


def _rows_at(ref, start):
    return ref.at[pl.ds(pl.multiple_of(start, RUN_ALIGN), RUN_ALIGN), :]


def _compact_kernel(fill_base_ref, fill_cnt_ref, put_ref, pput_ref,
                    h_ref, mod_ref, g_ref, lpos_ref, xs_hbm, comp_ref, zero_ref, sems):
    i = pl.program_id(0)
    s = i % 2
    m = mod_ref[0]
    f = _norm_mod(h_ref[...], g_ref[...], m[:, 3 * D:4 * D], m[:, 4 * D:5 * D]).astype(BF16)
    cr, tm = comp_ref.shape[1], h_ref.shape[0]
    q = lax.broadcasted_iota(jnp.int32, (cr, tm), 0).astype(F32)
    lp = lpos_ref[...].astype(F32)
    miss = (lp[0:1, :] - q) * (lp[1:2, :] - q) * ((lp[2:3, :] - q) * (lp[3:4, :] - q))
    comp_ref[s] = _pack_rows(jnp.dot(jnp.where(miss == 0.0, 1.0, 0.0).astype(BF16), f, preferred_element_type=F32))

    def copies(slot, rows_ref, wait):
        src = lambda j: _rows_at(comp_ref.at[slot], j * RUN_ALIGN)
        dst = lambda j: _rows_at(xs_hbm, rows_ref[0, 0, j])
        _group_copies(cr // RUN_ALIGN, src, dst, sems.at[slot], wait)

    @pl.when(i == 0)
    def _():
        zero_ref[...] = jnp.zeros_like(zero_ref)
        for e in range(fill_cnt_ref.shape[0]):
            def fill(c, carry, wait=False, e=e):
                cp = pltpu.make_async_copy(zero_ref, _rows_at(xs_hbm, fill_base_ref[e] + c * RUN_ALIGN), sems.at[0])
                cp.wait() if wait else cp.start()
                return carry

            lax.fori_loop(0, fill_cnt_ref[e], fill, 0)
            lax.fori_loop(0, fill_cnt_ref[e], functools.partial(fill, wait=True), 0)

    def wait_slot(slot):
        pltpu.make_async_copy(comp_ref.at[slot], xs_hbm.at[pl.ds(0, cr), :], sems.at[slot]).wait()

    del pput_ref

    @pl.when(i > 0)
    def _():
        wait_slot(1 - s)

    copies(s, put_ref, False)

    @pl.when(i == pl.num_programs(0) - 1)
    def _():
        wait_slot(s)


def _group_table(shift, nt, ngrp):
    def idx(i, *_):
        return (jnp.clip(i + shift, 0, nt - 1), 0, 0)

    return pl.BlockSpec((1, 1, ngrp), idx, memory_space=pltpu.SMEM)


def _moe_compact(lay, h, mod, g, lpos, put_rows, fill_base, fill_cnt, n_rows, n_blocks):
    tm = lay.tm
    nt, ngrp = n_rows // tm, _comp_rows(tm) // RUN_ALIGN
    grid_spec = pltpu.PrefetchScalarGridSpec(
        num_scalar_prefetch=2,
        grid=(nt,),
        in_specs=[_group_table(0, nt, ngrp), _group_table(-1, nt, ngrp)]
                 + [pl.BlockSpec((tm, D), lambda i, *_: (i, 0)),
                    pl.BlockSpec((1, 1, 6 * D), lambda i, *_: (lay.mod_idx(i), 0, 0)),
                    pl.BlockSpec((1, D), lambda i, *_: (0, 0)),
                    pl.BlockSpec((TOP_K, tm), lambda i, *_: (0, i))],
        out_specs=pl.BlockSpec(memory_space=pl.ANY),
        scratch_shapes=[pltpu.VMEM((2, _comp_rows(tm), DP), jnp.uint32), pltpu.VMEM((RUN_ALIGN, DP), jnp.uint32),
                        pltpu.SemaphoreType.DMA((2,))],
    )
    return pl.pallas_call(
        _compact_kernel,
        grid_spec=grid_spec,
        out_shape=jax.ShapeDtypeStruct((n_blocks * MOE_BLOCK + 2 * _comp_rows(tm), DP), jnp.uint32),
        compiler_params=_cparams(("arbitrary",)),
        name="moe_compact",
    )(fill_base, fill_cnt, put_rows, put_rows, h, mod, g.reshape(1, D), lpos)


def _expert_kernel(be_ref, nu_ref, x_ref, w1_ref, b1_ref, w2_ref, b2_ref, o_ref, w1b_ref, w2b_ref):
    i = pl.program_id(0)

    @pl.when((i == 0) | (be_ref[i] != be_ref[jnp.maximum(i - 1, 0)]))
    def _():
        w1b_ref[...] = w1_ref[0, 0].astype(BF16)
        w2b_ref[...] = w2_ref[0, 0].astype(BF16)

    @pl.when(i < nu_ref[0])
    def _():
        x = jnp.concatenate(_unpack_rows(x_ref[...]), axis=1)
        hcat = jnp.dot(x, w1b_ref[...], preferred_element_type=F32) + b1_ref[0]
        h_glu = jnp.minimum(hcat[:, :D_FF], SWIGLU_LIMIT)
        h_lin = jnp.clip(hcat[:, D_FF:], -SWIGLU_LIMIT, SWIGLU_LIMIT)
        act = h_glu * jax.nn.sigmoid(SWIGLU_ALPHA * h_glu) * (h_lin + 1.0)
        y = jnp.dot(act.astype(BF16), w2b_ref[...], preferred_element_type=F32) + b2_ref[0]
        o_ref[...] = _pack_rows(y.astype(BF16).astype(F32))

    @pl.when(i >= nu_ref[0])
    def _():
        o_ref[...] = jnp.zeros_like(o_ref)


def _moe_experts(xs, block_expert, n_used, n_blocks, layer, w1_all, b1, w2_all, b2):
    def xrow(i, be, nu):
        return (jnp.minimum(i, nu[0] - 1), 0)

    grid_spec = pltpu.PrefetchScalarGridSpec(
        num_scalar_prefetch=2,
        grid=(n_blocks,),
        in_specs=[pl.BlockSpec((MOE_BLOCK, DP), xrow),
                  pl.BlockSpec((1, 1, D, 2 * D_FF), lambda i, be, nu: (layer, be[i], 0, 0)),
                  pl.BlockSpec((1, 1, 2 * D_FF), lambda i, be, nu: (be[i], 0, 0)),
                  pl.BlockSpec((1, 1, D_FF, D), lambda i, be, nu: (layer, be[i], 0, 0)),
                  pl.BlockSpec((1, 1, D), lambda i, be, nu: (be[i], 0, 0))],
        out_specs=pl.BlockSpec((MOE_BLOCK, DP), lambda i, be, nu: (i, 0)),
        scratch_shapes=[pltpu.VMEM((D, 2 * D_FF), BF16), pltpu.VMEM((D_FF, D), BF16)],
    )
    return pl.pallas_call(
        _expert_kernel,
        grid_spec=grid_spec,
        out_shape=jax.ShapeDtypeStruct((n_blocks * MOE_BLOCK, DP), jnp.uint32),
        compiler_params=_cparams(("arbitrary",)),
        name="moe_experts",
    )(block_expert, n_used, xs, w1_all, b1.reshape(N_EXPERTS, 1, 2 * D_FF), w2_all, b2.reshape(N_EXPERTS, 1, D))


def _combine_kernel(get_ref, nget_ref, h_ref, gate_ref, lpos_ref, mod_ref, y_hbm, *rest, final):
    fg_ref = rest[0] if final else None
    o_ref, comp_ref, sems = rest[-3:]
    i = pl.program_id(0)
    s = i % 2
    cr, tm = comp_ref.shape[1], h_ref.shape[0]

    def copies(slot, rows_ref, wait):
        src = lambda j: _rows_at(y_hbm, rows_ref[0, 0, j])
        dst = lambda j: _rows_at(comp_ref.at[slot], j * RUN_ALIGN)
        _group_copies(cr // RUN_ALIGN, src, dst, sems.at[slot], wait)

    @pl.when(i == 0)
    def _():
        copies(0, get_ref, False)

    @pl.when(i + 1 < pl.num_programs(0))
    def _():
        copies(1 - s, nget_ref, False)

    q = lax.broadcasted_iota(jnp.int32, (tm, cr), 1)
    gates, lp = gate_ref[...], lpos_ref[...]
    pg = jnp.zeros((tm, cr), F32)
    for k in range(TOP_K):
        pg = jnp.where(lp[:, k:k + 1] == q, gates[:, k:k + 1], pg)
    pltpu.make_async_copy(y_hbm.at[pl.ds(0, cr), :], comp_ref.at[s], sems.at[s]).wait()
    y_lo, y_hi = _unpack_rows(comp_ref[s])
    pgb = pg.astype(BF16)
    y = jnp.concatenate([jnp.dot(pgb, y_lo, preferred_element_type=F32),
                         jnp.dot(pgb, y_hi, preferred_element_type=F32)], axis=1)
    h_new = h_ref[...] + mod_ref[0][:, 5 * D:6 * D] * y
    if final:
        ms = jnp.mean(h_new * h_new, axis=-1, keepdims=True)
        h_new = h_new * lax.rsqrt(ms + NORM_EPS) * fg_ref[...]
    o_ref[...] = h_new


def _moe_combine(lay, h, mod, yb, gates_t, lpos_t, get_rows, n_rows, final_g=None):
    tm = lay.tm
    nt, ngrp = n_rows // tm, _comp_rows(tm) // RUN_ALIGN
    final = final_g is not None
    extra_specs = [pl.BlockSpec((1, D), lambda i, *_: (0, 0))] if final else []
    extra_args = [final_g.reshape(1, D)] if final else []
    grid_spec = pltpu.PrefetchScalarGridSpec(
        num_scalar_prefetch=0,
        grid=(nt,),
        in_specs=[_group_table(0, nt, ngrp), _group_table(1, nt, ngrp),
                  pl.BlockSpec((tm, D), lambda i, *_: (i, 0)),
                  pl.BlockSpec((tm, TOP_K), lambda i, *_: (i, 0)),
                  pl.BlockSpec((tm, TOP_K), lambda i, *_: (i, 0)),
                  pl.BlockSpec((1, 1, 6 * D), lambda i, *_: (lay.mod_idx(i), 0, 0)),
                  pl.BlockSpec(memory_space=pl.ANY)] + extra_specs,
        out_specs=pl.BlockSpec((tm, D), lambda i, *_: (i, 0)),
        scratch_shapes=[pltpu.VMEM((2, _comp_rows(tm), DP), jnp.uint32), pltpu.SemaphoreType.DMA((2,))],
    )
    return pl.pallas_call(
        functools.partial(_combine_kernel, final=final),
        grid_spec=grid_spec,
        out_shape=jax.ShapeDtypeStruct((n_rows, D), F32),
        compiler_params=_cparams(("arbitrary",)),
        name="moe_combine",
    )(get_rows, get_rows, h, gates_t, lpos_t, mod, yb, *extra_args)


def _moe_layer(lay, h, mod, g, w_router, b_router, layer, w1_all, b1, w2_all, b2, n_rows, final_g=None):
    gates, lpos, cnt = _moe_router(lay, h, mod, g, w_router, b_router, n_rows)
    put_rows, get_rows, block_expert, fill_base, fill_cnt, n_used, n_blocks = _moe_plan(cnt[:, :, 0], n_rows, lay.tm)
    xs = _moe_compact(lay, h, mod, g, lpos, put_rows, fill_base, fill_cnt, n_rows, n_blocks)
    yb = _moe_experts(xs, block_expert, n_used, n_blocks, layer, w1_all, b1, w2_all, b2)
    return _moe_combine(lay, h, mod, yb, gates.T, lpos.T, get_rows, n_rows, final_g)


def kernel(x, c, ctx, c_ctx, ada_w, ada_b, norm_g, attn_w_qkv, attn_b_qkv, attn_w_o, attn_b_o, attn_sinks, pool_w, pool_b, pool_scale, rwkv_mix, rwkv_w_rkv, rwkv_w0, rwkv_w1, rwkv_w2, rwkv_a0, rwkv_a1, rwkv_a2, rwkv_g1, rwkv_g2, rwkv_k_k, rwkv_k_a, rwkv_r_k, rwkv_ln_w, rwkv_ln_b, rwkv_w_o, moe_w_router, moe_b_router, moe_w1, moe_b1, moe_w2, moe_b2, final_g):
    B, T, _ = x.shape
    L = ctx.shape[1]
    depth = ada_w.shape[0]
    lay = _Layout(B, T, L)
    h, h_ctx = x.reshape(B * T, D), ctx.reshape(B * L, D)

    n_c = B + 1
    n_c_pad = -(-n_c // 8) * 8
    c_all = jnp.concatenate([c, c_ctx[None, :], jnp.zeros((n_c_pad - n_c, D), F32)], axis=0)
    mods = _ada_mods(c_all, ada_w, ada_b)
    cos, sin = _rope_tables(T, L)

    for layer in range(depth):
        last = layer == depth - 1
        kind, j = layer % 3, layer // 3
        mod = mods[layer].reshape(n_c_pad, 1, 6 * D)
        n_rows = lay.rows_x if last else lay.N
        if kind == 0:
            q, k, v = _attn_qkv(lay, h, mod, norm_g[layer, 0], attn_w_qkv[j], attn_b_qkv[j], cos, sin, h_ctx)
            o = _attention(lay, q, k, v, attn_sinks[j], not last)
            h = _linear_residual(lay, h, o, mod, attn_w_o[j], attn_b_o[j], 2, n_rows, h_ctx)
            h_ctx = None
        elif kind == 1:
            h = _pool_mixer(lay, h, mod, norm_g[layer, 0], pool_w[j], pool_b[j], pool_scale[j], n_rows)
        else:
            p = dict(mix=rwkv_mix[j], w_rkv=rwkv_w_rkv[j], w0=rwkv_w0[j], w1=rwkv_w1[j], w2=rwkv_w2[j],
                     a0=rwkv_a0[j], a1=rwkv_a1[j], a2=rwkv_a2[j], g1=rwkv_g1[j], g2=rwkv_g2[j],
                     k_k=rwkv_k_k[j], k_a=rwkv_k_a[j], r_k=rwkv_r_k[j])
            r, v, kn, g, bonus, lw0, k0, b0, lw1, k1, b1 = _rwkv_proj(lay, h, mod, norm_g[layer, 0], p)
            yf, yb = _rwkv_scan(lay, r, v, kn, lw0, k0, b0, lw1, k1, b1)
            h = _rwkv_out(lay, h, yf, yb, bonus, g, mod, rwkv_ln_w[j], rwkv_ln_b[j], rwkv_w_o[j])
        h = _moe_layer(lay, h, mod, norm_g[layer, 1], moe_w_router[layer], moe_b_router[layer],
                       layer, moe_w1, moe_b1[layer], moe_w2, moe_b2[layer], n_rows,
                       final_g if last else None)
    return h.reshape(B, T, D)
```
